```python
import math
import jax
import jax.numpy as jnp
from jax import lax
import numpy as np

D_MODEL = 1024
BATCH = 32
SEQ = 256
DEPTH = 1
DEC_BATCH = 8
DEC_SEQ = 4096
PAST_LEN = 512

GRID_W = 64
HG_HEADS = 4
HG_DK = 128
HG_DV = 128
HG_CHUNK = 64
DA_HEADS = 4
DA_DQK = 64
DA_DV = 2 * DA_DQK
Q_BLOCK = 128
N_EXPERTS = 16
EC_CAPACITY = 2
D_FF_EXPERT = 1024
ROPE_BASE = 10000.0
EPS = 1e-6
HG_W = HG_HEADS * HG_DK
HG_WV = HG_HEADS * HG_DV
DA_WQK = DA_HEADS * 2 * DA_DQK
DA_WV = DA_HEADS * DA_DV
SPLIT_SIZES = (HG_W, HG_W, HG_W, HG_WV, HG_WV, DA_WQK, DA_WQK, DA_WV, 2 * D_MODEL)
N_IN = sum(SPLIT_SIZES)

kernel_name = 'hybrid_hgrn2_diffattn_ec_moe_diffusion_step'


def rms_norm(x, w):
    xf = x.astype(jnp.float32)
    y = xf * lax.rsqrt(jnp.mean(xf * xf, axis=-1, keepdims=True) + EPS)
    return (y * w.astype(jnp.float32)).astype(x.dtype)


def axial_rope_tables(rows):
    n_freq = DA_DQK // 4
    inv = ROPE_BASE ** (-jnp.arange(n_freq, dtype=jnp.float32) / n_freq)
    row = jnp.repeat(jnp.arange(rows, dtype=jnp.float32), GRID_W)
    col = jnp.tile(jnp.arange(GRID_W, dtype=jnp.float32), rows)
    ang = jnp.stack([row[:, None] * inv, col[:, None] * inv], axis=1)
    return jnp.cos(ang), jnp.sin(ang)


def apply_axial_rope(x, cos, sin):
    B, T, H, M, _ = x.shape
    xr = x.reshape(B, T, H, M, 2, 2, DA_DQK // 4)
    x1, x2 = xr[..., 0, :], xr[..., 1, :]
    c = cos.astype(x.dtype)[None, :, None, None]
    s = sin.astype(x.dtype)[None, :, None, None]
    out = jnp.stack([x1 * c - x2 * s, x2 * c + x1 * s], axis=-2)
    return out.reshape(x.shape)


def hgrn2_scan(q, f, i, s0):
    B, T, H, dk = q.shape
    dv = i.shape[-1]
    nc = T // HG_CHUNK

    def chunks(a):
        return a.astype(jnp.float32).reshape(B, nc, HG_CHUNK, H, a.shape[-1]).transpose(1, 0, 3, 2, 4)

    ff = f.astype(jnp.float32)
    mask = jnp.tril(jnp.ones((HG_CHUNK, HG_CHUNK), dtype=bool))[None, None, :, :, None]

    def step(S, inp):
        qc, lfc, kc, vc = inp
        b = jnp.cumsum(lfc, axis=2)
        inter = jnp.einsum('bhtd,bhde->bhte', qc * jnp.exp(b), S)
        rel = jnp.where(mask, b[:, :, :, None, :] - b[:, :, None, :, :], -jnp.inf)
        attn = jnp.einsum('bhtd,bhtsd,bhsd->bhts', qc, jnp.exp(rel), kc)
        o = inter + jnp.einsum('bhts,bhse->bhte', attn, vc)
        bL = b[:, :, -1]
        S = jnp.exp(bL)[..., None] * S + jnp.einsum('bhsd,bhse->bhde', kc * jnp.exp(bL[:, :, None] - b), vc)
        return S, o

    S, o = lax.scan(step, s0.astype(jnp.float32),
                    (chunks(q), chunks(jnp.log(ff)), chunks(1.0 - ff), chunks(i)))
    o = o.transpose(1, 0, 3, 2, 4).reshape(B, T, H, dv)
    return o, S


def hgrn2_branch(hq, hff, hfb, hi, hg, lb, norm_w, s0):
    B, T, _ = hq.shape
    q = jax.nn.silu(hq).reshape(B, T, HG_HEADS, HG_DK)
    i = hi.reshape(B, T, HG_HEADS, HG_DV)

    def forget(z, lbd):
        f = lbd + (1.0 - lbd) * jax.nn.sigmoid(z.astype(jnp.float32))
        return f.reshape(B, T, HG_HEADS, HG_DK)

    o_f, s_f = hgrn2_scan(q, forget(hff, lb[0]), i, s0[:, 0])
    o_b, s_b = hgrn2_scan(q[:, ::-1], forget(hfb, lb[1])[:, ::-1], i[:, ::-1], s0[:, 1])
    o = rms_norm(o_f + o_b[:, ::-1], norm_w).astype(hq.dtype).reshape(B, T, HG_WV)
    return o * jax.nn.silu(hg), jnp.stack([s_f, s_b], axis=1)


def diff_attention(q, k, v, lam, lam_init, subln_w):
    B, Tq, H, _, d = q.shape
    nb = Tq // Q_BLOCK
    qb = q.reshape(B, nb, Q_BLOCK, H, 2, d).transpose(1, 0, 2, 3, 4, 5)
    scale = d ** -0.5

    def block(qblk):
        s = jnp.einsum('bqhmd,bkhmd->bhmqk', qblk, k, preferred_element_type=jnp.float32) * scale
        p = jax.nn.softmax(s, axis=-1)
        a = p[:, :, 0] - lam * p[:, :, 1]
        return jnp.einsum('bhqk,bkhe->bqhe', a.astype(v.dtype), v)

    o = lax.map(block, qb)
    o = o.transpose(1, 0, 2, 3, 4).reshape(B, Tq, H, DA_DV)
    o = rms_norm(o, subln_w) * (1.0 - lam_init)
    return o.reshape(B, Tq, DA_WV)


def mixer(h, w_in, b_gate, lb, hgrn_norm_w, qk_norm_w, lam, lam_init, subln_w,
          w_bh, w_ba, w_out, ctx_k, ctx_v, s0, rope):
    B, T, _ = h.shape
    split_at = [int(s) for s in np.cumsum(SPLIT_SIZES)[:-1]]
    hq, hff, hfb, hi, hg, dq, dk, dv, gl = jnp.split(h @ w_in, split_at, axis=-1)
    o_h, s_new = hgrn2_branch(hq, hff, hfb, hi, hg, lb, hgrn_norm_w, s0)
    q = rms_norm(dq.reshape(B, T, DA_HEADS, 2, DA_DQK), qk_norm_w[0])
    k = rms_norm(dk.reshape(B, T, DA_HEADS, 2, DA_DQK), qk_norm_w[1])
    v = dv.reshape(B, T, DA_HEADS, DA_DV)
    if rope is None:
        keys, vals = k, v
    else:
        q = rope(q)
        keys = jnp.concatenate([ctx_k.astype(k.dtype), rope(k)], axis=1)
        vals = jnp.concatenate([ctx_v.astype(v.dtype), v], axis=1)
    o_a = diff_attention(q, keys, vals, lam, lam_init, subln_w)
    g = jax.nn.sigmoid((gl + b_gate).astype(jnp.float32)).astype(h.dtype)
    g_h, g_a = jnp.split(g, 2, axis=-1)
    merged = g_h * (o_h @ w_bh) + g_a * (o_a @ w_ba)
    return merged @ w_out, k, v, s_new


def ec_moe(h, router_w, w_gate, w_up, w_down):
    B, T, D = h.shape
    cap = EC_CAPACITY * T // N_EXPERTS
    aff = jax.nn.softmax(jnp.einsum('btd,de->bte', h, router_w, preferred_element_type=jnp.float32), axis=-1)
    gate, idx = lax.top_k(jnp.swapaxes(aff, 1, 2), cap)
    flat = idx.reshape(B, N_EXPERTS * cap)
    xg = jnp.take_along_axis(h, flat[..., None], axis=1).reshape(B, N_EXPERTS, cap, D)
    a = jax.nn.silu(jnp.einsum('becd,edf->becf', xg, w_gate)) * jnp.einsum('becd,edf->becf', xg, w_up)
    y = jnp.einsum('becf,efd->becd', a, w_down) * gate[..., None].astype(h.dtype)
    return jnp.zeros_like(h).at[jnp.arange(B)[:, None], flat].add(y.reshape(B, N_EXPERTS * cap, D))


def trunk_layer(x, mod, norm1_w, norm2_w, mp, moe_p, ctx_k, ctx_v, s0, rope):
    sh1, sc1, g1, sh2, sc2, g2 = jnp.split(mod, 6, axis=-1)
    h = rms_norm(x, norm1_w) * (1.0 + sc1) + sh1
    y, k, v, s_new = mixer(h, *mp, ctx_k, ctx_v, s0, rope)
    x = x + g1 * y
    h = rms_norm(x, norm2_w) * (1.0 + sc2) + sh2
    x = x + g2 * ec_moe(h, *moe_p)
    return x, k, v, s_new


def setup_inputs(seed: int = 0) -> dict:
    key = jax.random.key(seed)
    ks = jax.random.split(key, 26)

    def nrm(k, shape, s):
        return jax.random.normal(k, shape, jnp.float32) * s

    D = D_MODEL
    return {
        'x_prompt': nrm(ks[0], (BATCH, SEQ, D), 1.0),
        'x_sample': nrm(ks[1], (DEC_BATCH, DEC_SEQ, D), 1.0),
        'cache_k': nrm(ks[2], (DEC_BATCH, DEPTH, PAST_LEN, DA_HEADS, 2, DA_DQK), 1.0),
        'cache_v': nrm(ks[3], (DEC_BATCH, DEPTH, PAST_LEN, DA_HEADS, DA_DV), 1.0),
        'state_hgrn': nrm(ks[4], (DEC_BATCH, DEPTH, 2, HG_HEADS, HG_DK, HG_DV), 0.5),
        'c': nrm(ks[5], (DEC_BATCH, D), 1.0),
        'c_ctx': nrm(ks[6], (D,), 1.0),
        'norm1_w': 1.0 + nrm(ks[7], (DEPTH, D), 0.02),
        'norm2_w': 1.0 + nrm(ks[8], (DEPTH, D), 0.02),
        'w_mod': nrm(ks[9], (DEPTH, D, 6 * D), 0.5 * D ** -0.5),
        'b_mod': nrm(ks[10], (DEPTH, 6 * D), 0.02),
        'w_in': nrm(ks[11], (DEPTH, D, N_IN), D ** -0.5),
        'b_gate': nrm(ks[12], (DEPTH, 2 * D), 0.02),
        'hgrn_lb_logits': nrm(ks[13], (2, DEPTH + 1, HG_W), 0.1),
        'hgrn_norm_w': 1.0 + nrm(ks[14], (DEPTH, HG_HEADS, HG_DV), 0.02),
        'qk_norm_w': 1.0 + nrm(ks[15], (DEPTH, 2, DA_DQK), 0.02),
        'diff_lambda': nrm(ks[16], (DEPTH, 4, DA_DQK), 0.1),
        'diff_subln_w': 1.0 + nrm(ks[17], (DEPTH, DA_DV), 0.02),
        'w_branch_hgrn': nrm(ks[18], (DEPTH, HG_WV, D), HG_WV ** -0.5),
        'w_branch_attn': nrm(ks[19], (DEPTH, DA_WV, D), DA_WV ** -0.5),
        'w_out': nrm(ks[20], (DEPTH, D, D), D ** -0.5),
        'router_w': nrm(ks[21], (DEPTH, D, N_EXPERTS), D ** -0.5),
        'w_exp_gate': nrm(ks[22], (DEPTH, N_EXPERTS, D, D_FF_EXPERT), D ** -0.5),
        'w_exp_up': nrm(ks[23], (DEPTH, N_EXPERTS, D, D_FF_EXPERT), D ** -0.5),
        'w_exp_down': nrm(ks[24], (DEPTH, N_EXPERTS, D_FF_EXPERT, D), D_FF_EXPERT ** -0.5),
    }


def reference(x_prompt, x_sample, cache_k, cache_v, state_hgrn, c, c_ctx,
              norm1_w, norm2_w, w_mod, b_mod, w_in, b_gate, hgrn_lb_logits, hgrn_norm_w,
              qk_norm_w, diff_lambda, diff_subln_w, w_branch_hgrn, w_branch_attn, w_out,
              router_w, w_exp_gate, w_exp_up, w_exp_down):
    lb_all = jnp.cumsum(jax.nn.softmax(hgrn_lb_logits.astype(jnp.float32), axis=1), axis=1)
    rows = x_sample.shape[1] // GRID_W
    cos, sin = axial_rope_tables(rows)

    def rope(t):
        return apply_axial_rope(t, cos, sin)

    cond_ctx = jax.nn.silu(c_ctx)[None]
    cond_lat = jax.nn.silu(c)
    s0_ctx = jnp.zeros((x_prompt.shape[0], 2, HG_HEADS, HG_DK, HG_DV), jnp.float32)
    yp, ys = x_prompt, x_sample
    k_list, v_list, s_list = [], [], []
    for l in range(DEPTH):
        lam_init = 0.8 - 0.6 * math.exp(-0.3 * l)
        lv = diff_lambda[l].astype(jnp.float32)
        lam = jnp.exp(jnp.sum(lv[0] * lv[1])) - jnp.exp(jnp.sum(lv[2] * lv[3])) + lam_init
        mp = (w_in[l], b_gate[l], lb_all[:, l], hgrn_norm_w[l], qk_norm_w[l], lam, lam_init,
              diff_subln_w[l], w_branch_hgrn[l], w_branch_attn[l], w_out[l])
        moe_p = (router_w[l], w_exp_gate[l], w_exp_up[l], w_exp_down[l])
        mod_ctx = (cond_ctx @ w_mod[l] + b_mod[l])[:, None, :]
        yp, k_l, v_l, s_l = trunk_layer(yp, mod_ctx, norm1_w[l], norm2_w[l], mp, moe_p,
                                        None, None, s0_ctx, None)
        k_list.append(k_l)
        v_list.append(v_l)
        s_list.append(s_l.astype(x_prompt.dtype))
        mod_lat = (cond_lat @ w_mod[l] + b_mod[l])[:, None, :]
        ys, _, _, _ = trunk_layer(ys, mod_lat, norm1_w[l], norm2_w[l], mp, moe_p,
                                  cache_k[:, l], cache_v[:, l], state_hgrn[:, l], rope)
    new_cache_k = jnp.stack(k_list, axis=1)
    new_cache_v = jnp.stack(v_list, axis=1)
    new_state_hgrn = jnp.stack(s_list, axis=1)
    return (yp, ys, new_cache_k, new_cache_v, new_state_hgrn)
```

```python
import functools
import math

import jax
import jax.numpy as jnp
from jax import lax
from jax.experimental import pallas as pl
from jax.experimental.pallas import tpu as pltpu

F32 = jnp.float32
BF16 = jnp.bfloat16
I32 = jnp.int32

EPS = 1e-6
GRID_W = 64
HG_HEADS = 4
HG_DK = 128
HG_DV = 128
HG_CHUNK = 64
DA_HEADS = 4
DA_DQK = 64
DA_DV = 128
N_EXPERTS = 16
EC_CAPACITY = 2
ROPE_BASE = 10000.0
SEG = 512
N_SEG = 12
LAM_INIT = 0.8 - 0.6 * math.exp(-0.3 * 0)
LANES = 128
HG_SAFE_DECAY = 80.0
VMEM_LIMIT = 56 * 1024 * 1024


def _dot(a, b):
    return jnp.dot(a, b, preferred_element_type=F32)


def _dot_nt(a, b):
    return lax.dot_general(a, b, (((1,), (1,)), ((), ())), preferred_element_type=F32)


def _dot_tn(a, b):
    return lax.dot_general(a, b, (((0,), (0,)), ((), ())), preferred_element_type=F32)


def _split2(x):
    hi = x.astype(BF16)
    lo = (x - hi.astype(F32)).astype(BF16)
    return hi, lo


def _split3(x):
    hi = x.astype(BF16)
    r = x - hi.astype(F32)
    mid = r.astype(BF16)
    lo = (r - mid.astype(F32)).astype(BF16)
    return hi, mid, lo


def _silu(x):
    return x * jax.nn.sigmoid(x)


def _params(sem):
    return pltpu.CompilerParams(dimension_semantics=sem, vmem_limit_bytes=VMEM_LIMIT)


def _mod_kernel(c_ref, w_ref, b_ref, o_ref):
    s_hi, s_lo = _split2(_silu(c_ref[...]))
    w_hi, w_lo = _split2(w_ref[...])
    o_ref[...] = _dot(s_hi, w_hi) + _dot(s_hi, w_lo) + _dot(s_lo, w_hi) + b_ref[...]


def _modulation(cond, w_mod, b_mod):
    rows, d = cond.shape
    n = w_mod.shape[1]
    bn = 512
    return pl.pallas_call(
        _mod_kernel,
        grid=(n // bn,),
        in_specs=[pl.BlockSpec((rows, d), lambda j: (0, 0)),
                  pl.BlockSpec((d, bn), lambda j: (0, j)),
                  pl.BlockSpec((1, bn), lambda j: (0, j))],
        out_specs=pl.BlockSpec((rows, bn), lambda j: (0, j)),
        out_shape=jax.ShapeDtypeStruct((rows, n), F32),
        compiler_params=_params(("arbitrary",)),
    )(cond, w_mod, b_mod.reshape(1, n))


def _group_rms(z, gm_ref, w):
    s_hi, s_lo = _split2(z * z)
    ms = _dot(s_hi, gm_ref[...]) + _dot(s_lo, gm_ref[...])
    return z * lax.rsqrt(ms + EPS) * w


def _rope(x, cos, sin_signed):
    n = x.shape[-1]
    lane = lax.broadcasted_iota(I32, x.shape, 1)
    partner = jnp.where((lane & 16) == 0, pltpu.roll(x, n - 16, 1), pltpu.roll(x, 16, 1))
    return x * cos + partner * sin_signed


def _premix_kernel(*refs, latent):
    if latent:
        (x_ref, mod_ref, n1_ref, win_ref, bg_ref, lbl_ref, qkw_ref, gm_ref, cos_ref, sin_ref,
         q_o, lff_o, lfb_o, kff_o, kfb_o, i_o, g_o, dq_o, dk_o, dv_o, gate_o) = refs
    else:
        (x_ref, mod_ref, n1_ref, win_ref, bg_ref, lbl_ref, qkw_ref, gm_ref,
         q_o, lff_o, lfb_o, kff_o, kfb_o, i_o, g_o, dq_o, dk_o, dv_o, gate_o) = refs
    d = x_ref.shape[1]
    mod = mod_ref[0]
    sh1, sc1 = mod[:, 0:d], mod[:, d:2 * d]
    x = x_ref[...]
    xn = x * lax.rsqrt(jnp.mean(x * x, axis=-1, keepdims=True) + EPS) * n1_ref[...]
    hb = (xn * (1.0 + sc1) + sh1).astype(BF16)

    def seg(j):
        return _dot(hb, win_ref[:, j * SEG:(j + 1) * SEG])

    def lower_bound(direction):
        l0 = lbl_ref[2 * direction:2 * direction + 1, :]
        l1 = lbl_ref[2 * direction + 1:2 * direction + 2, :]
        mx = jnp.maximum(l0, l1)
        e0, e1 = jnp.exp(l0 - mx), jnp.exp(l1 - mx)
        return e0 / (e0 + e1)

    q_o[...] = _silu(seg(0)).astype(q_o.dtype)
    for j, lf_o, kf_o in ((1, lff_o, kff_o), (2, lfb_o, kfb_o)):
        lbd = lower_bound(j - 1)
        f = lbd + (1.0 - lbd) * jax.nn.sigmoid(seg(j))
        lf_o[...] = jnp.log(f)
        kf_o[...] = (1.0 - f).astype(kf_o.dtype)
    i_o[...] = seg(3).astype(i_o.dtype)
    g_o[...] = _silu(seg(4)).astype(g_o.dtype)
    qn = _group_rms(seg(5), gm_ref, qkw_ref[0:1, :]) * (DA_DQK ** -0.5)
    kn = _group_rms(seg(6), gm_ref, qkw_ref[1:2, :])
    if latent:
        qn = _rope(qn, cos_ref[...], sin_ref[...])
        kn = _rope(kn, cos_ref[...], sin_ref[...])
    dq_o[...] = qn.astype(dq_o.dtype)
    dk_o[...] = kn.astype(dk_o.dtype)
    dv_o[...] = seg(7).astype(dv_o.dtype)
    for j in range(4):
        z = seg(8 + j) + bg_ref[:, j * SEG:(j + 1) * SEG]
        gate_o[:, j * SEG:(j + 1) * SEG] = jax.nn.sigmoid(z).astype(gate_o.dtype)


def _premix(x, mod3, mod_row, norm1_w, w_in, b_gate, lb_logits, qkw, gm, rope, tm, kv_dtype):
    n, d = x.shape
    latent = rope is not None
    const = lambda i: (0, 0)
    in_specs = [pl.BlockSpec((tm, d), lambda i: (i, 0)),
                pl.BlockSpec((1, 1, mod3.shape[2]), lambda i: (mod_row(i), 0, 0)),
                pl.BlockSpec((1, d), const),
                pl.BlockSpec(w_in.shape, const),
                pl.BlockSpec(b_gate.shape, const),
                pl.BlockSpec(lb_logits.shape, const),
                pl.BlockSpec(qkw.shape, const),
                pl.BlockSpec(gm.shape, const)]
    args = [x, mod3, norm1_w, w_in, b_gate, lb_logits, qkw, gm]
    if latent:
        cos, sin = rope
        nblk = cos.shape[0] // tm
        in_specs += [pl.BlockSpec((tm, SEG), lambda i: (i % nblk, 0))] * 2
        args += [cos, sin]
    seg_spec = pl.BlockSpec((tm, SEG), lambda i: (i, 0))
    out_dtypes = [BF16, F32, F32, BF16, BF16, BF16, BF16, BF16, kv_dtype, kv_dtype]
    out_shape = [jax.ShapeDtypeStruct((n, SEG), t) for t in out_dtypes]
    out_shape.append(jax.ShapeDtypeStruct((n, 4 * SEG), BF16))
    out_specs = [seg_spec] * 10 + [pl.BlockSpec((tm, 4 * SEG), lambda i: (i, 0))]
    return pl.pallas_call(
        functools.partial(_premix_kernel, latent=latent),
        grid=(n // tm,),
        in_specs=in_specs, out_specs=out_specs, out_shape=out_shape,
        compiler_params=_params(("parallel",)),
    )(*args)


def _hgrn_chunk(q, lf, k, v, st, tri, keep, safe):
    hi, mid, lo = _split3(lf)
    b = _dot(tri, hi) + _dot(tri, mid) + _dot(tri, lo)
    total = jnp.sum(lf, axis=0, keepdims=True)
    inter = _dot_nt((q * jnp.exp(b)).astype(BF16), st.astype(BF16))
    if safe:
        ref = b[HG_CHUNK // 2:HG_CHUNK // 2 + 1, :]
        qa = (q * jnp.exp(b - ref)).astype(BF16)
        kb = (k * jnp.exp(ref - b)).astype(BF16)
        attn = jnp.where(keep, _dot_nt(qa, kb), 0.0)
    else:
        col = lax.broadcasted_iota(I32, (HG_CHUNK, HG_CHUNK), 1)

        def column(s, acc):
            onehot = (lax.broadcasted_iota(I32, (HG_CHUNK, 1), 0) == s).astype(F32)
            bs = jnp.sum(b * onehot, axis=0, keepdims=True)
            ks = jnp.sum(k * onehot, axis=0, keepdims=True)
            w = jnp.sum(q * ks * jnp.exp(jnp.minimum(b - bs, 0.0)), axis=1, keepdims=True)
            return jnp.where(col == s, w, acc)

        attn = lax.fori_loop(0, HG_CHUNK, column, jnp.zeros((HG_CHUNK, HG_CHUNK), F32))
        attn = jnp.where(keep, attn, 0.0)
    vb = v.astype(BF16)
    o = inter + _dot(attn.astype(BF16), vb)
    kd = (k * jnp.exp(total - b)).astype(BF16)
    st = st * jnp.exp(total) + _dot_tn(vb, kd)
    return o, st


def _hgrn_kernel(*refs, has_state):
    if has_state:
        (q_ref, lff_ref, lfb_ref, kff_ref, kfb_ref, i_ref, g_ref, nw_ref, s0_ref,
         o_ref, sn_ref, of_s, ob_s) = refs
    else:
        (q_ref, lff_ref, lfb_ref, kff_ref, kfb_ref, i_ref, g_ref, nw_ref,
         o_ref, sn_ref, of_s, ob_s) = refs
    t_len = q_ref.shape[1]
    nc = t_len // HG_CHUNK
    head = pl.program_id(1)
    row = lax.broadcasted_iota(I32, (HG_CHUNK, HG_CHUNK), 0)
    col = lax.broadcasted_iota(I32, (HG_CHUNK, HG_CHUNK), 1)
    keep_f, keep_b = row >= col, row <= col
    tri_f, tri_b = keep_f.astype(BF16), keep_b.astype(BF16)

    def chunk_sums(ref):
        return jnp.sum(ref[0].reshape(nc, HG_CHUNK, HG_DK), axis=1)

    worst = jnp.minimum(jnp.min(chunk_sums(lff_ref)), jnp.min(chunk_sums(lfb_ref)))

    if has_state:
        st_f0, st_b0 = s0_ref[0, 0, 0].T, s0_ref[0, 1, 0].T
    else:
        st_f0 = st_b0 = jnp.zeros((HG_DV, HG_DK), F32)

    def scan(safe):
        def body(c, carry):
            st_f, st_b = carry
            rf = pl.ds(pl.multiple_of(c * HG_CHUNK, HG_CHUNK), HG_CHUNK)
            rb = pl.ds(pl.multiple_of((nc - 1 - c) * HG_CHUNK, HG_CHUNK), HG_CHUNK)
            o_f, st_f = _hgrn_chunk(q_ref[0, rf, :].astype(F32), lff_ref[0, rf, :],
                                    kff_ref[0, rf, :].astype(F32), i_ref[0, rf, :], st_f,
                                    tri_f, keep_f, safe)
            o_b, st_b = _hgrn_chunk(q_ref[0, rb, :].astype(F32), lfb_ref[0, rb, :],
                                    kfb_ref[0, rb, :].astype(F32), i_ref[0, rb, :], st_b,
                                    tri_b, keep_b, safe)
            of_s[rf, :] = o_f
            ob_s[rb, :] = o_b
            return st_f, st_b
        return lax.fori_loop(0, nc, body, (st_f0, st_b0))

    st_f, st_b = lax.cond(worst >= -HG_SAFE_DECAY, lambda: scan(True), lambda: scan(False))
    sn_ref[0, 0, 0] = st_f.T
    sn_ref[0, 1, 0] = st_b.T
    o = of_s[...] + ob_s[...]
    nw = nw_ref[pl.ds(head, 1), :]
    on = o * lax.rsqrt(jnp.mean(o * o, axis=-1, keepdims=True) + EPS) * nw
    o_ref[0] = (on * g_ref[0].astype(F32)).astype(o_ref.dtype)


def _hgrn(q, lff, lfb, kff, kfb, iv, g, norm_w, s0, batch):
    n = q.shape[0]
    t_len = n // batch
    shp = (batch, t_len, SEG)
    blk = pl.BlockSpec((1, t_len, HG_DK), lambda b, h: (b, 0, h))
    st_blk = pl.BlockSpec((1, 2, 1, HG_DK, HG_DV), lambda b, h: (b, 0, h, 0, 0))
    args = [a.reshape(shp) for a in (q, lff, lfb, kff, kfb, iv, g)] + [norm_w]
    in_specs = [blk] * 7 + [pl.BlockSpec(norm_w.shape, lambda b, h: (0, 0))]
    if s0 is not None:
        args.append(s0)
        in_specs.append(st_blk)
    o, s_new = pl.pallas_call(
        functools.partial(_hgrn_kernel, has_state=s0 is not None),
        grid=(batch, HG_HEADS),
        in_specs=in_specs,
        out_specs=[blk, st_blk],
        out_shape=[jax.ShapeDtypeStruct(shp, BF16),
                   jax.ShapeDtypeStruct((batch, 2, HG_HEADS, HG_DK, HG_DV), F32)],
        scratch_shapes=[pltpu.VMEM((t_len, HG_DV), F32), pltpu.VMEM((t_len, HG_DV), F32)],
        compiler_params=_params(("parallel", "parallel")),
    )(*args)
    return o.reshape(n, SEG), s_new


def _attn_kernel(*refs, has_ctx):
    if has_ctx:
        q_ref, k_ref, v_ref, ck_ref, cv_ref, lam_ref, sw_ref, o_ref = refs
    else:
        q_ref, k_ref, v_ref, lam_ref, sw_ref, o_ref = refs
    lv = lam_ref[...]
    lam = (jnp.exp(jnp.sum(lv[0:1] * lv[1:2], keepdims=True))
           - jnp.exp(jnp.sum(lv[2:3] * lv[3:4], keepdims=True)) + LAM_INIT)
    q = q_ref[0]
    lane = lax.broadcasted_iota(I32, q.shape, 1)
    zero = jnp.zeros_like(q)
    q_maps = (jnp.where(lane < DA_DQK, q, zero), jnp.where(lane >= DA_DQK, q, zero))
    keys = [k_ref[0].astype(BF16)]
    vals = [v_ref[0].astype(BF16)]
    if has_ctx:
        keys.append(ck_ref[0].astype(BF16))
        vals.append(cv_ref[0].astype(BF16))

    def softmax(qm):
        s = [_dot_nt(qm, kk) for kk in keys]
        m = functools.reduce(jnp.maximum, [jnp.max(x, axis=-1, keepdims=True) for x in s])
        p = [jnp.exp(x - m) for x in s]
        inv = 1.0 / functools.reduce(jnp.add, [jnp.sum(x, axis=-1, keepdims=True) for x in p])
        return [x * inv for x in p]

    p1, p2 = softmax(q_maps[0]), softmax(q_maps[1])
    o = functools.reduce(jnp.add, [_dot((a - lam * b).astype(BF16), vv)
                                   for a, b, vv in zip(p1, p2, vals)])
    on = o * lax.rsqrt(jnp.mean(o * o, axis=-1, keepdims=True) + EPS) * sw_ref[...]
    o_ref[0] = (on * (1.0 - LAM_INIT)).astype(o_ref.dtype)


def _attention(q, k, v, ctx_k, ctx_v, lam_p, subln_w, batch, tq):
    n = q.shape[0]
    t_len = n // batch
    shp = (batch, t_len, SEG)
    q_blk = pl.BlockSpec((1, tq, DA_DV), lambda b, h, i: (b, i, h))
    kv_blk = pl.BlockSpec((1, t_len, DA_DV), lambda b, h, i: (b, 0, h))
    const = lambda b, h, i: (0, 0)
    args = [q.reshape(shp), k.reshape(shp), v.reshape(shp)]
    in_specs = [q_blk, kv_blk, kv_blk]
    if ctx_k is not None:
        past = ctx_k.shape[1]
        c_blk = pl.BlockSpec((1, past, DA_DV), lambda b, h, i: (b, 0, h))
        args += [ctx_k, ctx_v]
        in_specs += [c_blk, c_blk]
    args += [lam_p, subln_w]
    in_specs += [pl.BlockSpec(lam_p.shape, const), pl.BlockSpec(subln_w.shape, const)]
    o = pl.pallas_call(
        functools.partial(_attn_kernel, has_ctx=ctx_k is not None),
        grid=(batch, DA_HEADS, t_len // tq),
        in_specs=in_specs, out_specs=q_blk,
        out_shape=jax.ShapeDtypeStruct(shp, BF16),
        compiler_params=_params(("parallel", "parallel", "arbitrary")),
    )(*args)
    return o.reshape(n, SEG)


def _postmix_kernel(oh_ref, oa_ref, gate_ref, x_ref, mod_ref, wbh_ref, wba_ref, wout_ref,
                    n2_ref, rw_ref, x1_o, h2_o, afft_o, afftok_o):
    d = x_ref.shape[1]
    mod = mod_ref[0]
    g1, sh2, sc2 = mod[:, 2 * d:3 * d], mod[:, 3 * d:4 * d], mod[:, 4 * d:5 * d]
    g_h = gate_ref[:, 0:d].astype(F32)
    g_a = gate_ref[:, d:2 * d].astype(F32)
    merged = g_h * _dot(oh_ref[...], wbh_ref[...]) + g_a * _dot(oa_ref[...], wba_ref[...])
    x1 = x_ref[...] + g1 * _dot(merged.astype(BF16), wout_ref[...])
    x1_o[...] = x1
    xn = x1 * lax.rsqrt(jnp.mean(x1 * x1, axis=-1, keepdims=True) + EPS) * n2_ref[...]
    h2 = xn * (1.0 + sc2) + sh2
    h2_o[...] = h2.astype(h2_o.dtype)
    h_hi, h_lo = _split2(h2)
    rw = rw_ref[...]
    t1 = _dot_nt(rw, h_hi)
    t2 = _dot_nt(rw, h_lo)
    e = N_EXPERTS
    logits = t1[0:e] + t1[e:2 * e] + t2[0:e]
    mx = jnp.max(logits, axis=0, keepdims=True)
    p = jnp.exp(logits - mx)
    aff = p / jnp.sum(p, axis=0, keepdims=True)
    afft_o[...] = aff
    pad = jnp.zeros((LANES - e, aff.shape[1]), F32)
    afftok_o[...] = jnp.concatenate([aff, pad], axis=0).T


def _postmix(o_h, o_a, gates, x, mod3, mod_row, w_bh, w_ba, w_out, norm2_w, rw_cat, tm):
    n, d = x.shape
    const = lambda i: (0, 0)
    row = lambda i: (i, 0)
    return pl.pallas_call(
        _postmix_kernel,
        grid=(n // tm,),
        in_specs=[pl.BlockSpec((tm, SEG), row), pl.BlockSpec((tm, SEG), row),
                  pl.BlockSpec((tm, 4 * SEG), row), pl.BlockSpec((tm, d), row),
                  pl.BlockSpec((1, 1, mod3.shape[2]), lambda i: (mod_row(i), 0, 0)),
                  pl.BlockSpec(w_bh.shape, const), pl.BlockSpec(w_ba.shape, const),
                  pl.BlockSpec(w_out.shape, const), pl.BlockSpec((1, d), const),
                  pl.BlockSpec(rw_cat.shape, const)],
        out_specs=[pl.BlockSpec((tm, d), row), pl.BlockSpec((tm, d), row),
                   pl.BlockSpec((N_EXPERTS, tm), lambda i: (0, i)),
                   pl.BlockSpec((tm, LANES), row)],
        out_shape=[jax.ShapeDtypeStruct((n, d), F32), jax.ShapeDtypeStruct((n, d), BF16),
                   jax.ShapeDtypeStruct((N_EXPERTS, n), F32),
                   jax.ShapeDtypeStruct((n, LANES), F32)],
        compiler_params=_params(("parallel",)),
    )(o_h, o_a, gates, x, mod3, w_bh, w_ba, w_out, norm2_w, rw_cat)


def _lane_cumsum_exclusive(x, blk):
    e, t = x.shape
    r = lax.broadcasted_iota(I32, (blk, blk), 0)
    c = lax.broadcasted_iota(I32, (blk, blk), 1)
    upper = (r < c).astype(BF16)
    carry = jnp.zeros((e, 1), F32)
    parts = []
    for j in range(t // blk):
        xb = x[:, j * blk:(j + 1) * blk]
        parts.append(_dot(xb.astype(BF16), upper) + carry)
        carry = carry + jnp.sum(xb, axis=1, keepdims=True)
    return parts[0] if len(parts) == 1 else jnp.concatenate(parts, axis=1)


def _route_kernel(aff_ref, pos_ref, *, cap, req_per_group):
    aff = aff_ref[...]
    bits = pltpu.bitcast(aff, I32)
    t_len = aff.shape[1]

    def count(mask):
        return jnp.sum(mask.astype(F32), axis=1, keepdims=True)

    def step(i, th):
        cand = th | (jnp.int32(1) << (30 - i))
        return jnp.where(count(bits >= cand) >= cap, cand, th)

    th = lax.fori_loop(0, 31, step, jnp.zeros((aff.shape[0], 1), I32))
    gt = bits > th
    eq = (bits == th).astype(F32)
    need = cap - count(gt)
    blk = min(t_len, 256)
    tie_rank = _lane_cumsum_exclusive(eq, blk)
    sel = jnp.where(gt, 1.0, jnp.where(tie_rank < need, eq, 0.0))
    slot = _lane_cumsum_exclusive(sel, blk)
    offset = (pl.program_id(0) % req_per_group) * cap
    pos_ref[...] = jnp.where(sel > 0.0, slot.astype(I32) + offset, -1)


def _route(afft, batch, cap, req_per_group):
    e, n = afft.shape
    t_len = n // batch
    blk = pl.BlockSpec((e, t_len), lambda b: (0, b))
    return pl.pallas_call(
        functools.partial(_route_kernel, cap=cap, req_per_group=req_per_group),
        grid=(batch,), in_specs=[blk], out_specs=blk,
        out_shape=jax.ShapeDtypeStruct((e, n), I32),
        compiler_params=_params(("parallel",)),
    )(afft)


def _one_hot_rows(pos_row, n_slots):
    slot = lax.broadcasted_iota(I32, (n_slots, pos_row.shape[1]), 0)
    return (slot == pos_row).astype(BF16)


def _dispatch_kernel(pos_ref, h_ref, xg_ref):
    e = pl.program_id(1)
    p = _one_hot_rows(pos_ref[pl.ds(e, 1), :], xg_ref.shape[1])
    xg_ref[0] = _dot(p, h_ref[...]).astype(xg_ref.dtype)


def _dispatch(pos, h2, group_tokens, group_slots):
    e, n = pos.shape
    d = h2.shape[1]
    groups = n // group_tokens
    return pl.pallas_call(
        _dispatch_kernel,
        grid=(groups, e),
        in_specs=[pl.BlockSpec((e, group_tokens), lambda g, x: (0, g)),
                  pl.BlockSpec((group_tokens, d), lambda g, x: (g, 0))],
        out_specs=pl.BlockSpec((1, group_slots, d), lambda g, x: (x, g, 0)),
        out_shape=jax.ShapeDtypeStruct((e, groups * group_slots, d), BF16),
        compiler_params=_params(("parallel", "arbitrary")),
    )(pos, h2)


def _expert_kernel(xg_ref, wg_ref, wu_ref, wd_ref, y_ref):
    xg = xg_ref[0]
    a = _silu(_dot(xg, wg_ref[0])) * _dot(xg, wu_ref[0])
    y_ref[0] = _dot(a.astype(BF16), wd_ref[0]).astype(y_ref.dtype)


def _experts(xg, w_gate, w_up, w_down, ts):
    e, s, d = xg.shape
    f = w_gate.shape[2]
    x_blk = pl.BlockSpec((1, ts, d), lambda x, i: (x, i, 0))
    return pl.pallas_call(
        _expert_kernel,
        grid=(e, s // ts),
        in_specs=[x_blk, pl.BlockSpec((1, d, f), lambda x, i: (x, 0, 0)),
                  pl.BlockSpec((1, d, f), lambda x, i: (x, 0, 0)),
                  pl.BlockSpec((1, f, d), lambda x, i: (x, 0, 0))],
        out_specs=x_blk,
        out_shape=jax.ShapeDtypeStruct((e, s, d), BF16),
        compiler_params=_params(("parallel", "arbitrary")),
    )(xg, w_gate, w_up, w_down)


def _combine_kernel(pos_ref, afftok_ref, y_ref, x1_ref, mod_ref, o_ref):
    e = pl.program_id(2)
    d = x1_ref.shape[1]

    @pl.when(e == 0)
    def _():
        o_ref[...] = jnp.zeros_like(o_ref)

    p = _one_hot_rows(pos_ref[pl.ds(e, 1), :], y_ref.shape[1])
    lane = lax.broadcasted_iota(I32, afftok_ref.shape, 1)
    gate = jnp.sum(jnp.where(lane == e, afftok_ref[...], 0.0), axis=1, keepdims=True)
    o_ref[...] += gate * _dot_tn(p, y_ref[0])

    @pl.when(e == pl.num_programs(2) - 1)
    def _():
        g2 = mod_ref[0][:, 5 * d:6 * d]
        o_ref[...] = x1_ref[...] + g2 * o_ref[...]


def _combine(pos, afftok, y, x1, mod3, mod_row, group_tokens, group_slots, tc):
    e, n = pos.shape
    d = x1.shape[1]
    groups = n // group_tokens
    per = group_tokens // tc
    tok = lambda g, j, x: (g * per + j, 0)
    return pl.pallas_call(
        _combine_kernel,
        grid=(groups, per, e),
        in_specs=[pl.BlockSpec((e, tc), lambda g, j, x: (0, g * per + j)),
                  pl.BlockSpec((tc, LANES), tok),
                  pl.BlockSpec((1, group_slots, d), lambda g, j, x: (x, g, 0)),
                  pl.BlockSpec((tc, d), tok),
                  pl.BlockSpec((1, 1, mod3.shape[2]), lambda g, j, x: (mod_row(g), 0, 0))],
        out_specs=pl.BlockSpec((tc, d), tok),
        out_shape=jax.ShapeDtypeStruct((n, d), F32),
        compiler_params=_params(("parallel", "parallel", "arbitrary")),
    )(pos, afftok, y, x1, mod3)


def _rope_tables(t_len):
    n_freq = DA_DQK // 4
    inv = ROPE_BASE ** (-jnp.arange(n_freq, dtype=F32) / n_freq)
    t = jnp.arange(t_len)
    pos = jnp.stack([(t // GRID_W).astype(F32), (t % GRID_W).astype(F32)], axis=1)
    ang = pos[:, :, None, None] * inv[None, None, None, :]
    ang = jnp.broadcast_to(ang, (t_len, 2, 2, n_freq))
    sign = jnp.array([-1.0, 1.0], F32)[None, None, :, None]
    cos = jnp.cos(ang).reshape(t_len, DA_DQK)
    sin = (jnp.sin(ang) * sign).reshape(t_len, DA_DQK)
    reps = SEG // DA_DQK
    return jnp.tile(cos, (1, reps)), jnp.tile(sin, (1, reps))


def _trunk(x, batch, mod3, mod_row_tok, mod_row_grp, weights, ctx_k, ctx_v, s0, rope, group_tokens):
    (norm1_w, norm2_w, w_in, b_gate, lb_logits, hgrn_norm_w, qkw, gm, lam_p, subln_w,
     w_bh, w_ba, w_out, rw_cat, w_eg, w_eu, w_ed) = weights
    n, d = x.shape
    t_len = n // batch
    latent = rope is not None
    tm = 256
    (q_h, lff, lfb, kff, kfb, i_h, g_h, dq, dk, dv, gates) = _premix(
        x, mod3, functools.partial(mod_row_tok, tm=tm), norm1_w, w_in, b_gate, lb_logits, qkw, gm,
        rope, tm, BF16 if latent else F32)
    o_h, s_new = _hgrn(q_h, lff, lfb, kff, kfb, i_h, g_h, hgrn_norm_w, s0, batch)
    o_a = _attention(dq, dk, dv, ctx_k, ctx_v, lam_p, subln_w, batch, min(t_len, 256))
    x1, h2, afft, afftok = _postmix(o_h, o_a, gates, x, mod3, functools.partial(mod_row_tok, tm=tm),
                                    w_bh, w_ba, w_out, norm2_w, rw_cat, tm)
    cap = EC_CAPACITY * t_len // N_EXPERTS
    req_per_group = group_tokens // t_len
    group_slots = cap * req_per_group
    pos = _route(afft, batch, cap, req_per_group)
    xg = _dispatch(pos, h2, group_tokens, group_slots)
    y = _experts(xg, w_eg, w_eu, w_ed, group_slots)
    out = _combine(pos, afftok, y, x1, mod3, mod_row_grp, group_tokens, group_slots, 1024)
    return out, dk, dv, s_new


def kernel(x_prompt, x_sample, cache_k, cache_v, state_hgrn, c, c_ctx, norm1_w, norm2_w, w_mod,
           b_mod, w_in, b_gate, hgrn_lb_logits, hgrn_norm_w, qk_norm_w, diff_lambda, diff_subln_w,
           w_branch_hgrn, w_branch_attn, w_out, router_w, w_exp_gate, w_exp_up, w_exp_down):
    batch, seq, d = x_prompt.shape
    dec_batch, dec_seq, _ = x_sample.shape
    past = cache_k.shape[2]
    depth = w_in.shape[0]
    assert depth == 1
    group_tokens = dec_seq
    assert group_tokens % seq == 0 and (batch * seq) % group_tokens == 0
    l = 0

    rows = -(-(1 + dec_batch) // 8) * 8
    cond = jnp.zeros((rows, d), F32).at[0].set(c_ctx).at[1:1 + dec_batch].set(c)
    mod = _modulation(cond, w_mod[l], b_mod[l])
    mod3 = mod.reshape(rows, 1, 6 * d)

    gidx = jnp.arange(SEG) // DA_DQK
    gm = (gidx[:, None] == gidx[None, :]).astype(BF16) * (1.0 / DA_DQK)
    qkw = jnp.tile(qk_norm_w[l], (1, SEG // DA_DQK))
    rw_t = router_w[l].T
    rw_hi = rw_t.astype(BF16)
    rw_cat = jnp.concatenate([rw_hi, (rw_t - rw_hi.astype(F32)).astype(BF16)], axis=0)
    weights = (norm1_w[l][None], norm2_w[l][None], w_in[l].astype(BF16), b_gate[l][None],
               hgrn_lb_logits.reshape(4, SEG), hgrn_norm_w[l], qkw, gm, diff_lambda[l], diff_subln_w[l][None],
               w_branch_hgrn[l].astype(BF16), w_branch_attn[l].astype(BF16), w_out[l].astype(BF16),
               rw_cat, w_exp_gate[l].astype(BF16), w_exp_up[l].astype(BF16),
               w_exp_down[l].astype(BF16))

    yp, k_new, v_new, s_new = _trunk(
        x_prompt.reshape(batch * seq, d), batch, mod3,
        lambda i, tm: 0, lambda g: 0, weights, None, None, None, None, group_tokens)
    per_req = dec_seq
    ys, _, _, _ = _trunk(
        x_sample.reshape(dec_batch * dec_seq, d), dec_batch, mod3,
        lambda i, tm: 1 + (i * tm) // per_req, lambda g: 1 + g, weights,
        cache_k[:, l].reshape(dec_batch, past, SEG), cache_v[:, l].reshape(dec_batch, past, SEG),
        state_hgrn[:, l], _rope_tables(dec_seq), group_tokens)

    return (yp.reshape(batch, seq, d), ys.reshape(dec_batch, dec_seq, d),
            k_new.reshape(batch, 1, seq, DA_HEADS, 2, DA_DQK),
            v_new.reshape(batch, 1, seq, DA_HEADS, DA_DV),
            s_new.reshape(batch, 1, 2, HG_HEADS, HG_DK, HG_DV))
```

```python
import functools
import math

import jax
import jax.numpy as jnp
from jax import lax
from jax.experimental import pallas as pl
from jax.experimental.pallas import tpu as pltpu

F32 = jnp.float32
BF16 = jnp.bfloat16
I32 = jnp.int32

EPS = 1e-6
GRID_W = 64
HG_HEADS = 4
HG_DK = 128
HG_DV = 128
HG_CHUNK_LOG2 = 7
HG_CHUNK = 1 << HG_CHUNK_LOG2
HG_UNROLL = 4
DA_HEADS = 4
DA_DQK = 64
DA_DV = 128
N_EXPERTS = 16
EC_CAPACITY = 2
ROPE_BASE = 10000.0
SEG = 512
N_SEG = 12
LAM_INIT = 0.8 - 0.6 * math.exp(-0.3 * 0)
LANES = 128
ONES_ROWS = 16
Q_SCALE = DA_DQK ** -0.5 * math.log2(math.e)
HG_SAFE_DECAY = 80.0
VMEM_LIMIT = 56 * 1024 * 1024


def _dot(a, b):
    return jnp.dot(a, b, preferred_element_type=F32)


def _dot_nt(a, b):
    return lax.dot_general(a, b, (((1,), (1,)), ((), ())), preferred_element_type=F32)


def _dot_tn(a, b):
    return lax.dot_general(a, b, (((0,), (0,)), ((), ())), preferred_element_type=F32)


def _split2(x):
    hi = x.astype(BF16)
    lo = (x - hi.astype(F32)).astype(BF16)
    return hi, lo


def _split3(x):
    hi = x.astype(BF16)
    r = x - hi.astype(F32)
    mid = r.astype(BF16)
    lo = (r - mid.astype(F32)).astype(BF16)
    return hi, mid, lo


def _silu(x):
    return x * jax.nn.sigmoid(x)


def _params(sem):
    return pltpu.CompilerParams(dimension_semantics=sem, vmem_limit_bytes=VMEM_LIMIT)


def _mod_kernel(c_ref, w_ref, b_ref, o_ref):
    s_hi, s_lo = _split2(_silu(c_ref[...]))
    w_hi, w_lo = _split2(w_ref[...])
    o_ref[...] = _dot(s_hi, w_hi) + _dot(s_hi, w_lo) + _dot(s_lo, w_hi) + b_ref[...]


def _modulation(cond, w_mod, b_mod):
    rows, d = cond.shape
    n = w_mod.shape[1]
    bn = 512
    return pl.pallas_call(
        _mod_kernel,
        name="modulation", grid=(n // bn,),
        in_specs=[pl.BlockSpec((rows, d), lambda j: (0, 0)),
                  pl.BlockSpec((d, bn), lambda j: (0, j)),
                  pl.BlockSpec((1, bn), lambda j: (0, j))],
        out_specs=pl.BlockSpec((rows, bn), lambda j: (0, j)),
        out_shape=jax.ShapeDtypeStruct((rows, n), F32),
        compiler_params=_params(("arbitrary",)),
    )(cond, w_mod, b_mod.reshape(1, n))


def _group_rms(z, gm_ref, w):
    s_hi, s_lo = _split2(z * z)
    ms = _dot(s_hi, gm_ref[...]) + _dot(s_lo, gm_ref[...])
    return z * lax.rsqrt(ms + EPS) * w


def _rope(x, cos, sin_signed):
    n = x.shape[-1]
    lane = lax.broadcasted_iota(I32, x.shape, 1)
    partner = jnp.where((lane & 16) == 0, pltpu.roll(x, n - 16, 1), pltpu.roll(x, 16, 1))
    return x * cos + partner * sin_signed


def _premix_kernel(*refs, latent):
    if latent:
        (x_ref, mod_ref, n1_ref, win_ref, bg_ref, lbl_ref, qkw_ref, gm_ref, cos_ref, sin_ref,
         q_o, bf_o, bb_o, kff_o, kfb_o, i_o, g_o, dq_o, dk_o, dv_o, gate_o) = refs
    else:
        (x_ref, mod_ref, n1_ref, win_ref, bg_ref, lbl_ref, qkw_ref, gm_ref,
         q_o, bf_o, bb_o, kff_o, kfb_o, i_o, g_o, dq_o, dk_o, dv_o, gate_o) = refs
    d = x_ref.shape[1]
    mod = mod_ref[0]
    sh1, sc1 = mod[:, 0:d], mod[:, d:2 * d]
    x = x_ref[...]
    xn = x * lax.rsqrt(jnp.mean(x * x, axis=-1, keepdims=True) + EPS) * n1_ref[...]
    hb = (xn * (1.0 + sc1) + sh1).astype(BF16)

    def seg(j):
        return _dot(hb, win_ref[:, j * SEG:(j + 1) * SEG])

    def lower_bound(direction):
        l0 = lbl_ref[2 * direction:2 * direction + 1, :]
        l1 = lbl_ref[2 * direction + 1:2 * direction + 2, :]
        mx = jnp.maximum(l0, l1)
        e0, e1 = jnp.exp(l0 - mx), jnp.exp(l1 - mx)
        return e0 / (e0 + e1)

    tm = x.shape[0]
    row = lax.broadcasted_iota(I32, (tm, tm), 0)
    col = lax.broadcasted_iota(I32, (tm, tm), 1)
    same_chunk = (row >> HG_CHUNK_LOG2) == (col >> HG_CHUNK_LOG2)

    q_o[...] = _silu(seg(0)).astype(q_o.dtype)
    for j, b_o, kf_o, order in ((1, bf_o, kff_o, row >= col), (2, bb_o, kfb_o, row <= col)):
        lbd = lower_bound(j - 1)
        f = lbd + (1.0 - lbd) * jax.nn.sigmoid(seg(j))
        tri = (same_chunk & order).astype(BF16)
        hi, mid, lo = _split3(jnp.log(f))
        b_o[...] = _dot(tri, hi) + _dot(tri, mid) + _dot(tri, lo)
        kf_o[...] = (1.0 - f).astype(kf_o.dtype)
    i_o[...] = seg(3).astype(i_o.dtype)
    g_o[...] = _silu(seg(4)).astype(g_o.dtype)
    qn = _group_rms(seg(5), gm_ref, qkw_ref[0:1, :]) * Q_SCALE
    kn = _group_rms(seg(6), gm_ref, qkw_ref[1:2, :])
    if latent:
        qn = _rope(qn, cos_ref[...], sin_ref[...])
        kn = _rope(kn, cos_ref[...], sin_ref[...])
    dq_o[...] = qn.astype(dq_o.dtype)
    dk_o[...] = kn.astype(dk_o.dtype)
    dv_o[...] = seg(7).astype(dv_o.dtype)
    for j in range(4):
        z = seg(8 + j) + bg_ref[:, j * SEG:(j + 1) * SEG]
        gate_o[:, j * SEG:(j + 1) * SEG] = jax.nn.sigmoid(z).astype(gate_o.dtype)


def _premix(x, mod3, mod_row, norm1_w, w_in, b_gate, lb_logits, qkw, gm, rope, tm, kv_dtype):
    n, d = x.shape
    latent = rope is not None
    const = lambda i: (0, 0)
    in_specs = [pl.BlockSpec((tm, d), lambda i: (i, 0)),
                pl.BlockSpec((1, 1, mod3.shape[2]), lambda i: (mod_row(i), 0, 0)),
                pl.BlockSpec((1, d), const),
                pl.BlockSpec(w_in.shape, const),
                pl.BlockSpec(b_gate.shape, const),
                pl.BlockSpec(lb_logits.shape, const),
                pl.BlockSpec(qkw.shape, const),
                pl.BlockSpec(gm.shape, const)]
    args = [x, mod3, norm1_w, w_in, b_gate, lb_logits, qkw, gm]
    if latent:
        cos, sin = rope
        nblk = cos.shape[0] // tm
        in_specs += [pl.BlockSpec((tm, SEG), lambda i: (i % nblk, 0))] * 2
        args += [cos, sin]
    seg_spec = pl.BlockSpec((tm, SEG), lambda i: (i, 0))
    out_dtypes = [BF16, F32, F32, BF16, BF16, BF16, BF16, BF16, kv_dtype, kv_dtype]
    out_shape = [jax.ShapeDtypeStruct((n, SEG), t) for t in out_dtypes]
    out_shape.append(jax.ShapeDtypeStruct((n, 4 * SEG), BF16))
    out_specs = [seg_spec] * 10 + [pl.BlockSpec((tm, 4 * SEG), lambda i: (i, 0))]
    return pl.pallas_call(
        functools.partial(_premix_kernel, latent=latent),
        name=f"premix_n{n}", grid=(n // tm,),
        in_specs=in_specs, out_specs=out_specs, out_shape=out_shape,
        compiler_params=_params(("parallel",)),
    )(*args)


def _hgrn_chunk_local(q, b, total, k, v, keep, safe):
    if safe:
        ref = b[HG_CHUNK // 2:HG_CHUNK // 2 + 1, :]
        qa = q * jnp.exp(b - ref)
        kb = k * jnp.exp(ref - b)
        attn = jnp.where(keep, _dot_nt(qa.astype(BF16), kb.astype(BF16)), 0.0)
        qe = qa * jnp.exp(ref)
        kd = kb * jnp.exp(total - ref)
    else:
        qe = q * jnp.exp(b)
        kd = k * jnp.exp(total - b)
        col = lax.broadcasted_iota(I32, (HG_CHUNK, HG_CHUNK), 1)

        def column(s, acc):
            onehot = (lax.broadcasted_iota(I32, (HG_CHUNK, 1), 0) == s).astype(F32)
            bs = jnp.sum(b * onehot, axis=0, keepdims=True)
            ks = jnp.sum(k * onehot, axis=0, keepdims=True)
            w = jnp.sum(q * ks * jnp.exp(jnp.minimum(b - bs, 0.0)), axis=1, keepdims=True)
            return jnp.where(col == s, w, acc)

        attn = lax.fori_loop(0, HG_CHUNK, column, jnp.zeros((HG_CHUNK, HG_CHUNK), F32))
        attn = jnp.where(keep, attn, 0.0)
    vt = v.astype(F32).T.astype(BF16)
    lhs = jnp.concatenate([qe.astype(BF16), attn.astype(BF16)], axis=1)
    return lhs, vt, _dot(vt, kd.astype(BF16)), jnp.exp(total)


def _hgrn_scan_group(chunks, st):
    outs = []
    for lhs, vt, inc, decay in chunks:
        outs.append(_dot_nt(lhs, jnp.concatenate([st.astype(BF16), vt], axis=1)))
        st = st * decay + inc
    return outs, st


def _hgrn_kernel(*refs, has_state):
    if has_state:
        (q_ref, bf_ref, bb_ref, kff_ref, kfb_ref, i_ref, g_ref, nw_ref, s0_ref,
         o_ref, sn_ref, of_s, ob_s) = refs
    else:
        (q_ref, bf_ref, bb_ref, kff_ref, kfb_ref, i_ref, g_ref, nw_ref,
         o_ref, sn_ref, of_s, ob_s) = refs
    t_len = q_ref.shape[1]
    nc = t_len // HG_CHUNK
    unroll = min(HG_UNROLL, nc)
    head = pl.program_id(1)
    row = lax.broadcasted_iota(I32, (HG_CHUNK, HG_CHUNK), 0)
    col = lax.broadcasted_iota(I32, (HG_CHUNK, HG_CHUNK), 1)
    keep_f, keep_b = row >= col, row <= col

    tot_f = bf_ref[0, pl.ds(HG_CHUNK - 1, nc, stride=HG_CHUNK), :]
    tot_b = bb_ref[0, pl.ds(0, nc, stride=HG_CHUNK), :]
    worst = jnp.minimum(jnp.min(tot_f), jnp.min(tot_b))

    if has_state:
        st_f0, st_b0 = s0_ref[0, 0, 0].T, s0_ref[0, 1, 0].T
    else:
        st_f0 = st_b0 = jnp.zeros((HG_DV, HG_DK), F32)

    def scan(safe):
        def body(it, carry):
            st_f, st_b = carry
            rows_f, rows_b, loc_f, loc_b = [], [], [], []
            for u in range(unroll):
                c = it * unroll + u
                sf = pl.multiple_of(c * HG_CHUNK, HG_CHUNK)
                sb = pl.multiple_of((nc - 1 - c) * HG_CHUNK, HG_CHUNK)
                rf, rb = pl.ds(sf, HG_CHUNK), pl.ds(sb, HG_CHUNK)
                rows_f.append(rf)
                rows_b.append(rb)
                loc_f.append(_hgrn_chunk_local(
                    q_ref[0, rf, :].astype(F32), bf_ref[0, rf, :],
                    bf_ref[0, pl.ds(sf + HG_CHUNK - 1, 1), :],
                    kff_ref[0, rf, :].astype(F32), i_ref[0, rf, :], keep_f, safe))
                loc_b.append(_hgrn_chunk_local(
                    q_ref[0, rb, :].astype(F32), bb_ref[0, rb, :], bb_ref[0, pl.ds(sb, 1), :],
                    kfb_ref[0, rb, :].astype(F32), i_ref[0, rb, :], keep_b, safe))
            outs_f, st_f = _hgrn_scan_group(loc_f, st_f)
            outs_b, st_b = _hgrn_scan_group(loc_b, st_b)
            for rf, rb, o_f, o_b in zip(rows_f, rows_b, outs_f, outs_b):
                of_s[rf, :] = o_f
                ob_s[rb, :] = o_b
            return st_f, st_b
        return lax.fori_loop(0, nc // unroll, body, (st_f0, st_b0))

    st_f, st_b = lax.cond(worst >= -HG_SAFE_DECAY, lambda: scan(True), lambda: scan(False))
    sn_ref[0, 0, 0] = st_f.T
    sn_ref[0, 1, 0] = st_b.T
    o = of_s[...] + ob_s[...]
    nw = nw_ref[pl.ds(head, 1), :]
    on = o * lax.rsqrt(jnp.mean(o * o, axis=-1, keepdims=True) + EPS) * nw
    o_ref[0] = (on * g_ref[0].astype(F32)).astype(o_ref.dtype)


def _hgrn(q, bf, bb, kff, kfb, iv, g, norm_w, s0, batch):
    n = q.shape[0]
    t_len = n // batch
    shp = (batch, t_len, SEG)
    blk = pl.BlockSpec((1, t_len, HG_DK), lambda b, h: (b, 0, h))
    st_blk = pl.BlockSpec((1, 2, 1, HG_DK, HG_DV), lambda b, h: (b, 0, h, 0, 0))
    args = [a.reshape(shp) for a in (q, bf, bb, kff, kfb, iv, g)] + [norm_w]
    in_specs = [blk] * 7 + [pl.BlockSpec(norm_w.shape, lambda b, h: (0, 0))]
    if s0 is not None:
        args.append(s0)
        in_specs.append(st_blk)
    o, s_new = pl.pallas_call(
        functools.partial(_hgrn_kernel, has_state=s0 is not None),
        name=f"hgrn_n{n}", grid=(batch, HG_HEADS),
        in_specs=in_specs,
        out_specs=[blk, st_blk],
        out_shape=[jax.ShapeDtypeStruct(shp, BF16),
                   jax.ShapeDtypeStruct((batch, 2, HG_HEADS, HG_DK, HG_DV), F32)],
        scratch_shapes=[pltpu.VMEM((t_len, HG_DV), F32), pltpu.VMEM((t_len, HG_DV), F32)],
        compiler_params=_params(("parallel", "parallel")),
    )(*args)
    return o.reshape(n, SEG), s_new


def _attn_kernel(*refs, has_ctx):
    if has_ctx:
        q_ref, k_ref, v_ref, ck_ref, cv_ref, lam_ref, sw_ref, o_ref, k_s, vt_s = refs
    else:
        q_ref, k_ref, v_ref, lam_ref, sw_ref, o_ref, k_s, vt_s = refs
    t_own = k_ref.shape[1]

    @pl.when(pl.program_id(2) == 0)
    def _():
        k_s[0:t_own, :] = k_ref[0].astype(BF16)
        vt_s[0:DA_DV, 0:t_own] = v_ref[0].astype(F32).T.astype(BF16)
        if has_ctx:
            k_s[t_own:, :] = ck_ref[0].astype(BF16)
            vt_s[0:DA_DV, t_own:] = cv_ref[0].T.astype(BF16)
        vt_s[DA_DV:, :] = jnp.ones((ONES_ROWS, vt_s.shape[1]), BF16)

    lv = lam_ref[...]
    lam = (jnp.exp(jnp.sum(lv[0:1] * lv[1:2], keepdims=True))
           - jnp.exp(jnp.sum(lv[2:3] * lv[3:4], keepdims=True)) + LAM_INIT)
    tq = q_ref.shape[1]
    qt = q_ref[0].astype(F32).T
    dim = lax.broadcasted_iota(I32, qt.shape, 0)
    q_both = jnp.concatenate([jnp.where(dim < DA_DQK, qt, 0.0),
                              jnp.where(dim >= DA_DQK, qt, 0.0)], axis=1).astype(BF16)
    st = _dot(k_s[...], q_both)
    pt = jnp.exp2(st - jnp.max(st, axis=0, keepdims=True)).astype(BF16)
    r = _dot(vt_s[...], pt)
    r = r[0:DA_DV] * (1.0 / r[DA_DV:DA_DV + 1])
    o = (r[:, 0:tq] - lam * r[:, tq:2 * tq]).T
    on = o * lax.rsqrt(jnp.mean(o * o, axis=-1, keepdims=True) + EPS) * sw_ref[...]
    o_ref[0] = (on * (1.0 - LAM_INIT)).astype(o_ref.dtype)


def _attention(q, k, v, ctx_k, ctx_v, lam_p, subln_w, batch, tq):
    n = q.shape[0]
    t_len = n // batch
    shp = (batch, t_len, SEG)
    q_blk = pl.BlockSpec((1, tq, DA_DV), lambda b, h, i: (b, i, h))
    kv_blk = pl.BlockSpec((1, t_len, DA_DV), lambda b, h, i: (b, 0, h))
    const = lambda b, h, i: (0, 0)
    args = [q.reshape(shp), k.reshape(shp), v.reshape(shp)]
    in_specs = [q_blk, kv_blk, kv_blk]
    n_keys = t_len
    if ctx_k is not None:
        past = ctx_k.shape[1]
        n_keys += past
        c_blk = pl.BlockSpec((1, past, DA_DV), lambda b, h, i: (b, 0, h))
        args += [ctx_k, ctx_v]
        in_specs += [c_blk, c_blk]
    args += [lam_p, subln_w]
    in_specs += [pl.BlockSpec(lam_p.shape, const), pl.BlockSpec(subln_w.shape, const)]
    o = pl.pallas_call(
        functools.partial(_attn_kernel, has_ctx=ctx_k is not None),
        name=f"attn_n{n}", grid=(batch, DA_HEADS, t_len // tq),
        in_specs=in_specs, out_specs=q_blk,
        out_shape=jax.ShapeDtypeStruct(shp, BF16),
        scratch_shapes=[pltpu.VMEM((n_keys, 2 * DA_DQK), BF16),
                        pltpu.VMEM((DA_DV + ONES_ROWS, n_keys), BF16)],
        compiler_params=_params(("parallel", "parallel", "arbitrary")),
    )(*args)
    return o.reshape(n, SEG)


def _postmix_kernel(oh_ref, oa_ref, gate_ref, x_ref, mod_ref, wbh_ref, wba_ref, wout_ref,
                    n2_ref, rw_ref, x1_o, h2_o, afft_o, afftok_o):
    d = x_ref.shape[1]
    mod = mod_ref[0]
    g1, sh2, sc2 = mod[:, 2 * d:3 * d], mod[:, 3 * d:4 * d], mod[:, 4 * d:5 * d]
    g_h = gate_ref[:, 0:d].astype(F32)
    g_a = gate_ref[:, d:2 * d].astype(F32)
    merged = g_h * _dot(oh_ref[...], wbh_ref[...]) + g_a * _dot(oa_ref[...], wba_ref[...])
    x1 = x_ref[...] + g1 * _dot(merged.astype(BF16), wout_ref[...])
    x1_o[...] = x1
    xn = x1 * lax.rsqrt(jnp.mean(x1 * x1, axis=-1, keepdims=True) + EPS) * n2_ref[...]
    h2 = xn * (1.0 + sc2) + sh2
    h2_o[...] = h2.astype(h2_o.dtype)
    h_hi, h_lo = _split2(h2)
    rw = rw_ref[...]
    t1 = _dot_nt(rw, h_hi)
    t2 = _dot_nt(rw, h_lo)
    e = N_EXPERTS
    logits = t1[0:e] + t1[e:2 * e] + t2[0:e]
    mx = jnp.max(logits, axis=0, keepdims=True)
    p = jnp.exp(logits - mx)
    aff = p / jnp.sum(p, axis=0, keepdims=True)
    afft_o[...] = aff
    pad = jnp.zeros((LANES - e, aff.shape[1]), F32)
    afftok_o[...] = jnp.concatenate([aff, pad], axis=0).T


def _postmix(o_h, o_a, gates, x, mod3, mod_row, w_bh, w_ba, w_out, norm2_w, rw_cat, tm):
    n, d = x.shape
    const = lambda i: (0, 0)
    row = lambda i: (i, 0)
    return pl.pallas_call(
        _postmix_kernel,
        name=f"postmix_n{n}", grid=(n // tm,),
        in_specs=[pl.BlockSpec((tm, SEG), row), pl.BlockSpec((tm, SEG), row),
                  pl.BlockSpec((tm, 4 * SEG), row), pl.BlockSpec((tm, d), row),
                  pl.BlockSpec((1, 1, mod3.shape[2]), lambda i: (mod_row(i), 0, 0)),
                  pl.BlockSpec(w_bh.shape, const), pl.BlockSpec(w_ba.shape, const),
                  pl.BlockSpec(w_out.shape, const), pl.BlockSpec((1, d), const),
                  pl.BlockSpec(rw_cat.shape, const)],
        out_specs=[pl.BlockSpec((tm, d), row), pl.BlockSpec((tm, d), row),
                   pl.BlockSpec((N_EXPERTS, tm), lambda i: (0, i)),
                   pl.BlockSpec((tm, LANES), row)],
        out_shape=[jax.ShapeDtypeStruct((n, d), F32), jax.ShapeDtypeStruct((n, d), BF16),
                   jax.ShapeDtypeStruct((N_EXPERTS, n), F32),
                   jax.ShapeDtypeStruct((n, LANES), F32)],
        compiler_params=_params(("parallel",)),
    )(o_h, o_a, gates, x, mod3, w_bh, w_ba, w_out, norm2_w, rw_cat)


def _lane_cumsum_exclusive(x, blk):
    e, t = x.shape
    r = lax.broadcasted_iota(I32, (blk, blk), 0)
    c = lax.broadcasted_iota(I32, (blk, blk), 1)
    upper = (r < c).astype(BF16)
    carry = jnp.zeros((e, 1), F32)
    parts = []
    for j in range(t // blk):
        xb = x[:, j * blk:(j + 1) * blk]
        parts.append(_dot(xb.astype(BF16), upper) + carry)
        carry = carry + jnp.sum(xb, axis=1, keepdims=True)
    return parts[0] if len(parts) == 1 else jnp.concatenate(parts, axis=1)


def _route_kernel(aff_ref, pos_ref, *, cap, req_per_group):
    aff = aff_ref[...]
    bits = pltpu.bitcast(aff, I32)
    t_len = aff.shape[1]

    def count(mask):
        return jnp.sum(mask.astype(F32), axis=1, keepdims=True)

    def step(i, th):
        cand = th | (jnp.int32(1) << (30 - i))
        return jnp.where(count(bits >= cand) >= cap, cand, th)

    th = lax.fori_loop(0, 31, step, jnp.zeros((aff.shape[0], 1), I32))
    gt = bits > th
    eq = (bits == th).astype(F32)
    need = cap - count(gt)
    blk = min(t_len, 256)
    tie_rank = _lane_cumsum_exclusive(eq, blk)
    sel = jnp.where(gt, 1.0, jnp.where(tie_rank < need, eq, 0.0))
    slot = _lane_cumsum_exclusive(sel, blk)
    offset = (pl.program_id(0) % req_per_group) * cap
    pos_ref[...] = jnp.where(sel > 0.0, slot.astype(I32) + offset, -1)


def _route(afft, batch, cap, req_per_group):
    e, n = afft.shape
    t_len = n // batch
    blk = pl.BlockSpec((e, t_len), lambda b: (0, b))
    return pl.pallas_call(
        functools.partial(_route_kernel, cap=cap, req_per_group=req_per_group),
        name=f"route_n{n}", grid=(batch,), in_specs=[blk], out_specs=blk,
        out_shape=jax.ShapeDtypeStruct((e, n), I32),
        compiler_params=_params(("parallel",)),
    )(afft)


def _one_hot_rows(pos_row, n_slots):
    slot = lax.broadcasted_iota(I32, (n_slots, pos_row.shape[1]), 0)
    return (slot == pos_row).astype(BF16)


def _dispatch_kernel(pos_ref, h_ref, xg_ref):
    e = pl.program_id(1)
    p = _one_hot_rows(pos_ref[pl.ds(e, 1), :], xg_ref.shape[1])
    xg_ref[0] = _dot(p, h_ref[...]).astype(xg_ref.dtype)


def _dispatch(pos, h2, group_tokens, group_slots):
    e, n = pos.shape
    d = h2.shape[1]
    groups = n // group_tokens
    return pl.pallas_call(
        _dispatch_kernel,
        name=f"dispatch_n{n}", grid=(groups, e),
        in_specs=[pl.BlockSpec((e, group_tokens), lambda g, x: (0, g)),
                  pl.BlockSpec((group_tokens, d), lambda g, x: (g, 0))],
        out_specs=pl.BlockSpec((1, group_slots, d), lambda g, x: (x, g, 0)),
        out_shape=jax.ShapeDtypeStruct((e, groups * group_slots, d), BF16),
        compiler_params=_params(("parallel", "arbitrary")),
    )(pos, h2)


def _expert_kernel(xg_ref, wg_ref, wu_ref, wd_ref, y_ref):
    xg = xg_ref[0]
    a = _silu(_dot(xg, wg_ref[0])) * _dot(xg, wu_ref[0])
    y_ref[0] = _dot(a.astype(BF16), wd_ref[0]).astype(y_ref.dtype)


def _experts(xg, w_gate, w_up, w_down, ts):
    e, s, d = xg.shape
    f = w_gate.shape[2]
    x_blk = pl.BlockSpec((1, ts, d), lambda x, i: (x, i, 0))
    return pl.pallas_call(
        _expert_kernel,
        name=f"experts_s{s}", grid=(e, s // ts),
        in_specs=[x_blk, pl.BlockSpec((1, d, f), lambda x, i: (x, 0, 0)),
                  pl.BlockSpec((1, d, f), lambda x, i: (x, 0, 0)),
                  pl.BlockSpec((1, f, d), lambda x, i: (x, 0, 0))],
        out_specs=x_blk,
        out_shape=jax.ShapeDtypeStruct((e, s, d), BF16),
        compiler_params=_params(("parallel", "arbitrary")),
    )(xg, w_gate, w_up, w_down)


def _combine_kernel(pos_ref, afftok_ref, y_ref, x1_ref, mod_ref, o_ref):
    e = pl.program_id(2)
    d = x1_ref.shape[1]

    @pl.when(e == 0)
    def _():
        o_ref[...] = jnp.zeros_like(o_ref)

    p = _one_hot_rows(pos_ref[pl.ds(e, 1), :], y_ref.shape[1])
    lane = lax.broadcasted_iota(I32, afftok_ref.shape, 1)
    gate = jnp.sum(jnp.where(lane == e, afftok_ref[...], 0.0), axis=1, keepdims=True)
    o_ref[...] += gate * _dot_tn(p, y_ref[0])

    @pl.when(e == pl.num_programs(2) - 1)
    def _():
        g2 = mod_ref[0][:, 5 * d:6 * d]
        o_ref[...] = x1_ref[...] + g2 * o_ref[...]


def _combine(pos, afftok, y, x1, mod3, mod_row, group_tokens, group_slots, tc):
    e, n = pos.shape
    d = x1.shape[1]
    groups = n // group_tokens
    per = group_tokens // tc
    tok = lambda g, j, x: (g * per + j, 0)
    return pl.pallas_call(
        _combine_kernel,
        name=f"combine_n{n}", grid=(groups, per, e),
        in_specs=[pl.BlockSpec((e, tc), lambda g, j, x: (0, g * per + j)),
                  pl.BlockSpec((tc, LANES), tok),
                  pl.BlockSpec((1, group_slots, d), lambda g, j, x: (x, g, 0)),
                  pl.BlockSpec((tc, d), tok),
                  pl.BlockSpec((1, 1, mod3.shape[2]), lambda g, j, x: (mod_row(g), 0, 0))],
        out_specs=pl.BlockSpec((tc, d), tok),
        out_shape=jax.ShapeDtypeStruct((n, d), F32),
        compiler_params=_params(("parallel", "parallel", "arbitrary")),
    )(pos, afftok, y, x1, mod3)


def _rope_tables(t_len):
    n_freq = DA_DQK // 4
    inv = ROPE_BASE ** (-jnp.arange(n_freq, dtype=F32) / n_freq)
    t = jnp.arange(t_len)
    pos = jnp.stack([(t // GRID_W).astype(F32), (t % GRID_W).astype(F32)], axis=1)
    ang = pos[:, :, None, None] * inv[None, None, None, :]
    ang = jnp.broadcast_to(ang, (t_len, 2, 2, n_freq))
    sign = jnp.array([-1.0, 1.0], F32)[None, None, :, None]
    cos = jnp.cos(ang).reshape(t_len, DA_DQK)
    sin = (jnp.sin(ang) * sign).reshape(t_len, DA_DQK)
    reps = SEG // DA_DQK
    return jnp.tile(cos, (1, reps)), jnp.tile(sin, (1, reps))


def _trunk(x, batch, mod3, mod_row_tok, mod_row_grp, weights, ctx_k, ctx_v, s0, rope, group_tokens):
    (norm1_w, norm2_w, w_in, b_gate, lb_logits, hgrn_norm_w, qkw, gm, lam_p, subln_w,
     w_bh, w_ba, w_out, rw_cat, w_eg, w_eu, w_ed) = weights
    n, d = x.shape
    t_len = n // batch
    latent = rope is not None
    tm = 256
    (q_h, bf, bb, kff, kfb, i_h, g_h, dq, dk, dv, gates) = _premix(
        x, mod3, functools.partial(mod_row_tok, tm=tm), norm1_w, w_in, b_gate, lb_logits, qkw, gm,
        rope, tm, BF16 if latent else F32)
    o_h, s_new = _hgrn(q_h, bf, bb, kff, kfb, i_h, g_h, hgrn_norm_w, s0, batch)
    o_a = _attention(dq, dk, dv, ctx_k, ctx_v, lam_p, subln_w, batch, min(t_len, 256))
    x1, h2, afft, afftok = _postmix(o_h, o_a, gates, x, mod3, functools.partial(mod_row_tok, tm=tm),
                                    w_bh, w_ba, w_out, norm2_w, rw_cat, tm)
    cap = EC_CAPACITY * t_len // N_EXPERTS
    req_per_group = group_tokens // t_len
    group_slots = cap * req_per_group
    pos = _route(afft, batch, cap, req_per_group)
    xg = _dispatch(pos, h2, group_tokens, group_slots)
    y = _experts(xg, w_eg, w_eu, w_ed, group_slots)
    out = _combine(pos, afftok, y, x1, mod3, mod_row_grp, group_tokens, group_slots, 1024)
    return out, dk, dv, s_new


def kernel(x_prompt, x_sample, cache_k, cache_v, state_hgrn, c, c_ctx, norm1_w, norm2_w, w_mod,
           b_mod, w_in, b_gate, hgrn_lb_logits, hgrn_norm_w, qk_norm_w, diff_lambda, diff_subln_w,
           w_branch_hgrn, w_branch_attn, w_out, router_w, w_exp_gate, w_exp_up, w_exp_down):
    batch, seq, d = x_prompt.shape
    dec_batch, dec_seq, _ = x_sample.shape
    past = cache_k.shape[2]
    depth = w_in.shape[0]
    assert depth == 1
    group_tokens = dec_seq
    assert group_tokens % seq == 0 and (batch * seq) % group_tokens == 0
    l = 0

    rows = -(-(1 + dec_batch) // 8) * 8
    cond = jnp.zeros((rows, d), F32).at[0].set(c_ctx).at[1:1 + dec_batch].set(c)
    mod = _modulation(cond, w_mod[l], b_mod[l])
    mod3 = mod.reshape(rows, 1, 6 * d)

    gidx = jnp.arange(SEG) // DA_DQK
    gm = (gidx[:, None] == gidx[None, :]).astype(BF16) * (1.0 / DA_DQK)
    qkw = jnp.tile(qk_norm_w[l], (1, SEG // DA_DQK))
    rw_t = router_w[l].T
    rw_hi = rw_t.astype(BF16)
    rw_cat = jnp.concatenate([rw_hi, (rw_t - rw_hi.astype(F32)).astype(BF16)], axis=0)
    weights = (norm1_w[l][None], norm2_w[l][None], w_in[l].astype(BF16), b_gate[l][None],
               hgrn_lb_logits.reshape(4, SEG), hgrn_norm_w[l], qkw, gm, diff_lambda[l], diff_subln_w[l][None],
               w_branch_hgrn[l].astype(BF16), w_branch_attn[l].astype(BF16), w_out[l].astype(BF16),
               rw_cat, w_exp_gate[l].astype(BF16), w_exp_up[l].astype(BF16),
               w_exp_down[l].astype(BF16))

    yp, k_new, v_new, s_new = _trunk(
        x_prompt.reshape(batch * seq, d), batch, mod3,
        lambda i, tm: 0, lambda g: 0, weights, None, None, None, None, group_tokens)
    per_req = dec_seq
    ys, _, _, _ = _trunk(
        x_sample.reshape(dec_batch * dec_seq, d), dec_batch, mod3,
        lambda i, tm: 1 + (i * tm) // per_req, lambda g: 1 + g, weights,
        cache_k[:, l].reshape(dec_batch, past, SEG), cache_v[:, l].reshape(dec_batch, past, SEG),
        state_hgrn[:, l], _rope_tables(dec_seq), group_tokens)

    return (yp.reshape(batch, seq, d), ys.reshape(dec_batch, dec_seq, d),
            k_new.reshape(batch, 1, seq, DA_HEADS, 2, DA_DQK),
            v_new.reshape(batch, 1, seq, DA_HEADS, DA_DV),
            s_new.reshape(batch, 1, 2, HG_HEADS, HG_DK, HG_DV))
```

```python
import functools
import math

import jax
import jax.numpy as jnp
from jax import lax
from jax.experimental import pallas as pl
from jax.experimental.pallas import tpu as pltpu

F32 = jnp.float32
BF16 = jnp.bfloat16
I32 = jnp.int32

EPS = 1e-6
GRID_W = 64
HG_HEADS = 4
HG_DK = 128
HG_DV = 128
HG_CHUNK_LOG2 = 7
HG_CHUNK = 1 << HG_CHUNK_LOG2
HG_UNROLL = 4
DA_HEADS = 4
DA_DQK = 64
DA_DV = 128
N_EXPERTS = 16
EC_CAPACITY = 2
ROPE_BASE = 10000.0
SEG = 512
N_SEG = 12
LAM_INIT = 0.8 - 0.6 * math.exp(-0.3 * 0)
LANES = 128
ONES_ROWS = 16
ATT_SPLIT = 2
Q_SCALE = DA_DQK ** -0.5 * math.log2(math.e)
HG_SAFE_DECAY = 80.0
VMEM_LIMIT = 56 * 1024 * 1024


def _dot(a, b):
    return jnp.dot(a, b, preferred_element_type=F32)


def _dot_nt(a, b):
    return lax.dot_general(a, b, (((1,), (1,)), ((), ())), preferred_element_type=F32)


def _dot_tn(a, b):
    return lax.dot_general(a, b, (((0,), (0,)), ((), ())), preferred_element_type=F32)


def _split2(x):
    hi = x.astype(BF16)
    lo = (x - hi.astype(F32)).astype(BF16)
    return hi, lo


def _silu(x):
    return x * jax.nn.sigmoid(x)


def _params(sem):
    return pltpu.CompilerParams(dimension_semantics=sem, vmem_limit_bytes=VMEM_LIMIT)


def _mod_kernel(c_ref, w_ref, b_ref, o_ref):
    s_hi, s_lo = _split2(_silu(c_ref[...]))
    w_hi, w_lo = _split2(w_ref[...])
    o_ref[...] = _dot(s_hi, w_hi) + _dot(s_hi, w_lo) + _dot(s_lo, w_hi) + b_ref[...]


def _modulation(cond, w_mod, b_mod):
    rows, d = cond.shape
    n = w_mod.shape[1]
    bn = 512
    return pl.pallas_call(
        _mod_kernel,
        name="modulation", grid=(n // bn,),
        in_specs=[pl.BlockSpec((rows, d), lambda j: (0, 0)),
                  pl.BlockSpec((d, bn), lambda j: (0, j)),
                  pl.BlockSpec((1, bn), lambda j: (0, j))],
        out_specs=pl.BlockSpec((rows, bn), lambda j: (0, j)),
        out_shape=jax.ShapeDtypeStruct((rows, n), F32),
        compiler_params=_params(("arbitrary",)),
    )(cond, w_mod, b_mod.reshape(1, n))


def _group_rms(z, gm_ref, w):
    s_hi, s_lo = _split2(z * z)
    ms = _dot(s_hi, gm_ref[...]) + _dot(s_lo, gm_ref[...])
    return z * lax.rsqrt(ms + EPS) * w


def _rope(x, cos, sin_signed):
    n = x.shape[-1]
    lane = lax.broadcasted_iota(I32, x.shape, 1)
    partner = jnp.where((lane & 16) == 0, pltpu.roll(x, n - 16, 1), pltpu.roll(x, 16, 1))
    return x * cos + partner * sin_signed


def _premix_kernel(*refs, latent):
    if latent:
        (x_ref, mod_ref, n1_ref, win_ref, bg_ref, lbl_ref, qkw_ref, gm_ref, cos_ref, sin_ref,
         q_o, bf_o, bb_o, kff_o, kfb_o, i_o, g_o, dq_o, dk_o, dv_o, gate_o) = refs
    else:
        (x_ref, mod_ref, n1_ref, win_ref, bg_ref, lbl_ref, qkw_ref, gm_ref,
         q_o, bf_o, bb_o, kff_o, kfb_o, i_o, g_o, dq_o, dk_o, dv_o, gate_o) = refs
    d = x_ref.shape[1]
    mod = mod_ref[0]
    sh1, sc1 = mod[:, 0:d], mod[:, d:2 * d]
    x = x_ref[...]
    xn = x * lax.rsqrt(jnp.mean(x * x, axis=-1, keepdims=True) + EPS) * n1_ref[...]
    hb = (xn * (1.0 + sc1) + sh1).astype(BF16)

    def seg(j):
        return _dot(hb, win_ref[:, j * SEG:(j + 1) * SEG])

    def lower_bound(direction):
        l0 = lbl_ref[2 * direction:2 * direction + 1, :]
        l1 = lbl_ref[2 * direction + 1:2 * direction + 2, :]
        mx = jnp.maximum(l0, l1)
        e0, e1 = jnp.exp(l0 - mx), jnp.exp(l1 - mx)
        return e0 / (e0 + e1)

    tm = x.shape[0]
    row = lax.broadcasted_iota(I32, (tm, tm), 0)
    col = lax.broadcasted_iota(I32, (tm, tm), 1)
    same_chunk = (row >> HG_CHUNK_LOG2) == (col >> HG_CHUNK_LOG2)

    def store(o_ref, val):
        val = val.astype(o_ref.dtype)
        if len(o_ref.shape) == 2:
            o_ref[...] = val
        else:
            for h in range(o_ref.shape[0]):
                o_ref[h] = val[:, h * LANES:(h + 1) * LANES]

    store(q_o, _silu(seg(0)))
    for j, b_o, kf_o, order in ((1, bf_o, kff_o, row >= col), (2, bb_o, kfb_o, row <= col)):
        lbd = lower_bound(j - 1)
        f = lbd + (1.0 - lbd) * jax.nn.sigmoid(seg(j))
        tri = (same_chunk & order).astype(BF16)
        hi, lo = _split2(jnp.log(f))
        store(b_o, _dot(tri, hi) + _dot(tri, lo))
        store(kf_o, 1.0 - f)
    store(i_o, seg(3))
    store(g_o, _silu(seg(4)))
    qn = _group_rms(seg(5), gm_ref, qkw_ref[0:1, :]) * Q_SCALE
    kn = _group_rms(seg(6), gm_ref, qkw_ref[1:2, :])
    if latent:
        qn = _rope(qn, cos_ref[...], sin_ref[...])
        kn = _rope(kn, cos_ref[...], sin_ref[...])
    store(dq_o, qn)
    store(dk_o, kn)
    store(dv_o, seg(7))
    for j in range(4):
        z = seg(8 + j) + bg_ref[:, j * SEG:(j + 1) * SEG]
        gate_o[:, j * SEG:(j + 1) * SEG] = jax.nn.sigmoid(z).astype(gate_o.dtype)


def _premix(x, mod3, mod_row, norm1_w, w_in, b_gate, lb_logits, qkw, gm, rope, tm, kv_dtype,
            kv_head_major):
    n, d = x.shape
    latent = rope is not None
    const = lambda i: (0, 0)
    in_specs = [pl.BlockSpec((tm, d), lambda i: (i, 0)),
                pl.BlockSpec((1, 1, mod3.shape[2]), lambda i: (mod_row(i), 0, 0)),
                pl.BlockSpec((1, d), const),
                pl.BlockSpec(w_in.shape, const),
                pl.BlockSpec(b_gate.shape, const),
                pl.BlockSpec(lb_logits.shape, const),
                pl.BlockSpec(qkw.shape, const),
                pl.BlockSpec(gm.shape, const)]
    args = [x, mod3, norm1_w, w_in, b_gate, lb_logits, qkw, gm]
    if latent:
        cos, sin = rope
        nblk = cos.shape[0] // tm
        in_specs += [pl.BlockSpec((tm, SEG), lambda i: (i % nblk, 0))] * 2
        args += [cos, sin]
    heads = SEG // LANES
    head_spec = pl.BlockSpec((heads, tm, LANES), lambda i: (0, i, 0))
    tok_spec = pl.BlockSpec((tm, SEG), lambda i: (i, 0))
    kv_spec = head_spec if kv_head_major else tok_spec
    kv_shape = (heads, n, LANES) if kv_head_major else (n, SEG)
    out_dtypes = [BF16, F32, F32, BF16, BF16, BF16, BF16, BF16]
    out_shape = [jax.ShapeDtypeStruct((heads, n, LANES), t) for t in out_dtypes]
    out_shape += [jax.ShapeDtypeStruct(kv_shape, kv_dtype)] * 2
    out_shape.append(jax.ShapeDtypeStruct((n, 4 * SEG), BF16))
    out_specs = [head_spec] * 8 + [kv_spec] * 2 + [pl.BlockSpec((tm, 4 * SEG), lambda i: (i, 0))]
    return pl.pallas_call(
        functools.partial(_premix_kernel, latent=latent),
        name=f"premix_n{n}", grid=(n // tm,),
        in_specs=in_specs, out_specs=out_specs, out_shape=out_shape,
        compiler_params=_params(("parallel",)),
    )(*args)


def _hgrn_chunk_local(q, b, total, k, v, keep, safe):
    if safe:
        ref = b[HG_CHUNK // 2:HG_CHUNK // 2 + 1, :]
        qa = q * jnp.exp(b - ref)
        kb = k * jnp.exp(ref - b)
        attn = jnp.where(keep, _dot_nt(qa.astype(BF16), kb.astype(BF16)), 0.0)
        qe = qa * jnp.exp(ref)
        kd = kb * jnp.exp(total - ref)
    else:
        qe = q * jnp.exp(b)
        kd = k * jnp.exp(total - b)
        col = lax.broadcasted_iota(I32, (HG_CHUNK, HG_CHUNK), 1)

        def column(s, acc):
            onehot = (lax.broadcasted_iota(I32, (HG_CHUNK, 1), 0) == s).astype(F32)
            bs = jnp.sum(b * onehot, axis=0, keepdims=True)
            ks = jnp.sum(k * onehot, axis=0, keepdims=True)
            w = jnp.sum(q * ks * jnp.exp(jnp.minimum(b - bs, 0.0)), axis=1, keepdims=True)
            return jnp.where(col == s, w, acc)

        attn = lax.fori_loop(0, HG_CHUNK, column, jnp.zeros((HG_CHUNK, HG_CHUNK), F32))
        attn = jnp.where(keep, attn, 0.0)
    vt = v.astype(F32).T.astype(BF16)
    lhs = jnp.concatenate([qe.astype(BF16), attn.astype(BF16)], axis=1)
    return lhs, vt, _dot(vt, kd.astype(BF16)), jnp.exp(total)


def _hgrn_scan_group(chunks, st):
    outs = []
    for lhs, vt, inc, decay in chunks:
        outs.append(_dot_nt(lhs, jnp.concatenate([st.astype(BF16), vt], axis=1)))
        st = st * decay + inc
    return outs, st


def _hgrn_kernel(*refs, has_state):
    if has_state:
        (q_ref, bf_ref, bb_ref, kff_ref, kfb_ref, i_ref, g_ref, nw_ref, s0_ref,
         o_ref, sn_ref, of_s, ob_s) = refs
    else:
        (q_ref, bf_ref, bb_ref, kff_ref, kfb_ref, i_ref, g_ref, nw_ref,
         o_ref, sn_ref, of_s, ob_s) = refs
    t_len = q_ref.shape[1]
    nc = t_len // HG_CHUNK
    unroll = min(HG_UNROLL, nc)
    head = pl.program_id(1)
    row = lax.broadcasted_iota(I32, (HG_CHUNK, HG_CHUNK), 0)
    col = lax.broadcasted_iota(I32, (HG_CHUNK, HG_CHUNK), 1)
    keep_f, keep_b = row >= col, row <= col

    tot_f = bf_ref[0, pl.ds(HG_CHUNK - 1, nc, stride=HG_CHUNK), :]
    tot_b = bb_ref[0, pl.ds(0, nc, stride=HG_CHUNK), :]
    worst = jnp.minimum(jnp.min(tot_f), jnp.min(tot_b))

    if has_state:
        st_f0, st_b0 = s0_ref[0, 0, 0].T, s0_ref[0, 1, 0].T
    else:
        st_f0 = st_b0 = jnp.zeros((HG_DV, HG_DK), F32)

    def scan(safe):
        def body(it, carry):
            st_f, st_b = carry
            rows_f, rows_b, loc_f, loc_b = [], [], [], []
            for u in range(unroll):
                c = it * unroll + u
                sf = pl.multiple_of(c * HG_CHUNK, HG_CHUNK)
                sb = pl.multiple_of((nc - 1 - c) * HG_CHUNK, HG_CHUNK)
                rf, rb = pl.ds(sf, HG_CHUNK), pl.ds(sb, HG_CHUNK)
                rows_f.append(rf)
                rows_b.append(rb)
                loc_f.append(_hgrn_chunk_local(
                    q_ref[0, rf, :].astype(F32), bf_ref[0, rf, :],
                    bf_ref[0, pl.ds(sf + HG_CHUNK - 1, 1), :],
                    kff_ref[0, rf, :].astype(F32), i_ref[0, rf, :], keep_f, safe))
                loc_b.append(_hgrn_chunk_local(
                    q_ref[0, rb, :].astype(F32), bb_ref[0, rb, :], bb_ref[0, pl.ds(sb, 1), :],
                    kfb_ref[0, rb, :].astype(F32), i_ref[0, rb, :], keep_b, safe))
            outs_f, st_f = _hgrn_scan_group(loc_f, st_f)
            outs_b, st_b = _hgrn_scan_group(loc_b, st_b)
            for rf, rb, o_f, o_b in zip(rows_f, rows_b, outs_f, outs_b):
                of_s[rf, :] = o_f
                ob_s[rb, :] = o_b
            return st_f, st_b
        return lax.fori_loop(0, nc // unroll, body, (st_f0, st_b0))

    st_f, st_b = lax.cond(worst >= -HG_SAFE_DECAY, lambda: scan(True), lambda: scan(False))
    sn_ref[0, 0, 0] = st_f.T
    sn_ref[0, 1, 0] = st_b.T
    o = of_s[...] + ob_s[...]
    nw = nw_ref[pl.ds(head, 1), :]
    on = o * lax.rsqrt(jnp.mean(o * o, axis=-1, keepdims=True) + EPS) * nw
    o_ref[0] = (on * g_ref[0].astype(F32)).astype(o_ref.dtype)


def _hgrn(q, bf, bb, kff, kfb, iv, g, norm_w, s0, batch):
    heads, n, _ = q.shape
    t_len = n // batch
    blk = pl.BlockSpec((1, t_len, HG_DK), lambda b, h: (h, b, 0))
    st_blk = pl.BlockSpec((1, 2, 1, HG_DK, HG_DV), lambda b, h: (b, 0, h, 0, 0))
    args = [q, bf, bb, kff, kfb, iv, g, norm_w]
    in_specs = [blk] * 7 + [pl.BlockSpec(norm_w.shape, lambda b, h: (0, 0))]
    if s0 is not None:
        args.append(s0)
        in_specs.append(st_blk)
    o, s_new = pl.pallas_call(
        functools.partial(_hgrn_kernel, has_state=s0 is not None),
        name=f"hgrn_n{n}", grid=(batch, HG_HEADS),
        in_specs=in_specs,
        out_specs=[blk, st_blk],
        out_shape=[jax.ShapeDtypeStruct((heads, n, HG_DV), BF16),
                   jax.ShapeDtypeStruct((batch, 2, HG_HEADS, HG_DK, HG_DV), F32)],
        scratch_shapes=[pltpu.VMEM((t_len, HG_DV), F32), pltpu.VMEM((t_len, HG_DV), F32)],
        compiler_params=_params(("parallel", "parallel")),
    )(*args)
    return o, s_new


def _attn_kernel(*refs, has_ctx):
    if has_ctx:
        q_ref, k_ref, v_ref, ck_ref, cv_ref, lam_ref, sw_ref, o_ref, k_s, vt_s = refs
    else:
        q_ref, k_ref, v_ref, lam_ref, sw_ref, o_ref, k_s, vt_s = refs
    t_own = k_ref.shape[1]

    @pl.when(pl.program_id(2) == 0)
    def _():
        k_s[0:t_own, :] = k_ref[0].astype(BF16)
        vt_s[0:DA_DV, 0:t_own] = v_ref[0].astype(F32).T.astype(BF16)
        if has_ctx:
            k_s[t_own:, :] = ck_ref[0].astype(BF16)
            vt_s[0:DA_DV, t_own:] = cv_ref[0].T.astype(BF16)
        vt_s[DA_DV:, :] = jnp.ones((ONES_ROWS, vt_s.shape[1]), BF16)

    lv = lam_ref[...]
    lam = (jnp.exp(jnp.sum(lv[0:1] * lv[1:2], keepdims=True))
           - jnp.exp(jnp.sum(lv[2:3] * lv[3:4], keepdims=True)) + LAM_INIT)
    tq = q_ref.shape[1] // ATT_SPLIT
    dim = lax.broadcasted_iota(I32, (2 * DA_DQK, tq), 0)

    def scores(i):
        qt = q_ref[0, i * tq:(i + 1) * tq, :].astype(F32).T
        q_both = jnp.concatenate([jnp.where(dim < DA_DQK, qt, 0.0),
                                  jnp.where(dim >= DA_DQK, qt, 0.0)], axis=1).astype(BF16)
        return _dot(k_s[...], q_both)

    def finish(i, st):
        pt = jnp.exp2(st - jnp.max(st, axis=0, keepdims=True)).astype(BF16)
        r = _dot(vt_s[...], pt)
        r = r[0:DA_DV] * (1.0 / r[DA_DV:DA_DV + 1])
        o = (r[:, 0:tq] - lam * r[:, tq:2 * tq]).T
        on = o * lax.rsqrt(jnp.mean(o * o, axis=-1, keepdims=True) + EPS) * sw_ref[...]
        o_ref[0, i * tq:(i + 1) * tq, :] = (on * (1.0 - LAM_INIT)).astype(o_ref.dtype)

    sts = [scores(i) for i in range(ATT_SPLIT)]
    for i in range(ATT_SPLIT):
        finish(i, sts[i])


def _attention(q, k, v, ctx_k, ctx_v, lam_p, subln_w, batch, tq):
    heads, n, _ = q.shape
    t_len = n // batch
    nq = t_len // tq
    q_blk = pl.BlockSpec((1, tq, DA_DV), lambda b, h, i: (h, b * nq + i, 0))
    const = lambda b, h, i: (0, 0)
    if k.ndim == 3:
        kv_blk = pl.BlockSpec((1, t_len, DA_DV), lambda b, h, i: (h, b, 0))
        args = [q, k, v]
    else:
        kv_blk = pl.BlockSpec((1, t_len, DA_DV), lambda b, h, i: (b, 0, h))
        args = [q, k.reshape(batch, t_len, SEG), v.reshape(batch, t_len, SEG)]
    in_specs = [q_blk, kv_blk, kv_blk]
    n_keys = t_len
    if ctx_k is not None:
        past = ctx_k.shape[1]
        n_keys += past
        c_blk = pl.BlockSpec((1, past, DA_DV), lambda b, h, i: (b, 0, h))
        args += [ctx_k, ctx_v]
        in_specs += [c_blk, c_blk]
    args += [lam_p, subln_w]
    in_specs += [pl.BlockSpec(lam_p.shape, const), pl.BlockSpec(subln_w.shape, const)]
    o = pl.pallas_call(
        functools.partial(_attn_kernel, has_ctx=ctx_k is not None),
        name=f"attn_n{n}", grid=(batch, DA_HEADS, nq),
        in_specs=in_specs, out_specs=q_blk,
        out_shape=jax.ShapeDtypeStruct((heads, n, DA_DV), BF16),
        scratch_shapes=[pltpu.VMEM((n_keys, 2 * DA_DQK), BF16),
                        pltpu.VMEM((DA_DV + ONES_ROWS, n_keys), BF16)],
        compiler_params=_params(("parallel", "parallel", "arbitrary")),
    )(*args)
    return o


def _postmix_kernel(oh_ref, oa_ref, gate_ref, x_ref, mod_ref, wbh_ref, wba_ref, wout_ref,
                    n2_ref, rw_ref, x1_o, h2_o, afft_o, afftok_o):
    d = x_ref.shape[1]
    mod = mod_ref[0]
    g1, sh2, sc2 = mod[:, 2 * d:3 * d], mod[:, 3 * d:4 * d], mod[:, 4 * d:5 * d]
    g_h = gate_ref[:, 0:d].astype(F32)
    g_a = gate_ref[:, d:2 * d].astype(F32)

    def heads_on_lanes(ref):
        return jnp.concatenate([ref[h] for h in range(ref.shape[0])], axis=1)

    merged = (g_h * _dot(heads_on_lanes(oh_ref), wbh_ref[...])
              + g_a * _dot(heads_on_lanes(oa_ref), wba_ref[...]))
    x1 = x_ref[...] + g1 * _dot(merged.astype(BF16), wout_ref[...])
    x1_o[...] = x1
    xn = x1 * lax.rsqrt(jnp.mean(x1 * x1, axis=-1, keepdims=True) + EPS) * n2_ref[...]
    h2 = xn * (1.0 + sc2) + sh2
    h2_o[...] = h2.astype(h2_o.dtype)
    h_hi, h_lo = _split2(h2)
    rw = rw_ref[...]
    t1 = _dot_nt(rw, h_hi)
    t2 = _dot_nt(rw, h_lo)
    e = N_EXPERTS
    logits = t1[0:e] + t1[e:2 * e] + t2[0:e]
    mx = jnp.max(logits, axis=0, keepdims=True)
    p = jnp.exp(logits - mx)
    aff = p / jnp.sum(p, axis=0, keepdims=True)
    afft_o[...] = aff
    pad = jnp.zeros((LANES - e, aff.shape[1]), F32)
    afftok_o[...] = jnp.concatenate([aff, pad], axis=0).T


def _postmix(o_h, o_a, gates, x, mod3, mod_row, w_bh, w_ba, w_out, norm2_w, rw_cat, tm):
    n, d = x.shape
    const = lambda i: (0, 0)
    row = lambda i: (i, 0)
    return pl.pallas_call(
        _postmix_kernel,
        name=f"postmix_n{n}", grid=(n // tm,),
        in_specs=[pl.BlockSpec((o_h.shape[0], tm, LANES), lambda i: (0, i, 0)),
                  pl.BlockSpec((o_a.shape[0], tm, LANES), lambda i: (0, i, 0)),
                  pl.BlockSpec((tm, 4 * SEG), row), pl.BlockSpec((tm, d), row),
                  pl.BlockSpec((1, 1, mod3.shape[2]), lambda i: (mod_row(i), 0, 0)),
                  pl.BlockSpec(w_bh.shape, const), pl.BlockSpec(w_ba.shape, const),
                  pl.BlockSpec(w_out.shape, const), pl.BlockSpec((1, d), const),
                  pl.BlockSpec(rw_cat.shape, const)],
        out_specs=[pl.BlockSpec((tm, d), row), pl.BlockSpec((tm, d), row),
                   pl.BlockSpec((N_EXPERTS, tm), lambda i: (0, i)),
                   pl.BlockSpec((tm, LANES), row)],
        out_shape=[jax.ShapeDtypeStruct((n, d), F32), jax.ShapeDtypeStruct((n, d), BF16),
                   jax.ShapeDtypeStruct((N_EXPERTS, n), F32),
                   jax.ShapeDtypeStruct((n, LANES), F32)],
        compiler_params=_params(("parallel",)),
    )(o_h, o_a, gates, x, mod3, w_bh, w_ba, w_out, norm2_w, rw_cat)


def _lane_cumsum_exclusive(x, blk):
    e, t = x.shape
    r = lax.broadcasted_iota(I32, (blk, blk), 0)
    c = lax.broadcasted_iota(I32, (blk, blk), 1)
    upper = (r < c).astype(BF16)
    carry = jnp.zeros((e, 1), F32)
    parts = []
    for j in range(t // blk):
        xb = x[:, j * blk:(j + 1) * blk]
        parts.append(_dot(xb.astype(BF16), upper) + carry)
        carry = carry + jnp.sum(xb, axis=1, keepdims=True)
    return parts[0] if len(parts) == 1 else jnp.concatenate(parts, axis=1)


def _route_kernel(aff_ref, pos_ref, *, cap, req_per_group):
    aff = aff_ref[...]
    bits = pltpu.bitcast(aff, I32)
    t_len = aff.shape[1]

    def count(mask):
        return jnp.sum(mask.astype(F32), axis=1, keepdims=True)

    def step(i, th):
        cand = th | (jnp.int32(1) << (30 - i))
        return jnp.where(count(bits >= cand) >= cap, cand, th)

    th = lax.fori_loop(0, 31, step, jnp.zeros((aff.shape[0], 1), I32))
    gt = bits > th
    eq = (bits == th).astype(F32)
    need = cap - count(gt)
    blk = min(t_len, 256)
    tie_rank = _lane_cumsum_exclusive(eq, blk)
    sel = jnp.where(gt, 1.0, jnp.where(tie_rank < need, eq, 0.0))
    slot = _lane_cumsum_exclusive(sel, blk)
    offset = (pl.program_id(0) % req_per_group) * cap
    pos_ref[...] = jnp.where(sel > 0.0, slot.astype(I32) + offset, -1)


def _route(afft, batch, cap, req_per_group):
    e, n = afft.shape
    t_len = n // batch
    blk = pl.BlockSpec((e, t_len), lambda b: (0, b))
    return pl.pallas_call(
        functools.partial(_route_kernel, cap=cap, req_per_group=req_per_group),
        name=f"route_n{n}", grid=(batch,), in_specs=[blk], out_specs=blk,
        out_shape=jax.ShapeDtypeStruct((e, n), I32),
        compiler_params=_params(("parallel",)),
    )(afft)


def _one_hot_rows(pos_row, n_slots):
    slot = lax.broadcasted_iota(I32, (n_slots, pos_row.shape[1]), 0)
    return (slot == pos_row).astype(BF16)


def _dispatch_kernel(pos_ref, h_ref, xg_ref):
    e = pl.program_id(1)
    p = _one_hot_rows(pos_ref[pl.ds(e, 1), :], xg_ref.shape[1])
    xg_ref[0] = _dot(p, h_ref[...]).astype(xg_ref.dtype)


def _dispatch(pos, h2, group_tokens, group_slots):
    e, n = pos.shape
    d = h2.shape[1]
    groups = n // group_tokens
    return pl.pallas_call(
        _dispatch_kernel,
        name=f"dispatch_n{n}", grid=(groups, e),
        in_specs=[pl.BlockSpec((e, group_tokens), lambda g, x: (0, g)),
                  pl.BlockSpec((group_tokens, d), lambda g, x: (g, 0))],
        out_specs=pl.BlockSpec((1, group_slots, d), lambda g, x: (x, g, 0)),
        out_shape=jax.ShapeDtypeStruct((e, groups * group_slots, d), BF16),
        compiler_params=_params(("parallel", "arbitrary")),
    )(pos, h2)


def _expert_kernel(xg_ref, wg_ref, wu_ref, wd_ref, y_ref, wg_s, wu_s, wd_s):
    @pl.when(pl.program_id(1) == 0)
    def _():
        wg_s[...] = wg_ref[0].astype(BF16)
        wu_s[...] = wu_ref[0].astype(BF16)
        wd_s[...] = wd_ref[0].astype(BF16)

    xg = xg_ref[0]
    a = _silu(_dot(xg, wg_s[...])) * _dot(xg, wu_s[...])
    y_ref[0] = _dot(a.astype(BF16), wd_s[...]).astype(y_ref.dtype)


def _experts(xg, w_gate, w_up, w_down, ts):
    e, s, d = xg.shape
    f = w_gate.shape[2]
    x_blk = pl.BlockSpec((1, ts, d), lambda x, i: (x, i, 0))
    return pl.pallas_call(
        _expert_kernel,
        name=f"experts_s{s}", grid=(e, s // ts),
        in_specs=[x_blk, pl.BlockSpec((1, d, f), lambda x, i: (x, 0, 0)),
                  pl.BlockSpec((1, d, f), lambda x, i: (x, 0, 0)),
                  pl.BlockSpec((1, f, d), lambda x, i: (x, 0, 0))],
        out_specs=x_blk,
        out_shape=jax.ShapeDtypeStruct((e, s, d), BF16),
        scratch_shapes=[pltpu.VMEM((d, f), BF16), pltpu.VMEM((d, f), BF16),
                        pltpu.VMEM((f, d), BF16)],
        compiler_params=_params(("parallel", "arbitrary")),
    )(xg, w_gate, w_up, w_down)


def _combine_kernel(pos_ref, afftok_ref, y_ref, x1_ref, mod_ref, o_ref):
    e = pl.program_id(2)
    d = x1_ref.shape[1]

    @pl.when(e == 0)
    def _():
        o_ref[...] = jnp.zeros_like(o_ref)

    p = _one_hot_rows(pos_ref[pl.ds(e, 1), :], y_ref.shape[1])
    lane = lax.broadcasted_iota(I32, afftok_ref.shape, 1)
    gate = jnp.sum(jnp.where(lane == e, afftok_ref[...], 0.0), axis=1, keepdims=True)
    o_ref[...] += gate * _dot_tn(p, y_ref[0])

    @pl.when(e == pl.num_programs(2) - 1)
    def _():
        g2 = mod_ref[0][:, 5 * d:6 * d]
        o_ref[...] = x1_ref[...] + g2 * o_ref[...]


def _combine(pos, afftok, y, x1, mod3, mod_row, group_tokens, group_slots, tc):
    e, n = pos.shape
    d = x1.shape[1]
    groups = n // group_tokens
    per = group_tokens // tc
    tok = lambda g, j, x: (g * per + j, 0)
    return pl.pallas_call(
        _combine_kernel,
        name=f"combine_n{n}", grid=(groups, per, e),
        in_specs=[pl.BlockSpec((e, tc), lambda g, j, x: (0, g * per + j)),
                  pl.BlockSpec((tc, LANES), tok),
                  pl.BlockSpec((1, group_slots, d), lambda g, j, x: (x, g, 0)),
                  pl.BlockSpec((tc, d), tok),
                  pl.BlockSpec((1, 1, mod3.shape[2]), lambda g, j, x: (mod_row(g), 0, 0))],
        out_specs=pl.BlockSpec((tc, d), tok),
        out_shape=jax.ShapeDtypeStruct((n, d), F32),
        compiler_params=_params(("parallel", "parallel", "arbitrary")),
    )(pos, afftok, y, x1, mod3)


def _rope_tables(t_len):
    n_freq = DA_DQK // 4
    inv = ROPE_BASE ** (-jnp.arange(n_freq, dtype=F32) / n_freq)
    t = jnp.arange(t_len)
    pos = jnp.stack([(t // GRID_W).astype(F32), (t % GRID_W).astype(F32)], axis=1)
    ang = pos[:, :, None, None] * inv[None, None, None, :]
    ang = jnp.broadcast_to(ang, (t_len, 2, 2, n_freq))
    sign = jnp.array([-1.0, 1.0], F32)[None, None, :, None]
    cos = jnp.cos(ang).reshape(t_len, DA_DQK)
    sin = (jnp.sin(ang) * sign).reshape(t_len, DA_DQK)
    reps = SEG // DA_DQK
    return jnp.tile(cos, (1, reps)), jnp.tile(sin, (1, reps))


def _trunk(x, batch, mod3, mod_row_tok, mod_row_grp, weights, ctx_k, ctx_v, s0, rope, group_tokens):
    (norm1_w, norm2_w, w_in, b_gate, lb_logits, hgrn_norm_w, qkw, gm, lam_p, subln_w,
     w_bh, w_ba, w_out, rw_cat, w_eg, w_eu, w_ed) = weights
    n, d = x.shape
    t_len = n // batch
    latent = rope is not None
    tm = 256
    (q_h, bf, bb, kff, kfb, i_h, g_h, dq, dk, dv, gates) = _premix(
        x, mod3, functools.partial(mod_row_tok, tm=tm), norm1_w, w_in, b_gate, lb_logits, qkw, gm,
        rope, tm, BF16 if latent else F32, latent)
    o_h, s_new = _hgrn(q_h, bf, bb, kff, kfb, i_h, g_h, hgrn_norm_w, s0, batch)
    o_a = _attention(dq, dk, dv, ctx_k, ctx_v, lam_p, subln_w, batch, min(t_len, 256 * ATT_SPLIT))
    x1, h2, afft, afftok = _postmix(o_h, o_a, gates, x, mod3, functools.partial(mod_row_tok, tm=tm),
                                    w_bh, w_ba, w_out, norm2_w, rw_cat, tm)
    cap = EC_CAPACITY * t_len // N_EXPERTS
    req_per_group = group_tokens // t_len
    group_slots = cap * req_per_group
    pos = _route(afft, batch, cap, req_per_group)
    xg = _dispatch(pos, h2, group_tokens, group_slots)
    y = _experts(xg, w_eg, w_eu, w_ed, group_slots)
    out = _combine(pos, afftok, y, x1, mod3, mod_row_grp, group_tokens, group_slots, 1024)
    return out, dk, dv, s_new


def kernel(x_prompt, x_sample, cache_k, cache_v, state_hgrn, c, c_ctx, norm1_w, norm2_w, w_mod,
           b_mod, w_in, b_gate, hgrn_lb_logits, hgrn_norm_w, qk_norm_w, diff_lambda, diff_subln_w,
           w_branch_hgrn, w_branch_attn, w_out, router_w, w_exp_gate, w_exp_up, w_exp_down):
    batch, seq, d = x_prompt.shape
    dec_batch, dec_seq, _ = x_sample.shape
    past = cache_k.shape[2]
    depth = w_in.shape[0]
    assert depth == 1
    group_tokens = dec_seq
    assert group_tokens % seq == 0 and (batch * seq) % group_tokens == 0
    l = 0

    rows = -(-(1 + dec_batch) // 8) * 8
    cond = jnp.zeros((rows, d), F32).at[0].set(c_ctx).at[1:1 + dec_batch].set(c)
    mod = _modulation(cond, w_mod[l], b_mod[l])
    mod3 = mod.reshape(rows, 1, 6 * d)

    gidx = jnp.arange(SEG) // DA_DQK
    gm = (gidx[:, None] == gidx[None, :]).astype(BF16) * (1.0 / DA_DQK)
    qkw = jnp.tile(qk_norm_w[l], (1, SEG // DA_DQK))
    rw_t = router_w[l].T
    rw_hi = rw_t.astype(BF16)
    rw_cat = jnp.concatenate([rw_hi, (rw_t - rw_hi.astype(F32)).astype(BF16)], axis=0)
    weights = (norm1_w[l][None], norm2_w[l][None], w_in[l].astype(BF16), b_gate[l][None],
               hgrn_lb_logits.reshape(4, SEG), hgrn_norm_w[l], qkw, gm, diff_lambda[l],
               diff_subln_w[l][None], w_branch_hgrn[l].astype(BF16),
               w_branch_attn[l].astype(BF16), w_out[l].astype(BF16),
               rw_cat, w_exp_gate[l], w_exp_up[l], w_exp_down[l])

    yp, k_new, v_new, s_new = _trunk(
        x_prompt.reshape(batch * seq, d), batch, mod3,
        lambda i, tm: 0, lambda g: 0, weights, None, None, None, None, group_tokens)
    per_req = dec_seq
    ys, _, _, _ = _trunk(
        x_sample.reshape(dec_batch * dec_seq, d), dec_batch, mod3,
        lambda i, tm: 1 + (i * tm) // per_req, lambda g: 1 + g, weights,
        cache_k[:, l].reshape(dec_batch, past, SEG), cache_v[:, l].reshape(dec_batch, past, SEG),
        state_hgrn[:, l], _rope_tables(dec_seq), group_tokens)

    return (yp.reshape(batch, seq, d), ys.reshape(dec_batch, dec_seq, d),
            k_new.reshape(batch, 1, seq, DA_HEADS, 2, DA_DQK),
            v_new.reshape(batch, 1, seq, DA_HEADS, DA_DV),
            s_new.reshape(batch, 1, 2, HG_HEADS, HG_DK, HG_DV))
```

```python
import functools
import math

import jax
import jax.numpy as jnp
from jax import lax
from jax.experimental import pallas as pl
from jax.experimental.pallas import tpu as pltpu

F32 = jnp.float32
BF16 = jnp.bfloat16
I32 = jnp.int32

EPS = 1e-6
GRID_W = 64
HG_HEADS = 4
HG_DK = 128
HG_DV = 128
HG_CHUNK_LOG2 = 7
HG_CHUNK = 1 << HG_CHUNK_LOG2
HG_UNROLL = 4
DA_HEADS = 4
DA_DQK = 64
DA_DV = 128
N_EXPERTS = 16
EC_CAPACITY = 2
ROPE_BASE = 10000.0
SEG = 512
N_SEG = 12
LAM_INIT = 0.8 - 0.6 * math.exp(-0.3 * 0)
LANES = 128
ONES_ROWS = 16
ATT_SPLIT = 2
Q_SCALE = DA_DQK ** -0.5 * math.log2(math.e)
HG_SAFE_DECAY = 80.0
VMEM_LIMIT = 56 * 1024 * 1024


def _dot(a, b):
    return jnp.dot(a, b, preferred_element_type=F32)


def _dot_nt(a, b):
    return lax.dot_general(a, b, (((1,), (1,)), ((), ())), preferred_element_type=F32)


def _dot_tn(a, b):
    return lax.dot_general(a, b, (((0,), (0,)), ((), ())), preferred_element_type=F32)


def _split2(x):
    hi = x.astype(BF16)
    lo = (x - hi.astype(F32)).astype(BF16)
    return hi, lo


def _silu(x):
    return x * jax.nn.sigmoid(x)


def _params(sem):
    return pltpu.CompilerParams(dimension_semantics=sem, vmem_limit_bytes=VMEM_LIMIT)


def _mod_kernel(c_ref, w_ref, b_ref, o_ref):
    s_hi, s_lo = _split2(_silu(c_ref[...]))
    w_hi, w_lo = _split2(w_ref[...])
    o_ref[...] = _dot(s_hi, w_hi) + _dot(s_hi, w_lo) + _dot(s_lo, w_hi) + b_ref[...]


def _modulation(cond, w_mod, b_mod):
    rows, d = cond.shape
    n = w_mod.shape[1]
    bn = 512
    return pl.pallas_call(
        _mod_kernel,
        name="modulation", grid=(n // bn,),
        in_specs=[pl.BlockSpec((rows, d), lambda j: (0, 0)),
                  pl.BlockSpec((d, bn), lambda j: (0, j)),
                  pl.BlockSpec((1, bn), lambda j: (0, j))],
        out_specs=pl.BlockSpec((rows, bn), lambda j: (0, j)),
        out_shape=jax.ShapeDtypeStruct((rows, n), F32),
        compiler_params=_params(("arbitrary",)),
    )(cond, w_mod, b_mod.reshape(1, n))


def _group_rms(z, gm_ref, w):
    ms = _dot((z * z).astype(BF16), gm_ref[...])
    return z * lax.rsqrt(ms + EPS) * w


def _rope(x, cos, sin_signed):
    n = x.shape[-1]
    lane = lax.broadcasted_iota(I32, x.shape, 1)
    partner = jnp.where((lane & 16) == 0, pltpu.roll(x, n - 16, 1), pltpu.roll(x, 16, 1))
    return x * cos + partner * sin_signed


def _premix_kernel(*refs, latent):
    if latent:
        (x_ref, mod_ref, n1_ref, win_ref, bg_ref, lbl_ref, qkw_ref, gm_ref, cos_ref, sin_ref,
         q_o, bf_o, bb_o, kff_o, kfb_o, i_o, g_o, dq_o, dk_o, dv_o, gate_o) = refs
    else:
        (x_ref, mod_ref, n1_ref, win_ref, bg_ref, lbl_ref, qkw_ref, gm_ref,
         q_o, bf_o, bb_o, kff_o, kfb_o, i_o, g_o, dq_o, dk_o, dv_o, gate_o) = refs
    d = x_ref.shape[1]
    mod = mod_ref[0]
    sh1, sc1 = mod[:, 0:d], mod[:, d:2 * d]
    x = x_ref[...]
    xn = x * lax.rsqrt(jnp.mean(x * x, axis=-1, keepdims=True) + EPS) * n1_ref[...]
    hb = (xn * (1.0 + sc1) + sh1).astype(BF16)

    def seg(j):
        return _dot(hb, win_ref[:, j * SEG:(j + 1) * SEG])

    def lower_bound(direction):
        l0 = lbl_ref[2 * direction:2 * direction + 1, :]
        l1 = lbl_ref[2 * direction + 1:2 * direction + 2, :]
        mx = jnp.maximum(l0, l1)
        e0, e1 = jnp.exp(l0 - mx), jnp.exp(l1 - mx)
        return e0 / (e0 + e1)

    tm = x.shape[0]
    row = lax.broadcasted_iota(I32, (tm, tm), 0)
    col = lax.broadcasted_iota(I32, (tm, tm), 1)
    same_chunk = (row >> HG_CHUNK_LOG2) == (col >> HG_CHUNK_LOG2)

    def store(o_ref, val):
        val = val.astype(o_ref.dtype)
        if len(o_ref.shape) == 2:
            o_ref[...] = val
        else:
            for h in range(o_ref.shape[0]):
                o_ref[h] = val[:, h * LANES:(h + 1) * LANES]

    store(q_o, _silu(seg(0)))
    for j, b_o, kf_o, order in ((1, bf_o, kff_o, row >= col), (2, bb_o, kfb_o, row <= col)):
        lbd = lower_bound(j - 1)
        f = lbd + (1.0 - lbd) * jax.nn.sigmoid(seg(j))
        tri = (same_chunk & order).astype(BF16)
        hi, lo = _split2(jnp.log(f))
        store(b_o, _dot(tri, hi) + _dot(tri, lo))
        store(kf_o, 1.0 - f)
    store(i_o, seg(3))
    store(g_o, _silu(seg(4)))
    qn = _group_rms(seg(5), gm_ref, qkw_ref[0:1, :]) * Q_SCALE
    kn = _group_rms(seg(6), gm_ref, qkw_ref[1:2, :])
    if latent:
        qn = _rope(qn, cos_ref[...], sin_ref[...])
        kn = _rope(kn, cos_ref[...], sin_ref[...])
    store(dq_o, qn)
    store(dk_o, kn)
    store(dv_o, seg(7))
    for j in range(4):
        z = seg(8 + j) + bg_ref[:, j * SEG:(j + 1) * SEG]
        gate_o[:, j * SEG:(j + 1) * SEG] = jax.nn.sigmoid(z).astype(gate_o.dtype)


def _premix(x, mod3, mod_row, norm1_w, w_in, b_gate, lb_logits, qkw, gm, rope, tm, kv_dtype,
            kv_head_major):
    n, d = x.shape
    latent = rope is not None
    const = lambda i: (0, 0)
    in_specs = [pl.BlockSpec((tm, d), lambda i: (i, 0)),
                pl.BlockSpec((1, 1, mod3.shape[2]), lambda i: (mod_row(i), 0, 0)),
                pl.BlockSpec((1, d), const),
                pl.BlockSpec(w_in.shape, const),
                pl.BlockSpec(b_gate.shape, const),
                pl.BlockSpec(lb_logits.shape, const),
                pl.BlockSpec(qkw.shape, const),
                pl.BlockSpec(gm.shape, const)]
    args = [x, mod3, norm1_w, w_in, b_gate, lb_logits, qkw, gm]
    if latent:
        cos, sin = rope
        nblk = cos.shape[0] // tm
        in_specs += [pl.BlockSpec((tm, SEG), lambda i: (i % nblk, 0))] * 2
        args += [cos, sin]
    heads = SEG // LANES
    head_spec = pl.BlockSpec((heads, tm, LANES), lambda i: (0, i, 0))
    tok_spec = pl.BlockSpec((tm, SEG), lambda i: (i, 0))
    kv_spec = head_spec if kv_head_major else tok_spec
    kv_shape = (heads, n, LANES) if kv_head_major else (n, SEG)
    out_dtypes = [BF16, F32, F32, BF16, BF16, BF16, BF16, BF16]
    out_shape = [jax.ShapeDtypeStruct((heads, n, LANES), t) for t in out_dtypes]
    out_shape += [jax.ShapeDtypeStruct(kv_shape, kv_dtype)] * 2
    out_shape.append(jax.ShapeDtypeStruct((n, 4 * SEG), BF16))
    out_specs = [head_spec] * 8 + [kv_spec] * 2 + [pl.BlockSpec((tm, 4 * SEG), lambda i: (i, 0))]
    return pl.pallas_call(
        functools.partial(_premix_kernel, latent=latent),
        name=f"premix_n{n}", grid=(n // tm,),
        in_specs=in_specs, out_specs=out_specs, out_shape=out_shape,
        compiler_params=_params(("parallel",)),
    )(*args)


def _hgrn_chunk_local(q, b, total, k, v, keep, safe):
    if safe:
        ref = b[HG_CHUNK // 2:HG_CHUNK // 2 + 1, :]
        qa = q * jnp.exp(b - ref)
        kb = k * jnp.exp(ref - b)
        attn = jnp.where(keep, _dot_nt(qa.astype(BF16), kb.astype(BF16)), 0.0)
        qe = qa * jnp.exp(ref)
        kd = kb * jnp.exp(total - ref)
    else:
        qe = q * jnp.exp(b)
        kd = k * jnp.exp(total - b)
        col = lax.broadcasted_iota(I32, (HG_CHUNK, HG_CHUNK), 1)

        def column(s, acc):
            onehot = (lax.broadcasted_iota(I32, (HG_CHUNK, 1), 0) == s).astype(F32)
            bs = jnp.sum(b * onehot, axis=0, keepdims=True)
            ks = jnp.sum(k * onehot, axis=0, keepdims=True)
            w = jnp.sum(q * ks * jnp.exp(jnp.minimum(b - bs, 0.0)), axis=1, keepdims=True)
            return jnp.where(col == s, w, acc)

        attn = lax.fori_loop(0, HG_CHUNK, column, jnp.zeros((HG_CHUNK, HG_CHUNK), F32))
        attn = jnp.where(keep, attn, 0.0)
    vt = v.astype(F32).T.astype(BF16)
    lhs = jnp.concatenate([qe.astype(BF16), attn.astype(BF16)], axis=1)
    return lhs, vt, _dot(vt, kd.astype(BF16)), jnp.exp(total)


def _hgrn_scan_group(chunks, st):
    outs = []
    for lhs, vt, inc, decay in chunks:
        outs.append(_dot_nt(lhs, jnp.concatenate([st.astype(BF16), vt], axis=1)))
        st = st * decay + inc
    return outs, st


def _hgrn_kernel(*refs, has_state):
    if has_state:
        (q_ref, bf_ref, bb_ref, kff_ref, kfb_ref, i_ref, g_ref, nw_ref, s0_ref,
         o_ref, sn_ref, of_s, ob_s) = refs
    else:
        (q_ref, bf_ref, bb_ref, kff_ref, kfb_ref, i_ref, g_ref, nw_ref,
         o_ref, sn_ref, of_s, ob_s) = refs
    t_len = q_ref.shape[1]
    nc = t_len // HG_CHUNK
    unroll = min(HG_UNROLL, nc)
    head = pl.program_id(1)
    row = lax.broadcasted_iota(I32, (HG_CHUNK, HG_CHUNK), 0)
    col = lax.broadcasted_iota(I32, (HG_CHUNK, HG_CHUNK), 1)
    keep_f, keep_b = row >= col, row <= col

    mid_f = bf_ref[0, pl.ds(HG_CHUNK // 2, nc, stride=HG_CHUNK), :]
    tot_f = bf_ref[0, pl.ds(HG_CHUNK - 1, nc, stride=HG_CHUNK), :]
    mid_b = bb_ref[0, pl.ds(HG_CHUNK // 2, nc, stride=HG_CHUNK), :]
    tot_b = bb_ref[0, pl.ds(0, nc, stride=HG_CHUNK), :]
    worst = jnp.minimum(jnp.min(jnp.minimum(mid_f, tot_f - mid_f)),
                        jnp.min(jnp.minimum(mid_b, tot_b - mid_b)))

    if has_state:
        st_f0, st_b0 = s0_ref[0, 0, 0].T, s0_ref[0, 1, 0].T
    else:
        st_f0 = st_b0 = jnp.zeros((HG_DV, HG_DK), F32)

    def scan(safe):
        def body(it, carry):
            st_f, st_b = carry
            rows_f, rows_b, loc_f, loc_b = [], [], [], []
            for u in range(unroll):
                c = it * unroll + u
                sf = pl.multiple_of(c * HG_CHUNK, HG_CHUNK)
                sb = pl.multiple_of((nc - 1 - c) * HG_CHUNK, HG_CHUNK)
                rf, rb = pl.ds(sf, HG_CHUNK), pl.ds(sb, HG_CHUNK)
                rows_f.append(rf)
                rows_b.append(rb)
                loc_f.append(_hgrn_chunk_local(
                    q_ref[0, rf, :].astype(F32), bf_ref[0, rf, :],
                    bf_ref[0, pl.ds(sf + HG_CHUNK - 1, 1), :],
                    kff_ref[0, rf, :].astype(F32), i_ref[0, rf, :], keep_f, safe))
                loc_b.append(_hgrn_chunk_local(
                    q_ref[0, rb, :].astype(F32), bb_ref[0, rb, :], bb_ref[0, pl.ds(sb, 1), :],
                    kfb_ref[0, rb, :].astype(F32), i_ref[0, rb, :], keep_b, safe))
            outs_f, st_f = _hgrn_scan_group(loc_f, st_f)
            outs_b, st_b = _hgrn_scan_group(loc_b, st_b)
            for rf, rb, o_f, o_b in zip(rows_f, rows_b, outs_f, outs_b):
                of_s[rf, :] = o_f
                ob_s[rb, :] = o_b
            return st_f, st_b
        return lax.fori_loop(0, nc // unroll, body, (st_f0, st_b0))

    st_f, st_b = lax.cond(worst >= -HG_SAFE_DECAY, lambda: scan(True), lambda: scan(False))
    sn_ref[0, 0, 0] = st_f.T
    sn_ref[0, 1, 0] = st_b.T
    o = of_s[...] + ob_s[...]
    nw = nw_ref[pl.ds(head, 1), :]
    on = o * lax.rsqrt(jnp.mean(o * o, axis=-1, keepdims=True) + EPS) * nw
    o_ref[0] = (on * g_ref[0].astype(F32)).astype(o_ref.dtype)


def _hgrn(q, bf, bb, kff, kfb, iv, g, norm_w, s0, batch):
    heads, n, _ = q.shape
    t_len = n // batch
    blk = pl.BlockSpec((1, t_len, HG_DK), lambda b, h: (h, b, 0))
    st_blk = pl.BlockSpec((1, 2, 1, HG_DK, HG_DV), lambda b, h: (b, 0, h, 0, 0))
    args = [q, bf, bb, kff, kfb, iv, g, norm_w]
    in_specs = [blk] * 7 + [pl.BlockSpec(norm_w.shape, lambda b, h: (0, 0))]
    if s0 is not None:
        args.append(s0)
        in_specs.append(st_blk)
    o, s_new = pl.pallas_call(
        functools.partial(_hgrn_kernel, has_state=s0 is not None),
        name=f"hgrn_n{n}", grid=(batch, HG_HEADS),
        in_specs=in_specs,
        out_specs=[blk, st_blk],
        out_shape=[jax.ShapeDtypeStruct((heads, n, HG_DV), BF16),
                   jax.ShapeDtypeStruct((batch, 2, HG_HEADS, HG_DK, HG_DV), F32)],
        scratch_shapes=[pltpu.VMEM((t_len, HG_DV), F32), pltpu.VMEM((t_len, HG_DV), F32)],
        compiler_params=_params(("parallel", "parallel")),
    )(*args)
    return o, s_new


def _attn_kernel(*refs, has_ctx):
    if has_ctx:
        q_ref, k_ref, v_ref, ck_ref, cv_ref, lam_ref, sw_ref, o_ref, k_s, vt_s = refs
    else:
        q_ref, k_ref, v_ref, lam_ref, sw_ref, o_ref, k_s, vt_s = refs
    t_own = k_ref.shape[1]

    @pl.when(pl.program_id(2) == 0)
    def _():
        k_s[0:t_own, :] = k_ref[0].astype(BF16)
        vt_s[0:DA_DV, 0:t_own] = v_ref[0].astype(F32).T.astype(BF16)
        if has_ctx:
            k_s[t_own:, :] = ck_ref[0].astype(BF16)
            vt_s[0:DA_DV, t_own:] = cv_ref[0].T.astype(BF16)
        vt_s[DA_DV:, :] = jnp.ones((ONES_ROWS, vt_s.shape[1]), BF16)

    lv = lam_ref[...]
    lam = (jnp.exp(jnp.sum(lv[0:1] * lv[1:2], keepdims=True))
           - jnp.exp(jnp.sum(lv[2:3] * lv[3:4], keepdims=True)) + LAM_INIT)
    tq = q_ref.shape[1] // ATT_SPLIT
    dim = lax.broadcasted_iota(I32, (2 * DA_DQK, tq), 0)

    def scores(i):
        qt = q_ref[0, i * tq:(i + 1) * tq, :].astype(F32).T
        q_both = jnp.concatenate([jnp.where(dim < DA_DQK, qt, 0.0),
                                  jnp.where(dim >= DA_DQK, qt, 0.0)], axis=1).astype(BF16)
        return _dot(k_s[...], q_both)

    def finish(i, st):
        pt = jnp.exp2(st - jnp.max(st, axis=0, keepdims=True)).astype(BF16)
        r = _dot(vt_s[...], pt)
        r = r[0:DA_DV] * (1.0 / r[DA_DV:DA_DV + 1])
        o = (r[:, 0:tq] - lam * r[:, tq:2 * tq]).T
        on = o * lax.rsqrt(jnp.mean(o * o, axis=-1, keepdims=True) + EPS) * sw_ref[...]
        o_ref[0, i * tq:(i + 1) * tq, :] = (on * (1.0 - LAM_INIT)).astype(o_ref.dtype)

    sts = [scores(i) for i in range(ATT_SPLIT)]
    for i in range(ATT_SPLIT):
        finish(i, sts[i])


def _attention(q, k, v, ctx_k, ctx_v, lam_p, subln_w, batch, tq):
    heads, n, _ = q.shape
    t_len = n // batch
    nq = t_len // tq
    q_blk = pl.BlockSpec((1, tq, DA_DV), lambda b, h, i: (h, b * nq + i, 0))
    const = lambda b, h, i: (0, 0)
    if k.ndim == 3:
        kv_blk = pl.BlockSpec((1, t_len, DA_DV), lambda b, h, i: (h, b, 0))
        args = [q, k, v]
    else:
        kv_blk = pl.BlockSpec((1, t_len, DA_DV), lambda b, h, i: (b, 0, h))
        args = [q, k.reshape(batch, t_len, SEG), v.reshape(batch, t_len, SEG)]
    in_specs = [q_blk, kv_blk, kv_blk]
    n_keys = t_len
    if ctx_k is not None:
        past = ctx_k.shape[1]
        n_keys += past
        c_blk = pl.BlockSpec((1, past, DA_DV), lambda b, h, i: (b, 0, h))
        args += [ctx_k, ctx_v]
        in_specs += [c_blk, c_blk]
    args += [lam_p, subln_w]
    in_specs += [pl.BlockSpec(lam_p.shape, const), pl.BlockSpec(subln_w.shape, const)]
    o = pl.pallas_call(
        functools.partial(_attn_kernel, has_ctx=ctx_k is not None),
        name=f"attn_n{n}", grid=(batch, DA_HEADS, nq),
        in_specs=in_specs, out_specs=q_blk,
        out_shape=jax.ShapeDtypeStruct((heads, n, DA_DV), BF16),
        scratch_shapes=[pltpu.VMEM((n_keys, 2 * DA_DQK), BF16),
                        pltpu.VMEM((DA_DV + ONES_ROWS, n_keys), BF16)],
        compiler_params=_params(("parallel", "parallel", "arbitrary")),
    )(*args)
    return o


def _postmix_kernel(oh_ref, oa_ref, gate_ref, x_ref, mod_ref, wbh_ref, wba_ref, wout_ref,
                    n2_ref, rw_ref, x1_o, h2_o, afft_o, afftok_o):
    d = x_ref.shape[1]
    mod = mod_ref[0]
    g1, sh2, sc2 = mod[:, 2 * d:3 * d], mod[:, 3 * d:4 * d], mod[:, 4 * d:5 * d]
    g_h = gate_ref[:, 0:d].astype(F32)
    g_a = gate_ref[:, d:2 * d].astype(F32)

    def heads_on_lanes(ref):
        return jnp.concatenate([ref[h] for h in range(ref.shape[0])], axis=1)

    merged = (g_h * _dot(heads_on_lanes(oh_ref), wbh_ref[...])
              + g_a * _dot(heads_on_lanes(oa_ref), wba_ref[...]))
    x1 = x_ref[...] + g1 * _dot(merged.astype(BF16), wout_ref[...])
    x1_o[...] = x1
    xn = x1 * lax.rsqrt(jnp.mean(x1 * x1, axis=-1, keepdims=True) + EPS) * n2_ref[...]
    h2 = xn * (1.0 + sc2) + sh2
    h2_o[...] = h2.astype(h2_o.dtype)
    h_hi, h_lo = _split2(h2)
    rw = rw_ref[...]
    t1 = _dot_nt(rw, h_hi)
    t2 = _dot_nt(rw, h_lo)
    e = N_EXPERTS
    logits = t1[0:e] + t1[e:2 * e] + t2[0:e]
    mx = jnp.max(logits, axis=0, keepdims=True)
    p = jnp.exp(logits - mx)
    aff = p / jnp.sum(p, axis=0, keepdims=True)
    afft_o[...] = aff
    pad = jnp.zeros((LANES - e, aff.shape[1]), F32)
    afftok_o[...] = jnp.concatenate([aff, pad], axis=0).T


def _postmix(o_h, o_a, gates, x, mod3, mod_row, w_bh, w_ba, w_out, norm2_w, rw_cat, tm):
    n, d = x.shape
    const = lambda i: (0, 0)
    row = lambda i: (i, 0)
    return pl.pallas_call(
        _postmix_kernel,
        name=f"postmix_n{n}", grid=(n // tm,),
        in_specs=[pl.BlockSpec((o_h.shape[0], tm, LANES), lambda i: (0, i, 0)),
                  pl.BlockSpec((o_a.shape[0], tm, LANES), lambda i: (0, i, 0)),
                  pl.BlockSpec((tm, 4 * SEG), row), pl.BlockSpec((tm, d), row),
                  pl.BlockSpec((1, 1, mod3.shape[2]), lambda i: (mod_row(i), 0, 0)),
                  pl.BlockSpec(w_bh.shape, const), pl.BlockSpec(w_ba.shape, const),
                  pl.BlockSpec(w_out.shape, const), pl.BlockSpec((1, d), const),
                  pl.BlockSpec(rw_cat.shape, const)],
        out_specs=[pl.BlockSpec((tm, d), row), pl.BlockSpec((tm, d), row),
                   pl.BlockSpec((N_EXPERTS, tm), lambda i: (0, i)),
                   pl.BlockSpec((tm, LANES), row)],
        out_shape=[jax.ShapeDtypeStruct((n, d), F32), jax.ShapeDtypeStruct((n, d), BF16),
                   jax.ShapeDtypeStruct((N_EXPERTS, n), F32),
                   jax.ShapeDtypeStruct((n, LANES), F32)],
        compiler_params=_params(("parallel",)),
    )(o_h, o_a, gates, x, mod3, w_bh, w_ba, w_out, norm2_w, rw_cat)


def _lane_cumsum_exclusive(x, blk):
    e, t = x.shape
    r = lax.broadcasted_iota(I32, (blk, blk), 0)
    c = lax.broadcasted_iota(I32, (blk, blk), 1)
    upper = (r < c).astype(BF16)
    carry = jnp.zeros((e, 1), F32)
    parts = []
    for j in range(t // blk):
        xb = x[:, j * blk:(j + 1) * blk]
        parts.append(_dot(xb.astype(BF16), upper) + carry)
        carry = carry + jnp.sum(xb, axis=1, keepdims=True)
    return parts[0] if len(parts) == 1 else jnp.concatenate(parts, axis=1)


def _route_kernel(aff_ref, pos_ref, *, cap, req_per_group):
    aff = aff_ref[...]
    bits = pltpu.bitcast(aff, I32)
    t_len = aff.shape[1]

    def count(mask):
        return jnp.sum(mask.astype(F32), axis=1, keepdims=True)

    def step(i, th):
        cand = th | (jnp.int32(1) << (30 - i))
        return jnp.where(count(bits >= cand) >= cap, cand, th)

    th = lax.fori_loop(0, 31, step, jnp.zeros((aff.shape[0], 1), I32))
    gt = bits > th
    eq = (bits == th).astype(F32)
    need = cap - count(gt)
    blk = min(t_len, 256)
    tie_rank = _lane_cumsum_exclusive(eq, blk)
    sel = jnp.where(gt, 1.0, jnp.where(tie_rank < need, eq, 0.0))
    slot = _lane_cumsum_exclusive(sel, blk)
    offset = (pl.program_id(0) % req_per_group) * cap
    pos_ref[...] = jnp.where(sel > 0.0, slot.astype(I32) + offset, -1)


def _route(afft, batch, cap, req_per_group):
    e, n = afft.shape
    t_len = n // batch
    blk = pl.BlockSpec((e, t_len), lambda b: (0, b))
    return pl.pallas_call(
        functools.partial(_route_kernel, cap=cap, req_per_group=req_per_group),
        name=f"route_n{n}", grid=(batch,), in_specs=[blk], out_specs=blk,
        out_shape=jax.ShapeDtypeStruct((e, n), I32),
        compiler_params=_params(("parallel",)),
    )(afft)


def _one_hot_rows(pos_row, n_slots):
    slot = lax.broadcasted_iota(I32, (n_slots, pos_row.shape[1]), 0)
    return (slot == pos_row).astype(BF16)


def _dispatch_kernel(pos_ref, h_ref, xg_ref):
    e = pl.program_id(1)
    p = _one_hot_rows(pos_ref[pl.ds(e, 1), :], xg_ref.shape[1])
    xg_ref[0] = _dot(p, h_ref[...]).astype(xg_ref.dtype)


def _dispatch(pos, h2, group_tokens, group_slots):
    e, n = pos.shape
    d = h2.shape[1]
    groups = n // group_tokens
    return pl.pallas_call(
        _dispatch_kernel,
        name=f"dispatch_n{n}", grid=(groups, e),
        in_specs=[pl.BlockSpec((e, group_tokens), lambda g, x: (0, g)),
                  pl.BlockSpec((group_tokens, d), lambda g, x: (g, 0))],
        out_specs=pl.BlockSpec((1, group_slots, d), lambda g, x: (x, g, 0)),
        out_shape=jax.ShapeDtypeStruct((e, groups * group_slots, d), BF16),
        compiler_params=_params(("parallel", "arbitrary")),
    )(pos, h2)


def _expert_kernel(xg_ref, wg_ref, wu_ref, wd_ref, y_ref, wg_s, wu_s, wd_s):
    @pl.when(pl.program_id(1) == 0)
    def _():
        wg_s[...] = wg_ref[0].astype(BF16)
        wu_s[...] = wu_ref[0].astype(BF16)
        wd_s[...] = wd_ref[0].astype(BF16)

    xg = xg_ref[0]
    a = _silu(_dot(xg, wg_s[...])) * _dot(xg, wu_s[...])
    y_ref[0] = _dot(a.astype(BF16), wd_s[...]).astype(y_ref.dtype)


def _experts(xg, w_gate, w_up, w_down, ts):
    e, s, d = xg.shape
    f = w_gate.shape[2]
    x_blk = pl.BlockSpec((1, ts, d), lambda x, i: (x, i, 0))
    return pl.pallas_call(
        _expert_kernel,
        name=f"experts_s{s}", grid=(e, s // ts),
        in_specs=[x_blk, pl.BlockSpec((1, d, f), lambda x, i: (x, 0, 0)),
                  pl.BlockSpec((1, d, f), lambda x, i: (x, 0, 0)),
                  pl.BlockSpec((1, f, d), lambda x, i: (x, 0, 0))],
        out_specs=x_blk,
        out_shape=jax.ShapeDtypeStruct((e, s, d), BF16),
        scratch_shapes=[pltpu.VMEM((d, f), BF16), pltpu.VMEM((d, f), BF16),
                        pltpu.VMEM((f, d), BF16)],
        compiler_params=_params(("parallel", "arbitrary")),
    )(xg, w_gate, w_up, w_down)


def _combine_kernel(pos_ref, afftok_ref, y_ref, x1_ref, mod_ref, o_ref):
    e = pl.program_id(2)
    d = x1_ref.shape[1]

    @pl.when(e == 0)
    def _():
        o_ref[...] = jnp.zeros_like(o_ref)

    p = _one_hot_rows(pos_ref[pl.ds(e, 1), :], y_ref.shape[1])
    lane = lax.broadcasted_iota(I32, afftok_ref.shape, 1)
    gate = jnp.sum(jnp.where(lane == e, afftok_ref[...], 0.0), axis=1, keepdims=True)
    o_ref[...] += gate * _dot_tn(p, y_ref[0])

    @pl.when(e == pl.num_programs(2) - 1)
    def _():
        g2 = mod_ref[0][:, 5 * d:6 * d]
        o_ref[...] = x1_ref[...] + g2 * o_ref[...]


def _combine(pos, afftok, y, x1, mod3, mod_row, group_tokens, group_slots, tc):
    e, n = pos.shape
    d = x1.shape[1]
    groups = n // group_tokens
    per = group_tokens // tc
    tok = lambda g, j, x: (g * per + j, 0)
    return pl.pallas_call(
        _combine_kernel,
        name=f"combine_n{n}", grid=(groups, per, e),
        in_specs=[pl.BlockSpec((e, tc), lambda g, j, x: (0, g * per + j)),
                  pl.BlockSpec((tc, LANES), tok),
                  pl.BlockSpec((1, group_slots, d), lambda g, j, x: (x, g, 0)),
                  pl.BlockSpec((tc, d), tok),
                  pl.BlockSpec((1, 1, mod3.shape[2]), lambda g, j, x: (mod_row(g), 0, 0))],
        out_specs=pl.BlockSpec((tc, d), tok),
        out_shape=jax.ShapeDtypeStruct((n, d), F32),
        compiler_params=_params(("parallel", "parallel", "arbitrary")),
    )(pos, afftok, y, x1, mod3)


def _rope_tables(t_len):
    n_freq = DA_DQK // 4
    inv = ROPE_BASE ** (-jnp.arange(n_freq, dtype=F32) / n_freq)
    t = jnp.arange(t_len)
    pos = jnp.stack([(t // GRID_W).astype(F32), (t % GRID_W).astype(F32)], axis=1)
    ang = pos[:, :, None, None] * inv[None, None, None, :]
    ang = jnp.broadcast_to(ang, (t_len, 2, 2, n_freq))
    sign = jnp.array([-1.0, 1.0], F32)[None, None, :, None]
    cos = jnp.cos(ang).reshape(t_len, DA_DQK)
    sin = (jnp.sin(ang) * sign).reshape(t_len, DA_DQK)
    reps = SEG // DA_DQK
    return jnp.tile(cos, (1, reps)), jnp.tile(sin, (1, reps))


def _trunk(x, batch, mod3, mod_row_tok, mod_row_grp, weights, ctx_k, ctx_v, s0, rope, group_tokens):
    (norm1_w, norm2_w, w_in, b_gate, lb_logits, hgrn_norm_w, qkw, gm, lam_p, subln_w,
     w_bh, w_ba, w_out, rw_cat, w_eg, w_eu, w_ed) = weights
    n, d = x.shape
    t_len = n // batch
    latent = rope is not None
    tm = 256
    (q_h, bf, bb, kff, kfb, i_h, g_h, dq, dk, dv, gates) = _premix(
        x, mod3, functools.partial(mod_row_tok, tm=tm), norm1_w, w_in, b_gate, lb_logits, qkw, gm,
        rope, tm, BF16 if latent else F32, latent)
    o_h, s_new = _hgrn(q_h, bf, bb, kff, kfb, i_h, g_h, hgrn_norm_w, s0, batch)
    o_a = _attention(dq, dk, dv, ctx_k, ctx_v, lam_p, subln_w, batch, min(t_len, 256 * ATT_SPLIT))
    x1, h2, afft, afftok = _postmix(o_h, o_a, gates, x, mod3, functools.partial(mod_row_tok, tm=tm),
                                    w_bh, w_ba, w_out, norm2_w, rw_cat, tm)
    cap = EC_CAPACITY * t_len // N_EXPERTS
    req_per_group = group_tokens // t_len
    group_slots = cap * req_per_group
    pos = _route(afft, batch, cap, req_per_group)
    xg = _dispatch(pos, h2, group_tokens, group_slots)
    y = _experts(xg, w_eg, w_eu, w_ed, group_slots)
    out = _combine(pos, afftok, y, x1, mod3, mod_row_grp, group_tokens, group_slots, 1024)
    return out, dk, dv, s_new


def kernel(x_prompt, x_sample, cache_k, cache_v, state_hgrn, c, c_ctx, norm1_w, norm2_w, w_mod,
           b_mod, w_in, b_gate, hgrn_lb_logits, hgrn_norm_w, qk_norm_w, diff_lambda, diff_subln_w,
           w_branch_hgrn, w_branch_attn, w_out, router_w, w_exp_gate, w_exp_up, w_exp_down):
    batch, seq, d = x_prompt.shape
    dec_batch, dec_seq, _ = x_sample.shape
    past = cache_k.shape[2]
    depth = w_in.shape[0]
    assert depth == 1
    group_tokens = dec_seq
    assert group_tokens % seq == 0 and (batch * seq) % group_tokens == 0
    l = 0

    rows = -(-(1 + dec_batch) // 8) * 8
    cond = jnp.zeros((rows, d), F32).at[0].set(c_ctx).at[1:1 + dec_batch].set(c)
    mod = _modulation(cond, w_mod[l], b_mod[l])
    mod3 = mod.reshape(rows, 1, 6 * d)

    gidx = jnp.arange(SEG) // DA_DQK
    gm = (gidx[:, None] == gidx[None, :]).astype(BF16) * (1.0 / DA_DQK)
    qkw = jnp.tile(qk_norm_w[l], (1, SEG // DA_DQK))
    rw_t = router_w[l].T
    rw_hi = rw_t.astype(BF16)
    rw_cat = jnp.concatenate([rw_hi, (rw_t - rw_hi.astype(F32)).astype(BF16)], axis=0)
    weights = (norm1_w[l][None], norm2_w[l][None], w_in[l].astype(BF16), b_gate[l][None],
               hgrn_lb_logits.reshape(4, SEG), hgrn_norm_w[l], qkw, gm, diff_lambda[l],
               diff_subln_w[l][None], w_branch_hgrn[l].astype(BF16),
               w_branch_attn[l].astype(BF16), w_out[l].astype(BF16),
               rw_cat, w_exp_gate[l], w_exp_up[l], w_exp_down[l])

    yp, k_new, v_new, s_new = _trunk(
        x_prompt.reshape(batch * seq, d), batch, mod3,
        lambda i, tm: 0, lambda g: 0, weights, None, None, None, None, group_tokens)
    per_req = dec_seq
    ys, _, _, _ = _trunk(
        x_sample.reshape(dec_batch * dec_seq, d), dec_batch, mod3,
        lambda i, tm: 1 + (i * tm) // per_req, lambda g: 1 + g, weights,
        cache_k[:, l].reshape(dec_batch, past, SEG), cache_v[:, l].reshape(dec_batch, past, SEG),
        state_hgrn[:, l], _rope_tables(dec_seq), group_tokens)

    return (yp.reshape(batch, seq, d), ys.reshape(dec_batch, dec_seq, d),
            k_new.reshape(batch, 1, seq, DA_HEADS, 2, DA_DQK),
            v_new.reshape(batch, 1, seq, DA_HEADS, DA_DV),
            s_new.reshape(batch, 1, 2, HG_HEADS, HG_DK, HG_DV))
```

```python
import functools
import math

import jax
import jax.numpy as jnp
from jax import lax
from jax.experimental import pallas as pl
from jax.experimental.pallas import tpu as pltpu
from jax.experimental.pallas import tpu_sc as plsc

F32 = jnp.float32
BF16 = jnp.bfloat16
I32 = jnp.int32

EPS = 1e-6
GRID_W = 64
HG_HEADS = 4
HG_DK = 128
HG_DV = 128
HG_CHUNK_LOG2 = 7
HG_CHUNK = 1 << HG_CHUNK_LOG2
HG_UNROLL = 4
DA_HEADS = 4
DA_DQK = 64
DA_DV = 128
N_EXPERTS = 16
EC_CAPACITY = 2
ROPE_BASE = 10000.0
SEG = 512
N_SEG = 12
LAM_INIT = 0.8 - 0.6 * math.exp(-0.3 * 0)
LANES = 128
ONES_ROWS = 16
ATT_SPLIT = 2
Q_SCALE = DA_DQK ** -0.5 * math.log2(math.e)
HG_SAFE_DECAY = 80.0
VMEM_LIMIT = 56 * 1024 * 1024
SC_CORES = 2
SC_SUBCORES = 16
SC_LANES = 16
SC_GATHER_ROWS = 128


def _dot(a, b):
    return jnp.dot(a, b, preferred_element_type=F32)


def _dot_nt(a, b):
    return lax.dot_general(a, b, (((1,), (1,)), ((), ())), preferred_element_type=F32)


def _dot_tn(a, b):
    return lax.dot_general(a, b, (((0,), (0,)), ((), ())), preferred_element_type=F32)


def _split2(x):
    hi = x.astype(BF16)
    lo = (x - hi.astype(F32)).astype(BF16)
    return hi, lo


def _silu(x):
    return x * jax.nn.sigmoid(x)


def _pack_bf16_pairs(x):
    w = x.shape[1] // 2
    bits = pltpu.bitcast(x.astype(BF16).astype(F32), I32)
    return lax.shift_right_logical(bits[:, :w], 16) | bits[:, w:]


def _unpack_bf16_pairs(words):
    lo = pltpu.bitcast(words << 16, F32)
    hi = pltpu.bitcast(words & jnp.int32(-65536), F32)
    return jnp.concatenate([lo, hi], axis=1).astype(BF16)


def _params(sem):
    return pltpu.CompilerParams(dimension_semantics=sem, vmem_limit_bytes=VMEM_LIMIT)


def _mod_kernel(c_ref, w_ref, b_ref, o_ref):
    s_hi, s_lo = _split2(_silu(c_ref[...]))
    w_hi, w_lo = _split2(w_ref[...])
    o_ref[...] = _dot(s_hi, w_hi) + _dot(s_hi, w_lo) + _dot(s_lo, w_hi) + b_ref[...]


def _modulation(cond, w_mod, b_mod):
    rows, d = cond.shape
    n = w_mod.shape[1]
    bn = 512
    return pl.pallas_call(
        _mod_kernel,
        name="modulation", grid=(n // bn,),
        in_specs=[pl.BlockSpec((rows, d), lambda j: (0, 0)),
                  pl.BlockSpec((d, bn), lambda j: (0, j)),
                  pl.BlockSpec((1, bn), lambda j: (0, j))],
        out_specs=pl.BlockSpec((rows, bn), lambda j: (0, j)),
        out_shape=jax.ShapeDtypeStruct((rows, n), F32),
        compiler_params=_params(("arbitrary",)),
    )(cond, w_mod, b_mod.reshape(1, n))


def _group_rms(z, gm_ref, w):
    ms = _dot((z * z).astype(BF16), gm_ref[...])
    return z * lax.rsqrt(ms + EPS) * w


def _rope(x, cos, sin_signed):
    n = x.shape[-1]
    lane = lax.broadcasted_iota(I32, x.shape, 1)
    partner = jnp.where((lane & 16) == 0, pltpu.roll(x, n - 16, 1), pltpu.roll(x, 16, 1))
    return x * cos + partner * sin_signed


def _premix_kernel(*refs, latent):
    if latent:
        (x_ref, mod_ref, n1_ref, win_ref, bg_ref, lbl_ref, qkw_ref, gm_ref, cos_ref, sin_ref,
         q_o, bf_o, bb_o, kff_o, kfb_o, i_o, g_o, dq_o, dk_o, dv_o, gate_o) = refs
    else:
        (x_ref, mod_ref, n1_ref, win_ref, bg_ref, lbl_ref, qkw_ref, gm_ref,
         q_o, bf_o, bb_o, kff_o, kfb_o, i_o, g_o, dq_o, dk_o, dv_o, gate_o) = refs
    d = x_ref.shape[1]
    mod = mod_ref[0]
    sh1, sc1 = mod[:, 0:d], mod[:, d:2 * d]
    x = x_ref[...]
    xn = x * lax.rsqrt(jnp.mean(x * x, axis=-1, keepdims=True) + EPS) * n1_ref[...]
    hb = (xn * (1.0 + sc1) + sh1).astype(BF16)

    def seg(j):
        return _dot(hb, win_ref[:, j * SEG:(j + 1) * SEG])

    def lower_bound(direction):
        l0 = lbl_ref[2 * direction:2 * direction + 1, :]
        l1 = lbl_ref[2 * direction + 1:2 * direction + 2, :]
        mx = jnp.maximum(l0, l1)
        e0, e1 = jnp.exp(l0 - mx), jnp.exp(l1 - mx)
        return e0 / (e0 + e1)

    tm = x.shape[0]
    row = lax.broadcasted_iota(I32, (tm, tm), 0)
    col = lax.broadcasted_iota(I32, (tm, tm), 1)
    same_chunk = (row >> HG_CHUNK_LOG2) == (col >> HG_CHUNK_LOG2)

    def store(o_ref, val):
        val = val.astype(o_ref.dtype)
        if len(o_ref.shape) == 2:
            o_ref[...] = val
        else:
            for h in range(o_ref.shape[0]):
                o_ref[h] = val[:, h * LANES:(h + 1) * LANES]

    store(q_o, _silu(seg(0)))
    for j, b_o, kf_o, order in ((1, bf_o, kff_o, row >= col), (2, bb_o, kfb_o, row <= col)):
        lbd = lower_bound(j - 1)
        f = lbd + (1.0 - lbd) * jax.nn.sigmoid(seg(j))
        tri = (same_chunk & order).astype(BF16)
        hi, lo = _split2(jnp.log(f))
        store(b_o, _dot(tri, hi) + _dot(tri, lo))
        store(kf_o, 1.0 - f)
    store(i_o, seg(3))
    store(g_o, _silu(seg(4)))
    qn = _group_rms(seg(5), gm_ref, qkw_ref[0:1, :]) * Q_SCALE
    kn = _group_rms(seg(6), gm_ref, qkw_ref[1:2, :])
    if latent:
        qn = _rope(qn, cos_ref[...], sin_ref[...])
        kn = _rope(kn, cos_ref[...], sin_ref[...])
    store(dq_o, qn)
    store(dk_o, kn)
    store(dv_o, seg(7))
    for j in range(4):
        z = seg(8 + j) + bg_ref[:, j * SEG:(j + 1) * SEG]
        gate_o[:, j * SEG:(j + 1) * SEG] = jax.nn.sigmoid(z).astype(gate_o.dtype)


def _premix(x, mod3, mod_row, norm1_w, w_in, b_gate, lb_logits, qkw, gm, rope, tm, kv_dtype,
            kv_head_major):
    n, d = x.shape
    latent = rope is not None
    const = lambda i: (0, 0)
    in_specs = [pl.BlockSpec((tm, d), lambda i: (i, 0)),
                pl.BlockSpec((1, 1, mod3.shape[2]), lambda i: (mod_row(i), 0, 0)),
                pl.BlockSpec((1, d), const),
                pl.BlockSpec(w_in.shape, const),
                pl.BlockSpec(b_gate.shape, const),
                pl.BlockSpec(lb_logits.shape, const),
                pl.BlockSpec(qkw.shape, const),
                pl.BlockSpec(gm.shape, const)]
    args = [x, mod3, norm1_w, w_in, b_gate, lb_logits, qkw, gm]
    if latent:
        cos, sin = rope
        nblk = cos.shape[0] // tm
        in_specs += [pl.BlockSpec((tm, SEG), lambda i: (i % nblk, 0))] * 2
        args += [cos, sin]
    heads = SEG // LANES
    head_spec = pl.BlockSpec((heads, tm, LANES), lambda i: (0, i, 0))
    tok_spec = pl.BlockSpec((tm, SEG), lambda i: (i, 0))
    kv_spec = head_spec if kv_head_major else tok_spec
    kv_shape = (heads, n, LANES) if kv_head_major else (n, SEG)
    out_dtypes = [BF16, F32, F32, BF16, BF16, BF16, BF16, BF16]
    out_shape = [jax.ShapeDtypeStruct((heads, n, LANES), t) for t in out_dtypes]
    out_shape += [jax.ShapeDtypeStruct(kv_shape, kv_dtype)] * 2
    out_shape.append(jax.ShapeDtypeStruct((n, 4 * SEG), BF16))
    out_specs = [head_spec] * 8 + [kv_spec] * 2 + [pl.BlockSpec((tm, 4 * SEG), lambda i: (i, 0))]
    return pl.pallas_call(
        functools.partial(_premix_kernel, latent=latent),
        name=f"premix_n{n}", grid=(n // tm,),
        in_specs=in_specs, out_specs=out_specs, out_shape=out_shape,
        compiler_params=_params(("parallel",)),
    )(*args)


def _hgrn_chunk_local(q, b, total, k, v, keep, safe):
    if safe:
        ref = b[HG_CHUNK // 2:HG_CHUNK // 2 + 1, :]
        qa = q * jnp.exp(b - ref)
        kb = k * jnp.exp(ref - b)
        attn = jnp.where(keep, _dot_nt(qa.astype(BF16), kb.astype(BF16)), 0.0)
        qe = qa * jnp.exp(ref)
        kd = kb * jnp.exp(total - ref)
    else:
        qe = q * jnp.exp(b)
        kd = k * jnp.exp(total - b)
        col = lax.broadcasted_iota(I32, (HG_CHUNK, HG_CHUNK), 1)

        def column(s, acc):
            onehot = (lax.broadcasted_iota(I32, (HG_CHUNK, 1), 0) == s).astype(F32)
            bs = jnp.sum(b * onehot, axis=0, keepdims=True)
            ks = jnp.sum(k * onehot, axis=0, keepdims=True)
            w = jnp.sum(q * ks * jnp.exp(jnp.minimum(b - bs, 0.0)), axis=1, keepdims=True)
            return jnp.where(col == s, w, acc)

        attn = lax.fori_loop(0, HG_CHUNK, column, jnp.zeros((HG_CHUNK, HG_CHUNK), F32))
        attn = jnp.where(keep, attn, 0.0)
    vt = v.astype(F32).T.astype(BF16)
    lhs = jnp.concatenate([qe.astype(BF16), attn.astype(BF16)], axis=1)
    return lhs, vt, _dot(vt, kd.astype(BF16)), jnp.exp(total)


def _hgrn_scan_group(chunks, st):
    outs = []
    for lhs, vt, inc, decay in chunks:
        outs.append(_dot_nt(lhs, jnp.concatenate([st.astype(BF16), vt], axis=1)))
        st = st * decay + inc
    return outs, st


def _hgrn_kernel(*refs, has_state):
    if has_state:
        (q_ref, bf_ref, bb_ref, kff_ref, kfb_ref, i_ref, g_ref, nw_ref, s0_ref,
         o_ref, sn_ref, of_s, ob_s) = refs
    else:
        (q_ref, bf_ref, bb_ref, kff_ref, kfb_ref, i_ref, g_ref, nw_ref,
         o_ref, sn_ref, of_s, ob_s) = refs
    t_len = q_ref.shape[1]
    nc = t_len // HG_CHUNK
    unroll = min(HG_UNROLL, nc)
    head = pl.program_id(1)
    row = lax.broadcasted_iota(I32, (HG_CHUNK, HG_CHUNK), 0)
    col = lax.broadcasted_iota(I32, (HG_CHUNK, HG_CHUNK), 1)
    keep_f, keep_b = row >= col, row <= col

    mid_f = bf_ref[0, pl.ds(HG_CHUNK // 2, nc, stride=HG_CHUNK), :]
    tot_f = bf_ref[0, pl.ds(HG_CHUNK - 1, nc, stride=HG_CHUNK), :]
    mid_b = bb_ref[0, pl.ds(HG_CHUNK // 2, nc, stride=HG_CHUNK), :]
    tot_b = bb_ref[0, pl.ds(0, nc, stride=HG_CHUNK), :]
    worst = jnp.minimum(jnp.min(jnp.minimum(mid_f, tot_f - mid_f)),
                        jnp.min(jnp.minimum(mid_b, tot_b - mid_b)))

    if has_state:
        st_f0, st_b0 = s0_ref[0, 0, 0].T, s0_ref[0, 1, 0].T
    else:
        st_f0 = st_b0 = jnp.zeros((HG_DV, HG_DK), F32)

    def scan(safe):
        def body(it, carry):
            st_f, st_b = carry
            rows_f, rows_b, loc_f, loc_b = [], [], [], []
            for u in range(unroll):
                c = it * unroll + u
                sf = pl.multiple_of(c * HG_CHUNK, HG_CHUNK)
                sb = pl.multiple_of((nc - 1 - c) * HG_CHUNK, HG_CHUNK)
                rf, rb = pl.ds(sf, HG_CHUNK), pl.ds(sb, HG_CHUNK)
                rows_f.append(rf)
                rows_b.append(rb)
                loc_f.append(_hgrn_chunk_local(
                    q_ref[0, rf, :].astype(F32), bf_ref[0, rf, :],
                    bf_ref[0, pl.ds(sf + HG_CHUNK - 1, 1), :],
                    kff_ref[0, rf, :].astype(F32), i_ref[0, rf, :], keep_f, safe))
                loc_b.append(_hgrn_chunk_local(
                    q_ref[0, rb, :].astype(F32), bb_ref[0, rb, :], bb_ref[0, pl.ds(sb, 1), :],
                    kfb_ref[0, rb, :].astype(F32), i_ref[0, rb, :], keep_b, safe))
            outs_f, st_f = _hgrn_scan_group(loc_f, st_f)
            outs_b, st_b = _hgrn_scan_group(loc_b, st_b)
            for rf, rb, o_f, o_b in zip(rows_f, rows_b, outs_f, outs_b):
                of_s[rf, :] = o_f
                ob_s[rb, :] = o_b
            return st_f, st_b
        return lax.fori_loop(0, nc // unroll, body, (st_f0, st_b0))

    st_f, st_b = lax.cond(worst >= -HG_SAFE_DECAY, lambda: scan(True), lambda: scan(False))
    sn_ref[0, 0, 0] = st_f.T
    sn_ref[0, 1, 0] = st_b.T
    o = of_s[...] + ob_s[...]
    nw = nw_ref[pl.ds(head, 1), :]
    on = o * lax.rsqrt(jnp.mean(o * o, axis=-1, keepdims=True) + EPS) * nw
    o_ref[0] = (on * g_ref[0].astype(F32)).astype(o_ref.dtype)


def _hgrn(q, bf, bb, kff, kfb, iv, g, norm_w, s0, batch):
    heads, n, _ = q.shape
    t_len = n // batch
    blk = pl.BlockSpec((1, t_len, HG_DK), lambda b, h: (h, b, 0))
    st_blk = pl.BlockSpec((1, 2, 1, HG_DK, HG_DV), lambda b, h: (b, 0, h, 0, 0))
    args = [q, bf, bb, kff, kfb, iv, g, norm_w]
    in_specs = [blk] * 7 + [pl.BlockSpec(norm_w.shape, lambda b, h: (0, 0))]
    if s0 is not None:
        args.append(s0)
        in_specs.append(st_blk)
    o, s_new = pl.pallas_call(
        functools.partial(_hgrn_kernel, has_state=s0 is not None),
        name=f"hgrn_n{n}", grid=(batch, HG_HEADS),
        in_specs=in_specs,
        out_specs=[blk, st_blk],
        out_shape=[jax.ShapeDtypeStruct((heads, n, HG_DV), BF16),
                   jax.ShapeDtypeStruct((batch, 2, HG_HEADS, HG_DK, HG_DV), F32)],
        scratch_shapes=[pltpu.VMEM((t_len, HG_DV), F32), pltpu.VMEM((t_len, HG_DV), F32)],
        compiler_params=_params(("parallel", "parallel")),
    )(*args)
    return o, s_new


def _attn_kernel(*refs, has_ctx):
    if has_ctx:
        q_ref, k_ref, v_ref, ck_ref, cv_ref, lam_ref, sw_ref, o_ref, k_s, vt_s = refs
    else:
        q_ref, k_ref, v_ref, lam_ref, sw_ref, o_ref, k_s, vt_s = refs
    t_own = k_ref.shape[1]

    @pl.when(pl.program_id(2) == 0)
    def _():
        k_s[0:t_own, :] = k_ref[0].astype(BF16)
        vt_s[0:DA_DV, 0:t_own] = v_ref[0].astype(F32).T.astype(BF16)
        if has_ctx:
            k_s[t_own:, :] = ck_ref[0].astype(BF16)
            vt_s[0:DA_DV, t_own:] = cv_ref[0].T.astype(BF16)
        vt_s[DA_DV:, :] = jnp.ones((ONES_ROWS, vt_s.shape[1]), BF16)

    lv = lam_ref[...]
    lam = (jnp.exp(jnp.sum(lv[0:1] * lv[1:2], keepdims=True))
           - jnp.exp(jnp.sum(lv[2:3] * lv[3:4], keepdims=True)) + LAM_INIT)
    tq = q_ref.shape[1] // ATT_SPLIT
    dim = lax.broadcasted_iota(I32, (2 * DA_DQK, tq), 0)

    def scores(i):
        qt = q_ref[0, i * tq:(i + 1) * tq, :].astype(F32).T
        q_both = jnp.concatenate([jnp.where(dim < DA_DQK, qt, 0.0),
                                  jnp.where(dim >= DA_DQK, qt, 0.0)], axis=1).astype(BF16)
        return _dot(k_s[...], q_both)

    def finish(i, st):
        pt = jnp.exp2(st - jnp.max(st, axis=0, keepdims=True)).astype(BF16)
        r = _dot(vt_s[...], pt)
        r = r[0:DA_DV] * (1.0 / r[DA_DV:DA_DV + 1])
        o = (r[:, 0:tq] - lam * r[:, tq:2 * tq]).T
        on = o * lax.rsqrt(jnp.mean(o * o, axis=-1, keepdims=True) + EPS) * sw_ref[...]
        o_ref[0, i * tq:(i + 1) * tq, :] = (on * (1.0 - LAM_INIT)).astype(o_ref.dtype)

    sts = [scores(i) for i in range(ATT_SPLIT)]
    for i in range(ATT_SPLIT):
        finish(i, sts[i])


def _attention(q, k, v, ctx_k, ctx_v, lam_p, subln_w, batch, tq):
    heads, n, _ = q.shape
    t_len = n // batch
    nq = t_len // tq
    q_blk = pl.BlockSpec((1, tq, DA_DV), lambda b, h, i: (h, b * nq + i, 0))
    const = lambda b, h, i: (0, 0)
    if k.ndim == 3:
        kv_blk = pl.BlockSpec((1, t_len, DA_DV), lambda b, h, i: (h, b, 0))
        args = [q, k, v]
    else:
        kv_blk = pl.BlockSpec((1, t_len, DA_DV), lambda b, h, i: (b, 0, h))
        args = [q, k.reshape(batch, t_len, SEG), v.reshape(batch, t_len, SEG)]
    in_specs = [q_blk, kv_blk, kv_blk]
    n_keys = t_len
    if ctx_k is not None:
        past = ctx_k.shape[1]
        n_keys += past
        c_blk = pl.BlockSpec((1, past, DA_DV), lambda b, h, i: (b, 0, h))
        args += [ctx_k, ctx_v]
        in_specs += [c_blk, c_blk]
    args += [lam_p, subln_w]
    in_specs += [pl.BlockSpec(lam_p.shape, const), pl.BlockSpec(subln_w.shape, const)]
    o = pl.pallas_call(
        functools.partial(_attn_kernel, has_ctx=ctx_k is not None),
        name=f"attn_n{n}", grid=(batch, DA_HEADS, nq),
        in_specs=in_specs, out_specs=q_blk,
        out_shape=jax.ShapeDtypeStruct((heads, n, DA_DV), BF16),
        scratch_shapes=[pltpu.VMEM((n_keys, 2 * DA_DQK), BF16),
                        pltpu.VMEM((DA_DV + ONES_ROWS, n_keys), BF16)],
        compiler_params=_params(("parallel", "parallel", "arbitrary")),
    )(*args)
    return o


def _postmix_kernel(oh_ref, oa_ref, gate_ref, x_ref, mod_ref, wbh_ref, wba_ref, wout_ref,
                    n2_ref, rw_ref, x1_o, h2_o, afft_o, afftok_o):
    d = x_ref.shape[1]
    mod = mod_ref[0]
    g1, sh2, sc2 = mod[:, 2 * d:3 * d], mod[:, 3 * d:4 * d], mod[:, 4 * d:5 * d]
    g_h = gate_ref[:, 0:d].astype(F32)
    g_a = gate_ref[:, d:2 * d].astype(F32)

    def heads_on_lanes(ref):
        return jnp.concatenate([ref[h] for h in range(ref.shape[0])], axis=1)

    merged = (g_h * _dot(heads_on_lanes(oh_ref), wbh_ref[...])
              + g_a * _dot(heads_on_lanes(oa_ref), wba_ref[...]))
    x1 = x_ref[...] + g1 * _dot(merged.astype(BF16), wout_ref[...])
    x1_o[...] = x1
    xn = x1 * lax.rsqrt(jnp.mean(x1 * x1, axis=-1, keepdims=True) + EPS) * n2_ref[...]
    h2 = xn * (1.0 + sc2) + sh2
    h2_o[...] = _pack_bf16_pairs(h2)
    h_hi, h_lo = _split2(h2)
    rw = rw_ref[...]
    t1 = _dot_nt(rw, h_hi)
    t2 = _dot_nt(rw, h_lo)
    e = N_EXPERTS
    logits = t1[0:e] + t1[e:2 * e] + t2[0:e]
    mx = jnp.max(logits, axis=0, keepdims=True)
    p = jnp.exp(logits - mx)
    aff = p / jnp.sum(p, axis=0, keepdims=True)
    afft_o[...] = aff
    pad = jnp.zeros((LANES - e, aff.shape[1]), F32)
    afftok_o[...] = jnp.concatenate([aff, pad], axis=0).T


def _postmix(o_h, o_a, gates, x, mod3, mod_row, w_bh, w_ba, w_out, norm2_w, rw_cat, tm):
    n, d = x.shape
    const = lambda i: (0, 0)
    row = lambda i: (i, 0)
    return pl.pallas_call(
        _postmix_kernel,
        name=f"postmix_n{n}", grid=(n // tm,),
        in_specs=[pl.BlockSpec((o_h.shape[0], tm, LANES), lambda i: (0, i, 0)),
                  pl.BlockSpec((o_a.shape[0], tm, LANES), lambda i: (0, i, 0)),
                  pl.BlockSpec((tm, 4 * SEG), row), pl.BlockSpec((tm, d), row),
                  pl.BlockSpec((1, 1, mod3.shape[2]), lambda i: (mod_row(i), 0, 0)),
                  pl.BlockSpec(w_bh.shape, const), pl.BlockSpec(w_ba.shape, const),
                  pl.BlockSpec(w_out.shape, const), pl.BlockSpec((1, d), const),
                  pl.BlockSpec(rw_cat.shape, const)],
        out_specs=[pl.BlockSpec((tm, d), row), pl.BlockSpec((tm, d // 2), row),
                   pl.BlockSpec((N_EXPERTS, tm), lambda i: (0, i)),
                   pl.BlockSpec((tm, LANES), row)],
        out_shape=[jax.ShapeDtypeStruct((n, d), F32), jax.ShapeDtypeStruct((n, d // 2), I32),
                   jax.ShapeDtypeStruct((N_EXPERTS, n), F32),
                   jax.ShapeDtypeStruct((n, LANES), F32)],
        compiler_params=_params(("parallel",)),
    )(o_h, o_a, gates, x, mod3, w_bh, w_ba, w_out, norm2_w, rw_cat)


def _lane_cumsum_exclusive(x, blk):
    e, t = x.shape
    r = lax.broadcasted_iota(I32, (blk, blk), 0)
    c = lax.broadcasted_iota(I32, (blk, blk), 1)
    upper = (r < c).astype(BF16)
    carry = jnp.zeros((e, 1), F32)
    parts = []
    for j in range(t // blk):
        xb = x[:, j * blk:(j + 1) * blk]
        parts.append(_dot(xb.astype(BF16), upper) + carry)
        carry = carry + jnp.sum(xb, axis=1, keepdims=True)
    return parts[0] if len(parts) == 1 else jnp.concatenate(parts, axis=1)


def _route_kernel(aff_ref, pos_ref, *, cap, req_per_group):
    aff = aff_ref[...]
    bits = pltpu.bitcast(aff, I32)
    t_len = aff.shape[1]

    def count(mask):
        return jnp.sum(mask.astype(F32), axis=1, keepdims=True)

    def step(i, th):
        cand = th | (jnp.int32(1) << (30 - i))
        return jnp.where(count(bits >= cand) >= cap, cand, th)

    th = lax.fori_loop(0, 31, step, jnp.zeros((aff.shape[0], 1), I32))
    gt = bits > th
    eq = (bits == th).astype(F32)
    need = cap - count(gt)
    blk = min(t_len, 256)
    tie_rank = _lane_cumsum_exclusive(eq, blk)
    sel = jnp.where(gt, 1.0, jnp.where(tie_rank < need, eq, 0.0))
    slot = _lane_cumsum_exclusive(sel, blk)
    offset = (pl.program_id(0) % req_per_group) * cap
    pos_ref[...] = jnp.where(sel > 0.0, slot.astype(I32) + offset, -1)


def _route(afft, batch, cap, req_per_group):
    e, n = afft.shape
    t_len = n // batch
    blk = pl.BlockSpec((e, t_len), lambda b: (0, b))
    return pl.pallas_call(
        functools.partial(_route_kernel, cap=cap, req_per_group=req_per_group),
        name=f"route_n{n}", grid=(batch,), in_specs=[blk], out_specs=blk,
        out_shape=jax.ShapeDtypeStruct((e, n), I32),
        compiler_params=_params(("parallel",)),
    )(afft)


def _one_hot_rows(pos_row, n_slots):
    slot = lax.broadcasted_iota(I32, (n_slots, pos_row.shape[1]), 0)
    return (slot == pos_row).astype(BF16)


def _dispatch(pos, h2p, group_tokens, group_slots):
    n_exp, n = pos.shape
    width = h2p.shape[1]
    groups = n // group_tokens
    workers = SC_CORES * SC_SUBCORES
    pairs = groups * n_exp
    chunks = group_slots // SC_GATHER_ROWS
    assert pairs % workers == 0 and group_slots % SC_GATHER_ROWS == 0
    assert group_tokens % SC_LANES == 0 and n_exp & (n_exp - 1) == 0
    exp_shift = n_exp.bit_length() - 1
    row_shift = SC_GATHER_ROWS.bit_length() - 1

    def body(pos_hbm, h_hbm, out_hbm, pos_v, idx_v, rows_v, sem):
        wid = lax.axis_index("s") * SC_CORES + lax.axis_index("c")
        lane = lax.iota(I32, SC_LANES)
        for k in range(pairs // workers):
            pair = wid + workers * k
            g = pair >> exp_shift
            e = pair & (n_exp - 1)
            pltpu.sync_copy(pos_hbm.at[e, pl.ds(g * group_tokens, group_tokens)], pos_v)

            @pl.loop(0, group_tokens // SC_LANES)
            def _(i):
                p = pos_v[pl.ds(i * SC_LANES, SC_LANES)]
                tok = g * group_tokens + i * SC_LANES + lane
                slot = jnp.maximum(p, 0)
                plsc.store_scatter(idx_v, [slot >> row_shift, slot & (SC_GATHER_ROWS - 1)], tok,
                                   mask=p >= 0)

            for c in range(chunks):
                pltpu.async_copy(h_hbm.at[idx_v.at[c]], rows_v, sem).wait()
                pltpu.sync_copy(
                    rows_v, out_hbm.at[e, pl.ds(g * group_slots + c * SC_GATHER_ROWS, SC_GATHER_ROWS)])

    mesh = plsc.VectorSubcoreMesh(core_axis_name="c", subcore_axis_name="s",
                                  num_cores=SC_CORES, num_subcores=SC_SUBCORES)
    return pl.kernel(
        body, out_type=jax.ShapeDtypeStruct((n_exp, groups * group_slots, width), I32), mesh=mesh,
        scratch_types=[pltpu.VMEM((group_tokens,), I32), pltpu.VMEM((chunks, SC_GATHER_ROWS), I32),
                       pltpu.VMEM((SC_GATHER_ROWS, width), I32), pltpu.SemaphoreType.DMA],
        compiler_params=pltpu.CompilerParams(needs_layout_passes=False),
        name=f"dispatch_n{n}",
    )(pos, h2p)


def _expert_kernel(xg_ref, wg_ref, wu_ref, wd_ref, y_ref, wg_s, wu_s, wd_s):
    @pl.when(pl.program_id(1) == 0)
    def _():
        wg_s[...] = wg_ref[0].astype(BF16)
        wu_s[...] = wu_ref[0].astype(BF16)
        wd_s[...] = wd_ref[0].astype(BF16)

    xg = _unpack_bf16_pairs(xg_ref[0])
    a = _silu(_dot(xg, wg_s[...])) * _dot(xg, wu_s[...])
    y_ref[0] = _dot(a.astype(BF16), wd_s[...]).astype(y_ref.dtype)


def _experts(xg, w_gate, w_up, w_down, ts):
    e, s, half = xg.shape
    d, f = w_gate.shape[1:]
    assert d == 2 * half
    x_blk = pl.BlockSpec((1, ts, d), lambda x, i: (x, i, 0))
    return pl.pallas_call(
        _expert_kernel,
        name=f"experts_s{s}", grid=(e, s // ts),
        in_specs=[pl.BlockSpec((1, ts, half), lambda x, i: (x, i, 0)),
                  pl.BlockSpec((1, d, f), lambda x, i: (x, 0, 0)),
                  pl.BlockSpec((1, d, f), lambda x, i: (x, 0, 0)),
                  pl.BlockSpec((1, f, d), lambda x, i: (x, 0, 0))],
        out_specs=x_blk,
        out_shape=jax.ShapeDtypeStruct((e, s, d), BF16),
        scratch_shapes=[pltpu.VMEM((d, f), BF16), pltpu.VMEM((d, f), BF16),
                        pltpu.VMEM((f, d), BF16)],
        compiler_params=_params(("parallel", "arbitrary")),
    )(xg, w_gate, w_up, w_down)


def _combine_kernel(pos_ref, afftok_ref, y_ref, x1_ref, mod_ref, o_ref):
    e = pl.program_id(2)
    d = x1_ref.shape[1]

    @pl.when(e == 0)
    def _():
        o_ref[...] = jnp.zeros_like(o_ref)

    p = _one_hot_rows(pos_ref[pl.ds(e, 1), :], y_ref.shape[1])
    lane = lax.broadcasted_iota(I32, afftok_ref.shape, 1)
    gate = jnp.sum(jnp.where(lane == e, afftok_ref[...], 0.0), axis=1, keepdims=True)
    o_ref[...] += gate * _dot_tn(p, y_ref[0])

    @pl.when(e == pl.num_programs(2) - 1)
    def _():
        g2 = mod_ref[0][:, 5 * d:6 * d]
        o_ref[...] = x1_ref[...] + g2 * o_ref[...]


def _combine(pos, afftok, y, x1, mod3, mod_row, group_tokens, group_slots, tc):
    e, n = pos.shape
    d = x1.shape[1]
    groups = n // group_tokens
    per = group_tokens // tc
    tok = lambda g, j, x: (g * per + j, 0)
    return pl.pallas_call(
        _combine_kernel,
        name=f"combine_n{n}", grid=(groups, per, e),
        in_specs=[pl.BlockSpec((e, tc), lambda g, j, x: (0, g * per + j)),
                  pl.BlockSpec((tc, LANES), tok),
                  pl.BlockSpec((1, group_slots, d), lambda g, j, x: (x, g, 0)),
                  pl.BlockSpec((tc, d), tok),
                  pl.BlockSpec((1, 1, mod3.shape[2]), lambda g, j, x: (mod_row(g), 0, 0))],
        out_specs=pl.BlockSpec((tc, d), tok),
        out_shape=jax.ShapeDtypeStruct((n, d), F32),
        compiler_params=_params(("parallel", "parallel", "arbitrary")),
    )(pos, afftok, y, x1, mod3)


def _rope_tables(t_len):
    n_freq = DA_DQK // 4
    inv = ROPE_BASE ** (-jnp.arange(n_freq, dtype=F32) / n_freq)
    t = jnp.arange(t_len)
    pos = jnp.stack([(t // GRID_W).astype(F32), (t % GRID_W).astype(F32)], axis=1)
    ang = pos[:, :, None, None] * inv[None, None, None, :]
    ang = jnp.broadcast_to(ang, (t_len, 2, 2, n_freq))
    sign = jnp.array([-1.0, 1.0], F32)[None, None, :, None]
    cos = jnp.cos(ang).reshape(t_len, DA_DQK)
    sin = (jnp.sin(ang) * sign).reshape(t_len, DA_DQK)
    reps = SEG // DA_DQK
    return jnp.tile(cos, (1, reps)), jnp.tile(sin, (1, reps))


def _trunk(x, batch, mod3, mod_row_tok, mod_row_grp, weights, ctx_k, ctx_v, s0, rope, group_tokens):
    (norm1_w, norm2_w, w_in, b_gate, lb_logits, hgrn_norm_w, qkw, gm, lam_p, subln_w,
     w_bh, w_ba, w_out, rw_cat, w_eg, w_eu, w_ed) = weights
    n, d = x.shape
    t_len = n // batch
    latent = rope is not None
    tm = 256
    (q_h, bf, bb, kff, kfb, i_h, g_h, dq, dk, dv, gates) = _premix(
        x, mod3, functools.partial(mod_row_tok, tm=tm), norm1_w, w_in, b_gate, lb_logits, qkw, gm,
        rope, tm, BF16 if latent else F32, latent)
    o_h, s_new = _hgrn(q_h, bf, bb, kff, kfb, i_h, g_h, hgrn_norm_w, s0, batch)
    o_a = _attention(dq, dk, dv, ctx_k, ctx_v, lam_p, subln_w, batch, min(t_len, 256 * ATT_SPLIT))
    x1, h2, afft, afftok = _postmix(o_h, o_a, gates, x, mod3, functools.partial(mod_row_tok, tm=tm),
                                    w_bh, w_ba, w_out, norm2_w, rw_cat, tm)
    cap = EC_CAPACITY * t_len // N_EXPERTS
    req_per_group = group_tokens // t_len
    group_slots = cap * req_per_group
    pos = _route(afft, batch, cap, req_per_group)
    xg = _dispatch(pos, h2, group_tokens, group_slots)
    y = _experts(xg, w_eg, w_eu, w_ed, group_slots)
    out = _combine(pos, afftok, y, x1, mod3, mod_row_grp, group_tokens, group_slots, 1024)
    return out, dk, dv, s_new


def kernel(x_prompt, x_sample, cache_k, cache_v, state_hgrn, c, c_ctx, norm1_w, norm2_w, w_mod,
           b_mod, w_in, b_gate, hgrn_lb_logits, hgrn_norm_w, qk_norm_w, diff_lambda, diff_subln_w,
           w_branch_hgrn, w_branch_attn, w_out, router_w, w_exp_gate, w_exp_up, w_exp_down):
    batch, seq, d = x_prompt.shape
    dec_batch, dec_seq, _ = x_sample.shape
    past = cache_k.shape[2]
    depth = w_in.shape[0]
    assert depth == 1
    group_tokens = dec_seq
    assert group_tokens % seq == 0 and (batch * seq) % group_tokens == 0
    l = 0

    rows = -(-(1 + dec_batch) // 8) * 8
    cond = jnp.zeros((rows, d), F32).at[0].set(c_ctx).at[1:1 + dec_batch].set(c)
    mod = _modulation(cond, w_mod[l], b_mod[l])
    mod3 = mod.reshape(rows, 1, 6 * d)

    gidx = jnp.arange(SEG) // DA_DQK
    gm = (gidx[:, None] == gidx[None, :]).astype(BF16) * (1.0 / DA_DQK)
    qkw = jnp.tile(qk_norm_w[l], (1, SEG // DA_DQK))
    rw_t = router_w[l].T
    rw_hi = rw_t.astype(BF16)
    rw_cat = jnp.concatenate([rw_hi, (rw_t - rw_hi.astype(F32)).astype(BF16)], axis=0)
    weights = (norm1_w[l][None], norm2_w[l][None], w_in[l].astype(BF16), b_gate[l][None],
               hgrn_lb_logits.reshape(4, SEG), hgrn_norm_w[l], qkw, gm, diff_lambda[l],
               diff_subln_w[l][None], w_branch_hgrn[l].astype(BF16),
               w_branch_attn[l].astype(BF16), w_out[l].astype(BF16),
               rw_cat, w_exp_gate[l], w_exp_up[l], w_exp_down[l])

    yp, k_new, v_new, s_new = _trunk(
        x_prompt.reshape(batch * seq, d), batch, mod3,
        lambda i, tm: 0, lambda g: 0, weights, None, None, None, None, group_tokens)
    per_req = dec_seq
    ys, _, _, _ = _trunk(
        x_sample.reshape(dec_batch * dec_seq, d), dec_batch, mod3,
        lambda i, tm: 1 + (i * tm) // per_req, lambda g: 1 + g, weights,
        cache_k[:, l].reshape(dec_batch, past, SEG), cache_v[:, l].reshape(dec_batch, past, SEG),
        state_hgrn[:, l], _rope_tables(dec_seq), group_tokens)

    return (yp.reshape(batch, seq, d), ys.reshape(dec_batch, dec_seq, d),
            k_new.reshape(batch, 1, seq, DA_HEADS, 2, DA_DQK),
            v_new.reshape(batch, 1, seq, DA_HEADS, DA_DV),
            s_new.reshape(batch, 1, 2, HG_HEADS, HG_DK, HG_DV))
```

```python
import functools
import math

import jax
import jax.numpy as jnp
from jax import lax
from jax.experimental import pallas as pl
from jax.experimental.pallas import tpu as pltpu
from jax.experimental.pallas import tpu_sc as plsc

F32 = jnp.float32
BF16 = jnp.bfloat16
I32 = jnp.int32

EPS = 1e-6
GRID_W = 64
HG_HEADS = 4
HG_DK = 128
HG_DV = 128
HG_CHUNK_LOG2 = 7
HG_CHUNK = 1 << HG_CHUNK_LOG2
HG_UNROLL = 4
DA_HEADS = 4
DA_DQK = 64
DA_DV = 128
N_EXPERTS = 16
EC_CAPACITY = 2
ROPE_BASE = 10000.0
SEG = 512
N_SEG = 12
LAM_INIT = 0.8 - 0.6 * math.exp(-0.3 * 0)
LANES = 128
ONES_ROWS = 16
ATT_SPLIT = 2
Q_SCALE = DA_DQK ** -0.5 * math.log2(math.e)
HG_SAFE_DECAY = 80.0
VMEM_LIMIT = 56 * 1024 * 1024
SC_CORES = 2
SC_SUBCORES = 16
SC_LANES = 16
SC_GATHER_ROWS = 128


def _dot(a, b):
    return jnp.dot(a, b, preferred_element_type=F32)


def _dot_nt(a, b):
    return lax.dot_general(a, b, (((1,), (1,)), ((), ())), preferred_element_type=F32)


def _dot_tn(a, b):
    return lax.dot_general(a, b, (((0,), (0,)), ((), ())), preferred_element_type=F32)


def _split2(x):
    hi = x.astype(BF16)
    lo = (x - hi.astype(F32)).astype(BF16)
    return hi, lo


def _silu(x):
    return x * jax.nn.sigmoid(x)


def _pack_bf16_pairs(x):
    w = x.shape[1] // 2
    bits = pltpu.bitcast(x.astype(BF16).astype(F32), I32)
    return lax.shift_right_logical(bits[:, :w], 16) | bits[:, w:]


def _unpack_bf16_pairs(words):
    lo = pltpu.bitcast(words << 16, F32)
    hi = pltpu.bitcast(words & jnp.int32(-65536), F32)
    return jnp.concatenate([lo, hi], axis=1).astype(BF16)


def _params(sem):
    return pltpu.CompilerParams(dimension_semantics=sem, vmem_limit_bytes=VMEM_LIMIT)


def _mod_kernel(c_ref, w_ref, b_ref, o_ref):
    s_hi, s_lo = _split2(_silu(c_ref[...]))
    w_hi, w_lo = _split2(w_ref[...])
    o_ref[...] = _dot(s_hi, w_hi) + _dot(s_hi, w_lo) + _dot(s_lo, w_hi) + b_ref[...]


def _modulation(cond, w_mod, b_mod):
    rows, d = cond.shape
    n = w_mod.shape[1]
    bn = 512
    return pl.pallas_call(
        _mod_kernel,
        name="modulation", grid=(n // bn,),
        in_specs=[pl.BlockSpec((rows, d), lambda j: (0, 0)),
                  pl.BlockSpec((d, bn), lambda j: (0, j)),
                  pl.BlockSpec((1, bn), lambda j: (0, j))],
        out_specs=pl.BlockSpec((rows, bn), lambda j: (0, j)),
        out_shape=jax.ShapeDtypeStruct((rows, n), F32),
        compiler_params=_params(("arbitrary",)),
    )(cond, w_mod, b_mod.reshape(1, n))


def _group_rms(z, gm_ref, w):
    ms = _dot((z * z).astype(BF16), gm_ref[...])
    return z * lax.rsqrt(ms + EPS) * w


def _rope(x, cos, sin_signed):
    n = x.shape[-1]
    lane = lax.broadcasted_iota(I32, x.shape, 1)
    partner = jnp.where((lane & 16) == 0, pltpu.roll(x, n - 16, 1), pltpu.roll(x, 16, 1))
    return x * cos + partner * sin_signed


def _premix_kernel(*refs, latent):
    if latent:
        (x_ref, mod_ref, n1_ref, win_ref, bg_ref, lbl_ref, qkw_ref, gm_ref, cos_ref, sin_ref,
         q_o, bf_o, bb_o, kff_o, kfb_o, i_o, g_o, dq_o, dk_o, dv_o, gate_o) = refs
    else:
        (x_ref, mod_ref, n1_ref, win_ref, bg_ref, lbl_ref, qkw_ref, gm_ref,
         q_o, bf_o, bb_o, kff_o, kfb_o, i_o, g_o, dq_o, dk_o, dv_o, gate_o) = refs
    d = x_ref.shape[1]
    mod = mod_ref[0]
    sh1, sc1 = mod[:, 0:d], mod[:, d:2 * d]
    x = x_ref[...]
    xn = x * lax.rsqrt(jnp.mean(x * x, axis=-1, keepdims=True) + EPS) * n1_ref[...]
    hb = (xn * (1.0 + sc1) + sh1).astype(BF16)

    def seg(j):
        return _dot(hb, win_ref[:, j * SEG:(j + 1) * SEG])

    def lower_bound(direction):
        l0 = lbl_ref[2 * direction:2 * direction + 1, :]
        l1 = lbl_ref[2 * direction + 1:2 * direction + 2, :]
        mx = jnp.maximum(l0, l1)
        e0, e1 = jnp.exp(l0 - mx), jnp.exp(l1 - mx)
        return e0 / (e0 + e1)

    tm = x.shape[0]
    row = lax.broadcasted_iota(I32, (tm, tm), 0)
    col = lax.broadcasted_iota(I32, (tm, tm), 1)
    same_chunk = (row >> HG_CHUNK_LOG2) == (col >> HG_CHUNK_LOG2)

    def store(o_ref, val):
        val = val.astype(o_ref.dtype)
        if len(o_ref.shape) == 2:
            o_ref[...] = val
        else:
            for h in range(o_ref.shape[0]):
                o_ref[h] = val[:, h * LANES:(h + 1) * LANES]

    store(q_o, _silu(seg(0)))
    for j, b_o, kf_o, order in ((1, bf_o, kff_o, row >= col), (2, bb_o, kfb_o, row <= col)):
        lbd = lower_bound(j - 1)
        f = lbd + (1.0 - lbd) * jax.nn.sigmoid(seg(j))
        tri = (same_chunk & order).astype(BF16)
        hi, lo = _split2(jnp.log(f))
        store(b_o, _dot(tri, hi) + _dot(tri, lo))
        store(kf_o, 1.0 - f)
    store(i_o, seg(3))
    store(g_o, _silu(seg(4)))
    qn = _group_rms(seg(5), gm_ref, qkw_ref[0:1, :]) * Q_SCALE
    kn = _group_rms(seg(6), gm_ref, qkw_ref[1:2, :])
    if latent:
        qn = _rope(qn, cos_ref[...], sin_ref[...])
        kn = _rope(kn, cos_ref[...], sin_ref[...])
    store(dq_o, qn)
    store(dk_o, kn)
    store(dv_o, seg(7))
    for j in range(4):
        z = seg(8 + j) + bg_ref[:, j * SEG:(j + 1) * SEG]
        gate_o[:, j * SEG:(j + 1) * SEG] = jax.nn.sigmoid(z).astype(gate_o.dtype)


def _premix(x, mod3, mod_row, norm1_w, w_in, b_gate, lb_logits, qkw, gm, rope, tm, kv_dtype,
            kv_head_major):
    n, d = x.shape
    latent = rope is not None
    const = lambda i: (0, 0)
    in_specs = [pl.BlockSpec((tm, d), lambda i: (i, 0)),
                pl.BlockSpec((1, 1, mod3.shape[2]), lambda i: (mod_row(i), 0, 0)),
                pl.BlockSpec((1, d), const),
                pl.BlockSpec(w_in.shape, const),
                pl.BlockSpec(b_gate.shape, const),
                pl.BlockSpec(lb_logits.shape, const),
                pl.BlockSpec(qkw.shape, const),
                pl.BlockSpec(gm.shape, const)]
    args = [x, mod3, norm1_w, w_in, b_gate, lb_logits, qkw, gm]
    if latent:
        cos, sin = rope
        nblk = cos.shape[0] // tm
        in_specs += [pl.BlockSpec((tm, SEG), lambda i: (i % nblk, 0))] * 2
        args += [cos, sin]
    heads = SEG // LANES
    head_spec = pl.BlockSpec((heads, tm, LANES), lambda i: (0, i, 0))
    tok_spec = pl.BlockSpec((tm, SEG), lambda i: (i, 0))
    kv_spec = head_spec if kv_head_major else tok_spec
    kv_shape = (heads, n, LANES) if kv_head_major else (n, SEG)
    out_dtypes = [BF16, F32, F32, BF16, BF16, BF16, BF16, BF16]
    out_shape = [jax.ShapeDtypeStruct((heads, n, LANES), t) for t in out_dtypes]
    out_shape += [jax.ShapeDtypeStruct(kv_shape, kv_dtype)] * 2
    out_shape.append(jax.ShapeDtypeStruct((n, 4 * SEG), BF16))
    out_specs = [head_spec] * 8 + [kv_spec] * 2 + [pl.BlockSpec((tm, 4 * SEG), lambda i: (i, 0))]
    return pl.pallas_call(
        functools.partial(_premix_kernel, latent=latent),
        name=f"premix_n{n}", grid=(n // tm,),
        in_specs=in_specs, out_specs=out_specs, out_shape=out_shape,
        compiler_params=_params(("parallel",)),
    )(*args)


def _hgrn_chunk_local(q, b, total, k, v, keep, safe):
    if safe:
        ref = b[HG_CHUNK // 2:HG_CHUNK // 2 + 1, :]
        qa = q * jnp.exp(b - ref)
        kb = k * jnp.exp(ref - b)
        attn = jnp.where(keep, _dot_nt(qa.astype(BF16), kb.astype(BF16)), 0.0)
        qe = qa * jnp.exp(ref)
        kd = kb * jnp.exp(total - ref)
    else:
        qe = q * jnp.exp(b)
        kd = k * jnp.exp(total - b)
        col = lax.broadcasted_iota(I32, (HG_CHUNK, HG_CHUNK), 1)

        def column(s, acc):
            onehot = (lax.broadcasted_iota(I32, (HG_CHUNK, 1), 0) == s).astype(F32)
            bs = jnp.sum(b * onehot, axis=0, keepdims=True)
            ks = jnp.sum(k * onehot, axis=0, keepdims=True)
            w = jnp.sum(q * ks * jnp.exp(jnp.minimum(b - bs, 0.0)), axis=1, keepdims=True)
            return jnp.where(col == s, w, acc)

        attn = lax.fori_loop(0, HG_CHUNK, column, jnp.zeros((HG_CHUNK, HG_CHUNK), F32))
        attn = jnp.where(keep, attn, 0.0)
    vt = v.astype(F32).T.astype(BF16)
    lhs = jnp.concatenate([qe.astype(BF16), attn.astype(BF16)], axis=1)
    return lhs, vt, _dot(vt, kd.astype(BF16)), jnp.exp(total)


def _hgrn_scan_group(chunks, st):
    outs = []
    for lhs, vt, inc, decay in chunks:
        outs.append(_dot_nt(lhs, jnp.concatenate([st.astype(BF16), vt], axis=1)))
        st = st * decay + inc
    return outs, st


def _hgrn_kernel(*refs, has_state):
    if has_state:
        (q_ref, bf_ref, bb_ref, kff_ref, kfb_ref, i_ref, g_ref, nw_ref, s0_ref,
         o_ref, sn_ref, of_s, ob_s) = refs
    else:
        (q_ref, bf_ref, bb_ref, kff_ref, kfb_ref, i_ref, g_ref, nw_ref,
         o_ref, sn_ref, of_s, ob_s) = refs
    t_len = q_ref.shape[1]
    nc = t_len // HG_CHUNK
    unroll = min(HG_UNROLL, nc)
    head = pl.program_id(1)
    row = lax.broadcasted_iota(I32, (HG_CHUNK, HG_CHUNK), 0)
    col = lax.broadcasted_iota(I32, (HG_CHUNK, HG_CHUNK), 1)
    keep_f, keep_b = row >= col, row <= col

    mid_f = bf_ref[0, pl.ds(HG_CHUNK // 2, nc, stride=HG_CHUNK), :]
    tot_f = bf_ref[0, pl.ds(HG_CHUNK - 1, nc, stride=HG_CHUNK), :]
    mid_b = bb_ref[0, pl.ds(HG_CHUNK // 2, nc, stride=HG_CHUNK), :]
    tot_b = bb_ref[0, pl.ds(0, nc, stride=HG_CHUNK), :]
    worst = jnp.minimum(jnp.min(jnp.minimum(mid_f, tot_f - mid_f)),
                        jnp.min(jnp.minimum(mid_b, tot_b - mid_b)))

    if has_state:
        st_f0, st_b0 = s0_ref[0, 0, 0].T, s0_ref[0, 1, 0].T
    else:
        st_f0 = st_b0 = jnp.zeros((HG_DV, HG_DK), F32)

    def scan(safe):
        def body(it, carry):
            st_f, st_b = carry
            rows_f, rows_b, loc_f, loc_b = [], [], [], []
            for u in range(unroll):
                c = it * unroll + u
                sf = pl.multiple_of(c * HG_CHUNK, HG_CHUNK)
                sb = pl.multiple_of((nc - 1 - c) * HG_CHUNK, HG_CHUNK)
                rf, rb = pl.ds(sf, HG_CHUNK), pl.ds(sb, HG_CHUNK)
                rows_f.append(rf)
                rows_b.append(rb)
                loc_f.append(_hgrn_chunk_local(
                    q_ref[0, rf, :].astype(F32), bf_ref[0, rf, :],
                    bf_ref[0, pl.ds(sf + HG_CHUNK - 1, 1), :],
                    kff_ref[0, rf, :].astype(F32), i_ref[0, rf, :], keep_f, safe))
                loc_b.append(_hgrn_chunk_local(
                    q_ref[0, rb, :].astype(F32), bb_ref[0, rb, :], bb_ref[0, pl.ds(sb, 1), :],
                    kfb_ref[0, rb, :].astype(F32), i_ref[0, rb, :], keep_b, safe))
            outs_f, st_f = _hgrn_scan_group(loc_f, st_f)
            outs_b, st_b = _hgrn_scan_group(loc_b, st_b)
            for rf, rb, o_f, o_b in zip(rows_f, rows_b, outs_f, outs_b):
                of_s[rf, :] = o_f
                ob_s[rb, :] = o_b
            return st_f, st_b
        return lax.fori_loop(0, nc // unroll, body, (st_f0, st_b0))

    st_f, st_b = lax.cond(worst >= -HG_SAFE_DECAY, lambda: scan(True), lambda: scan(False))
    sn_ref[0, 0, 0] = st_f.T
    sn_ref[0, 1, 0] = st_b.T
    o = of_s[...] + ob_s[...]
    nw = nw_ref[pl.ds(head, 1), :]
    on = o * lax.rsqrt(jnp.mean(o * o, axis=-1, keepdims=True) + EPS) * nw
    o_ref[0] = (on * g_ref[0].astype(F32)).astype(o_ref.dtype)


def _hgrn(q, bf, bb, kff, kfb, iv, g, norm_w, s0, batch):
    heads, n, _ = q.shape
    t_len = n // batch
    blk = pl.BlockSpec((1, t_len, HG_DK), lambda b, h: (h, b, 0))
    st_blk = pl.BlockSpec((1, 2, 1, HG_DK, HG_DV), lambda b, h: (b, 0, h, 0, 0))
    args = [q, bf, bb, kff, kfb, iv, g, norm_w]
    in_specs = [blk] * 7 + [pl.BlockSpec(norm_w.shape, lambda b, h: (0, 0))]
    if s0 is not None:
        args.append(s0)
        in_specs.append(st_blk)
    o, s_new = pl.pallas_call(
        functools.partial(_hgrn_kernel, has_state=s0 is not None),
        name=f"hgrn_n{n}", grid=(batch, HG_HEADS),
        in_specs=in_specs,
        out_specs=[blk, st_blk],
        out_shape=[jax.ShapeDtypeStruct((heads, n, HG_DV), BF16),
                   jax.ShapeDtypeStruct((batch, 2, HG_HEADS, HG_DK, HG_DV), F32)],
        scratch_shapes=[pltpu.VMEM((t_len, HG_DV), F32), pltpu.VMEM((t_len, HG_DV), F32)],
        compiler_params=_params(("parallel", "parallel")),
    )(*args)
    return o, s_new


def _attn_kernel(*refs, has_ctx):
    if has_ctx:
        q_ref, k_ref, v_ref, ck_ref, cv_ref, lam_ref, sw_ref, o_ref, k_s, vt_s = refs
    else:
        q_ref, k_ref, v_ref, lam_ref, sw_ref, o_ref, k_s, vt_s = refs
    t_own = k_ref.shape[1]

    @pl.when(pl.program_id(2) == 0)
    def _():
        k_s[0:t_own, :] = k_ref[0].astype(BF16)
        vt_s[0:DA_DV, 0:t_own] = v_ref[0].astype(F32).T.astype(BF16)
        if has_ctx:
            k_s[t_own:, :] = ck_ref[0].astype(BF16)
            vt_s[0:DA_DV, t_own:] = cv_ref[0].T.astype(BF16)
        vt_s[DA_DV:, :] = jnp.ones((ONES_ROWS, vt_s.shape[1]), BF16)

    lv = lam_ref[...]
    lam = (jnp.exp(jnp.sum(lv[0:1] * lv[1:2], keepdims=True))
           - jnp.exp(jnp.sum(lv[2:3] * lv[3:4], keepdims=True)) + LAM_INIT)
    tq = q_ref.shape[1] // ATT_SPLIT
    dim = lax.broadcasted_iota(I32, (2 * DA_DQK, tq), 0)

    def scores(i):
        qt = q_ref[0, i * tq:(i + 1) * tq, :].astype(F32).T
        q_both = jnp.concatenate([jnp.where(dim < DA_DQK, qt, 0.0),
                                  jnp.where(dim >= DA_DQK, qt, 0.0)], axis=1).astype(BF16)
        return _dot(k_s[...], q_both)

    def finish(i, st):
        pt = jnp.exp2((st - jnp.max(st, axis=0, keepdims=True)).astype(BF16))
        r = _dot(vt_s[...], pt)
        r = r[0:DA_DV] * (1.0 / r[DA_DV:DA_DV + 1])
        o = (r[:, 0:tq] - lam * r[:, tq:2 * tq]).T
        on = o * lax.rsqrt(jnp.mean(o * o, axis=-1, keepdims=True) + EPS) * sw_ref[...]
        o_ref[0, i * tq:(i + 1) * tq, :] = (on * (1.0 - LAM_INIT)).astype(o_ref.dtype)

    sts = [scores(i) for i in range(ATT_SPLIT)]
    for i in range(ATT_SPLIT):
        finish(i, sts[i])


def _attention(q, k, v, ctx_k, ctx_v, lam_p, subln_w, batch, tq):
    heads, n, _ = q.shape
    t_len = n // batch
    nq = t_len // tq
    q_blk = pl.BlockSpec((1, tq, DA_DV), lambda b, h, i: (h, b * nq + i, 0))
    const = lambda b, h, i: (0, 0)
    if k.ndim == 3:
        kv_blk = pl.BlockSpec((1, t_len, DA_DV), lambda b, h, i: (h, b, 0))
        args = [q, k, v]
    else:
        kv_blk = pl.BlockSpec((1, t_len, DA_DV), lambda b, h, i: (b, 0, h))
        args = [q, k.reshape(batch, t_len, SEG), v.reshape(batch, t_len, SEG)]
    in_specs = [q_blk, kv_blk, kv_blk]
    n_keys = t_len
    if ctx_k is not None:
        past = ctx_k.shape[1]
        n_keys += past
        c_blk = pl.BlockSpec((1, past, DA_DV), lambda b, h, i: (b, 0, h))
        args += [ctx_k, ctx_v]
        in_specs += [c_blk, c_blk]
    args += [lam_p, subln_w]
    in_specs += [pl.BlockSpec(lam_p.shape, const), pl.BlockSpec(subln_w.shape, const)]
    o = pl.pallas_call(
        functools.partial(_attn_kernel, has_ctx=ctx_k is not None),
        name=f"attn_n{n}", grid=(batch, DA_HEADS, nq),
        in_specs=in_specs, out_specs=q_blk,
        out_shape=jax.ShapeDtypeStruct((heads, n, DA_DV), BF16),
        scratch_shapes=[pltpu.VMEM((n_keys, 2 * DA_DQK), BF16),
                        pltpu.VMEM((DA_DV + ONES_ROWS, n_keys), BF16)],
        compiler_params=_params(("parallel", "parallel", "arbitrary")),
    )(*args)
    return o


def _postmix_kernel(oh_ref, oa_ref, gate_ref, x_ref, mod_ref, wbh_ref, wba_ref, wout_ref,
                    n2_ref, rw_ref, x1_o, h2_o, afft_o, afftok_o):
    d = x_ref.shape[1]
    mod = mod_ref[0]
    g1, sh2, sc2 = mod[:, 2 * d:3 * d], mod[:, 3 * d:4 * d], mod[:, 4 * d:5 * d]
    g_h = gate_ref[:, 0:d].astype(F32)
    g_a = gate_ref[:, d:2 * d].astype(F32)

    def heads_on_lanes(ref):
        return jnp.concatenate([ref[h] for h in range(ref.shape[0])], axis=1)

    merged = (g_h * _dot(heads_on_lanes(oh_ref), wbh_ref[...])
              + g_a * _dot(heads_on_lanes(oa_ref), wba_ref[...]))
    x1 = x_ref[...] + g1 * _dot(merged.astype(BF16), wout_ref[...])
    x1_o[...] = x1
    xn = x1 * lax.rsqrt(jnp.mean(x1 * x1, axis=-1, keepdims=True) + EPS) * n2_ref[...]
    h2 = xn * (1.0 + sc2) + sh2
    h2_o[...] = _pack_bf16_pairs(h2)
    h_hi, h_lo = _split2(h2)
    rw = rw_ref[...]
    t1 = _dot_nt(rw, h_hi)
    t2 = _dot_nt(rw, h_lo)
    e = N_EXPERTS
    logits = t1[0:e] + t1[e:2 * e] + t2[0:e]
    mx = jnp.max(logits, axis=0, keepdims=True)
    p = jnp.exp(logits - mx)
    aff = p / jnp.sum(p, axis=0, keepdims=True)
    afft_o[...] = aff
    pad = jnp.zeros((LANES - e, aff.shape[1]), F32)
    afftok_o[...] = jnp.concatenate([aff, pad], axis=0).T


def _postmix(o_h, o_a, gates, x, mod3, mod_row, w_bh, w_ba, w_out, norm2_w, rw_cat, tm):
    n, d = x.shape
    const = lambda i: (0, 0)
    row = lambda i: (i, 0)
    return pl.pallas_call(
        _postmix_kernel,
        name=f"postmix_n{n}", grid=(n // tm,),
        in_specs=[pl.BlockSpec((o_h.shape[0], tm, LANES), lambda i: (0, i, 0)),
                  pl.BlockSpec((o_a.shape[0], tm, LANES), lambda i: (0, i, 0)),
                  pl.BlockSpec((tm, 4 * SEG), row), pl.BlockSpec((tm, d), row),
                  pl.BlockSpec((1, 1, mod3.shape[2]), lambda i: (mod_row(i), 0, 0)),
                  pl.BlockSpec(w_bh.shape, const), pl.BlockSpec(w_ba.shape, const),
                  pl.BlockSpec(w_out.shape, const), pl.BlockSpec((1, d), const),
                  pl.BlockSpec(rw_cat.shape, const)],
        out_specs=[pl.BlockSpec((tm, d), row), pl.BlockSpec((tm, d // 2), row),
                   pl.BlockSpec((N_EXPERTS, tm), lambda i: (0, i)),
                   pl.BlockSpec((tm, LANES), row)],
        out_shape=[jax.ShapeDtypeStruct((n, d), F32), jax.ShapeDtypeStruct((n, d // 2), I32),
                   jax.ShapeDtypeStruct((N_EXPERTS, n), F32),
                   jax.ShapeDtypeStruct((n, LANES), F32)],
        compiler_params=_params(("parallel",)),
    )(o_h, o_a, gates, x, mod3, w_bh, w_ba, w_out, norm2_w, rw_cat)


def _lane_cumsum_exclusive(x, blk):
    e, t = x.shape
    r = lax.broadcasted_iota(I32, (blk, blk), 0)
    c = lax.broadcasted_iota(I32, (blk, blk), 1)
    upper = (r < c).astype(BF16)
    carry = jnp.zeros((e, 1), F32)
    parts = []
    for j in range(t // blk):
        xb = x[:, j * blk:(j + 1) * blk]
        parts.append(_dot(xb.astype(BF16), upper) + carry)
        carry = carry + jnp.sum(xb, axis=1, keepdims=True)
    return parts[0] if len(parts) == 1 else jnp.concatenate(parts, axis=1)


def _route_kernel(aff_ref, pos_ref, *, cap, req_per_group):
    aff = aff_ref[...]
    bits = pltpu.bitcast(aff, I32)
    t_len = aff.shape[1]

    def count(mask):
        return jnp.sum(mask.astype(F32), axis=1, keepdims=True)

    def step(i, th):
        cand = th | (jnp.int32(1) << (30 - i))
        return jnp.where(count(bits >= cand) >= cap, cand, th)

    th = lax.fori_loop(0, 31, step, jnp.zeros((aff.shape[0], 1), I32))
    gt = bits > th
    eq = (bits == th).astype(F32)
    need = cap - count(gt)
    blk = min(t_len, 256)
    tie_rank = _lane_cumsum_exclusive(eq, blk)
    sel = jnp.where(gt, 1.0, jnp.where(tie_rank < need, eq, 0.0))
    slot = _lane_cumsum_exclusive(sel, blk)
    offset = (pl.program_id(0) % req_per_group) * cap
    pos_ref[...] = jnp.where(sel > 0.0, slot.astype(I32) + offset, -1)


def _route(afft, batch, cap, req_per_group):
    e, n = afft.shape
    t_len = n // batch
    blk = pl.BlockSpec((e, t_len), lambda b: (0, b))
    return pl.pallas_call(
        functools.partial(_route_kernel, cap=cap, req_per_group=req_per_group),
        name=f"route_n{n}", grid=(batch,), in_specs=[blk], out_specs=blk,
        out_shape=jax.ShapeDtypeStruct((e, n), I32),
        compiler_params=_params(("parallel",)),
    )(afft)


def _one_hot_rows(pos_row, n_slots):
    slot = lax.broadcasted_iota(I32, (n_slots, pos_row.shape[1]), 0)
    return (slot == pos_row).astype(BF16)


def _dispatch(parts, group_tokens, group_slots):
    n_exp = parts[0][0].shape[0]
    width = parts[0][1].shape[1]
    part_groups = [pos.shape[1] // group_tokens for pos, _ in parts]
    workers = SC_CORES * SC_SUBCORES
    chunks = group_slots // SC_GATHER_ROWS
    assert all(g * n_exp % workers == 0 for g in part_groups)
    assert group_slots % SC_GATHER_ROWS == 0 and group_tokens % SC_LANES == 0
    assert n_exp & (n_exp - 1) == 0
    exp_shift = n_exp.bit_length() - 1
    row_shift = SC_GATHER_ROWS.bit_length() - 1

    def body(*refs):
        ins, (out_hbm, pos_v, idx_v, rows_v, sem) = refs[:2 * len(parts)], refs[2 * len(parts):]
        wid = lax.axis_index("s") * SC_CORES + lax.axis_index("c")
        lane = lax.iota(I32, SC_LANES)
        group_base = 0
        for part, groups in enumerate(part_groups):
            pos_hbm, h_hbm = ins[2 * part], ins[2 * part + 1]
            for k in range(groups * n_exp // workers):
                pair = wid + workers * k
                g = pair >> exp_shift
                e = pair & (n_exp - 1)
                pltpu.sync_copy(pos_hbm.at[e, pl.ds(g * group_tokens, group_tokens)], pos_v)

                @pl.loop(0, group_tokens // SC_LANES)
                def _(i):
                    p = pos_v[pl.ds(i * SC_LANES, SC_LANES)]
                    tok = g * group_tokens + i * SC_LANES + lane
                    slot = jnp.maximum(p, 0)
                    plsc.store_scatter(idx_v, [slot >> row_shift, slot & (SC_GATHER_ROWS - 1)],
                                       tok, mask=p >= 0)

                for c in range(chunks):
                    row0 = (group_base + g) * group_slots + c * SC_GATHER_ROWS
                    pltpu.async_copy(h_hbm.at[idx_v.at[c]], rows_v, sem).wait()
                    pltpu.sync_copy(rows_v, out_hbm.at[e, pl.ds(row0, SC_GATHER_ROWS)])
            group_base += groups

    mesh = plsc.VectorSubcoreMesh(core_axis_name="c", subcore_axis_name="s",
                                  num_cores=SC_CORES, num_subcores=SC_SUBCORES)
    return pl.kernel(
        body,
        out_type=jax.ShapeDtypeStruct((n_exp, sum(part_groups) * group_slots, width), I32),
        mesh=mesh,
        scratch_types=[pltpu.VMEM((group_tokens,), I32), pltpu.VMEM((chunks, SC_GATHER_ROWS), I32),
                       pltpu.VMEM((SC_GATHER_ROWS, width), I32), pltpu.SemaphoreType.DMA],
        compiler_params=pltpu.CompilerParams(needs_layout_passes=False),
        name="dispatch",
    )(*[a for part in parts for a in part])


def _expert_kernel(xg_ref, wg_ref, wu_ref, wd_ref, y_ref, wg_s, wu_s, wd_s):
    @pl.when(pl.program_id(1) == 0)
    def _():
        wg_s[...] = wg_ref[0].astype(BF16)
        wu_s[...] = wu_ref[0].astype(BF16)
        wd_s[...] = wd_ref[0].astype(BF16)

    xg = _unpack_bf16_pairs(xg_ref[0])
    a = _silu(_dot(xg, wg_s[...])) * _dot(xg, wu_s[...])
    y_ref[0] = _dot(a.astype(BF16), wd_s[...]).astype(y_ref.dtype)


def _experts(xg, w_gate, w_up, w_down, ts):
    e, s, half = xg.shape
    d, f = w_gate.shape[1:]
    assert d == 2 * half
    x_blk = pl.BlockSpec((1, ts, d), lambda x, i: (x, i, 0))
    return pl.pallas_call(
        _expert_kernel,
        name=f"experts_s{s}", grid=(e, s // ts),
        in_specs=[pl.BlockSpec((1, ts, half), lambda x, i: (x, i, 0)),
                  pl.BlockSpec((1, d, f), lambda x, i: (x, 0, 0)),
                  pl.BlockSpec((1, d, f), lambda x, i: (x, 0, 0)),
                  pl.BlockSpec((1, f, d), lambda x, i: (x, 0, 0))],
        out_specs=x_blk,
        out_shape=jax.ShapeDtypeStruct((e, s, d), BF16),
        scratch_shapes=[pltpu.VMEM((d, f), BF16), pltpu.VMEM((d, f), BF16),
                        pltpu.VMEM((f, d), BF16)],
        compiler_params=_params(("parallel", "arbitrary")),
    )(xg, w_gate, w_up, w_down)


def _combine_kernel(pos_ref, afftok_ref, y_ref, x1_ref, mod_ref, o_ref):
    e = pl.program_id(2)
    d = x1_ref.shape[1]

    @pl.when(e == 0)
    def _():
        o_ref[...] = jnp.zeros_like(o_ref)

    p = _one_hot_rows(pos_ref[pl.ds(e, 1), :], y_ref.shape[1])
    lane = lax.broadcasted_iota(I32, afftok_ref.shape, 1)
    gate = jnp.sum(jnp.where(lane == e, afftok_ref[...], 0.0), axis=1, keepdims=True)
    o_ref[...] += gate * _dot_tn(p, y_ref[0])

    @pl.when(e == pl.num_programs(2) - 1)
    def _():
        g2 = mod_ref[0][:, 5 * d:6 * d]
        o_ref[...] = x1_ref[...] + g2 * o_ref[...]


def _combine(pos, afftok, y, x1, mod3, mod_row, group_tokens, group_slots, group_base, tc):
    e, n = pos.shape
    d = x1.shape[1]
    groups = n // group_tokens
    per = group_tokens // tc
    tok = lambda g, j, x: (g * per + j, 0)
    return pl.pallas_call(
        _combine_kernel,
        name=f"combine_n{n}", grid=(groups, per, e),
        in_specs=[pl.BlockSpec((e, tc), lambda g, j, x: (0, g * per + j)),
                  pl.BlockSpec((tc, LANES), tok),
                  pl.BlockSpec((1, group_slots, d), lambda g, j, x: (x, group_base + g, 0)),
                  pl.BlockSpec((tc, d), tok),
                  pl.BlockSpec((1, 1, mod3.shape[2]), lambda g, j, x: (mod_row(g), 0, 0))],
        out_specs=pl.BlockSpec((tc, d), tok),
        out_shape=jax.ShapeDtypeStruct((n, d), F32),
        compiler_params=_params(("parallel", "parallel", "arbitrary")),
    )(pos, afftok, y, x1, mod3)


def _rope_tables(t_len):
    n_freq = DA_DQK // 4
    inv = ROPE_BASE ** (-jnp.arange(n_freq, dtype=F32) / n_freq)
    t = jnp.arange(t_len)
    pos = jnp.stack([(t // GRID_W).astype(F32), (t % GRID_W).astype(F32)], axis=1)
    ang = pos[:, :, None, None] * inv[None, None, None, :]
    ang = jnp.broadcast_to(ang, (t_len, 2, 2, n_freq))
    sign = jnp.array([-1.0, 1.0], F32)[None, None, :, None]
    cos = jnp.cos(ang).reshape(t_len, DA_DQK)
    sin = (jnp.sin(ang) * sign).reshape(t_len, DA_DQK)
    reps = SEG // DA_DQK
    return jnp.tile(cos, (1, reps)), jnp.tile(sin, (1, reps))


def _trunk(x, batch, mod3, mod_row_tok, weights, ctx_k, ctx_v, s0, rope, group_tokens):
    (norm1_w, norm2_w, w_in, b_gate, lb_logits, hgrn_norm_w, qkw, gm, lam_p, subln_w,
     w_bh, w_ba, w_out, rw_cat) = weights
    n, d = x.shape
    t_len = n // batch
    latent = rope is not None
    tm = 256
    (q_h, bf, bb, kff, kfb, i_h, g_h, dq, dk, dv, gates) = _premix(
        x, mod3, functools.partial(mod_row_tok, tm=tm), norm1_w, w_in, b_gate, lb_logits, qkw, gm,
        rope, tm, BF16 if latent else F32, latent)
    o_h, s_new = _hgrn(q_h, bf, bb, kff, kfb, i_h, g_h, hgrn_norm_w, s0, batch)
    o_a = _attention(dq, dk, dv, ctx_k, ctx_v, lam_p, subln_w, batch, min(t_len, 256 * ATT_SPLIT))
    x1, h2, afft, afftok = _postmix(o_h, o_a, gates, x, mod3, functools.partial(mod_row_tok, tm=tm),
                                    w_bh, w_ba, w_out, norm2_w, rw_cat, tm)
    cap = EC_CAPACITY * t_len // N_EXPERTS
    pos = _route(afft, batch, cap, group_tokens // t_len)
    return (pos, h2, afftok, x1), dk, dv, s_new


def _moe(routed, mod3, mod_row_grps, w_eg, w_eu, w_ed, group_tokens, group_slots):
    xg = _dispatch([(pos, h2p) for pos, h2p, _, _ in routed], group_tokens, group_slots)
    y = _experts(xg, w_eg, w_eu, w_ed, group_slots)
    outs, group_base = [], 0
    for (pos, _, afftok, x1), mod_row in zip(routed, mod_row_grps):
        outs.append(_combine(pos, afftok, y, x1, mod3, mod_row, group_tokens, group_slots,
                             group_base, 1024))
        group_base += pos.shape[1] // group_tokens
    return outs


def kernel(x_prompt, x_sample, cache_k, cache_v, state_hgrn, c, c_ctx, norm1_w, norm2_w, w_mod,
           b_mod, w_in, b_gate, hgrn_lb_logits, hgrn_norm_w, qk_norm_w, diff_lambda, diff_subln_w,
           w_branch_hgrn, w_branch_attn, w_out, router_w, w_exp_gate, w_exp_up, w_exp_down):
    batch, seq, d = x_prompt.shape
    dec_batch, dec_seq, _ = x_sample.shape
    past = cache_k.shape[2]
    depth = w_in.shape[0]
    assert depth == 1
    group_tokens = dec_seq
    assert group_tokens % seq == 0 and (batch * seq) % group_tokens == 0
    l = 0

    rows = -(-(1 + dec_batch) // 8) * 8
    cond = jnp.zeros((rows, d), F32).at[0].set(c_ctx).at[1:1 + dec_batch].set(c)
    mod = _modulation(cond, w_mod[l], b_mod[l])
    mod3 = mod.reshape(rows, 1, 6 * d)

    gidx = jnp.arange(SEG) // DA_DQK
    gm = (gidx[:, None] == gidx[None, :]).astype(BF16) * (1.0 / DA_DQK)
    qkw = jnp.tile(qk_norm_w[l], (1, SEG // DA_DQK))
    rw_t = router_w[l].T
    rw_hi = rw_t.astype(BF16)
    rw_cat = jnp.concatenate([rw_hi, (rw_t - rw_hi.astype(F32)).astype(BF16)], axis=0)
    weights = (norm1_w[l][None], norm2_w[l][None], w_in[l].astype(BF16), b_gate[l][None],
               hgrn_lb_logits.reshape(4, SEG), hgrn_norm_w[l], qkw, gm, diff_lambda[l],
               diff_subln_w[l][None], w_branch_hgrn[l].astype(BF16),
               w_branch_attn[l].astype(BF16), w_out[l].astype(BF16),
               rw_cat)

    routed_ctx, k_new, v_new, s_new = _trunk(
        x_prompt.reshape(batch * seq, d), batch, mod3,
        lambda i, tm: 0, weights, None, None, None, None, group_tokens)
    per_req = dec_seq
    routed_lat, _, _, _ = _trunk(
        x_sample.reshape(dec_batch * dec_seq, d), dec_batch, mod3,
        lambda i, tm: 1 + (i * tm) // per_req, weights,
        cache_k[:, l].reshape(dec_batch, past, SEG), cache_v[:, l].reshape(dec_batch, past, SEG),
        state_hgrn[:, l], _rope_tables(dec_seq), group_tokens)
    group_slots = EC_CAPACITY * group_tokens // N_EXPERTS
    yp, ys = _moe([routed_ctx, routed_lat], mod3, [lambda g: 0, lambda g: 1 + g],
                  w_exp_gate[l], w_exp_up[l], w_exp_down[l], group_tokens, group_slots)

    return (yp.reshape(batch, seq, d), ys.reshape(dec_batch, dec_seq, d),
            k_new.reshape(batch, 1, seq, DA_HEADS, 2, DA_DQK),
            v_new.reshape(batch, 1, seq, DA_HEADS, DA_DV),
            s_new.reshape(batch, 1, 2, HG_HEADS, HG_DK, HG_DV))
```

```python
import functools
import math

import jax
import jax.numpy as jnp
from jax import lax
from jax.experimental import pallas as pl
from jax.experimental.pallas import tpu as pltpu
from jax.experimental.pallas import tpu_sc as plsc

F32 = jnp.float32
BF16 = jnp.bfloat16
I32 = jnp.int32

EPS = 1e-6
GRID_W = 64
HG_HEADS = 4
HG_DK = 128
HG_DV = 128
HG_CHUNK_LOG2 = 7
HG_CHUNK = 1 << HG_CHUNK_LOG2
HG_UNROLL = 4
DA_HEADS = 4
DA_DQK = 64
DA_DV = 128
N_EXPERTS = 16
EC_CAPACITY = 2
ROPE_BASE = 10000.0
SEG = 512
N_SEG = 12
LAM_INIT = 0.8 - 0.6 * math.exp(-0.3 * 0)
LANES = 128
ONES_ROWS = 16
ATT_SPLIT = 2
Q_SCALE = DA_DQK ** -0.5 * math.log2(math.e)
HG_SAFE_DECAY = 80.0
VMEM_LIMIT = 56 * 1024 * 1024
COMBINE_TOKENS = 512
COMBINE_WIN = 256
SC_CORES = 2
SC_SUBCORES = 16
SC_LANES = 16
SC_GATHER_ROWS = 128


def _dot(a, b):
    return jnp.dot(a, b, preferred_element_type=F32)


def _dot_nt(a, b):
    return lax.dot_general(a, b, (((1,), (1,)), ((), ())), preferred_element_type=F32)


def _dot_tn(a, b):
    return lax.dot_general(a, b, (((0,), (0,)), ((), ())), preferred_element_type=F32)


def _split2(x):
    hi = x.astype(BF16)
    lo = (x - hi.astype(F32)).astype(BF16)
    return hi, lo


def _silu(x):
    return x * jax.nn.sigmoid(x)


def _pack_bf16_pairs(x):
    w = x.shape[1] // 2
    bits = pltpu.bitcast(x.astype(BF16).astype(F32), I32)
    return lax.shift_right_logical(bits[:, :w], 16) | bits[:, w:]


def _unpack_bf16_pairs(words):
    lo = pltpu.bitcast(words << 16, F32)
    hi = pltpu.bitcast(words & jnp.int32(-65536), F32)
    return jnp.concatenate([lo, hi], axis=1).astype(BF16)


def _params(sem):
    return pltpu.CompilerParams(dimension_semantics=sem, vmem_limit_bytes=VMEM_LIMIT)


def _mod_kernel(c_ref, w_ref, b_ref, o_ref):
    s_hi, s_lo = _split2(_silu(c_ref[...]))
    w_hi, w_lo = _split2(w_ref[...])
    o_ref[...] = _dot(s_hi, w_hi) + _dot(s_hi, w_lo) + _dot(s_lo, w_hi) + b_ref[...]


def _modulation(cond, w_mod, b_mod):
    rows, d = cond.shape
    n = w_mod.shape[1]
    bn = 512
    return pl.pallas_call(
        _mod_kernel,
        name="modulation", grid=(n // bn,),
        in_specs=[pl.BlockSpec((rows, d), lambda j: (0, 0)),
                  pl.BlockSpec((d, bn), lambda j: (0, j)),
                  pl.BlockSpec((1, bn), lambda j: (0, j))],
        out_specs=pl.BlockSpec((rows, bn), lambda j: (0, j)),
        out_shape=jax.ShapeDtypeStruct((rows, n), F32),
        compiler_params=_params(("arbitrary",)),
    )(cond, w_mod, b_mod.reshape(1, n))


def _group_rms(z, gm_ref, w):
    ms = _dot((z * z).astype(BF16), gm_ref[...])
    return z * lax.rsqrt(ms + EPS) * w


def _rope(x, cos, sin_signed):
    n = x.shape[-1]
    lane = lax.broadcasted_iota(I32, x.shape, 1)
    partner = jnp.where((lane & 16) == 0, pltpu.roll(x, n - 16, 1), pltpu.roll(x, 16, 1))
    return x * cos + partner * sin_signed


def _premix_kernel(*refs, latent):
    if latent:
        (x_ref, mod_ref, n1_ref, win_ref, bg_ref, lbl_ref, qkw_ref, gm_ref, cos_ref, sin_ref,
         q_o, bf_o, bb_o, kff_o, kfb_o, i_o, g_o, dq_o, dk_o, dv_o, gate_o) = refs
    else:
        (x_ref, mod_ref, n1_ref, win_ref, bg_ref, lbl_ref, qkw_ref, gm_ref,
         q_o, bf_o, bb_o, kff_o, kfb_o, i_o, g_o, dq_o, dk_o, dv_o, gate_o) = refs
    d = x_ref.shape[1]
    mod = mod_ref[0]
    sh1, sc1 = mod[:, 0:d], mod[:, d:2 * d]
    x = x_ref[...]
    xn = x * lax.rsqrt(jnp.mean(x * x, axis=-1, keepdims=True) + EPS) * n1_ref[...]
    hb = (xn * (1.0 + sc1) + sh1).astype(BF16)

    def seg(j):
        return _dot(hb, win_ref[:, j * SEG:(j + 1) * SEG])

    def lower_bound(direction):
        l0 = lbl_ref[2 * direction:2 * direction + 1, :]
        l1 = lbl_ref[2 * direction + 1:2 * direction + 2, :]
        mx = jnp.maximum(l0, l1)
        e0, e1 = jnp.exp(l0 - mx), jnp.exp(l1 - mx)
        return e0 / (e0 + e1)

    tm = x.shape[0]
    row = lax.broadcasted_iota(I32, (tm, tm), 0)
    col = lax.broadcasted_iota(I32, (tm, tm), 1)
    same_chunk = (row >> HG_CHUNK_LOG2) == (col >> HG_CHUNK_LOG2)

    def store(o_ref, val):
        val = val.astype(o_ref.dtype)
        if len(o_ref.shape) == 2:
            o_ref[...] = val
        else:
            for h in range(o_ref.shape[0]):
                o_ref[h] = val[:, h * LANES:(h + 1) * LANES]

    store(q_o, _silu(seg(0)))
    for j, b_o, kf_o, order in ((1, bf_o, kff_o, row >= col), (2, bb_o, kfb_o, row <= col)):
        lbd = lower_bound(j - 1)
        f = lbd + (1.0 - lbd) * jax.nn.sigmoid(seg(j))
        tri = (same_chunk & order).astype(BF16)
        hi, lo = _split2(jnp.log(f))
        store(b_o, _dot(tri, hi) + _dot(tri, lo))
        store(kf_o, 1.0 - f)
    store(i_o, seg(3))
    store(g_o, _silu(seg(4)))
    qn = _group_rms(seg(5), gm_ref, qkw_ref[0:1, :]) * Q_SCALE
    kn = _group_rms(seg(6), gm_ref, qkw_ref[1:2, :])
    if latent:
        qn = _rope(qn, cos_ref[...], sin_ref[...])
        kn = _rope(kn, cos_ref[...], sin_ref[...])
    store(dq_o, qn)
    store(dk_o, kn)
    store(dv_o, seg(7))
    for j in range(4):
        z = seg(8 + j) + bg_ref[:, j * SEG:(j + 1) * SEG]
        gate_o[:, j * SEG:(j + 1) * SEG] = jax.nn.sigmoid(z).astype(gate_o.dtype)


def _premix(x, mod3, mod_row, norm1_w, w_in, b_gate, lb_logits, qkw, gm, rope, tm, kv_dtype,
            kv_head_major):
    n, d = x.shape
    latent = rope is not None
    const = lambda i: (0, 0)
    in_specs = [pl.BlockSpec((tm, d), lambda i: (i, 0)),
                pl.BlockSpec((1, 1, mod3.shape[2]), lambda i: (mod_row(i), 0, 0)),
                pl.BlockSpec((1, d), const),
                pl.BlockSpec(w_in.shape, const),
                pl.BlockSpec(b_gate.shape, const),
                pl.BlockSpec(lb_logits.shape, const),
                pl.BlockSpec(qkw.shape, const),
                pl.BlockSpec(gm.shape, const)]
    args = [x, mod3, norm1_w, w_in, b_gate, lb_logits, qkw, gm]
    if latent:
        cos, sin = rope
        nblk = cos.shape[0] // tm
        in_specs += [pl.BlockSpec((tm, SEG), lambda i: (i % nblk, 0))] * 2
        args += [cos, sin]
    heads = SEG // LANES
    head_spec = pl.BlockSpec((heads, tm, LANES), lambda i: (0, i, 0))
    tok_spec = pl.BlockSpec((tm, SEG), lambda i: (i, 0))
    kv_spec = head_spec if kv_head_major else tok_spec
    kv_shape = (heads, n, LANES) if kv_head_major else (n, SEG)
    out_dtypes = [BF16, F32, F32, BF16, BF16, BF16, BF16, BF16]
    out_shape = [jax.ShapeDtypeStruct((heads, n, LANES), t) for t in out_dtypes]
    out_shape += [jax.ShapeDtypeStruct(kv_shape, kv_dtype)] * 2
    out_shape.append(jax.ShapeDtypeStruct((n, 4 * SEG), BF16))
    out_specs = [head_spec] * 8 + [kv_spec] * 2 + [pl.BlockSpec((tm, 4 * SEG), lambda i: (i, 0))]
    return pl.pallas_call(
        functools.partial(_premix_kernel, latent=latent),
        name=f"premix_n{n}", grid=(n // tm,),
        in_specs=in_specs, out_specs=out_specs, out_shape=out_shape,
        compiler_params=_params(("parallel",)),
    )(*args)


def _hgrn_chunk_local(q, b, total, k, v, keep, safe):
    if safe:
        ref = b[HG_CHUNK // 2:HG_CHUNK // 2 + 1, :]
        qa = q * jnp.exp(b - ref)
        kb = k * jnp.exp(ref - b)
        attn = jnp.where(keep, _dot_nt(qa.astype(BF16), kb.astype(BF16)), 0.0)
        qe = qa * jnp.exp(ref)
        kd = kb * jnp.exp(total - ref)
    else:
        qe = q * jnp.exp(b)
        kd = k * jnp.exp(total - b)
        col = lax.broadcasted_iota(I32, (HG_CHUNK, HG_CHUNK), 1)

        def column(s, acc):
            onehot = (lax.broadcasted_iota(I32, (HG_CHUNK, 1), 0) == s).astype(F32)
            bs = jnp.sum(b * onehot, axis=0, keepdims=True)
            ks = jnp.sum(k * onehot, axis=0, keepdims=True)
            w = jnp.sum(q * ks * jnp.exp(jnp.minimum(b - bs, 0.0)), axis=1, keepdims=True)
            return jnp.where(col == s, w, acc)

        attn = lax.fori_loop(0, HG_CHUNK, column, jnp.zeros((HG_CHUNK, HG_CHUNK), F32))
        attn = jnp.where(keep, attn, 0.0)
    vt = v.astype(F32).T.astype(BF16)
    lhs = jnp.concatenate([qe.astype(BF16), attn.astype(BF16)], axis=1)
    return lhs, vt, _dot(vt, kd.astype(BF16)), jnp.exp(total)


def _hgrn_scan_group(chunks, st):
    outs = []
    for lhs, vt, inc, decay in chunks:
        outs.append(_dot_nt(lhs, jnp.concatenate([st.astype(BF16), vt], axis=1)))
        st = st * decay + inc
    return outs, st


def _hgrn_kernel(*refs, has_state):
    if has_state:
        (q_ref, bf_ref, bb_ref, kff_ref, kfb_ref, i_ref, g_ref, nw_ref, s0_ref,
         o_ref, sn_ref, of_s, ob_s) = refs
    else:
        (q_ref, bf_ref, bb_ref, kff_ref, kfb_ref, i_ref, g_ref, nw_ref,
         o_ref, sn_ref, of_s, ob_s) = refs
    t_len = q_ref.shape[1]
    nc = t_len // HG_CHUNK
    unroll = min(HG_UNROLL, nc)
    head = pl.program_id(1)
    row = lax.broadcasted_iota(I32, (HG_CHUNK, HG_CHUNK), 0)
    col = lax.broadcasted_iota(I32, (HG_CHUNK, HG_CHUNK), 1)
    keep_f, keep_b = row >= col, row <= col

    mid_f = bf_ref[0, pl.ds(HG_CHUNK // 2, nc, stride=HG_CHUNK), :]
    tot_f = bf_ref[0, pl.ds(HG_CHUNK - 1, nc, stride=HG_CHUNK), :]
    mid_b = bb_ref[0, pl.ds(HG_CHUNK // 2, nc, stride=HG_CHUNK), :]
    tot_b = bb_ref[0, pl.ds(0, nc, stride=HG_CHUNK), :]
    worst = jnp.minimum(jnp.min(jnp.minimum(mid_f, tot_f - mid_f)),
                        jnp.min(jnp.minimum(mid_b, tot_b - mid_b)))

    if has_state:
        st_f0, st_b0 = s0_ref[0, 0, 0].T, s0_ref[0, 1, 0].T
    else:
        st_f0 = st_b0 = jnp.zeros((HG_DV, HG_DK), F32)

    def scan(safe):
        def body(it, carry):
            st_f, st_b = carry
            rows_f, rows_b, loc_f, loc_b = [], [], [], []
            for u in range(unroll):
                c = it * unroll + u
                sf = pl.multiple_of(c * HG_CHUNK, HG_CHUNK)
                sb = pl.multiple_of((nc - 1 - c) * HG_CHUNK, HG_CHUNK)
                rf, rb = pl.ds(sf, HG_CHUNK), pl.ds(sb, HG_CHUNK)
                rows_f.append(rf)
                rows_b.append(rb)
                loc_f.append(_hgrn_chunk_local(
                    q_ref[0, rf, :].astype(F32), bf_ref[0, rf, :],
                    bf_ref[0, pl.ds(sf + HG_CHUNK - 1, 1), :],
                    kff_ref[0, rf, :].astype(F32), i_ref[0, rf, :], keep_f, safe))
                loc_b.append(_hgrn_chunk_local(
                    q_ref[0, rb, :].astype(F32), bb_ref[0, rb, :], bb_ref[0, pl.ds(sb, 1), :],
                    kfb_ref[0, rb, :].astype(F32), i_ref[0, rb, :], keep_b, safe))
            outs_f, st_f = _hgrn_scan_group(loc_f, st_f)
            outs_b, st_b = _hgrn_scan_group(loc_b, st_b)
            for rf, rb, o_f, o_b in zip(rows_f, rows_b, outs_f, outs_b):
                of_s[rf, :] = o_f
                ob_s[rb, :] = o_b
            return st_f, st_b
        return lax.fori_loop(0, nc // unroll, body, (st_f0, st_b0))

    st_f, st_b = lax.cond(worst >= -HG_SAFE_DECAY, lambda: scan(True), lambda: scan(False))
    sn_ref[0, 0, 0] = st_f.T
    sn_ref[0, 1, 0] = st_b.T
    o = of_s[...] + ob_s[...]
    nw = nw_ref[pl.ds(head, 1), :]
    on = o * lax.rsqrt(jnp.mean(o * o, axis=-1, keepdims=True) + EPS) * nw
    o_ref[0] = (on * g_ref[0].astype(F32)).astype(o_ref.dtype)


def _hgrn(q, bf, bb, kff, kfb, iv, g, norm_w, s0, batch):
    heads, n, _ = q.shape
    t_len = n // batch
    blk = pl.BlockSpec((1, t_len, HG_DK), lambda b, h: (h, b, 0))
    st_blk = pl.BlockSpec((1, 2, 1, HG_DK, HG_DV), lambda b, h: (b, 0, h, 0, 0))
    args = [q, bf, bb, kff, kfb, iv, g, norm_w]
    in_specs = [blk] * 7 + [pl.BlockSpec(norm_w.shape, lambda b, h: (0, 0))]
    if s0 is not None:
        args.append(s0)
        in_specs.append(st_blk)
    o, s_new = pl.pallas_call(
        functools.partial(_hgrn_kernel, has_state=s0 is not None),
        name=f"hgrn_n{n}", grid=(batch, HG_HEADS),
        in_specs=in_specs,
        out_specs=[blk, st_blk],
        out_shape=[jax.ShapeDtypeStruct((heads, n, HG_DV), BF16),
                   jax.ShapeDtypeStruct((batch, 2, HG_HEADS, HG_DK, HG_DV), F32)],
        scratch_shapes=[pltpu.VMEM((t_len, HG_DV), F32), pltpu.VMEM((t_len, HG_DV), F32)],
        compiler_params=_params(("parallel", "parallel")),
    )(*args)
    return o, s_new


def _attn_kernel(*refs, has_ctx):
    if has_ctx:
        q_ref, k_ref, v_ref, ck_ref, cv_ref, lam_ref, sw_ref, o_ref, k_s, vt_s = refs
    else:
        q_ref, k_ref, v_ref, lam_ref, sw_ref, o_ref, k_s, vt_s = refs
    t_own = k_ref.shape[1]

    @pl.when(pl.program_id(2) == 0)
    def _():
        k_s[0:t_own, :] = k_ref[0].astype(BF16)
        vt_s[0:DA_DV, 0:t_own] = v_ref[0].astype(F32).T.astype(BF16)
        if has_ctx:
            k_s[t_own:, :] = ck_ref[0].astype(BF16)
            vt_s[0:DA_DV, t_own:] = cv_ref[0].T.astype(BF16)
        vt_s[DA_DV:, :] = jnp.ones((ONES_ROWS, vt_s.shape[1]), BF16)

    lv = lam_ref[...]
    lam = (jnp.exp(jnp.sum(lv[0:1] * lv[1:2], keepdims=True))
           - jnp.exp(jnp.sum(lv[2:3] * lv[3:4], keepdims=True)) + LAM_INIT)
    tq = q_ref.shape[1] // ATT_SPLIT
    dim = lax.broadcasted_iota(I32, (2 * DA_DQK, tq), 0)

    def scores(i):
        qt = q_ref[0, i * tq:(i + 1) * tq, :].astype(F32).T
        q_both = jnp.concatenate([jnp.where(dim < DA_DQK, qt, 0.0),
                                  jnp.where(dim >= DA_DQK, qt, 0.0)], axis=1).astype(BF16)
        return _dot(k_s[...], q_both)

    def finish(i, st):
        pt = jnp.exp2(st - jnp.max(st, axis=0, keepdims=True)).astype(BF16)
        r = _dot(vt_s[...], pt)
        r = r[0:DA_DV] * (1.0 / r[DA_DV:DA_DV + 1])
        o = (r[:, 0:tq] - lam * r[:, tq:2 * tq]).T
        on = o * lax.rsqrt(jnp.mean(o * o, axis=-1, keepdims=True) + EPS) * sw_ref[...]
        o_ref[0, i * tq:(i + 1) * tq, :] = (on * (1.0 - LAM_INIT)).astype(o_ref.dtype)

    sts = [scores(i) for i in range(ATT_SPLIT)]
    for i in range(ATT_SPLIT):
        finish(i, sts[i])


def _attention(q, k, v, ctx_k, ctx_v, lam_p, subln_w, batch, tq):
    heads, n, _ = q.shape
    t_len = n // batch
    nq = t_len // tq
    q_blk = pl.BlockSpec((1, tq, DA_DV), lambda b, h, i: (h, b * nq + i, 0))
    const = lambda b, h, i: (0, 0)
    if k.ndim == 3:
        kv_blk = pl.BlockSpec((1, t_len, DA_DV), lambda b, h, i: (h, b, 0))
        args = [q, k, v]
    else:
        kv_blk = pl.BlockSpec((1, t_len, DA_DV), lambda b, h, i: (b, 0, h))
        args = [q, k.reshape(batch, t_len, SEG), v.reshape(batch, t_len, SEG)]
    in_specs = [q_blk, kv_blk, kv_blk]
    n_keys = t_len
    if ctx_k is not None:
        past = ctx_k.shape[1]
        n_keys += past
        c_blk = pl.BlockSpec((1, past, DA_DV), lambda b, h, i: (b, 0, h))
        args += [ctx_k, ctx_v]
        in_specs += [c_blk, c_blk]
    args += [lam_p, subln_w]
    in_specs += [pl.BlockSpec(lam_p.shape, const), pl.BlockSpec(subln_w.shape, const)]
    o = pl.pallas_call(
        functools.partial(_attn_kernel, has_ctx=ctx_k is not None),
        name=f"attn_n{n}", grid=(batch, DA_HEADS, nq),
        in_specs=in_specs, out_specs=q_blk,
        out_shape=jax.ShapeDtypeStruct((heads, n, DA_DV), BF16),
        scratch_shapes=[pltpu.VMEM((n_keys, 2 * DA_DQK), BF16),
                        pltpu.VMEM((DA_DV + ONES_ROWS, n_keys), BF16)],
        compiler_params=_params(("parallel", "parallel", "arbitrary")),
    )(*args)
    return o


def _postmix_kernel(oh_ref, oa_ref, gate_ref, x_ref, mod_ref, wbh_ref, wba_ref, wout_ref,
                    n2_ref, rw_ref, x1_o, h2_o, afft_o, afftok_o):
    d = x_ref.shape[1]
    mod = mod_ref[0]
    g1, sh2, sc2 = mod[:, 2 * d:3 * d], mod[:, 3 * d:4 * d], mod[:, 4 * d:5 * d]
    g_h = gate_ref[:, 0:d].astype(F32)
    g_a = gate_ref[:, d:2 * d].astype(F32)

    def heads_on_lanes(ref):
        return jnp.concatenate([ref[h] for h in range(ref.shape[0])], axis=1)

    merged = (g_h * _dot(heads_on_lanes(oh_ref), wbh_ref[...])
              + g_a * _dot(heads_on_lanes(oa_ref), wba_ref[...]))
    x1 = x_ref[...] + g1 * _dot(merged.astype(BF16), wout_ref[...])
    x1_o[...] = x1
    xn = x1 * lax.rsqrt(jnp.mean(x1 * x1, axis=-1, keepdims=True) + EPS) * n2_ref[...]
    h2 = xn * (1.0 + sc2) + sh2
    h2_o[...] = _pack_bf16_pairs(h2)
    h_hi, h_lo = _split2(h2)
    rw = rw_ref[...]
    t1 = _dot_nt(rw, h_hi)
    t2 = _dot_nt(rw, h_lo)
    e = N_EXPERTS
    logits = t1[0:e] + t1[e:2 * e] + t2[0:e]
    mx = jnp.max(logits, axis=0, keepdims=True)
    p = jnp.exp(logits - mx)
    aff = p / jnp.sum(p, axis=0, keepdims=True)
    afft_o[...] = aff
    pad = jnp.zeros((LANES - e, aff.shape[1]), F32)
    afftok_o[...] = jnp.concatenate([aff, pad], axis=0).T


def _postmix(o_h, o_a, gates, x, mod3, mod_row, w_bh, w_ba, w_out, norm2_w, rw_cat, tm):
    n, d = x.shape
    const = lambda i: (0, 0)
    row = lambda i: (i, 0)
    return pl.pallas_call(
        _postmix_kernel,
        name=f"postmix_n{n}", grid=(n // tm,),
        in_specs=[pl.BlockSpec((o_h.shape[0], tm, LANES), lambda i: (0, i, 0)),
                  pl.BlockSpec((o_a.shape[0], tm, LANES), lambda i: (0, i, 0)),
                  pl.BlockSpec((tm, 4 * SEG), row), pl.BlockSpec((tm, d), row),
                  pl.BlockSpec((1, 1, mod3.shape[2]), lambda i: (mod_row(i), 0, 0)),
                  pl.BlockSpec(w_bh.shape, const), pl.BlockSpec(w_ba.shape, const),
                  pl.BlockSpec(w_out.shape, const), pl.BlockSpec((1, d), const),
                  pl.BlockSpec(rw_cat.shape, const)],
        out_specs=[pl.BlockSpec((tm, d), row), pl.BlockSpec((tm, d // 2), row),
                   pl.BlockSpec((N_EXPERTS, tm), lambda i: (0, i)),
                   pl.BlockSpec((tm, LANES), row)],
        out_shape=[jax.ShapeDtypeStruct((n, d), F32), jax.ShapeDtypeStruct((n, d // 2), I32),
                   jax.ShapeDtypeStruct((N_EXPERTS, n), F32),
                   jax.ShapeDtypeStruct((n, LANES), F32)],
        compiler_params=_params(("parallel",)),
    )(o_h, o_a, gates, x, mod3, w_bh, w_ba, w_out, norm2_w, rw_cat)


def _lane_cumsum_exclusive(x, blk):
    e, t = x.shape
    r = lax.broadcasted_iota(I32, (blk, blk), 0)
    c = lax.broadcasted_iota(I32, (blk, blk), 1)
    upper = (r < c).astype(BF16)
    carry = jnp.zeros((e, 1), F32)
    parts = []
    for j in range(t // blk):
        xb = x[:, j * blk:(j + 1) * blk]
        parts.append(_dot(xb.astype(BF16), upper) + carry)
        carry = carry + jnp.sum(xb, axis=1, keepdims=True)
    return parts[0] if len(parts) == 1 else jnp.concatenate(parts, axis=1)


def _route_kernel(aff_ref, pos_ref, *, cap, req_per_group):
    aff = aff_ref[...]
    bits = pltpu.bitcast(aff, I32)
    t_len = aff.shape[1]

    def count(mask):
        return jnp.sum(mask.astype(F32), axis=1, keepdims=True)

    def step(i, th):
        cand = th | (jnp.int32(1) << (30 - i))
        return jnp.where(count(bits >= cand) >= cap, cand, th)

    th = lax.fori_loop(0, 31, step, jnp.zeros((aff.shape[0], 1), I32))
    gt = bits > th
    eq = (bits == th).astype(F32)
    need = cap - count(gt)
    blk = min(t_len, 256)
    tie_rank = _lane_cumsum_exclusive(eq, blk)
    sel = jnp.where(gt, 1.0, jnp.where(tie_rank < need, eq, 0.0))
    slot = _lane_cumsum_exclusive(sel, blk)
    offset = (pl.program_id(0) % req_per_group) * cap
    pos_ref[...] = jnp.where(sel > 0.0, slot.astype(I32) + offset, -1)


def _route(afft, batch, cap, req_per_group):
    e, n = afft.shape
    t_len = n // batch
    blk = pl.BlockSpec((e, t_len), lambda b: (0, b))
    return pl.pallas_call(
        functools.partial(_route_kernel, cap=cap, req_per_group=req_per_group),
        name=f"route_n{n}", grid=(batch,), in_specs=[blk], out_specs=blk,
        out_shape=jax.ShapeDtypeStruct((e, n), I32),
        compiler_params=_params(("parallel",)),
    )(afft)


def _one_hot_rows(pos_row, n_slots):
    slot = lax.broadcasted_iota(I32, (n_slots, pos_row.shape[1]), 0)
    return (slot == pos_row).astype(BF16)


def _dispatch(parts, group_tokens, group_slots):
    n_exp = parts[0][0].shape[0]
    width = parts[0][1].shape[1]
    part_groups = [pos.shape[1] // group_tokens for pos, _ in parts]
    workers = SC_CORES * SC_SUBCORES
    chunks = group_slots // SC_GATHER_ROWS
    assert all(g * n_exp % workers == 0 for g in part_groups)
    assert group_slots % SC_GATHER_ROWS == 0 and group_tokens % SC_LANES == 0
    assert n_exp & (n_exp - 1) == 0
    exp_shift = n_exp.bit_length() - 1
    row_shift = SC_GATHER_ROWS.bit_length() - 1

    def body(*refs):
        ins, (out_hbm, pos_v, idx_v, rows_v, sem) = refs[:2 * len(parts)], refs[2 * len(parts):]
        wid = lax.axis_index("s") * SC_CORES + lax.axis_index("c")
        lane = lax.iota(I32, SC_LANES)
        group_base = 0
        for part, groups in enumerate(part_groups):
            pos_hbm, h_hbm = ins[2 * part], ins[2 * part + 1]
            for k in range(groups * n_exp // workers):
                pair = wid + workers * k
                g = pair >> exp_shift
                e = pair & (n_exp - 1)
                pltpu.sync_copy(pos_hbm.at[e, pl.ds(g * group_tokens, group_tokens)], pos_v)

                @pl.loop(0, group_tokens // SC_LANES)
                def _(i):
                    p = pos_v[pl.ds(i * SC_LANES, SC_LANES)]
                    tok = g * group_tokens + i * SC_LANES + lane
                    slot = jnp.maximum(p, 0)
                    plsc.store_scatter(idx_v, [slot >> row_shift, slot & (SC_GATHER_ROWS - 1)],
                                       tok, mask=p >= 0)

                for c in range(chunks):
                    row0 = (group_base + g) * group_slots + c * SC_GATHER_ROWS
                    pltpu.async_copy(h_hbm.at[idx_v.at[c]], rows_v, sem).wait()
                    pltpu.sync_copy(rows_v, out_hbm.at[e, pl.ds(row0, SC_GATHER_ROWS)])
            group_base += groups

    mesh = plsc.VectorSubcoreMesh(core_axis_name="c", subcore_axis_name="s",
                                  num_cores=SC_CORES, num_subcores=SC_SUBCORES)
    return pl.kernel(
        body,
        out_type=jax.ShapeDtypeStruct((n_exp, sum(part_groups) * group_slots, width), I32),
        mesh=mesh,
        scratch_types=[pltpu.VMEM((group_tokens,), I32), pltpu.VMEM((chunks, SC_GATHER_ROWS), I32),
                       pltpu.VMEM((SC_GATHER_ROWS, width), I32), pltpu.SemaphoreType.DMA],
        compiler_params=pltpu.CompilerParams(needs_layout_passes=False),
        name="dispatch",
    )(*[a for part in parts for a in part])


def _expert_kernel(xg_ref, wg_ref, wu_ref, wd_ref, y_ref, wg_s, wu_s, wd_s):
    @pl.when(pl.program_id(1) == 0)
    def _():
        wg_s[...] = wg_ref[0].astype(BF16)
        wu_s[...] = wu_ref[0].astype(BF16)
        wd_s[...] = wd_ref[0].astype(BF16)

    xg = _unpack_bf16_pairs(xg_ref[0])
    a = _silu(_dot(xg, wg_s[...])) * _dot(xg, wu_s[...])
    y_ref[0] = _dot(a.astype(BF16), wd_s[...]).astype(y_ref.dtype)


def _experts(xg, w_gate, w_up, w_down, ts):
    e, s, half = xg.shape
    d, f = w_gate.shape[1:]
    assert d == 2 * half
    x_blk = pl.BlockSpec((1, ts, d), lambda x, i: (x, i, 0))
    return pl.pallas_call(
        _expert_kernel,
        name=f"experts_s{s}", grid=(e, s // ts),
        in_specs=[pl.BlockSpec((1, ts, half), lambda x, i: (x, i, 0)),
                  pl.BlockSpec((1, d, f), lambda x, i: (x, 0, 0)),
                  pl.BlockSpec((1, d, f), lambda x, i: (x, 0, 0)),
                  pl.BlockSpec((1, f, d), lambda x, i: (x, 0, 0))],
        out_specs=x_blk,
        out_shape=jax.ShapeDtypeStruct((e, s, d), BF16),
        scratch_shapes=[pltpu.VMEM((d, f), BF16), pltpu.VMEM((d, f), BF16),
                        pltpu.VMEM((f, d), BF16)],
        compiler_params=_params(("parallel", "arbitrary")),
    )(xg, w_gate, w_up, w_down)


def _combine_kernel(pos_ref, afftok_ref, y_ref, x1_ref, mod_ref, o_ref):
    e = pl.program_id(2)
    d = x1_ref.shape[1]

    @pl.when(e == 0)
    def _():
        o_ref[...] = jnp.zeros_like(o_ref)

    n_slots = y_ref.shape[1]
    lane = lax.broadcasted_iota(I32, afftok_ref.shape, 1)
    gate = jnp.sum(jnp.where(lane == e, afftok_ref[...], 0.0), axis=1, keepdims=True)
    pos_row = pos_ref[pl.ds(e, 1), :]
    for h in range(o_ref.shape[0] // COMBINE_TOKENS):
        cols = slice(h * COMBINE_TOKENS, (h + 1) * COMBINE_TOKENS)
        ph = pos_row[:, cols]
        first = jnp.min(jnp.where(ph >= 0, ph, n_slots))
        last = jnp.max(ph)
        start = jnp.minimum((first >> 7) << 7, n_slots - COMBINE_WIN)
        fits = last < start + COMBINE_WIN
        gh = gate[cols]

        @pl.when(fits)
        def _():
            s = pl.multiple_of(start, 128)
            slot = lax.broadcasted_iota(I32, (COMBINE_WIN, COMBINE_TOKENS), 0) + s
            p = (slot == ph).astype(BF16)
            o_ref[cols, :] += gh * _dot_tn(p, y_ref[0, pl.ds(s, COMBINE_WIN), :])

        @pl.when(jnp.logical_not(fits))
        def _():
            o_ref[cols, :] += gh * _dot_tn(_one_hot_rows(ph, n_slots), y_ref[0])

    @pl.when(e == pl.num_programs(2) - 1)
    def _():
        g2 = mod_ref[0][:, 5 * d:6 * d]
        o_ref[...] = x1_ref[...] + g2 * o_ref[...]


def _combine(pos, afftok, y, x1, mod3, mod_row, group_tokens, group_slots, group_base, tc):
    e, n = pos.shape
    d = x1.shape[1]
    groups = n // group_tokens
    per = group_tokens // tc
    tok = lambda g, j, x: (g * per + j, 0)
    return pl.pallas_call(
        _combine_kernel,
        name=f"combine_n{n}", grid=(groups, per, e),
        in_specs=[pl.BlockSpec((e, tc), lambda g, j, x: (0, g * per + j)),
                  pl.BlockSpec((tc, LANES), tok),
                  pl.BlockSpec((1, group_slots, d), lambda g, j, x: (x, group_base + g, 0)),
                  pl.BlockSpec((tc, d), tok),
                  pl.BlockSpec((1, 1, mod3.shape[2]), lambda g, j, x: (mod_row(g), 0, 0))],
        out_specs=pl.BlockSpec((tc, d), tok),
        out_shape=jax.ShapeDtypeStruct((n, d), F32),
        compiler_params=_params(("parallel", "parallel", "arbitrary")),
    )(pos, afftok, y, x1, mod3)


def _rope_tables(t_len):
    n_freq = DA_DQK // 4
    inv = ROPE_BASE ** (-jnp.arange(n_freq, dtype=F32) / n_freq)
    t = jnp.arange(t_len)
    pos = jnp.stack([(t // GRID_W).astype(F32), (t % GRID_W).astype(F32)], axis=1)
    ang = pos[:, :, None, None] * inv[None, None, None, :]
    ang = jnp.broadcast_to(ang, (t_len, 2, 2, n_freq))
    sign = jnp.array([-1.0, 1.0], F32)[None, None, :, None]
    cos = jnp.cos(ang).reshape(t_len, DA_DQK)
    sin = (jnp.sin(ang) * sign).reshape(t_len, DA_DQK)
    reps = SEG // DA_DQK
    return jnp.tile(cos, (1, reps)), jnp.tile(sin, (1, reps))


def _trunk(x, batch, mod3, mod_row_tok, weights, ctx_k, ctx_v, s0, rope, group_tokens):
    (norm1_w, norm2_w, w_in, b_gate, lb_logits, hgrn_norm_w, qkw, gm, lam_p, subln_w,
     w_bh, w_ba, w_out, rw_cat) = weights
    n, d = x.shape
    t_len = n // batch
    latent = rope is not None
    tm = 256
    (q_h, bf, bb, kff, kfb, i_h, g_h, dq, dk, dv, gates) = _premix(
        x, mod3, functools.partial(mod_row_tok, tm=tm), norm1_w, w_in, b_gate, lb_logits, qkw, gm,
        rope, tm, BF16 if latent else F32, latent)
    o_h, s_new = _hgrn(q_h, bf, bb, kff, kfb, i_h, g_h, hgrn_norm_w, s0, batch)
    o_a = _attention(dq, dk, dv, ctx_k, ctx_v, lam_p, subln_w, batch, min(t_len, 256 * ATT_SPLIT))
    x1, h2, afft, afftok = _postmix(o_h, o_a, gates, x, mod3, functools.partial(mod_row_tok, tm=tm),
                                    w_bh, w_ba, w_out, norm2_w, rw_cat, tm)
    cap = EC_CAPACITY * t_len // N_EXPERTS
    pos = _route(afft, batch, cap, group_tokens // t_len)
    return (pos, h2, afftok, x1), dk, dv, s_new


def _moe(routed, mod3, mod_row_grps, w_eg, w_eu, w_ed, group_tokens, group_slots):
    xg = _dispatch([(pos, h2p) for pos, h2p, _, _ in routed], group_tokens, group_slots)
    y = _experts(xg, w_eg, w_eu, w_ed, group_slots)
    outs, group_base = [], 0
    for (pos, _, afftok, x1), mod_row in zip(routed, mod_row_grps):
        outs.append(_combine(pos, afftok, y, x1, mod3, mod_row, group_tokens, group_slots,
                             group_base, 1024))
        group_base += pos.shape[1] // group_tokens
    return outs


def kernel(x_prompt, x_sample, cache_k, cache_v, state_hgrn, c, c_ctx, norm1_w, norm2_w, w_mod,
           b_mod, w_in, b_gate, hgrn_lb_logits, hgrn_norm_w, qk_norm_w, diff_lambda, diff_subln_w,
           w_branch_hgrn, w_branch_attn, w_out, router_w, w_exp_gate, w_exp_up, w_exp_down):
    batch, seq, d = x_prompt.shape
    dec_batch, dec_seq, _ = x_sample.shape
    past = cache_k.shape[2]
    depth = w_in.shape[0]
    assert depth == 1
    group_tokens = dec_seq
    assert group_tokens % seq == 0 and (batch * seq) % group_tokens == 0
    l = 0

    rows = -(-(1 + dec_batch) // 8) * 8
    cond = jnp.zeros((rows, d), F32).at[0].set(c_ctx).at[1:1 + dec_batch].set(c)
    mod = _modulation(cond, w_mod[l], b_mod[l])
    mod3 = mod.reshape(rows, 1, 6 * d)

    gidx = jnp.arange(SEG) // DA_DQK
    gm = (gidx[:, None] == gidx[None, :]).astype(BF16) * (1.0 / DA_DQK)
    qkw = jnp.tile(qk_norm_w[l], (1, SEG // DA_DQK))
    rw_t = router_w[l].T
    rw_hi = rw_t.astype(BF16)
    rw_cat = jnp.concatenate([rw_hi, (rw_t - rw_hi.astype(F32)).astype(BF16)], axis=0)
    weights = (norm1_w[l][None], norm2_w[l][None], w_in[l].astype(BF16), b_gate[l][None],
               hgrn_lb_logits.reshape(4, SEG), hgrn_norm_w[l], qkw, gm, diff_lambda[l],
               diff_subln_w[l][None], w_branch_hgrn[l].astype(BF16),
               w_branch_attn[l].astype(BF16), w_out[l].astype(BF16),
               rw_cat)

    routed_ctx, k_new, v_new, s_new = _trunk(
        x_prompt.reshape(batch * seq, d), batch, mod3,
        lambda i, tm: 0, weights, None, None, None, None, group_tokens)
    per_req = dec_seq
    routed_lat, _, _, _ = _trunk(
        x_sample.reshape(dec_batch * dec_seq, d), dec_batch, mod3,
        lambda i, tm: 1 + (i * tm) // per_req, weights,
        cache_k[:, l].reshape(dec_batch, past, SEG), cache_v[:, l].reshape(dec_batch, past, SEG),
        state_hgrn[:, l], _rope_tables(dec_seq), group_tokens)
    group_slots = EC_CAPACITY * group_tokens // N_EXPERTS
    yp, ys = _moe([routed_ctx, routed_lat], mod3, [lambda g: 0, lambda g: 1 + g],
                  w_exp_gate[l], w_exp_up[l], w_exp_down[l], group_tokens, group_slots)

    return (yp.reshape(batch, seq, d), ys.reshape(dec_batch, dec_seq, d),
            k_new.reshape(batch, 1, seq, DA_HEADS, 2, DA_DQK),
            v_new.reshape(batch, 1, seq, DA_HEADS, DA_DV),
            s_new.reshape(batch, 1, 2, HG_HEADS, HG_DK, HG_DV))
```

```python
import functools
import math

import jax
import jax.numpy as jnp
from jax import lax
from jax.experimental import pallas as pl
from jax.experimental.pallas import tpu as pltpu
from jax.experimental.pallas import tpu_sc as plsc

F32 = jnp.float32
BF16 = jnp.bfloat16
I32 = jnp.int32

EPS = 1e-6
GRID_W = 64
HG_HEADS = 4
HG_DK = 128
HG_DV = 128
HG_CHUNK_LOG2 = 7
HG_CHUNK = 1 << HG_CHUNK_LOG2
HG_UNROLL = 4
DA_HEADS = 4
DA_DQK = 64
DA_DV = 128
N_EXPERTS = 16
EC_CAPACITY = 2
ROPE_BASE = 10000.0
SEG = 512
N_SEG = 12
LAM_INIT = 0.8 - 0.6 * math.exp(-0.3 * 0)
LANES = 128
ONES_ROWS = 16
ATT_SPLIT = 2
ATT_TQ = 256
PREMIX_TOKENS = 256
POSTMIX_TOKENS = 512
COMBINE_TOKENS = 2048
Q_SCALE = DA_DQK ** -0.5 * math.log2(math.e)
HG_SAFE_DECAY = 80.0
VMEM_LIMIT = 56 * 1024 * 1024
SC_CORES = 2
SC_SUBCORES = 16
SC_LANES = 16
SC_GATHER_ROWS = 128


def _dot(a, b):
    return jnp.dot(a, b, preferred_element_type=F32)


def _dot_nt(a, b):
    return lax.dot_general(a, b, (((1,), (1,)), ((), ())), preferred_element_type=F32)


def _dot_tn(a, b):
    return lax.dot_general(a, b, (((0,), (0,)), ((), ())), preferred_element_type=F32)


def _split2(x):
    hi = x.astype(BF16)
    lo = (x - hi.astype(F32)).astype(BF16)
    return hi, lo


def _silu(x):
    return x * jax.nn.sigmoid(x)


def _pack_bf16_pairs(x):
    w = x.shape[1] // 2
    bits = pltpu.bitcast(x.astype(BF16).astype(F32), I32)
    return lax.shift_right_logical(bits[:, :w], 16) | bits[:, w:]


def _unpack_bf16_pairs(words):
    lo = pltpu.bitcast(words << 16, F32)
    hi = pltpu.bitcast(words & jnp.int32(-65536), F32)
    return jnp.concatenate([lo, hi], axis=1).astype(BF16)


def _params(sem):
    return pltpu.CompilerParams(dimension_semantics=sem, vmem_limit_bytes=VMEM_LIMIT)


def _mod_kernel(c_ref, w_ref, b_ref, o_ref):
    s_hi, s_lo = _split2(_silu(c_ref[...]))
    w_hi, w_lo = _split2(w_ref[...])
    o_ref[...] = _dot(s_hi, w_hi) + _dot(s_hi, w_lo) + _dot(s_lo, w_hi) + b_ref[...]


def _modulation(cond, w_mod, b_mod):
    rows, d = cond.shape
    n = w_mod.shape[1]
    bn = 512
    return pl.pallas_call(
        _mod_kernel,
        name="modulation", grid=(n // bn,),
        in_specs=[pl.BlockSpec((rows, d), lambda j: (0, 0)),
                  pl.BlockSpec((d, bn), lambda j: (0, j)),
                  pl.BlockSpec((1, bn), lambda j: (0, j))],
        out_specs=pl.BlockSpec((rows, bn), lambda j: (0, j)),
        out_shape=jax.ShapeDtypeStruct((rows, n), F32),
        compiler_params=_params(("arbitrary",)),
    )(cond, w_mod, b_mod.reshape(1, n))


def _group_rms(z, gm_ref, w):
    ms = _dot((z * z).astype(BF16), gm_ref[...])
    return z * lax.rsqrt(ms + EPS) * w


def _rope(x, cos, sin_signed):
    n = x.shape[-1]
    lane = lax.broadcasted_iota(I32, x.shape, 1)
    partner = jnp.where((lane & 16) == 0, pltpu.roll(x, n - 16, 1), pltpu.roll(x, 16, 1))
    return x * cos + partner * sin_signed


def _premix_kernel(*refs, latent):
    if latent:
        (x_ref, mod_ref, n1_ref, win_ref, bg_ref, lbl_ref, qkw_ref, gm_ref, cos_ref, sin_ref,
         q_o, bf_o, bb_o, kff_o, kfb_o, i_o, g_o, dq_o, dk_o, dv_o, gate_o) = refs
    else:
        (x_ref, mod_ref, n1_ref, win_ref, bg_ref, lbl_ref, qkw_ref, gm_ref,
         q_o, bf_o, bb_o, kff_o, kfb_o, i_o, g_o, dq_o, dk_o, dv_o, gate_o) = refs
    d = x_ref.shape[1]
    mod = mod_ref[0]
    sh1, sc1 = mod[:, 0:d], mod[:, d:2 * d]
    x = x_ref[...]
    xn = x * lax.rsqrt(jnp.mean(x * x, axis=-1, keepdims=True) + EPS) * n1_ref[...]
    hb = (xn * (1.0 + sc1) + sh1).astype(BF16)

    def seg(j):
        return _dot(hb, win_ref[:, j * SEG:(j + 1) * SEG])

    def lower_bound(direction):
        l0 = lbl_ref[2 * direction:2 * direction + 1, :]
        l1 = lbl_ref[2 * direction + 1:2 * direction + 2, :]
        mx = jnp.maximum(l0, l1)
        e0, e1 = jnp.exp(l0 - mx), jnp.exp(l1 - mx)
        return e0 / (e0 + e1)

    tm = x.shape[0]
    row = lax.broadcasted_iota(I32, (tm, tm), 0)
    col = lax.broadcasted_iota(I32, (tm, tm), 1)
    same_chunk = (row >> HG_CHUNK_LOG2) == (col >> HG_CHUNK_LOG2)

    def store(o_ref, val):
        val = val.astype(o_ref.dtype)
        if len(o_ref.shape) == 2:
            o_ref[...] = val
        else:
            for h in range(o_ref.shape[0]):
                o_ref[h] = val[:, h * LANES:(h + 1) * LANES]

    store(q_o, _silu(seg(0)))
    for j, b_o, kf_o, order in ((1, bf_o, kff_o, row >= col), (2, bb_o, kfb_o, row <= col)):
        lbd = lower_bound(j - 1)
        f = lbd + (1.0 - lbd) * jax.nn.sigmoid(seg(j))
        tri = (same_chunk & order).astype(BF16)
        hi, lo = _split2(jnp.log(f))
        store(b_o, _dot(tri, hi) + _dot(tri, lo))
        store(kf_o, 1.0 - f)
    store(i_o, seg(3))
    store(g_o, _silu(seg(4)))
    qn = _group_rms(seg(5), gm_ref, qkw_ref[0:1, :]) * Q_SCALE
    kn = _group_rms(seg(6), gm_ref, qkw_ref[1:2, :])
    if latent:
        qn = _rope(qn, cos_ref[...], sin_ref[...])
        kn = _rope(kn, cos_ref[...], sin_ref[...])
    store(dq_o, qn)
    store(dk_o, kn)
    store(dv_o, seg(7))
    for j in range(4):
        z = seg(8 + j) + bg_ref[:, j * SEG:(j + 1) * SEG]
        gate_o[:, j * SEG:(j + 1) * SEG] = jax.nn.sigmoid(z).astype(gate_o.dtype)


def _premix(x, mod3, mod_row, norm1_w, w_in, b_gate, lb_logits, qkw, gm, rope, tm, kv_dtype,
            kv_head_major):
    n, d = x.shape
    latent = rope is not None
    const = lambda i: (0, 0)
    in_specs = [pl.BlockSpec((tm, d), lambda i: (i, 0)),
                pl.BlockSpec((1, 1, mod3.shape[2]), lambda i: (mod_row(i), 0, 0)),
                pl.BlockSpec((1, d), const),
                pl.BlockSpec(w_in.shape, const),
                pl.BlockSpec(b_gate.shape, const),
                pl.BlockSpec(lb_logits.shape, const),
                pl.BlockSpec(qkw.shape, const),
                pl.BlockSpec(gm.shape, const)]
    args = [x, mod3, norm1_w, w_in, b_gate, lb_logits, qkw, gm]
    if latent:
        cos, sin = rope
        nblk = cos.shape[0] // tm
        in_specs += [pl.BlockSpec((tm, SEG), lambda i: (i % nblk, 0))] * 2
        args += [cos, sin]
    heads = SEG // LANES
    head_spec = pl.BlockSpec((heads, tm, LANES), lambda i: (0, i, 0))
    tok_spec = pl.BlockSpec((tm, SEG), lambda i: (i, 0))
    kv_spec = head_spec if kv_head_major else tok_spec
    kv_shape = (heads, n, LANES) if kv_head_major else (n, SEG)
    out_dtypes = [BF16, F32, F32, BF16, BF16, BF16, BF16, BF16]
    out_shape = [jax.ShapeDtypeStruct((heads, n, LANES), t) for t in out_dtypes]
    out_shape += [jax.ShapeDtypeStruct(kv_shape, kv_dtype)] * 2
    out_shape.append(jax.ShapeDtypeStruct((n, 4 * SEG), BF16))
    out_specs = [head_spec] * 8 + [kv_spec] * 2 + [pl.BlockSpec((tm, 4 * SEG), lambda i: (i, 0))]
    return pl.pallas_call(
        functools.partial(_premix_kernel, latent=latent),
        name=f"premix_n{n}", grid=(n // tm,),
        in_specs=in_specs, out_specs=out_specs, out_shape=out_shape,
        compiler_params=_params(("parallel",)),
    )(*args)


def _hgrn_chunk_local(q, b, total, k, v, keep, safe):
    if safe:
        ref = b[HG_CHUNK // 2:HG_CHUNK // 2 + 1, :]
        qa = q * jnp.exp(b - ref)
        kb = k * jnp.exp(ref - b)
        attn = jnp.where(keep, _dot_nt(qa.astype(BF16), kb.astype(BF16)), 0.0)
        qe = qa * jnp.exp(ref)
        kd = kb * jnp.exp(total - ref)
    else:
        qe = q * jnp.exp(b)
        kd = k * jnp.exp(total - b)
        col = lax.broadcasted_iota(I32, (HG_CHUNK, HG_CHUNK), 1)

        def column(s, acc):
            onehot = (lax.broadcasted_iota(I32, (HG_CHUNK, 1), 0) == s).astype(F32)
            bs = jnp.sum(b * onehot, axis=0, keepdims=True)
            ks = jnp.sum(k * onehot, axis=0, keepdims=True)
            w = jnp.sum(q * ks * jnp.exp(jnp.minimum(b - bs, 0.0)), axis=1, keepdims=True)
            return jnp.where(col == s, w, acc)

        attn = lax.fori_loop(0, HG_CHUNK, column, jnp.zeros((HG_CHUNK, HG_CHUNK), F32))
        attn = jnp.where(keep, attn, 0.0)
    vt = v.astype(F32).T.astype(BF16)
    lhs = jnp.concatenate([qe.astype(BF16), attn.astype(BF16)], axis=1)
    return lhs, vt, _dot(vt, kd.astype(BF16)), jnp.exp(total)


def _hgrn_scan_group(chunks, st):
    outs = []
    for lhs, vt, inc, decay in chunks:
        outs.append(_dot_nt(lhs, jnp.concatenate([st.astype(BF16), vt], axis=1)))
        st = st * decay + inc
    return outs, st


def _hgrn_kernel(*refs, has_state):
    if has_state:
        (q_ref, bf_ref, bb_ref, kff_ref, kfb_ref, i_ref, g_ref, nw_ref, s0_ref,
         o_ref, sn_ref, of_s, ob_s) = refs
    else:
        (q_ref, bf_ref, bb_ref, kff_ref, kfb_ref, i_ref, g_ref, nw_ref,
         o_ref, sn_ref, of_s, ob_s) = refs
    t_len = q_ref.shape[1]
    nc = t_len // HG_CHUNK
    unroll = min(HG_UNROLL, nc)
    head = pl.program_id(1)
    row = lax.broadcasted_iota(I32, (HG_CHUNK, HG_CHUNK), 0)
    col = lax.broadcasted_iota(I32, (HG_CHUNK, HG_CHUNK), 1)
    keep_f, keep_b = row >= col, row <= col

    mid_f = bf_ref[0, pl.ds(HG_CHUNK // 2, nc, stride=HG_CHUNK), :]
    tot_f = bf_ref[0, pl.ds(HG_CHUNK - 1, nc, stride=HG_CHUNK), :]
    mid_b = bb_ref[0, pl.ds(HG_CHUNK // 2, nc, stride=HG_CHUNK), :]
    tot_b = bb_ref[0, pl.ds(0, nc, stride=HG_CHUNK), :]
    worst = jnp.minimum(jnp.min(jnp.minimum(mid_f, tot_f - mid_f)),
                        jnp.min(jnp.minimum(mid_b, tot_b - mid_b)))

    if has_state:
        st_f0, st_b0 = s0_ref[0, 0, 0].T, s0_ref[0, 1, 0].T
    else:
        st_f0 = st_b0 = jnp.zeros((HG_DV, HG_DK), F32)

    def scan(safe):
        def body(it, carry):
            st_f, st_b = carry
            rows_f, rows_b, loc_f, loc_b = [], [], [], []
            for u in range(unroll):
                c = it * unroll + u
                sf = pl.multiple_of(c * HG_CHUNK, HG_CHUNK)
                sb = pl.multiple_of((nc - 1 - c) * HG_CHUNK, HG_CHUNK)
                rf, rb = pl.ds(sf, HG_CHUNK), pl.ds(sb, HG_CHUNK)
                rows_f.append(rf)
                rows_b.append(rb)
                loc_f.append(_hgrn_chunk_local(
                    q_ref[0, rf, :].astype(F32), bf_ref[0, rf, :],
                    bf_ref[0, pl.ds(sf + HG_CHUNK - 1, 1), :],
                    kff_ref[0, rf, :].astype(F32), i_ref[0, rf, :], keep_f, safe))
                loc_b.append(_hgrn_chunk_local(
                    q_ref[0, rb, :].astype(F32), bb_ref[0, rb, :], bb_ref[0, pl.ds(sb, 1), :],
                    kfb_ref[0, rb, :].astype(F32), i_ref[0, rb, :], keep_b, safe))
            outs_f, st_f = _hgrn_scan_group(loc_f, st_f)
            outs_b, st_b = _hgrn_scan_group(loc_b, st_b)
            for rf, rb, o_f, o_b in zip(rows_f, rows_b, outs_f, outs_b):
                of_s[rf, :] = o_f
                ob_s[rb, :] = o_b
            return st_f, st_b
        return lax.fori_loop(0, nc // unroll, body, (st_f0, st_b0))

    st_f, st_b = lax.cond(worst >= -HG_SAFE_DECAY, lambda: scan(True), lambda: scan(False))
    sn_ref[0, 0, 0] = st_f.T
    sn_ref[0, 1, 0] = st_b.T
    o = of_s[...] + ob_s[...]
    nw = nw_ref[pl.ds(head, 1), :]
    on = o * lax.rsqrt(jnp.mean(o * o, axis=-1, keepdims=True) + EPS) * nw
    o_ref[0] = (on * g_ref[0].astype(F32)).astype(o_ref.dtype)


def _hgrn(q, bf, bb, kff, kfb, iv, g, norm_w, s0, batch):
    heads, n, _ = q.shape
    t_len = n // batch
    blk = pl.BlockSpec((1, t_len, HG_DK), lambda b, h: (h, b, 0))
    st_blk = pl.BlockSpec((1, 2, 1, HG_DK, HG_DV), lambda b, h: (b, 0, h, 0, 0))
    args = [q, bf, bb, kff, kfb, iv, g, norm_w]
    in_specs = [blk] * 7 + [pl.BlockSpec(norm_w.shape, lambda b, h: (0, 0))]
    if s0 is not None:
        args.append(s0)
        in_specs.append(st_blk)
    o, s_new = pl.pallas_call(
        functools.partial(_hgrn_kernel, has_state=s0 is not None),
        name=f"hgrn_n{n}", grid=(batch, HG_HEADS),
        in_specs=in_specs,
        out_specs=[blk, st_blk],
        out_shape=[jax.ShapeDtypeStruct((heads, n, HG_DV), BF16),
                   jax.ShapeDtypeStruct((batch, 2, HG_HEADS, HG_DK, HG_DV), F32)],
        scratch_shapes=[pltpu.VMEM((t_len, HG_DV), F32), pltpu.VMEM((t_len, HG_DV), F32)],
        compiler_params=_params(("parallel", "parallel")),
    )(*args)
    return o, s_new


def _attn_kernel(*refs, has_ctx):
    if has_ctx:
        q_ref, k_ref, v_ref, ck_ref, cv_ref, lam_ref, sw_ref, o_ref, k_s, vt_s = refs
    else:
        q_ref, k_ref, v_ref, lam_ref, sw_ref, o_ref, k_s, vt_s = refs
    t_own = k_ref.shape[1]

    @pl.when(pl.program_id(2) == 0)
    def _():
        k_s[0:t_own, :] = k_ref[0].astype(BF16)
        vt_s[0:DA_DV, 0:t_own] = v_ref[0].astype(F32).T.astype(BF16)
        if has_ctx:
            k_s[t_own:, :] = jnp.concatenate(
                [ck_ref[0, :, 0, 0, :], ck_ref[0, :, 0, 1, :]], axis=1).astype(BF16)
            vt_s[0:DA_DV, t_own:] = cv_ref[0].T.astype(BF16)
        vt_s[DA_DV:, :] = jnp.ones((ONES_ROWS, vt_s.shape[1]), BF16)

    lv = lam_ref[...]
    lam = (jnp.exp(jnp.sum(lv[0:1] * lv[1:2], keepdims=True))
           - jnp.exp(jnp.sum(lv[2:3] * lv[3:4], keepdims=True)) + LAM_INIT)
    tq = q_ref.shape[1] // ATT_SPLIT
    dim = lax.broadcasted_iota(I32, (2 * DA_DQK, tq), 0)

    def scores(i):
        qt = q_ref[0, i * tq:(i + 1) * tq, :].astype(F32).T
        q_both = jnp.concatenate([jnp.where(dim < DA_DQK, qt, 0.0),
                                  jnp.where(dim >= DA_DQK, qt, 0.0)], axis=1).astype(BF16)
        return _dot(k_s[...], q_both)

    def finish(i, st):
        pt = jnp.exp2(st - jnp.max(st, axis=0, keepdims=True)).astype(BF16)
        r = _dot(vt_s[...], pt)
        r = r[0:DA_DV] * (1.0 / r[DA_DV:DA_DV + 1])
        o = (r[:, 0:tq] - lam * r[:, tq:2 * tq]).T
        on = o * lax.rsqrt(jnp.mean(o * o, axis=-1, keepdims=True) + EPS) * sw_ref[...]
        o_ref[0, i * tq:(i + 1) * tq, :] = (on * (1.0 - LAM_INIT)).astype(o_ref.dtype)

    sts = [scores(i) for i in range(ATT_SPLIT)]
    for i in range(ATT_SPLIT):
        finish(i, sts[i])


def _attention(q, k, v, ctx_k, ctx_v, lam_p, subln_w, batch, tq):
    heads, n, _ = q.shape
    t_len = n // batch
    nq = t_len // tq
    q_blk = pl.BlockSpec((1, tq, DA_DV), lambda b, h, i: (h, b * nq + i, 0))
    const = lambda b, h, i: (0, 0)
    if k.ndim == 3:
        kv_blk = pl.BlockSpec((1, t_len, DA_DV), lambda b, h, i: (h, b, 0))
        args = [q, k, v]
    else:
        kv_blk = pl.BlockSpec((1, t_len, DA_DV), lambda b, h, i: (b, 0, h))
        args = [q, k.reshape(batch, t_len, SEG), v.reshape(batch, t_len, SEG)]
    in_specs = [q_blk, kv_blk, kv_blk]
    n_keys = t_len
    if ctx_k is not None:
        past = ctx_k.shape[1]
        n_keys += past
        ck_blk = pl.BlockSpec((1, past, 1, 2, DA_DQK), lambda b, h, i: (b, 0, h, 0, 0))
        cv_blk = pl.BlockSpec((1, past, DA_DV), lambda b, h, i: (b, 0, h))
        args += [ctx_k, ctx_v]
        in_specs += [ck_blk, cv_blk]
    args += [lam_p, subln_w]
    in_specs += [pl.BlockSpec(lam_p.shape, const), pl.BlockSpec(subln_w.shape, const)]
    o = pl.pallas_call(
        functools.partial(_attn_kernel, has_ctx=ctx_k is not None),
        name=f"attn_n{n}", grid=(batch, DA_HEADS, nq),
        in_specs=in_specs, out_specs=q_blk,
        out_shape=jax.ShapeDtypeStruct((heads, n, DA_DV), BF16),
        scratch_shapes=[pltpu.VMEM((n_keys, 2 * DA_DQK), BF16),
                        pltpu.VMEM((DA_DV + ONES_ROWS, n_keys), BF16)],
        compiler_params=_params(("parallel", "parallel", "arbitrary")),
    )(*args)
    return o


def _postmix_kernel(oh_ref, oa_ref, gate_ref, x_ref, mod_ref, wbh_ref, wba_ref, wout_ref,
                    n2_ref, rw_ref, x1_o, h2_o, afft_o, afftok_o):
    d = x_ref.shape[1]
    mod = mod_ref[0]
    g1, sh2, sc2 = mod[:, 2 * d:3 * d], mod[:, 3 * d:4 * d], mod[:, 4 * d:5 * d]
    g_h = gate_ref[:, 0:d].astype(F32)
    g_a = gate_ref[:, d:2 * d].astype(F32)

    def heads_on_lanes(ref):
        return jnp.concatenate([ref[h] for h in range(ref.shape[0])], axis=1)

    merged = (g_h * _dot(heads_on_lanes(oh_ref), wbh_ref[...])
              + g_a * _dot(heads_on_lanes(oa_ref), wba_ref[...]))
    x1 = x_ref[...] + g1 * _dot(merged.astype(BF16), wout_ref[...])
    x1_o[...] = x1
    xn = x1 * lax.rsqrt(jnp.mean(x1 * x1, axis=-1, keepdims=True) + EPS) * n2_ref[...]
    h2 = xn * (1.0 + sc2) + sh2
    h2_o[...] = _pack_bf16_pairs(h2)
    h_hi, h_lo = _split2(h2)
    rw = rw_ref[...]
    t1 = _dot_nt(rw, h_hi)
    t2 = _dot_nt(rw, h_lo)
    e = N_EXPERTS
    logits = t1[0:e] + t1[e:2 * e] + t2[0:e]
    mx = jnp.max(logits, axis=0, keepdims=True)
    p = jnp.exp(logits - mx)
    aff = p / jnp.sum(p, axis=0, keepdims=True)
    afft_o[...] = aff
    pad = jnp.zeros((LANES - e, aff.shape[1]), F32)
    afftok_o[...] = jnp.concatenate([aff, pad], axis=0).T


def _postmix(o_h, o_a, gates, x, mod3, mod_row, w_bh, w_ba, w_out, norm2_w, rw_cat, tm):
    n, d = x.shape
    const = lambda i: (0, 0)
    row = lambda i: (i, 0)
    return pl.pallas_call(
        _postmix_kernel,
        name=f"postmix_n{n}", grid=(n // tm,),
        in_specs=[pl.BlockSpec((o_h.shape[0], tm, LANES), lambda i: (0, i, 0)),
                  pl.BlockSpec((o_a.shape[0], tm, LANES), lambda i: (0, i, 0)),
                  pl.BlockSpec((tm, 4 * SEG), row), pl.BlockSpec((tm, d), row),
                  pl.BlockSpec((1, 1, mod3.shape[2]), lambda i: (mod_row(i), 0, 0)),
                  pl.BlockSpec(w_bh.shape, const), pl.BlockSpec(w_ba.shape, const),
                  pl.BlockSpec(w_out.shape, const), pl.BlockSpec((1, d), const),
                  pl.BlockSpec(rw_cat.shape, const)],
        out_specs=[pl.BlockSpec((tm, d), row), pl.BlockSpec((tm, d // 2), row),
                   pl.BlockSpec((N_EXPERTS, tm), lambda i: (0, i)),
                   pl.BlockSpec((tm, LANES), row)],
        out_shape=[jax.ShapeDtypeStruct((n, d), F32), jax.ShapeDtypeStruct((n, d // 2), I32),
                   jax.ShapeDtypeStruct((N_EXPERTS, n), F32),
                   jax.ShapeDtypeStruct((n, LANES), F32)],
        compiler_params=_params(("parallel",)),
    )(o_h, o_a, gates, x, mod3, w_bh, w_ba, w_out, norm2_w, rw_cat)


def _lane_cumsum_exclusive(x, blk):
    e, t = x.shape
    r = lax.broadcasted_iota(I32, (blk, blk), 0)
    c = lax.broadcasted_iota(I32, (blk, blk), 1)
    upper = (r < c).astype(BF16)
    carry = jnp.zeros((e, 1), F32)
    parts = []
    for j in range(t // blk):
        xb = x[:, j * blk:(j + 1) * blk]
        parts.append(_dot(xb.astype(BF16), upper) + carry)
        carry = carry + jnp.sum(xb, axis=1, keepdims=True)
    return parts[0] if len(parts) == 1 else jnp.concatenate(parts, axis=1)


def _route_kernel(aff_ref, pos_ref, *, cap, req_per_group):
    aff = aff_ref[...]
    bits = pltpu.bitcast(aff, I32)
    t_len = aff.shape[1]

    def count(mask):
        return jnp.sum(mask.astype(F32), axis=1, keepdims=True)

    def step(i, th):
        cand = th | (jnp.int32(1) << (30 - i))
        return jnp.where(count(bits >= cand) >= cap, cand, th)

    th = lax.fori_loop(0, 31, step, jnp.zeros((aff.shape[0], 1), I32))
    gt = bits > th
    eq = (bits == th).astype(F32)
    need = cap - count(gt)
    blk = min(t_len, 256)
    tie_rank = _lane_cumsum_exclusive(eq, blk)
    sel = jnp.where(gt, 1.0, jnp.where(tie_rank < need, eq, 0.0))
    slot = _lane_cumsum_exclusive(sel, blk)
    offset = (pl.program_id(0) % req_per_group) * cap
    pos_ref[...] = jnp.where(sel > 0.0, slot.astype(I32) + offset, -1)


def _route(afft, batch, cap, req_per_group):
    e, n = afft.shape
    t_len = n // batch
    blk = pl.BlockSpec((e, t_len), lambda b: (0, b))
    return pl.pallas_call(
        functools.partial(_route_kernel, cap=cap, req_per_group=req_per_group),
        name=f"route_n{n}", grid=(batch,), in_specs=[blk], out_specs=blk,
        out_shape=jax.ShapeDtypeStruct((e, n), I32),
        compiler_params=_params(("parallel",)),
    )(afft)


def _one_hot_rows(pos_row, n_slots):
    slot = lax.broadcasted_iota(I32, (n_slots, pos_row.shape[1]), 0)
    return (slot == pos_row).astype(BF16)


def _dispatch(parts, group_tokens, group_slots):
    n_exp = parts[0][0].shape[0]
    width = parts[0][1].shape[1]
    part_groups = [pos.shape[1] // group_tokens for pos, _ in parts]
    workers = SC_CORES * SC_SUBCORES
    chunks = group_slots // SC_GATHER_ROWS
    assert all(g * n_exp % workers == 0 for g in part_groups)
    assert group_slots % SC_GATHER_ROWS == 0 and group_tokens % SC_LANES == 0
    assert n_exp & (n_exp - 1) == 0
    exp_shift = n_exp.bit_length() - 1
    row_shift = SC_GATHER_ROWS.bit_length() - 1

    def body(*refs):
        ins, (out_hbm, pos_v, idx_v, rows_v, sem) = refs[:2 * len(parts)], refs[2 * len(parts):]
        wid = lax.axis_index("s") * SC_CORES + lax.axis_index("c")
        lane = lax.iota(I32, SC_LANES)
        group_base = 0
        for part, groups in enumerate(part_groups):
            pos_hbm, h_hbm = ins[2 * part], ins[2 * part + 1]
            for k in range(groups * n_exp // workers):
                pair = wid + workers * k
                g = pair >> exp_shift
                e = pair & (n_exp - 1)
                pltpu.sync_copy(pos_hbm.at[e, pl.ds(g * group_tokens, group_tokens)], pos_v)

                @pl.loop(0, group_tokens // SC_LANES)
                def _(i):
                    p = pos_v[pl.ds(i * SC_LANES, SC_LANES)]
                    tok = g * group_tokens + i * SC_LANES + lane
                    slot = jnp.maximum(p, 0)
                    plsc.store_scatter(idx_v, [slot >> row_shift, slot & (SC_GATHER_ROWS - 1)],
                                       tok, mask=p >= 0)

                for c in range(chunks):
                    row0 = (group_base + g) * group_slots + c * SC_GATHER_ROWS
                    pltpu.async_copy(h_hbm.at[idx_v.at[c]], rows_v, sem).wait()
                    pltpu.sync_copy(rows_v, out_hbm.at[e, pl.ds(row0, SC_GATHER_ROWS)])
            group_base += groups

    mesh = plsc.VectorSubcoreMesh(core_axis_name="c", subcore_axis_name="s",
                                  num_cores=SC_CORES, num_subcores=SC_SUBCORES)
    return pl.kernel(
        body,
        out_type=jax.ShapeDtypeStruct((n_exp, sum(part_groups) * group_slots, width), I32),
        mesh=mesh,
        scratch_types=[pltpu.VMEM((group_tokens,), I32), pltpu.VMEM((chunks, SC_GATHER_ROWS), I32),
                       pltpu.VMEM((SC_GATHER_ROWS, width), I32), pltpu.SemaphoreType.DMA],
        compiler_params=pltpu.CompilerParams(needs_layout_passes=False),
        name="dispatch",
    )(*[a for part in parts for a in part])


def _expert_kernel(xg_ref, wg_ref, wu_ref, wd_ref, y_ref, wg_s, wu_s, wd_s):
    @pl.when(pl.program_id(1) == 0)
    def _():
        wg_s[...] = wg_ref[0].astype(BF16)
        wu_s[...] = wu_ref[0].astype(BF16)
        wd_s[...] = wd_ref[0].astype(BF16)

    xg = _unpack_bf16_pairs(xg_ref[0])
    a = _silu(_dot(xg, wg_s[...])) * _dot(xg, wu_s[...])
    y_ref[0] = _dot(a.astype(BF16), wd_s[...]).astype(y_ref.dtype)


def _experts(xg, w_gate, w_up, w_down, ts):
    e, s, half = xg.shape
    d, f = w_gate.shape[1:]
    assert d == 2 * half
    x_blk = pl.BlockSpec((1, ts, d), lambda x, i: (x, i, 0))
    return pl.pallas_call(
        _expert_kernel,
        name=f"experts_s{s}", grid=(e, s // ts),
        in_specs=[pl.BlockSpec((1, ts, half), lambda x, i: (x, i, 0)),
                  pl.BlockSpec((1, d, f), lambda x, i: (x, 0, 0)),
                  pl.BlockSpec((1, d, f), lambda x, i: (x, 0, 0)),
                  pl.BlockSpec((1, f, d), lambda x, i: (x, 0, 0))],
        out_specs=x_blk,
        out_shape=jax.ShapeDtypeStruct((e, s, d), BF16),
        scratch_shapes=[pltpu.VMEM((d, f), BF16), pltpu.VMEM((d, f), BF16),
                        pltpu.VMEM((f, d), BF16)],
        compiler_params=_params(("parallel", "arbitrary")),
    )(xg, w_gate, w_up, w_down)


def _combine_kernel(pos_ref, afftok_ref, y_ref, x1_ref, mod_ref, o_ref):
    e = pl.program_id(2)
    d = x1_ref.shape[1]

    @pl.when(e == 0)
    def _():
        o_ref[...] = jnp.zeros_like(o_ref)

    p = _one_hot_rows(pos_ref[pl.ds(e, 1), :], y_ref.shape[1])
    lane = lax.broadcasted_iota(I32, afftok_ref.shape, 1)
    gate = jnp.sum(jnp.where(lane == e, afftok_ref[...], 0.0), axis=1, keepdims=True)
    o_ref[...] += gate * _dot_tn(p, y_ref[0])

    @pl.when(e == pl.num_programs(2) - 1)
    def _():
        g2 = mod_ref[0][:, 5 * d:6 * d]
        o_ref[...] = x1_ref[...] + g2 * o_ref[...]


def _combine(pos, afftok, y, x1, mod3, mod_row, group_tokens, group_slots, group_base, tc):
    e, n = pos.shape
    d = x1.shape[1]
    groups = n // group_tokens
    per = group_tokens // tc
    tok = lambda g, j, x: (g * per + j, 0)
    return pl.pallas_call(
        _combine_kernel,
        name=f"combine_n{n}", grid=(groups, per, e),
        in_specs=[pl.BlockSpec((e, tc), lambda g, j, x: (0, g * per + j)),
                  pl.BlockSpec((tc, LANES), tok),
                  pl.BlockSpec((1, group_slots, d), lambda g, j, x: (x, group_base + g, 0)),
                  pl.BlockSpec((tc, d), tok),
                  pl.BlockSpec((1, 1, mod3.shape[2]), lambda g, j, x: (mod_row(g), 0, 0))],
        out_specs=pl.BlockSpec((tc, d), tok),
        out_shape=jax.ShapeDtypeStruct((n, d), F32),
        compiler_params=_params(("parallel", "parallel", "arbitrary")),
    )(pos, afftok, y, x1, mod3)


def _rope_tables(t_len):
    n_freq = DA_DQK // 4
    inv = ROPE_BASE ** (-jnp.arange(n_freq, dtype=F32) / n_freq)
    t = jnp.arange(t_len)
    pos = jnp.stack([(t // GRID_W).astype(F32), (t % GRID_W).astype(F32)], axis=1)
    ang = pos[:, :, None, None] * inv[None, None, None, :]
    ang = jnp.broadcast_to(ang, (t_len, 2, 2, n_freq))
    sign = jnp.array([-1.0, 1.0], F32)[None, None, :, None]
    cos = jnp.cos(ang).reshape(t_len, DA_DQK)
    sin = (jnp.sin(ang) * sign).reshape(t_len, DA_DQK)
    reps = SEG // DA_DQK
    return jnp.tile(cos, (1, reps)), jnp.tile(sin, (1, reps))


def _trunk(x, batch, mod3, mod_row_tok, weights, ctx_k, ctx_v, s0, rope, group_tokens):
    (norm1_w, norm2_w, w_in, b_gate, lb_logits, hgrn_norm_w, qkw, gm, lam_p, subln_w,
     w_bh, w_ba, w_out, rw_cat) = weights
    n, d = x.shape
    t_len = n // batch
    latent = rope is not None
    (q_h, bf, bb, kff, kfb, i_h, g_h, dq, dk, dv, gates) = _premix(
        x, mod3, functools.partial(mod_row_tok, tm=PREMIX_TOKENS), norm1_w, w_in, b_gate, lb_logits,
        qkw, gm, rope, PREMIX_TOKENS, BF16 if latent else F32, latent)
    o_h, s_new = _hgrn(q_h, bf, bb, kff, kfb, i_h, g_h, hgrn_norm_w, s0, batch)
    o_a = _attention(dq, dk, dv, ctx_k, ctx_v, lam_p, subln_w, batch, min(t_len, ATT_TQ * ATT_SPLIT))
    x1, h2, afft, afftok = _postmix(
        o_h, o_a, gates, x, mod3, functools.partial(mod_row_tok, tm=POSTMIX_TOKENS),
        w_bh, w_ba, w_out, norm2_w, rw_cat, POSTMIX_TOKENS)
    cap = EC_CAPACITY * t_len // N_EXPERTS
    pos = _route(afft, batch, cap, group_tokens // t_len)
    return (pos, h2, afftok, x1), dk, dv, s_new


def _moe(routed, mod3, mod_row_grps, w_eg, w_eu, w_ed, group_tokens, group_slots):
    xg = _dispatch([(pos, h2p) for pos, h2p, _, _ in routed], group_tokens, group_slots)
    y = _experts(xg, w_eg, w_eu, w_ed, group_slots)
    outs, group_base = [], 0
    for (pos, _, afftok, x1), mod_row in zip(routed, mod_row_grps):
        outs.append(_combine(pos, afftok, y, x1, mod3, mod_row, group_tokens, group_slots,
                             group_base, COMBINE_TOKENS))
        group_base += pos.shape[1] // group_tokens
    return outs


def kernel(x_prompt, x_sample, cache_k, cache_v, state_hgrn, c, c_ctx, norm1_w, norm2_w, w_mod,
           b_mod, w_in, b_gate, hgrn_lb_logits, hgrn_norm_w, qk_norm_w, diff_lambda, diff_subln_w,
           w_branch_hgrn, w_branch_attn, w_out, router_w, w_exp_gate, w_exp_up, w_exp_down):
    batch, seq, d = x_prompt.shape
    dec_batch, dec_seq, _ = x_sample.shape
    past = cache_k.shape[2]
    depth = w_in.shape[0]
    assert depth == 1
    group_tokens = dec_seq
    assert group_tokens % seq == 0 and (batch * seq) % group_tokens == 0
    l = 0

    rows = -(-(1 + dec_batch) // 8) * 8
    cond = jnp.zeros((rows, d), F32).at[0].set(c_ctx).at[1:1 + dec_batch].set(c)
    mod = _modulation(cond, w_mod[l], b_mod[l])
    mod3 = mod.reshape(rows, 1, 6 * d)

    gidx = jnp.arange(SEG) // DA_DQK
    gm = (gidx[:, None] == gidx[None, :]).astype(BF16) * (1.0 / DA_DQK)
    qkw = jnp.tile(qk_norm_w[l], (1, SEG // DA_DQK))
    rw_t = router_w[l].T
    rw_hi = rw_t.astype(BF16)
    rw_cat = jnp.concatenate([rw_hi, (rw_t - rw_hi.astype(F32)).astype(BF16)], axis=0)
    weights = (norm1_w[l][None], norm2_w[l][None], w_in[l].astype(BF16), b_gate[l][None],
               hgrn_lb_logits.reshape(4, SEG), hgrn_norm_w[l], qkw, gm, diff_lambda[l],
               diff_subln_w[l][None], w_branch_hgrn[l].astype(BF16),
               w_branch_attn[l].astype(BF16), w_out[l].astype(BF16),
               rw_cat)

    routed_ctx, k_new, v_new, s_new = _trunk(
        x_prompt.reshape(batch * seq, d), batch, mod3,
        lambda i, tm: 0, weights, None, None, None, None, group_tokens)
    per_req = dec_seq
    routed_lat, _, _, _ = _trunk(
        x_sample.reshape(dec_batch * dec_seq, d), dec_batch, mod3,
        lambda i, tm: 1 + (i * tm) // per_req, weights,
        cache_k[:, l], cache_v[:, l].reshape(dec_batch, past, SEG),
        state_hgrn[:, l], _rope_tables(dec_seq), group_tokens)
    group_slots = EC_CAPACITY * group_tokens // N_EXPERTS
    yp, ys = _moe([routed_ctx, routed_lat], mod3, [lambda g: 0, lambda g: 1 + g],
                  w_exp_gate[l], w_exp_up[l], w_exp_down[l], group_tokens, group_slots)

    return (yp.reshape(batch, seq, d), ys.reshape(dec_batch, dec_seq, d),
            k_new.reshape(batch, 1, seq, DA_HEADS, 2, DA_DQK),
            v_new.reshape(batch, 1, seq, DA_HEADS, DA_DV),
            s_new.reshape(batch, 1, 2, HG_HEADS, HG_DK, HG_DV))
```

```python
import functools
import math

import jax
import jax.numpy as jnp
from jax import lax
from jax.experimental import pallas as pl
from jax.experimental.pallas import tpu as pltpu
from jax.experimental.pallas import tpu_sc as plsc

F32 = jnp.float32
BF16 = jnp.bfloat16
I32 = jnp.int32

EPS = 1e-6
GRID_W = 64
HG_HEADS = 4
HG_DK = 128
HG_DV = 128
HG_CHUNK_LOG2 = 7
HG_CHUNK = 1 << HG_CHUNK_LOG2
HG_UNROLL = 4
DA_HEADS = 4
DA_DQK = 64
DA_DV = 128
N_EXPERTS = 16
EC_CAPACITY = 2
ROPE_BASE = 10000.0
SEG = 512
N_SEG = 12
LAM_INIT = 0.8 - 0.6 * math.exp(-0.3 * 0)
LANES = 128
ONES_ROWS = 16
ATT_SPLIT = 2
ATT_TQ = 256
PREMIX_TOKENS = 256
POSTMIX_TOKENS = 512
COMBINE_TOKENS = 2048
Q_SCALE = DA_DQK ** -0.5 * math.log2(math.e)
HG_SAFE_DECAY = 80.0
VMEM_LIMIT = 56 * 1024 * 1024
SC_CORES = 2
SC_SUBCORES = 16
SC_LANES = 16
SC_GATHER_ROWS = 128


def _dot(a, b):
    return jnp.dot(a, b, preferred_element_type=F32)


def _dot_nt(a, b):
    return lax.dot_general(a, b, (((1,), (1,)), ((), ())), preferred_element_type=F32)


def _dot_tn(a, b):
    return lax.dot_general(a, b, (((0,), (0,)), ((), ())), preferred_element_type=F32)


def _split2(x):
    hi = x.astype(BF16)
    lo = (x - hi.astype(F32)).astype(BF16)
    return hi, lo


def _silu(x):
    return x * jax.nn.sigmoid(x)


def _pack_bf16_pairs(x):
    w = x.shape[1] // 2
    bits = pltpu.bitcast(x.astype(BF16).astype(F32), I32)
    return lax.shift_right_logical(bits[:, :w], 16) | bits[:, w:]


def _unpack_bf16_pairs(words):
    lo = pltpu.bitcast(words << 16, F32)
    hi = pltpu.bitcast(words & jnp.int32(-65536), F32)
    return jnp.concatenate([lo, hi], axis=1).astype(BF16)


def _params(sem):
    return pltpu.CompilerParams(dimension_semantics=sem, vmem_limit_bytes=VMEM_LIMIT)


def _mod_kernel(c_ref, w_ref, b_ref, o_ref):
    s_hi, s_lo = _split2(_silu(c_ref[...]))
    w_hi, w_lo = _split2(w_ref[...])
    o_ref[...] = _dot(s_hi, w_hi) + _dot(s_hi, w_lo) + _dot(s_lo, w_hi) + b_ref[...]


def _modulation(cond, w_mod, b_mod):
    rows, d = cond.shape
    n = w_mod.shape[1]
    bn = 512
    return pl.pallas_call(
        _mod_kernel,
        name="modulation", grid=(n // bn,),
        in_specs=[pl.BlockSpec((rows, d), lambda j: (0, 0)),
                  pl.BlockSpec((d, bn), lambda j: (0, j)),
                  pl.BlockSpec((1, bn), lambda j: (0, j))],
        out_specs=pl.BlockSpec((rows, bn), lambda j: (0, j)),
        out_shape=jax.ShapeDtypeStruct((rows, n), F32),
        compiler_params=_params(("arbitrary",)),
    )(cond, w_mod, b_mod.reshape(1, n))


def _group_rms(z, gm_ref, w):
    ms = _dot((z * z).astype(BF16), gm_ref[...])
    return z * lax.rsqrt(ms + EPS) * w


def _rope(x, cos, sin_signed):
    n = x.shape[-1]
    lane = lax.broadcasted_iota(I32, x.shape, 1)
    partner = jnp.where((lane & 16) == 0, pltpu.roll(x, n - 16, 1), pltpu.roll(x, 16, 1))
    return x * cos + partner * sin_signed


def _premix_kernel(*refs, latent):
    if latent:
        (x_ref, mod_ref, n1_ref, win_ref, bg_ref, lbl_ref, qkw_ref, gm_ref, cos_ref, sin_ref,
         q_o, bf_o, bb_o, kff_o, kfb_o, i_o, g_o, dq_o, dk_o, dv_o, gate_o) = refs
    else:
        (x_ref, mod_ref, n1_ref, win_ref, bg_ref, lbl_ref, qkw_ref, gm_ref,
         q_o, bf_o, bb_o, kff_o, kfb_o, i_o, g_o, dq_o, dk_o, dv_o, gate_o) = refs
    d = x_ref.shape[1]
    mod = mod_ref[0]
    sh1, sc1 = mod[:, 0:d], mod[:, d:2 * d]
    x = x_ref[...]
    xn = x * lax.rsqrt(jnp.mean(x * x, axis=-1, keepdims=True) + EPS) * n1_ref[...]
    hb = (xn * (1.0 + sc1) + sh1).astype(BF16)

    def seg(j):
        return _dot(hb, win_ref[:, j * SEG:(j + 1) * SEG])

    def lower_bound(direction):
        l0 = lbl_ref[2 * direction:2 * direction + 1, :]
        l1 = lbl_ref[2 * direction + 1:2 * direction + 2, :]
        mx = jnp.maximum(l0, l1)
        e0, e1 = jnp.exp(l0 - mx), jnp.exp(l1 - mx)
        return e0 / (e0 + e1)

    tm = x.shape[0]
    row = lax.broadcasted_iota(I32, (tm, tm), 0)
    col = lax.broadcasted_iota(I32, (tm, tm), 1)
    same_chunk = (row >> HG_CHUNK_LOG2) == (col >> HG_CHUNK_LOG2)

    def store(o_ref, val):
        val = val.astype(o_ref.dtype)
        if len(o_ref.shape) == 2:
            o_ref[...] = val
        elif len(o_ref.shape) == 3:
            for h in range(o_ref.shape[0]):
                o_ref[h] = val[:, h * LANES:(h + 1) * LANES]
        else:
            for h in range(o_ref.shape[2]):
                for m in range(2):
                    lo = (2 * h + m) * DA_DQK
                    o_ref[0, :, h, m, :] = val[:, lo:lo + DA_DQK]

    store(q_o, _silu(seg(0)))
    for j, b_o, kf_o, order in ((1, bf_o, kff_o, row >= col), (2, bb_o, kfb_o, row <= col)):
        lbd = lower_bound(j - 1)
        f = lbd + (1.0 - lbd) * jax.nn.sigmoid(seg(j))
        tri = (same_chunk & order).astype(BF16)
        hi, lo = _split2(jnp.log(f))
        store(b_o, _dot(tri, hi) + _dot(tri, lo))
        store(kf_o, 1.0 - f)
    store(i_o, seg(3))
    store(g_o, _silu(seg(4)))
    qn = _group_rms(seg(5), gm_ref, qkw_ref[0:1, :]) * Q_SCALE
    kn = _group_rms(seg(6), gm_ref, qkw_ref[1:2, :])
    if latent:
        qn = _rope(qn, cos_ref[...], sin_ref[...])
        kn = _rope(kn, cos_ref[...], sin_ref[...])
    store(dq_o, qn)
    store(dk_o, kn)
    store(dv_o, seg(7))
    for j in range(4):
        z = seg(8 + j) + bg_ref[:, j * SEG:(j + 1) * SEG]
        gate_o[:, j * SEG:(j + 1) * SEG] = jax.nn.sigmoid(z).astype(gate_o.dtype)


def _premix(x, mod3, mod_row, norm1_w, w_in, b_gate, lb_logits, qkw, gm, rope, tm, kv_dtype,
            kv_head_major):
    n, d = x.shape
    latent = rope is not None
    const = lambda i: (0, 0)
    in_specs = [pl.BlockSpec((tm, d), lambda i: (i, 0)),
                pl.BlockSpec((1, 1, mod3.shape[2]), lambda i: (mod_row(i), 0, 0)),
                pl.BlockSpec((1, d), const),
                pl.BlockSpec(w_in.shape, const),
                pl.BlockSpec(b_gate.shape, const),
                pl.BlockSpec(lb_logits.shape, const),
                pl.BlockSpec(qkw.shape, const),
                pl.BlockSpec(gm.shape, const)]
    args = [x, mod3, norm1_w, w_in, b_gate, lb_logits, qkw, gm]
    if latent:
        cos, sin = rope
        nblk = cos.shape[0] // tm
        in_specs += [pl.BlockSpec((tm, SEG), lambda i: (i % nblk, 0))] * 2
        args += [cos, sin]
    heads = SEG // LANES
    head_spec = pl.BlockSpec((heads, tm, LANES), lambda i: (0, i, 0))
    tok_spec = pl.BlockSpec((tm, SEG), lambda i: (i, 0))
    out_dtypes = [BF16, F32, F32, BF16, BF16, BF16, BF16, BF16]
    out_shape = [jax.ShapeDtypeStruct((heads, n, LANES), t) for t in out_dtypes]
    if kv_head_major:
        out_shape += [jax.ShapeDtypeStruct((heads, n, LANES), kv_dtype)] * 2
        kv_specs = [head_spec] * 2
    else:
        k_shape = (n // tm, tm, DA_HEADS, 2, DA_DQK)
        out_shape += [jax.ShapeDtypeStruct(k_shape, kv_dtype),
                      jax.ShapeDtypeStruct((n, SEG), kv_dtype)]
        kv_specs = [pl.BlockSpec((1,) + k_shape[1:], lambda i: (i, 0, 0, 0, 0)), tok_spec]
    out_shape.append(jax.ShapeDtypeStruct((n, 4 * SEG), BF16))
    out_specs = [head_spec] * 8 + kv_specs + [pl.BlockSpec((tm, 4 * SEG), lambda i: (i, 0))]
    return pl.pallas_call(
        functools.partial(_premix_kernel, latent=latent),
        name=f"premix_n{n}", grid=(n // tm,),
        in_specs=in_specs, out_specs=out_specs, out_shape=out_shape,
        compiler_params=_params(("parallel",)),
    )(*args)


def _hgrn_chunk_local(q, b, total, k, v, keep, safe):
    if safe:
        ref = b[HG_CHUNK // 2:HG_CHUNK // 2 + 1, :]
        qa = q * jnp.exp(b - ref)
        kb = k * jnp.exp(ref - b)
        attn = jnp.where(keep, _dot_nt(qa.astype(BF16), kb.astype(BF16)), 0.0)
        qe = qa * jnp.exp(ref)
        kd = kb * jnp.exp(total - ref)
    else:
        qe = q * jnp.exp(b)
        kd = k * jnp.exp(total - b)
        col = lax.broadcasted_iota(I32, (HG_CHUNK, HG_CHUNK), 1)

        def column(s, acc):
            onehot = (lax.broadcasted_iota(I32, (HG_CHUNK, 1), 0) == s).astype(F32)
            bs = jnp.sum(b * onehot, axis=0, keepdims=True)
            ks = jnp.sum(k * onehot, axis=0, keepdims=True)
            w = jnp.sum(q * ks * jnp.exp(jnp.minimum(b - bs, 0.0)), axis=1, keepdims=True)
            return jnp.where(col == s, w, acc)

        attn = lax.fori_loop(0, HG_CHUNK, column, jnp.zeros((HG_CHUNK, HG_CHUNK), F32))
        attn = jnp.where(keep, attn, 0.0)
    vt = v.astype(F32).T.astype(BF16)
    lhs = jnp.concatenate([qe.astype(BF16), attn.astype(BF16)], axis=1)
    return lhs, vt, _dot(vt, kd.astype(BF16)), jnp.exp(total)


def _hgrn_scan_group(chunks, st):
    outs = []
    for lhs, vt, inc, decay in chunks:
        outs.append(_dot_nt(lhs, jnp.concatenate([st.astype(BF16), vt], axis=1)))
        st = st * decay + inc
    return outs, st


def _hgrn_kernel(*refs, has_state):
    if has_state:
        (q_ref, bf_ref, bb_ref, kff_ref, kfb_ref, i_ref, g_ref, nw_ref, s0_ref,
         o_ref, sn_ref, of_s, ob_s) = refs
    else:
        (q_ref, bf_ref, bb_ref, kff_ref, kfb_ref, i_ref, g_ref, nw_ref,
         o_ref, sn_ref, of_s, ob_s) = refs
    t_len = q_ref.shape[1]
    nc = t_len // HG_CHUNK
    unroll = min(HG_UNROLL, nc)
    head = pl.program_id(1)
    row = lax.broadcasted_iota(I32, (HG_CHUNK, HG_CHUNK), 0)
    col = lax.broadcasted_iota(I32, (HG_CHUNK, HG_CHUNK), 1)
    keep_f, keep_b = row >= col, row <= col

    mid_f = bf_ref[0, pl.ds(HG_CHUNK // 2, nc, stride=HG_CHUNK), :]
    tot_f = bf_ref[0, pl.ds(HG_CHUNK - 1, nc, stride=HG_CHUNK), :]
    mid_b = bb_ref[0, pl.ds(HG_CHUNK // 2, nc, stride=HG_CHUNK), :]
    tot_b = bb_ref[0, pl.ds(0, nc, stride=HG_CHUNK), :]
    worst = jnp.minimum(jnp.min(jnp.minimum(mid_f, tot_f - mid_f)),
                        jnp.min(jnp.minimum(mid_b, tot_b - mid_b)))

    if has_state:
        st_f0, st_b0 = s0_ref[0, 0, 0].T, s0_ref[0, 1, 0].T
    else:
        st_f0 = st_b0 = jnp.zeros((HG_DV, HG_DK), F32)

    def scan(safe):
        def body(it, carry):
            st_f, st_b = carry
            rows_f, rows_b, loc_f, loc_b = [], [], [], []
            for u in range(unroll):
                c = it * unroll + u
                sf = pl.multiple_of(c * HG_CHUNK, HG_CHUNK)
                sb = pl.multiple_of((nc - 1 - c) * HG_CHUNK, HG_CHUNK)
                rf, rb = pl.ds(sf, HG_CHUNK), pl.ds(sb, HG_CHUNK)
                rows_f.append(rf)
                rows_b.append(rb)
                loc_f.append(_hgrn_chunk_local(
                    q_ref[0, rf, :].astype(F32), bf_ref[0, rf, :],
                    bf_ref[0, pl.ds(sf + HG_CHUNK - 1, 1), :],
                    kff_ref[0, rf, :].astype(F32), i_ref[0, rf, :], keep_f, safe))
                loc_b.append(_hgrn_chunk_local(
                    q_ref[0, rb, :].astype(F32), bb_ref[0, rb, :], bb_ref[0, pl.ds(sb, 1), :],
                    kfb_ref[0, rb, :].astype(F32), i_ref[0, rb, :], keep_b, safe))
            outs_f, st_f = _hgrn_scan_group(loc_f, st_f)
            outs_b, st_b = _hgrn_scan_group(loc_b, st_b)
            for rf, rb, o_f, o_b in zip(rows_f, rows_b, outs_f, outs_b):
                of_s[rf, :] = o_f
                ob_s[rb, :] = o_b
            return st_f, st_b
        return lax.fori_loop(0, nc // unroll, body, (st_f0, st_b0))

    st_f, st_b = lax.cond(worst >= -HG_SAFE_DECAY, lambda: scan(True), lambda: scan(False))
    sn_ref[0, 0, 0] = st_f.T
    sn_ref[0, 1, 0] = st_b.T
    o = of_s[...] + ob_s[...]
    nw = nw_ref[pl.ds(head, 1), :]
    on = o * lax.rsqrt(jnp.mean(o * o, axis=-1, keepdims=True) + EPS) * nw
    o_ref[0] = (on * g_ref[0].astype(F32)).astype(o_ref.dtype)


def _hgrn(q, bf, bb, kff, kfb, iv, g, norm_w, s0, batch):
    heads, n, _ = q.shape
    t_len = n // batch
    blk = pl.BlockSpec((1, t_len, HG_DK), lambda b, h: (h, b, 0))
    st_blk = pl.BlockSpec((1, 2, 1, HG_DK, HG_DV), lambda b, h: (b, 0, h, 0, 0))
    args = [q, bf, bb, kff, kfb, iv, g, norm_w]
    in_specs = [blk] * 7 + [pl.BlockSpec(norm_w.shape, lambda b, h: (0, 0))]
    if s0 is not None:
        args.append(s0)
        in_specs.append(st_blk)
    o, s_new = pl.pallas_call(
        functools.partial(_hgrn_kernel, has_state=s0 is not None),
        name=f"hgrn_n{n}", grid=(batch, HG_HEADS),
        in_specs=in_specs,
        out_specs=[blk, st_blk],
        out_shape=[jax.ShapeDtypeStruct((heads, n, HG_DV), BF16),
                   jax.ShapeDtypeStruct((batch, 2, HG_HEADS, HG_DK, HG_DV), F32)],
        scratch_shapes=[pltpu.VMEM((t_len, HG_DV), F32), pltpu.VMEM((t_len, HG_DV), F32)],
        compiler_params=_params(("parallel", "parallel")),
    )(*args)
    return o, s_new


def _attn_kernel(*refs, has_ctx):
    if has_ctx:
        q_ref, k_ref, v_ref, ck_ref, cv_ref, lam_ref, sw_ref, o_ref, k_s, vt_s = refs
    else:
        q_ref, k_ref, v_ref, lam_ref, sw_ref, o_ref, k_s, vt_s = refs
    t_own = k_ref.shape[1]

    def load_keys(ref):
        if len(ref.shape) == 3:
            return ref[0]
        return jnp.concatenate([ref[0, :, 0, 0, :], ref[0, :, 0, 1, :]], axis=1)

    @pl.when(pl.program_id(2) == 0)
    def _():
        k_s[0:t_own, :] = load_keys(k_ref).astype(BF16)
        vt_s[0:DA_DV, 0:t_own] = v_ref[0].astype(F32).T.astype(BF16)
        if has_ctx:
            k_s[t_own:, :] = load_keys(ck_ref).astype(BF16)
            vt_s[0:DA_DV, t_own:] = cv_ref[0].T.astype(BF16)
        vt_s[DA_DV:, :] = jnp.ones((ONES_ROWS, vt_s.shape[1]), BF16)

    lv = lam_ref[...]
    lam = (jnp.exp(jnp.sum(lv[0:1] * lv[1:2], keepdims=True))
           - jnp.exp(jnp.sum(lv[2:3] * lv[3:4], keepdims=True)) + LAM_INIT)
    tq = q_ref.shape[1] // ATT_SPLIT
    dim = lax.broadcasted_iota(I32, (2 * DA_DQK, tq), 0)

    def scores(i):
        qt = q_ref[0, i * tq:(i + 1) * tq, :].astype(F32).T
        q_both = jnp.concatenate([jnp.where(dim < DA_DQK, qt, 0.0),
                                  jnp.where(dim >= DA_DQK, qt, 0.0)], axis=1).astype(BF16)
        return _dot(k_s[...], q_both)

    def finish(i, st):
        pt = jnp.exp2(st - jnp.max(st, axis=0, keepdims=True)).astype(BF16)
        r = _dot(vt_s[...], pt)
        r = r[0:DA_DV] * (1.0 / r[DA_DV:DA_DV + 1])
        o = (r[:, 0:tq] - lam * r[:, tq:2 * tq]).T
        on = o * lax.rsqrt(jnp.mean(o * o, axis=-1, keepdims=True) + EPS) * sw_ref[...]
        o_ref[0, i * tq:(i + 1) * tq, :] = (on * (1.0 - LAM_INIT)).astype(o_ref.dtype)

    sts = [scores(i) for i in range(ATT_SPLIT)]
    for i in range(ATT_SPLIT):
        finish(i, sts[i])


def _attention(q, k, v, ctx_k, ctx_v, lam_p, subln_w, batch, tq):
    heads, n, _ = q.shape
    t_len = n // batch
    nq = t_len // tq
    q_blk = pl.BlockSpec((1, tq, DA_DV), lambda b, h, i: (h, b * nq + i, 0))
    const = lambda b, h, i: (0, 0)
    if k.ndim == 3:
        kv_blk = pl.BlockSpec((1, t_len, DA_DV), lambda b, h, i: (h, b, 0))
        in_specs = [q_blk, kv_blk, kv_blk]
        args = [q, k, v]
    else:
        assert k.shape == (batch, t_len, DA_HEADS, 2, DA_DQK)
        in_specs = [q_blk, pl.BlockSpec((1, t_len, 1, 2, DA_DQK), lambda b, h, i: (b, 0, h, 0, 0)),
                    pl.BlockSpec((1, t_len, DA_DV), lambda b, h, i: (b, 0, h))]
        args = [q, k, v.reshape(batch, t_len, SEG)]
    n_keys = t_len
    if ctx_k is not None:
        past = ctx_k.shape[1]
        n_keys += past
        ck_blk = pl.BlockSpec((1, past, 1, 2, DA_DQK), lambda b, h, i: (b, 0, h, 0, 0))
        cv_blk = pl.BlockSpec((1, past, DA_DV), lambda b, h, i: (b, 0, h))
        args += [ctx_k, ctx_v]
        in_specs += [ck_blk, cv_blk]
    args += [lam_p, subln_w]
    in_specs += [pl.BlockSpec(lam_p.shape, const), pl.BlockSpec(subln_w.shape, const)]
    o = pl.pallas_call(
        functools.partial(_attn_kernel, has_ctx=ctx_k is not None),
        name=f"attn_n{n}", grid=(batch, DA_HEADS, nq),
        in_specs=in_specs, out_specs=q_blk,
        out_shape=jax.ShapeDtypeStruct((heads, n, DA_DV), BF16),
        scratch_shapes=[pltpu.VMEM((n_keys, 2 * DA_DQK), BF16),
                        pltpu.VMEM((DA_DV + ONES_ROWS, n_keys), BF16)],
        compiler_params=_params(("parallel", "parallel", "arbitrary")),
    )(*args)
    return o


def _postmix_kernel(oh_ref, oa_ref, gate_ref, x_ref, mod_ref, wbh_ref, wba_ref, wout_ref,
                    n2_ref, rw_ref, x1_o, h2_o, afft_o, afftok_o):
    d = x_ref.shape[1]
    mod = mod_ref[0]
    g1, sh2, sc2 = mod[:, 2 * d:3 * d], mod[:, 3 * d:4 * d], mod[:, 4 * d:5 * d]
    g_h = gate_ref[:, 0:d].astype(F32)
    g_a = gate_ref[:, d:2 * d].astype(F32)

    def heads_on_lanes(ref):
        return jnp.concatenate([ref[h] for h in range(ref.shape[0])], axis=1)

    merged = (g_h * _dot(heads_on_lanes(oh_ref), wbh_ref[...])
              + g_a * _dot(heads_on_lanes(oa_ref), wba_ref[...]))
    x1 = x_ref[...] + g1 * _dot(merged.astype(BF16), wout_ref[...])
    x1_o[...] = x1
    xn = x1 * lax.rsqrt(jnp.mean(x1 * x1, axis=-1, keepdims=True) + EPS) * n2_ref[...]
    h2 = xn * (1.0 + sc2) + sh2
    h2_o[...] = _pack_bf16_pairs(h2)
    h_hi, h_lo = _split2(h2)
    rw = rw_ref[...]
    t1 = _dot_nt(rw, h_hi)
    t2 = _dot_nt(rw, h_lo)
    e = N_EXPERTS
    logits = t1[0:e] + t1[e:2 * e] + t2[0:e]
    mx = jnp.max(logits, axis=0, keepdims=True)
    p = jnp.exp(logits - mx)
    aff = p / jnp.sum(p, axis=0, keepdims=True)
    afft_o[...] = aff
    pad = jnp.zeros((LANES - e, aff.shape[1]), F32)
    afftok_o[...] = jnp.concatenate([aff, pad], axis=0).T


def _postmix(o_h, o_a, gates, x, mod3, mod_row, w_bh, w_ba, w_out, norm2_w, rw_cat, tm):
    n, d = x.shape
    const = lambda i: (0, 0)
    row = lambda i: (i, 0)
    return pl.pallas_call(
        _postmix_kernel,
        name=f"postmix_n{n}", grid=(n // tm,),
        in_specs=[pl.BlockSpec((o_h.shape[0], tm, LANES), lambda i: (0, i, 0)),
                  pl.BlockSpec((o_a.shape[0], tm, LANES), lambda i: (0, i, 0)),
                  pl.BlockSpec((tm, 4 * SEG), row), pl.BlockSpec((tm, d), row),
                  pl.BlockSpec((1, 1, mod3.shape[2]), lambda i: (mod_row(i), 0, 0)),
                  pl.BlockSpec(w_bh.shape, const), pl.BlockSpec(w_ba.shape, const),
                  pl.BlockSpec(w_out.shape, const), pl.BlockSpec((1, d), const),
                  pl.BlockSpec(rw_cat.shape, const)],
        out_specs=[pl.BlockSpec((tm, d), row), pl.BlockSpec((tm, d // 2), row),
                   pl.BlockSpec((N_EXPERTS, tm), lambda i: (0, i)),
                   pl.BlockSpec((tm, LANES), row)],
        out_shape=[jax.ShapeDtypeStruct((n, d), F32), jax.ShapeDtypeStruct((n, d // 2), I32),
                   jax.ShapeDtypeStruct((N_EXPERTS, n), F32),
                   jax.ShapeDtypeStruct((n, LANES), F32)],
        compiler_params=_params(("parallel",)),
    )(o_h, o_a, gates, x, mod3, w_bh, w_ba, w_out, norm2_w, rw_cat)


def _lane_cumsum_exclusive(x, blk):
    e, t = x.shape
    r = lax.broadcasted_iota(I32, (blk, blk), 0)
    c = lax.broadcasted_iota(I32, (blk, blk), 1)
    upper = (r < c).astype(BF16)
    carry = jnp.zeros((e, 1), F32)
    parts = []
    for j in range(t // blk):
        xb = x[:, j * blk:(j + 1) * blk]
        parts.append(_dot(xb.astype(BF16), upper) + carry)
        carry = carry + jnp.sum(xb, axis=1, keepdims=True)
    return parts[0] if len(parts) == 1 else jnp.concatenate(parts, axis=1)


def _route_kernel(aff_ref, pos_ref, *, cap, req_per_group):
    aff = aff_ref[...]
    bits = pltpu.bitcast(aff, I32)
    t_len = aff.shape[1]

    def count(mask):
        return jnp.sum(mask.astype(F32), axis=1, keepdims=True)

    def step(i, th):
        cand = th | (jnp.int32(1) << (30 - i))
        return jnp.where(count(bits >= cand) >= cap, cand, th)

    th = lax.fori_loop(0, 31, step, jnp.zeros((aff.shape[0], 1), I32))
    gt = bits > th
    eq = (bits == th).astype(F32)
    need = cap - count(gt)
    blk = min(t_len, 256)
    tie_rank = _lane_cumsum_exclusive(eq, blk)
    sel = jnp.where(gt, 1.0, jnp.where(tie_rank < need, eq, 0.0))
    slot = _lane_cumsum_exclusive(sel, blk)
    offset = (pl.program_id(0) % req_per_group) * cap
    pos_ref[...] = jnp.where(sel > 0.0, slot.astype(I32) + offset, -1)


def _route(afft, batch, cap, req_per_group):
    e, n = afft.shape
    t_len = n // batch
    blk = pl.BlockSpec((e, t_len), lambda b: (0, b))
    return pl.pallas_call(
        functools.partial(_route_kernel, cap=cap, req_per_group=req_per_group),
        name=f"route_n{n}", grid=(batch,), in_specs=[blk], out_specs=blk,
        out_shape=jax.ShapeDtypeStruct((e, n), I32),
        compiler_params=_params(("parallel",)),
    )(afft)


def _one_hot_rows(pos_row, n_slots):
    slot = lax.broadcasted_iota(I32, (n_slots, pos_row.shape[1]), 0)
    return (slot == pos_row).astype(BF16)


def _dispatch(parts, group_tokens, group_slots):
    n_exp = parts[0][0].shape[0]
    width = parts[0][1].shape[1]
    part_groups = [pos.shape[1] // group_tokens for pos, _ in parts]
    workers = SC_CORES * SC_SUBCORES
    chunks = group_slots // SC_GATHER_ROWS
    assert all(g * n_exp % workers == 0 for g in part_groups)
    assert group_slots % SC_GATHER_ROWS == 0 and group_tokens % SC_LANES == 0
    assert n_exp & (n_exp - 1) == 0
    exp_shift = n_exp.bit_length() - 1
    row_shift = SC_GATHER_ROWS.bit_length() - 1

    def body(*refs):
        ins, (out_hbm, pos_v, idx_v, rows_v, sem) = refs[:2 * len(parts)], refs[2 * len(parts):]
        wid = lax.axis_index("s") * SC_CORES + lax.axis_index("c")
        lane = lax.iota(I32, SC_LANES)
        group_base = 0
        for part, groups in enumerate(part_groups):
            pos_hbm, h_hbm = ins[2 * part], ins[2 * part + 1]
            for k in range(groups * n_exp // workers):
                pair = wid + workers * k
                g = pair >> exp_shift
                e = pair & (n_exp - 1)
                pltpu.sync_copy(pos_hbm.at[e, pl.ds(g * group_tokens, group_tokens)], pos_v)

                @pl.loop(0, group_tokens // SC_LANES)
                def _(i):
                    p = pos_v[pl.ds(i * SC_LANES, SC_LANES)]
                    tok = g * group_tokens + i * SC_LANES + lane
                    slot = jnp.maximum(p, 0)
                    plsc.store_scatter(idx_v, [slot >> row_shift, slot & (SC_GATHER_ROWS - 1)],
                                       tok, mask=p >= 0)

                for c in range(chunks):
                    row0 = (group_base + g) * group_slots + c * SC_GATHER_ROWS
                    pltpu.async_copy(h_hbm.at[idx_v.at[c]], rows_v, sem).wait()
                    pltpu.sync_copy(rows_v, out_hbm.at[e, pl.ds(row0, SC_GATHER_ROWS)])
            group_base += groups

    mesh = plsc.VectorSubcoreMesh(core_axis_name="c", subcore_axis_name="s",
                                  num_cores=SC_CORES, num_subcores=SC_SUBCORES)
    return pl.kernel(
        body,
        out_type=jax.ShapeDtypeStruct((n_exp, sum(part_groups) * group_slots, width), I32),
        mesh=mesh,
        scratch_types=[pltpu.VMEM((group_tokens,), I32), pltpu.VMEM((chunks, SC_GATHER_ROWS), I32),
                       pltpu.VMEM((SC_GATHER_ROWS, width), I32), pltpu.SemaphoreType.DMA],
        compiler_params=pltpu.CompilerParams(needs_layout_passes=False),
        name="dispatch",
    )(*[a for part in parts for a in part])


def _expert_kernel(xg_ref, wg_ref, wu_ref, wd_ref, y_ref, wg_s, wu_s, wd_s):
    @pl.when(pl.program_id(1) == 0)
    def _():
        wg_s[...] = wg_ref[0].astype(BF16)
        wu_s[...] = wu_ref[0].astype(BF16)
        wd_s[...] = wd_ref[0].astype(BF16)

    xg = _unpack_bf16_pairs(xg_ref[0])
    a = _silu(_dot(xg, wg_s[...])) * _dot(xg, wu_s[...])
    y_ref[0] = _dot(a.astype(BF16), wd_s[...]).astype(y_ref.dtype)


def _experts(xg, w_gate, w_up, w_down, ts):
    e, s, half = xg.shape
    d, f = w_gate.shape[1:]
    assert d == 2 * half
    x_blk = pl.BlockSpec((1, ts, d), lambda x, i: (x, i, 0))
    return pl.pallas_call(
        _expert_kernel,
        name=f"experts_s{s}", grid=(e, s // ts),
        in_specs=[pl.BlockSpec((1, ts, half), lambda x, i: (x, i, 0)),
                  pl.BlockSpec((1, d, f), lambda x, i: (x, 0, 0)),
                  pl.BlockSpec((1, d, f), lambda x, i: (x, 0, 0)),
                  pl.BlockSpec((1, f, d), lambda x, i: (x, 0, 0))],
        out_specs=x_blk,
        out_shape=jax.ShapeDtypeStruct((e, s, d), BF16),
        scratch_shapes=[pltpu.VMEM((d, f), BF16), pltpu.VMEM((d, f), BF16),
                        pltpu.VMEM((f, d), BF16)],
        compiler_params=_params(("parallel", "arbitrary")),
    )(xg, w_gate, w_up, w_down)


def _combine_kernel(pos_ref, afftok_ref, y_ref, x1_ref, mod_ref, o_ref):
    e = pl.program_id(2)
    d = x1_ref.shape[1]

    @pl.when(e == 0)
    def _():
        o_ref[...] = jnp.zeros_like(o_ref)

    p = _one_hot_rows(pos_ref[pl.ds(e, 1), :], y_ref.shape[1])
    lane = lax.broadcasted_iota(I32, afftok_ref.shape, 1)
    gate = jnp.sum(jnp.where(lane == e, afftok_ref[...], 0.0), axis=1, keepdims=True)
    o_ref[...] += gate * _dot_tn(p, y_ref[0])

    @pl.when(e == pl.num_programs(2) - 1)
    def _():
        g2 = mod_ref[0][:, 5 * d:6 * d]
        o_ref[...] = x1_ref[...] + g2 * o_ref[...]


def _combine(pos, afftok, y, x1, mod3, mod_row, group_tokens, group_slots, group_base, tc):
    e, n = pos.shape
    d = x1.shape[1]
    groups = n // group_tokens
    per = group_tokens // tc
    tok = lambda g, j, x: (g * per + j, 0)
    return pl.pallas_call(
        _combine_kernel,
        name=f"combine_n{n}", grid=(groups, per, e),
        in_specs=[pl.BlockSpec((e, tc), lambda g, j, x: (0, g * per + j)),
                  pl.BlockSpec((tc, LANES), tok),
                  pl.BlockSpec((1, group_slots, d), lambda g, j, x: (x, group_base + g, 0)),
                  pl.BlockSpec((tc, d), tok),
                  pl.BlockSpec((1, 1, mod3.shape[2]), lambda g, j, x: (mod_row(g), 0, 0))],
        out_specs=pl.BlockSpec((tc, d), tok),
        out_shape=jax.ShapeDtypeStruct((n, d), F32),
        compiler_params=_params(("parallel", "parallel", "arbitrary")),
    )(pos, afftok, y, x1, mod3)


def _rope_tables(t_len):
    n_freq = DA_DQK // 4
    inv = ROPE_BASE ** (-jnp.arange(n_freq, dtype=F32) / n_freq)
    t = jnp.arange(t_len)
    pos = jnp.stack([(t // GRID_W).astype(F32), (t % GRID_W).astype(F32)], axis=1)
    ang = pos[:, :, None, None] * inv[None, None, None, :]
    ang = jnp.broadcast_to(ang, (t_len, 2, 2, n_freq))
    sign = jnp.array([-1.0, 1.0], F32)[None, None, :, None]
    cos = jnp.cos(ang).reshape(t_len, DA_DQK)
    sin = (jnp.sin(ang) * sign).reshape(t_len, DA_DQK)
    reps = SEG // DA_DQK
    return jnp.tile(cos, (1, reps)), jnp.tile(sin, (1, reps))


def _trunk(x, batch, mod3, mod_row_tok, weights, ctx_k, ctx_v, s0, rope, group_tokens):
    (norm1_w, norm2_w, w_in, b_gate, lb_logits, hgrn_norm_w, qkw, gm, lam_p, subln_w,
     w_bh, w_ba, w_out, rw_cat) = weights
    n, d = x.shape
    t_len = n // batch
    latent = rope is not None
    (q_h, bf, bb, kff, kfb, i_h, g_h, dq, dk, dv, gates) = _premix(
        x, mod3, functools.partial(mod_row_tok, tm=PREMIX_TOKENS), norm1_w, w_in, b_gate, lb_logits,
        qkw, gm, rope, PREMIX_TOKENS, BF16 if latent else F32, latent)
    o_h, s_new = _hgrn(q_h, bf, bb, kff, kfb, i_h, g_h, hgrn_norm_w, s0, batch)
    o_a = _attention(dq, dk, dv, ctx_k, ctx_v, lam_p, subln_w, batch, min(t_len, ATT_TQ * ATT_SPLIT))
    x1, h2, afft, afftok = _postmix(
        o_h, o_a, gates, x, mod3, functools.partial(mod_row_tok, tm=POSTMIX_TOKENS),
        w_bh, w_ba, w_out, norm2_w, rw_cat, POSTMIX_TOKENS)
    cap = EC_CAPACITY * t_len // N_EXPERTS
    pos = _route(afft, batch, cap, group_tokens // t_len)
    return (pos, h2, afftok, x1), dk, dv, s_new


def _moe(routed, mod3, mod_row_grps, w_eg, w_eu, w_ed, group_tokens, group_slots):
    xg = _dispatch([(pos, h2p) for pos, h2p, _, _ in routed], group_tokens, group_slots)
    y = _experts(xg, w_eg, w_eu, w_ed, group_slots)
    outs, group_base = [], 0
    for (pos, _, afftok, x1), mod_row in zip(routed, mod_row_grps):
        outs.append(_combine(pos, afftok, y, x1, mod3, mod_row, group_tokens, group_slots,
                             group_base, COMBINE_TOKENS))
        group_base += pos.shape[1] // group_tokens
    return outs


def kernel(x_prompt, x_sample, cache_k, cache_v, state_hgrn, c, c_ctx, norm1_w, norm2_w, w_mod,
           b_mod, w_in, b_gate, hgrn_lb_logits, hgrn_norm_w, qk_norm_w, diff_lambda, diff_subln_w,
           w_branch_hgrn, w_branch_attn, w_out, router_w, w_exp_gate, w_exp_up, w_exp_down):
    batch, seq, d = x_prompt.shape
    dec_batch, dec_seq, _ = x_sample.shape
    past = cache_k.shape[2]
    depth = w_in.shape[0]
    assert depth == 1
    group_tokens = dec_seq
    assert group_tokens % seq == 0 and (batch * seq) % group_tokens == 0
    l = 0

    rows = -(-(1 + dec_batch) // 8) * 8
    cond = jnp.zeros((rows, d), F32).at[0].set(c_ctx).at[1:1 + dec_batch].set(c)
    mod = _modulation(cond, w_mod[l], b_mod[l])
    mod3 = mod.reshape(rows, 1, 6 * d)

    gidx = jnp.arange(SEG) // DA_DQK
    gm = (gidx[:, None] == gidx[None, :]).astype(BF16) * (1.0 / DA_DQK)
    qkw = jnp.tile(qk_norm_w[l], (1, SEG // DA_DQK))
    rw_t = router_w[l].T
    rw_hi = rw_t.astype(BF16)
    rw_cat = jnp.concatenate([rw_hi, (rw_t - rw_hi.astype(F32)).astype(BF16)], axis=0)
    weights = (norm1_w[l][None], norm2_w[l][None], w_in[l].astype(BF16), b_gate[l][None],
               hgrn_lb_logits.reshape(4, SEG), hgrn_norm_w[l], qkw, gm, diff_lambda[l],
               diff_subln_w[l][None], w_branch_hgrn[l].astype(BF16),
               w_branch_attn[l].astype(BF16), w_out[l].astype(BF16),
               rw_cat)

    routed_ctx, k_new, v_new, s_new = _trunk(
        x_prompt.reshape(batch * seq, d), batch, mod3,
        lambda i, tm: 0, weights, None, None, None, None, group_tokens)
    per_req = dec_seq
    routed_lat, _, _, _ = _trunk(
        x_sample.reshape(dec_batch * dec_seq, d), dec_batch, mod3,
        lambda i, tm: 1 + (i * tm) // per_req, weights,
        cache_k[:, l], cache_v[:, l].reshape(dec_batch, past, SEG),
        state_hgrn[:, l], _rope_tables(dec_seq), group_tokens)
    group_slots = EC_CAPACITY * group_tokens // N_EXPERTS
    yp, ys = _moe([routed_ctx, routed_lat], mod3, [lambda g: 0, lambda g: 1 + g],
                  w_exp_gate[l], w_exp_up[l], w_exp_down[l], group_tokens, group_slots)

    return (yp.reshape(batch, seq, d), ys.reshape(dec_batch, dec_seq, d),
            k_new.reshape(batch, 1, seq, DA_HEADS, 2, DA_DQK),
            v_new.reshape(batch, 1, seq, DA_HEADS, DA_DV),
            s_new.reshape(batch, 1, 2, HG_HEADS, HG_DK, HG_DV))
```

```python
import functools
import math

import jax
import jax.numpy as jnp
from jax import lax
from jax.experimental import pallas as pl
from jax.experimental.pallas import tpu as pltpu
from jax.experimental.pallas import tpu_sc as plsc

F32 = jnp.float32
BF16 = jnp.bfloat16
I32 = jnp.int32

EPS = 1e-6
GRID_W = 64
HG_HEADS = 4
HG_DK = 128
HG_DV = 128
HG_CHUNK_LOG2 = 7
HG_CHUNK = 1 << HG_CHUNK_LOG2
HG_UNROLL = 4
DA_HEADS = 4
DA_DQK = 64
DA_DV = 128
N_EXPERTS = 16
EC_CAPACITY = 2
ROPE_BASE = 10000.0
SEG = 512
N_SEG = 12
LAM_INIT = 0.8 - 0.6 * math.exp(-0.3 * 0)
LANES = 128
ONES_ROWS = 16
ATT_SPLIT = 2
ATT_TQ = 256
PREMIX_TOKENS = 256
POSTMIX_TOKENS = 512
COMBINE_TOKENS = 2048
ROUTE_LANES = 2048
Q_SCALE = DA_DQK ** -0.5 * math.log2(math.e)
HG_SAFE_DECAY = 80.0
VMEM_LIMIT = 56 * 1024 * 1024
SC_CORES = 2
SC_SUBCORES = 16
SC_LANES = 16
SC_GATHER_ROWS = 128


def _dot(a, b):
    return jnp.dot(a, b, preferred_element_type=F32)


def _dot_nt(a, b):
    return lax.dot_general(a, b, (((1,), (1,)), ((), ())), preferred_element_type=F32)


def _dot_tn(a, b):
    return lax.dot_general(a, b, (((0,), (0,)), ((), ())), preferred_element_type=F32)


def _split2(x):
    hi = x.astype(BF16)
    lo = (x - hi.astype(F32)).astype(BF16)
    return hi, lo


def _silu(x):
    return x * jax.nn.sigmoid(x)


def _pack_bf16_pairs(x):
    w = x.shape[1] // 2
    bits = pltpu.bitcast(x.astype(BF16).astype(F32), I32)
    return lax.shift_right_logical(bits[:, :w], 16) | bits[:, w:]


def _unpack_bf16_pairs(words):
    lo = pltpu.bitcast(words << 16, F32)
    hi = pltpu.bitcast(words & jnp.int32(-65536), F32)
    return jnp.concatenate([lo, hi], axis=1).astype(BF16)


def _params(sem):
    return pltpu.CompilerParams(dimension_semantics=sem, vmem_limit_bytes=VMEM_LIMIT)


def _mod_kernel(c_ref, w_ref, b_ref, o_ref):
    s_hi, s_lo = _split2(_silu(c_ref[...]))
    w_hi, w_lo = _split2(w_ref[...])
    o_ref[...] = _dot(s_hi, w_hi) + _dot(s_hi, w_lo) + _dot(s_lo, w_hi) + b_ref[...]


def _modulation(cond, w_mod, b_mod):
    rows, d = cond.shape
    n = w_mod.shape[1]
    bn = 512
    return pl.pallas_call(
        _mod_kernel,
        name="modulation", grid=(n // bn,),
        in_specs=[pl.BlockSpec((rows, d), lambda j: (0, 0)),
                  pl.BlockSpec((d, bn), lambda j: (0, j)),
                  pl.BlockSpec((1, bn), lambda j: (0, j))],
        out_specs=pl.BlockSpec((rows, bn), lambda j: (0, j)),
        out_shape=jax.ShapeDtypeStruct((rows, n), F32),
        compiler_params=_params(("arbitrary",)),
    )(cond, w_mod, b_mod.reshape(1, n))


def _group_rms(z, gm_ref, w):
    ms = _dot((z * z).astype(BF16), gm_ref[...])
    return z * lax.rsqrt(ms + EPS) * w


def _rope(x, cos, sin_signed):
    n = x.shape[-1]
    lane = lax.broadcasted_iota(I32, x.shape, 1)
    partner = jnp.where((lane & 16) == 0, pltpu.roll(x, n - 16, 1), pltpu.roll(x, 16, 1))
    return x * cos + partner * sin_signed


def _premix_kernel(*refs, latent):
    if latent:
        (x_ref, mod_ref, n1_ref, win_ref, bg_ref, lbl_ref, qkw_ref, gm_ref, cos_ref, sin_ref,
         q_o, bf_o, bb_o, kff_o, kfb_o, i_o, g_o, dq_o, dk_o, dv_o, gate_o) = refs
    else:
        (x_ref, mod_ref, n1_ref, win_ref, bg_ref, lbl_ref, qkw_ref, gm_ref,
         q_o, bf_o, bb_o, kff_o, kfb_o, i_o, g_o, dq_o, dk_o, dv_o, gate_o) = refs
    d = x_ref.shape[1]
    mod = mod_ref[0]
    sh1, sc1 = mod[:, 0:d], mod[:, d:2 * d]
    x = x_ref[...]
    xn = x * lax.rsqrt(jnp.mean(x * x, axis=-1, keepdims=True) + EPS) * n1_ref[...]
    hb = (xn * (1.0 + sc1) + sh1).astype(BF16)

    def seg(j):
        return _dot(hb, win_ref[:, j * SEG:(j + 1) * SEG])

    def lower_bound(direction):
        l0 = lbl_ref[2 * direction:2 * direction + 1, :]
        l1 = lbl_ref[2 * direction + 1:2 * direction + 2, :]
        mx = jnp.maximum(l0, l1)
        e0, e1 = jnp.exp(l0 - mx), jnp.exp(l1 - mx)
        return e0 / (e0 + e1)

    tm = x.shape[0]
    row = lax.broadcasted_iota(I32, (tm, tm), 0)
    col = lax.broadcasted_iota(I32, (tm, tm), 1)
    same_chunk = (row >> HG_CHUNK_LOG2) == (col >> HG_CHUNK_LOG2)

    def store(o_ref, val):
        val = val.astype(o_ref.dtype)
        if len(o_ref.shape) == 2:
            o_ref[...] = val
        elif len(o_ref.shape) == 3:
            for h in range(o_ref.shape[0]):
                o_ref[h] = val[:, h * LANES:(h + 1) * LANES]
        else:
            for h in range(o_ref.shape[2]):
                for m in range(2):
                    lo = (2 * h + m) * DA_DQK
                    o_ref[0, :, h, m, :] = val[:, lo:lo + DA_DQK]

    store(q_o, _silu(seg(0)))
    for j, b_o, kf_o, order in ((1, bf_o, kff_o, row >= col), (2, bb_o, kfb_o, row <= col)):
        lbd = lower_bound(j - 1)
        f = lbd + (1.0 - lbd) * jax.nn.sigmoid(seg(j))
        tri = (same_chunk & order).astype(BF16)
        hi, lo = _split2(jnp.log(f))
        store(b_o, _dot(tri, hi) + _dot(tri, lo))
        store(kf_o, 1.0 - f)
    store(i_o, seg(3))
    store(g_o, _silu(seg(4)))
    qn = _group_rms(seg(5), gm_ref, qkw_ref[0:1, :]) * Q_SCALE
    kn = _group_rms(seg(6), gm_ref, qkw_ref[1:2, :])
    if latent:
        qn = _rope(qn, cos_ref[...], sin_ref[...])
        kn = _rope(kn, cos_ref[...], sin_ref[...])
    store(dq_o, qn)
    store(dk_o, kn)
    store(dv_o, seg(7))
    for j in range(4):
        z = seg(8 + j) + bg_ref[:, j * SEG:(j + 1) * SEG]
        gate_o[:, j * SEG:(j + 1) * SEG] = jax.nn.sigmoid(z).astype(gate_o.dtype)


def _premix(x, mod3, mod_row, norm1_w, w_in, b_gate, lb_logits, qkw, gm, rope, tm, kv_dtype,
            kv_head_major):
    n, d = x.shape
    latent = rope is not None
    const = lambda i: (0, 0)
    in_specs = [pl.BlockSpec((tm, d), lambda i: (i, 0)),
                pl.BlockSpec((1, 1, mod3.shape[2]), lambda i: (mod_row(i), 0, 0)),
                pl.BlockSpec((1, d), const),
                pl.BlockSpec(w_in.shape, const),
                pl.BlockSpec(b_gate.shape, const),
                pl.BlockSpec(lb_logits.shape, const),
                pl.BlockSpec(qkw.shape, const),
                pl.BlockSpec(gm.shape, const)]
    args = [x, mod3, norm1_w, w_in, b_gate, lb_logits, qkw, gm]
    if latent:
        cos, sin = rope
        nblk = cos.shape[0] // tm
        in_specs += [pl.BlockSpec((tm, SEG), lambda i: (i % nblk, 0))] * 2
        args += [cos, sin]
    heads = SEG // LANES
    head_spec = pl.BlockSpec((heads, tm, LANES), lambda i: (0, i, 0))
    tok_spec = pl.BlockSpec((tm, SEG), lambda i: (i, 0))
    out_dtypes = [BF16, F32, F32, BF16, BF16, BF16, BF16, BF16]
    out_shape = [jax.ShapeDtypeStruct((heads, n, LANES), t) for t in out_dtypes]
    if kv_head_major:
        out_shape += [jax.ShapeDtypeStruct((heads, n, LANES), kv_dtype)] * 2
        kv_specs = [head_spec] * 2
    else:
        k_shape = (n // tm, tm, DA_HEADS, 2, DA_DQK)
        out_shape += [jax.ShapeDtypeStruct(k_shape, kv_dtype),
                      jax.ShapeDtypeStruct((n, SEG), kv_dtype)]
        kv_specs = [pl.BlockSpec((1,) + k_shape[1:], lambda i: (i, 0, 0, 0, 0)), tok_spec]
    out_shape.append(jax.ShapeDtypeStruct((n, 4 * SEG), BF16))
    out_specs = [head_spec] * 8 + kv_specs + [pl.BlockSpec((tm, 4 * SEG), lambda i: (i, 0))]
    return pl.pallas_call(
        functools.partial(_premix_kernel, latent=latent),
        name=f"premix_n{n}", grid=(n // tm,),
        in_specs=in_specs, out_specs=out_specs, out_shape=out_shape,
        compiler_params=_params(("parallel",)),
    )(*args)


def _hgrn_chunk_local(q, b, total, k, v, keep, safe):
    if safe:
        ref = b[HG_CHUNK // 2:HG_CHUNK // 2 + 1, :]
        qa = q * jnp.exp(b - ref)
        kb = k * jnp.exp(ref - b)
        attn = jnp.where(keep, _dot_nt(qa.astype(BF16), kb.astype(BF16)), 0.0)
        qe = qa * jnp.exp(ref)
        kd = kb * jnp.exp(total - ref)
    else:
        qe = q * jnp.exp(b)
        kd = k * jnp.exp(total - b)
        col = lax.broadcasted_iota(I32, (HG_CHUNK, HG_CHUNK), 1)

        def column(s, acc):
            onehot = (lax.broadcasted_iota(I32, (HG_CHUNK, 1), 0) == s).astype(F32)
            bs = jnp.sum(b * onehot, axis=0, keepdims=True)
            ks = jnp.sum(k * onehot, axis=0, keepdims=True)
            w = jnp.sum(q * ks * jnp.exp(jnp.minimum(b - bs, 0.0)), axis=1, keepdims=True)
            return jnp.where(col == s, w, acc)

        attn = lax.fori_loop(0, HG_CHUNK, column, jnp.zeros((HG_CHUNK, HG_CHUNK), F32))
        attn = jnp.where(keep, attn, 0.0)
    vt = v.astype(F32).T.astype(BF16)
    lhs = jnp.concatenate([qe.astype(BF16), attn.astype(BF16)], axis=1)
    return lhs, vt, _dot(vt, kd.astype(BF16)), jnp.exp(total)


def _hgrn_scan_group(chunks, st):
    outs = []
    for lhs, vt, inc, decay in chunks:
        outs.append(_dot_nt(lhs, jnp.concatenate([st.astype(BF16), vt], axis=1)))
        st = st * decay + inc
    return outs, st


def _hgrn_kernel(*refs, has_state):
    if has_state:
        (q_ref, bf_ref, bb_ref, kff_ref, kfb_ref, i_ref, g_ref, nw_ref, s0_ref,
         o_ref, sn_ref, of_s, ob_s) = refs
    else:
        (q_ref, bf_ref, bb_ref, kff_ref, kfb_ref, i_ref, g_ref, nw_ref,
         o_ref, sn_ref, of_s, ob_s) = refs
    t_len = q_ref.shape[1]
    nc = t_len // HG_CHUNK
    unroll = min(HG_UNROLL, nc)
    head = pl.program_id(1)
    row = lax.broadcasted_iota(I32, (HG_CHUNK, HG_CHUNK), 0)
    col = lax.broadcasted_iota(I32, (HG_CHUNK, HG_CHUNK), 1)
    keep_f, keep_b = row >= col, row <= col

    mid_f = bf_ref[0, pl.ds(HG_CHUNK // 2, nc, stride=HG_CHUNK), :]
    tot_f = bf_ref[0, pl.ds(HG_CHUNK - 1, nc, stride=HG_CHUNK), :]
    mid_b = bb_ref[0, pl.ds(HG_CHUNK // 2, nc, stride=HG_CHUNK), :]
    tot_b = bb_ref[0, pl.ds(0, nc, stride=HG_CHUNK), :]
    worst = jnp.minimum(jnp.min(jnp.minimum(mid_f, tot_f - mid_f)),
                        jnp.min(jnp.minimum(mid_b, tot_b - mid_b)))

    if has_state:
        st_f0, st_b0 = s0_ref[0, 0, 0].T, s0_ref[0, 1, 0].T
    else:
        st_f0 = st_b0 = jnp.zeros((HG_DV, HG_DK), F32)

    def scan(safe):
        def body(it, carry):
            st_f, st_b = carry
            rows_f, rows_b, loc_f, loc_b = [], [], [], []
            for u in range(unroll):
                c = it * unroll + u
                sf = pl.multiple_of(c * HG_CHUNK, HG_CHUNK)
                sb = pl.multiple_of((nc - 1 - c) * HG_CHUNK, HG_CHUNK)
                rf, rb = pl.ds(sf, HG_CHUNK), pl.ds(sb, HG_CHUNK)
                rows_f.append(rf)
                rows_b.append(rb)
                loc_f.append(_hgrn_chunk_local(
                    q_ref[0, rf, :].astype(F32), bf_ref[0, rf, :],
                    bf_ref[0, pl.ds(sf + HG_CHUNK - 1, 1), :],
                    kff_ref[0, rf, :].astype(F32), i_ref[0, rf, :], keep_f, safe))
                loc_b.append(_hgrn_chunk_local(
                    q_ref[0, rb, :].astype(F32), bb_ref[0, rb, :], bb_ref[0, pl.ds(sb, 1), :],
                    kfb_ref[0, rb, :].astype(F32), i_ref[0, rb, :], keep_b, safe))
            outs_f, st_f = _hgrn_scan_group(loc_f, st_f)
            outs_b, st_b = _hgrn_scan_group(loc_b, st_b)
            for rf, rb, o_f, o_b in zip(rows_f, rows_b, outs_f, outs_b):
                of_s[rf, :] = o_f
                ob_s[rb, :] = o_b
            return st_f, st_b
        return lax.fori_loop(0, nc // unroll, body, (st_f0, st_b0))

    st_f, st_b = lax.cond(worst >= -HG_SAFE_DECAY, lambda: scan(True), lambda: scan(False))
    sn_ref[0, 0, 0] = st_f.T
    sn_ref[0, 1, 0] = st_b.T
    o = of_s[...] + ob_s[...]
    nw = nw_ref[pl.ds(head, 1), :]
    on = o * lax.rsqrt(jnp.mean(o * o, axis=-1, keepdims=True) + EPS) * nw
    o_ref[0] = (on * g_ref[0].astype(F32)).astype(o_ref.dtype)


def _hgrn(q, bf, bb, kff, kfb, iv, g, norm_w, s0, batch):
    heads, n, _ = q.shape
    t_len = n // batch
    blk = pl.BlockSpec((1, t_len, HG_DK), lambda b, h: (h, b, 0))
    st_blk = pl.BlockSpec((1, 2, 1, HG_DK, HG_DV), lambda b, h: (b, 0, h, 0, 0))
    args = [q, bf, bb, kff, kfb, iv, g, norm_w]
    in_specs = [blk] * 7 + [pl.BlockSpec(norm_w.shape, lambda b, h: (0, 0))]
    if s0 is not None:
        args.append(s0)
        in_specs.append(st_blk)
    o, s_new = pl.pallas_call(
        functools.partial(_hgrn_kernel, has_state=s0 is not None),
        name=f"hgrn_n{n}", grid=(batch, HG_HEADS),
        in_specs=in_specs,
        out_specs=[blk, st_blk],
        out_shape=[jax.ShapeDtypeStruct((heads, n, HG_DV), BF16),
                   jax.ShapeDtypeStruct((batch, 2, HG_HEADS, HG_DK, HG_DV), F32)],
        scratch_shapes=[pltpu.VMEM((t_len, HG_DV), F32), pltpu.VMEM((t_len, HG_DV), F32)],
        compiler_params=_params(("parallel", "parallel")),
    )(*args)
    return o, s_new


def _attn_kernel(*refs, has_ctx):
    if has_ctx:
        q_ref, k_ref, v_ref, ck_ref, cv_ref, lam_ref, sw_ref, o_ref, k_s, vt_s = refs
    else:
        q_ref, k_ref, v_ref, lam_ref, sw_ref, o_ref, k_s, vt_s = refs
    t_own = k_ref.shape[1]

    def load_keys(ref):
        if len(ref.shape) == 3:
            return ref[0]
        return jnp.concatenate([ref[0, :, 0, 0, :], ref[0, :, 0, 1, :]], axis=1)

    @pl.when(pl.program_id(2) == 0)
    def _():
        k_s[0:t_own, :] = load_keys(k_ref).astype(BF16)
        vt_s[0:DA_DV, 0:t_own] = v_ref[0].astype(F32).T.astype(BF16)
        if has_ctx:
            k_s[t_own:, :] = load_keys(ck_ref).astype(BF16)
            vt_s[0:DA_DV, t_own:] = cv_ref[0].T.astype(BF16)
        vt_s[DA_DV:, :] = jnp.ones((ONES_ROWS, vt_s.shape[1]), BF16)

    lv = lam_ref[...]
    lam = (jnp.exp(jnp.sum(lv[0:1] * lv[1:2], keepdims=True))
           - jnp.exp(jnp.sum(lv[2:3] * lv[3:4], keepdims=True)) + LAM_INIT)
    tq = q_ref.shape[1] // ATT_SPLIT
    dim = lax.broadcasted_iota(I32, (2 * DA_DQK, tq), 0)

    def scores(i):
        qt = q_ref[0, i * tq:(i + 1) * tq, :].astype(F32).T
        q_both = jnp.concatenate([jnp.where(dim < DA_DQK, qt, 0.0),
                                  jnp.where(dim >= DA_DQK, qt, 0.0)], axis=1).astype(BF16)
        return _dot(k_s[...], q_both)

    def finish(i, st):
        pt = jnp.exp2(st - jnp.max(st, axis=0, keepdims=True)).astype(BF16)
        r = _dot(vt_s[...], pt)
        r = r[0:DA_DV] * (1.0 / r[DA_DV:DA_DV + 1])
        o = (r[:, 0:tq] - lam * r[:, tq:2 * tq]).T
        on = o * lax.rsqrt(jnp.mean(o * o, axis=-1, keepdims=True) + EPS) * sw_ref[...]
        o_ref[0, i * tq:(i + 1) * tq, :] = (on * (1.0 - LAM_INIT)).astype(o_ref.dtype)

    sts = [scores(i) for i in range(ATT_SPLIT)]
    for i in range(ATT_SPLIT):
        finish(i, sts[i])


def _attention(q, k, v, ctx_k, ctx_v, lam_p, subln_w, batch, tq):
    heads, n, _ = q.shape
    t_len = n // batch
    nq = t_len // tq
    q_blk = pl.BlockSpec((1, tq, DA_DV), lambda b, h, i: (h, b * nq + i, 0))
    const = lambda b, h, i: (0, 0)
    if k.ndim == 3:
        kv_blk = pl.BlockSpec((1, t_len, DA_DV), lambda b, h, i: (h, b, 0))
        in_specs = [q_blk, kv_blk, kv_blk]
        args = [q, k, v]
    else:
        assert k.shape == (batch, t_len, DA_HEADS, 2, DA_DQK)
        in_specs = [q_blk, pl.BlockSpec((1, t_len, 1, 2, DA_DQK), lambda b, h, i: (b, 0, h, 0, 0)),
                    pl.BlockSpec((1, t_len, DA_DV), lambda b, h, i: (b, 0, h))]
        args = [q, k, v.reshape(batch, t_len, SEG)]
    n_keys = t_len
    if ctx_k is not None:
        past = ctx_k.shape[1]
        n_keys += past
        ck_blk = pl.BlockSpec((1, past, 1, 2, DA_DQK), lambda b, h, i: (b, 0, h, 0, 0))
        cv_blk = pl.BlockSpec((1, past, DA_DV), lambda b, h, i: (b, 0, h))
        args += [ctx_k, ctx_v]
        in_specs += [ck_blk, cv_blk]
    args += [lam_p, subln_w]
    in_specs += [pl.BlockSpec(lam_p.shape, const), pl.BlockSpec(subln_w.shape, const)]
    o = pl.pallas_call(
        functools.partial(_attn_kernel, has_ctx=ctx_k is not None),
        name=f"attn_n{n}", grid=(batch, DA_HEADS, nq),
        in_specs=in_specs, out_specs=q_blk,
        out_shape=jax.ShapeDtypeStruct((heads, n, DA_DV), BF16),
        scratch_shapes=[pltpu.VMEM((n_keys, 2 * DA_DQK), BF16),
                        pltpu.VMEM((DA_DV + ONES_ROWS, n_keys), BF16)],
        compiler_params=_params(("parallel", "parallel", "arbitrary")),
    )(*args)
    return o


def _postmix_kernel(oh_ref, oa_ref, gate_ref, x_ref, mod_ref, wbh_ref, wba_ref, wout_ref,
                    n2_ref, rw_ref, x1_o, h2_o, afft_o, afftok_o):
    d = x_ref.shape[1]
    mod = mod_ref[0]
    g1, sh2, sc2 = mod[:, 2 * d:3 * d], mod[:, 3 * d:4 * d], mod[:, 4 * d:5 * d]
    g_h = gate_ref[:, 0:d].astype(F32)
    g_a = gate_ref[:, d:2 * d].astype(F32)

    def heads_on_lanes(ref):
        return jnp.concatenate([ref[h] for h in range(ref.shape[0])], axis=1)

    merged = (g_h * _dot(heads_on_lanes(oh_ref), wbh_ref[...])
              + g_a * _dot(heads_on_lanes(oa_ref), wba_ref[...]))
    x1 = x_ref[...] + g1 * _dot(merged.astype(BF16), wout_ref[...])
    x1_o[...] = x1
    xn = x1 * lax.rsqrt(jnp.mean(x1 * x1, axis=-1, keepdims=True) + EPS) * n2_ref[...]
    h2 = xn * (1.0 + sc2) + sh2
    h2_o[...] = _pack_bf16_pairs(h2)
    h_hi, h_lo = _split2(h2)
    rw = rw_ref[...]
    t1 = _dot_nt(rw, h_hi)
    t2 = _dot_nt(rw, h_lo)
    e = N_EXPERTS
    logits = t1[0:e] + t1[e:2 * e] + t2[0:e]
    mx = jnp.max(logits, axis=0, keepdims=True)
    p = jnp.exp(logits - mx)
    aff = p / jnp.sum(p, axis=0, keepdims=True)
    afft_o[...] = aff
    pad = jnp.zeros((LANES - e, aff.shape[1]), F32)
    afftok_o[...] = jnp.concatenate([aff, pad], axis=0).T


def _postmix(o_h, o_a, gates, x, mod3, mod_row, w_bh, w_ba, w_out, norm2_w, rw_cat, tm):
    n, d = x.shape
    const = lambda i: (0, 0)
    row = lambda i: (i, 0)
    return pl.pallas_call(
        _postmix_kernel,
        name=f"postmix_n{n}", grid=(n // tm,),
        in_specs=[pl.BlockSpec((o_h.shape[0], tm, LANES), lambda i: (0, i, 0)),
                  pl.BlockSpec((o_a.shape[0], tm, LANES), lambda i: (0, i, 0)),
                  pl.BlockSpec((tm, 4 * SEG), row), pl.BlockSpec((tm, d), row),
                  pl.BlockSpec((1, 1, mod3.shape[2]), lambda i: (mod_row(i), 0, 0)),
                  pl.BlockSpec(w_bh.shape, const), pl.BlockSpec(w_ba.shape, const),
                  pl.BlockSpec(w_out.shape, const), pl.BlockSpec((1, d), const),
                  pl.BlockSpec(rw_cat.shape, const)],
        out_specs=[pl.BlockSpec((tm, d), row), pl.BlockSpec((tm, d // 2), row),
                   pl.BlockSpec((N_EXPERTS, tm), lambda i: (0, i)),
                   pl.BlockSpec((tm, LANES), row)],
        out_shape=[jax.ShapeDtypeStruct((n, d), F32), jax.ShapeDtypeStruct((n, d // 2), I32),
                   jax.ShapeDtypeStruct((N_EXPERTS, n), F32),
                   jax.ShapeDtypeStruct((n, LANES), F32)],
        compiler_params=_params(("parallel",)),
    )(o_h, o_a, gates, x, mod3, w_bh, w_ba, w_out, norm2_w, rw_cat)


def _lane_cumsum_exclusive(x, blk):
    e, t = x.shape
    r = lax.broadcasted_iota(I32, (blk, blk), 0)
    c = lax.broadcasted_iota(I32, (blk, blk), 1)
    upper = (r < c).astype(BF16)
    carry = jnp.zeros((e, 1), F32)
    parts = []
    for j in range(t // blk):
        xb = x[:, j * blk:(j + 1) * blk]
        parts.append(_dot(xb.astype(BF16), upper) + carry)
        carry = carry + jnp.sum(xb, axis=1, keepdims=True)
    return parts[0] if len(parts) == 1 else jnp.concatenate(parts, axis=1)


def _route_kernel(aff_ref, pos_ref, *, cap, req_per_group, t_len):
    n_req = aff_ref.shape[1] // t_len
    bits = [pltpu.bitcast(aff_ref[:, r * t_len:(r + 1) * t_len], I32) for r in range(n_req)]

    def count(mask):
        return jnp.sum(mask.astype(F32), axis=1, keepdims=True)

    def step(i, ths):
        bit = jnp.int32(1) << (30 - i)
        return tuple(jnp.where(count(b >= (th | bit)) >= cap, th | bit, th)
                     for b, th in zip(bits, ths))

    zero = jnp.zeros((aff_ref.shape[0], 1), I32)
    ths = lax.fori_loop(0, 31, step, (zero,) * n_req)
    blk = min(t_len, 256)
    for r, (b, th) in enumerate(zip(bits, ths)):
        gt = b > th
        eq = (b == th).astype(F32)
        need = cap - count(gt)
        tie_rank = _lane_cumsum_exclusive(eq, blk)
        sel = jnp.where(gt, 1.0, jnp.where(tie_rank < need, eq, 0.0))
        slot = _lane_cumsum_exclusive(sel, blk)
        offset = ((pl.program_id(0) * n_req + r) % req_per_group) * cap
        pos_ref[:, r * t_len:(r + 1) * t_len] = jnp.where(sel > 0.0, slot.astype(I32) + offset, -1)


def _route(afft, batch, cap, req_per_group):
    e, n = afft.shape
    t_len = n // batch
    per_step = max(1, min(batch, ROUTE_LANES // t_len))
    assert batch % per_step == 0
    blk = pl.BlockSpec((e, per_step * t_len), lambda b: (0, b))
    return pl.pallas_call(
        functools.partial(_route_kernel, cap=cap, req_per_group=req_per_group, t_len=t_len),
        name=f"route_n{n}", grid=(batch // per_step,), in_specs=[blk], out_specs=blk,
        out_shape=jax.ShapeDtypeStruct((e, n), I32),
        compiler_params=_params(("parallel",)),
    )(afft)


def _one_hot_rows(pos_row, n_slots):
    slot = lax.broadcasted_iota(I32, (n_slots, pos_row.shape[1]), 0)
    return (slot == pos_row).astype(BF16)


def _dispatch(parts, group_tokens, group_slots):
    n_exp = parts[0][0].shape[0]
    width = parts[0][1].shape[1]
    part_groups = [pos.shape[1] // group_tokens for pos, _ in parts]
    workers = SC_CORES * SC_SUBCORES
    chunks = group_slots // SC_GATHER_ROWS
    assert all(g * n_exp % workers == 0 for g in part_groups)
    assert group_slots % SC_GATHER_ROWS == 0 and group_tokens % SC_LANES == 0
    assert n_exp & (n_exp - 1) == 0
    exp_shift = n_exp.bit_length() - 1
    row_shift = SC_GATHER_ROWS.bit_length() - 1

    def body(*refs):
        ins, (out_hbm, pos_v, idx_v, rows_v, sem) = refs[:2 * len(parts)], refs[2 * len(parts):]
        wid = lax.axis_index("s") * SC_CORES + lax.axis_index("c")
        lane = lax.iota(I32, SC_LANES)
        group_base = 0
        for part, groups in enumerate(part_groups):
            pos_hbm, h_hbm = ins[2 * part], ins[2 * part + 1]
            for k in range(groups * n_exp // workers):
                pair = wid + workers * k
                g = pair >> exp_shift
                e = pair & (n_exp - 1)
                pltpu.sync_copy(pos_hbm.at[e, pl.ds(g * group_tokens, group_tokens)], pos_v)

                @pl.loop(0, group_tokens // SC_LANES)
                def _(i):
                    p = pos_v[pl.ds(i * SC_LANES, SC_LANES)]
                    tok = g * group_tokens + i * SC_LANES + lane
                    slot = jnp.maximum(p, 0)
                    plsc.store_scatter(idx_v, [slot >> row_shift, slot & (SC_GATHER_ROWS - 1)],
                                       tok, mask=p >= 0)

                for c in range(chunks):
                    row0 = (group_base + g) * group_slots + c * SC_GATHER_ROWS
                    pltpu.async_copy(h_hbm.at[idx_v.at[c]], rows_v, sem).wait()
                    pltpu.sync_copy(rows_v, out_hbm.at[e, pl.ds(row0, SC_GATHER_ROWS)])
            group_base += groups

    mesh = plsc.VectorSubcoreMesh(core_axis_name="c", subcore_axis_name="s",
                                  num_cores=SC_CORES, num_subcores=SC_SUBCORES)
    return pl.kernel(
        body,
        out_type=jax.ShapeDtypeStruct((n_exp, sum(part_groups) * group_slots, width), I32),
        mesh=mesh,
        scratch_types=[pltpu.VMEM((group_tokens,), I32), pltpu.VMEM((chunks, SC_GATHER_ROWS), I32),
                       pltpu.VMEM((SC_GATHER_ROWS, width), I32), pltpu.SemaphoreType.DMA],
        compiler_params=pltpu.CompilerParams(needs_layout_passes=False),
        name="dispatch",
    )(*[a for part in parts for a in part])


def _expert_kernel(xg_ref, wg_ref, wu_ref, wd_ref, y_ref, wg_s, wu_s, wd_s):
    @pl.when(pl.program_id(1) == 0)
    def _():
        wg_s[...] = wg_ref[0].astype(BF16)
        wu_s[...] = wu_ref[0].astype(BF16)
        wd_s[...] = wd_ref[0].astype(BF16)

    xg = _unpack_bf16_pairs(xg_ref[0])
    a = _silu(_dot(xg, wg_s[...])) * _dot(xg, wu_s[...])
    y_ref[0] = _dot(a.astype(BF16), wd_s[...]).astype(y_ref.dtype)


def _experts(xg, w_gate, w_up, w_down, ts):
    e, s, half = xg.shape
    d, f = w_gate.shape[1:]
    assert d == 2 * half
    x_blk = pl.BlockSpec((1, ts, d), lambda x, i: (x, i, 0))
    return pl.pallas_call(
        _expert_kernel,
        name=f"experts_s{s}", grid=(e, s // ts),
        in_specs=[pl.BlockSpec((1, ts, half), lambda x, i: (x, i, 0)),
                  pl.BlockSpec((1, d, f), lambda x, i: (x, 0, 0)),
                  pl.BlockSpec((1, d, f), lambda x, i: (x, 0, 0)),
                  pl.BlockSpec((1, f, d), lambda x, i: (x, 0, 0))],
        out_specs=x_blk,
        out_shape=jax.ShapeDtypeStruct((e, s, d), BF16),
        scratch_shapes=[pltpu.VMEM((d, f), BF16), pltpu.VMEM((d, f), BF16),
                        pltpu.VMEM((f, d), BF16)],
        compiler_params=_params(("parallel", "arbitrary")),
    )(xg, w_gate, w_up, w_down)


def _combine_kernel(pos_ref, afftok_ref, y_ref, x1_ref, mod_ref, o_ref):
    e = pl.program_id(2)
    d = x1_ref.shape[1]

    @pl.when(e == 0)
    def _():
        o_ref[...] = jnp.zeros_like(o_ref)

    p = _one_hot_rows(pos_ref[pl.ds(e, 1), :], y_ref.shape[1])
    lane = lax.broadcasted_iota(I32, afftok_ref.shape, 1)
    gate = jnp.sum(jnp.where(lane == e, afftok_ref[...], 0.0), axis=1, keepdims=True)
    o_ref[...] += gate * _dot_tn(p, y_ref[0])

    @pl.when(e == pl.num_programs(2) - 1)
    def _():
        g2 = mod_ref[0][:, 5 * d:6 * d]
        o_ref[...] = x1_ref[...] + g2 * o_ref[...]


def _combine(pos, afftok, y, x1, mod3, mod_row, group_tokens, group_slots, group_base, tc):
    e, n = pos.shape
    d = x1.shape[1]
    groups = n // group_tokens
    per = group_tokens // tc
    tok = lambda g, j, x: (g * per + j, 0)
    return pl.pallas_call(
        _combine_kernel,
        name=f"combine_n{n}", grid=(groups, per, e),
        in_specs=[pl.BlockSpec((e, tc), lambda g, j, x: (0, g * per + j)),
                  pl.BlockSpec((tc, LANES), tok),
                  pl.BlockSpec((1, group_slots, d), lambda g, j, x: (x, group_base + g, 0)),
                  pl.BlockSpec((tc, d), tok),
                  pl.BlockSpec((1, 1, mod3.shape[2]), lambda g, j, x: (mod_row(g), 0, 0))],
        out_specs=pl.BlockSpec((tc, d), tok),
        out_shape=jax.ShapeDtypeStruct((n, d), F32),
        compiler_params=_params(("parallel", "parallel", "arbitrary")),
    )(pos, afftok, y, x1, mod3)


def _rope_tables(t_len):
    n_freq = DA_DQK // 4
    inv = ROPE_BASE ** (-jnp.arange(n_freq, dtype=F32) / n_freq)
    t = jnp.arange(t_len)
    pos = jnp.stack([(t // GRID_W).astype(F32), (t % GRID_W).astype(F32)], axis=1)
    ang = pos[:, :, None, None] * inv[None, None, None, :]
    ang = jnp.broadcast_to(ang, (t_len, 2, 2, n_freq))
    sign = jnp.array([-1.0, 1.0], F32)[None, None, :, None]
    cos = jnp.cos(ang).reshape(t_len, DA_DQK)
    sin = (jnp.sin(ang) * sign).reshape(t_len, DA_DQK)
    reps = SEG // DA_DQK
    return jnp.tile(cos, (1, reps)), jnp.tile(sin, (1, reps))


def _trunk(x, batch, mod3, mod_row_tok, weights, ctx_k, ctx_v, s0, rope, group_tokens):
    (norm1_w, norm2_w, w_in, b_gate, lb_logits, hgrn_norm_w, qkw, gm, lam_p, subln_w,
     w_bh, w_ba, w_out, rw_cat) = weights
    n, d = x.shape
    t_len = n // batch
    latent = rope is not None
    (q_h, bf, bb, kff, kfb, i_h, g_h, dq, dk, dv, gates) = _premix(
        x, mod3, functools.partial(mod_row_tok, tm=PREMIX_TOKENS), norm1_w, w_in, b_gate, lb_logits,
        qkw, gm, rope, PREMIX_TOKENS, BF16 if latent else F32, latent)
    o_h, s_new = _hgrn(q_h, bf, bb, kff, kfb, i_h, g_h, hgrn_norm_w, s0, batch)
    o_a = _attention(dq, dk, dv, ctx_k, ctx_v, lam_p, subln_w, batch, min(t_len, ATT_TQ * ATT_SPLIT))
    x1, h2, afft, afftok = _postmix(
        o_h, o_a, gates, x, mod3, functools.partial(mod_row_tok, tm=POSTMIX_TOKENS),
        w_bh, w_ba, w_out, norm2_w, rw_cat, POSTMIX_TOKENS)
    cap = EC_CAPACITY * t_len // N_EXPERTS
    pos = _route(afft, batch, cap, group_tokens // t_len)
    return (pos, h2, afftok, x1), dk, dv, s_new


def _moe(routed, mod3, mod_row_grps, w_eg, w_eu, w_ed, group_tokens, group_slots):
    xg = _dispatch([(pos, h2p) for pos, h2p, _, _ in routed], group_tokens, group_slots)
    y = _experts(xg, w_eg, w_eu, w_ed, group_slots)
    outs, group_base = [], 0
    for (pos, _, afftok, x1), mod_row in zip(routed, mod_row_grps):
        outs.append(_combine(pos, afftok, y, x1, mod3, mod_row, group_tokens, group_slots,
                             group_base, COMBINE_TOKENS))
        group_base += pos.shape[1] // group_tokens
    return outs


def kernel(x_prompt, x_sample, cache_k, cache_v, state_hgrn, c, c_ctx, norm1_w, norm2_w, w_mod,
           b_mod, w_in, b_gate, hgrn_lb_logits, hgrn_norm_w, qk_norm_w, diff_lambda, diff_subln_w,
           w_branch_hgrn, w_branch_attn, w_out, router_w, w_exp_gate, w_exp_up, w_exp_down):
    batch, seq, d = x_prompt.shape
    dec_batch, dec_seq, _ = x_sample.shape
    past = cache_k.shape[2]
    depth = w_in.shape[0]
    assert depth == 1
    group_tokens = dec_seq
    assert group_tokens % seq == 0 and (batch * seq) % group_tokens == 0
    l = 0

    rows = -(-(1 + dec_batch) // 8) * 8
    cond = jnp.zeros((rows, d), F32).at[0].set(c_ctx).at[1:1 + dec_batch].set(c)
    mod = _modulation(cond, w_mod[l], b_mod[l])
    mod3 = mod.reshape(rows, 1, 6 * d)

    gidx = jnp.arange(SEG) // DA_DQK
    gm = (gidx[:, None] == gidx[None, :]).astype(BF16) * (1.0 / DA_DQK)
    qkw = jnp.tile(qk_norm_w[l], (1, SEG // DA_DQK))
    rw_t = router_w[l].T
    rw_hi = rw_t.astype(BF16)
    rw_cat = jnp.concatenate([rw_hi, (rw_t - rw_hi.astype(F32)).astype(BF16)], axis=0)
    weights = (norm1_w[l][None], norm2_w[l][None], w_in[l].astype(BF16), b_gate[l][None],
               hgrn_lb_logits.reshape(4, SEG), hgrn_norm_w[l], qkw, gm, diff_lambda[l],
               diff_subln_w[l][None], w_branch_hgrn[l].astype(BF16),
               w_branch_attn[l].astype(BF16), w_out[l].astype(BF16),
               rw_cat)

    routed_ctx, k_new, v_new, s_new = _trunk(
        x_prompt.reshape(batch * seq, d), batch, mod3,
        lambda i, tm: 0, weights, None, None, None, None, group_tokens)
    per_req = dec_seq
    routed_lat, _, _, _ = _trunk(
        x_sample.reshape(dec_batch * dec_seq, d), dec_batch, mod3,
        lambda i, tm: 1 + (i * tm) // per_req, weights,
        cache_k[:, l], cache_v[:, l].reshape(dec_batch, past, SEG),
        state_hgrn[:, l], _rope_tables(dec_seq), group_tokens)
    group_slots = EC_CAPACITY * group_tokens // N_EXPERTS
    experts = (w_exp_gate[l], w_exp_up[l], w_exp_down[l], group_tokens, group_slots)
    yp, = _moe([routed_ctx], mod3, [lambda g: 0], *experts)
    ys, = _moe([routed_lat], mod3, [lambda g: 1 + g], *experts)

    return (yp.reshape(batch, seq, d), ys.reshape(dec_batch, dec_seq, d),
            k_new.reshape(batch, 1, seq, DA_HEADS, 2, DA_DQK),
            v_new.reshape(batch, 1, seq, DA_HEADS, DA_DV),
            s_new.reshape(batch, 1, 2, HG_HEADS, HG_DK, HG_DV))
```

```python
import functools
import math

import jax
import jax.numpy as jnp
from jax import lax
from jax.experimental import pallas as pl
from jax.experimental.pallas import tpu as pltpu
from jax.experimental.pallas import tpu_sc as plsc

F32 = jnp.float32
BF16 = jnp.bfloat16
I32 = jnp.int32

EPS = 1e-6
GRID_W = 64
HG_HEADS = 4
HG_DK = 128
HG_DV = 128
HG_CHUNK_LOG2 = 7
HG_CHUNK = 1 << HG_CHUNK_LOG2
HG_UNROLL = 4
DA_HEADS = 4
DA_DQK = 64
DA_DV = 128
N_EXPERTS = 16
EC_CAPACITY = 2
ROPE_BASE = 10000.0
SEG = 512
N_SEG = 12
LAM_INIT = 0.8 - 0.6 * math.exp(-0.3 * 0)
LANES = 128
ONES_ROWS = 16
ATT_SPLIT = 2
ATT_TQ = 256
PREMIX_TOKENS = 256
POSTMIX_TOKENS = 512
COMBINE_TOKENS = 512
COMBINE_EXPERTS = 8
ROUTE_LANES = 2048
Q_SCALE = DA_DQK ** -0.5 * math.log2(math.e)
HG_SAFE_DECAY = 80.0
VMEM_LIMIT = 56 * 1024 * 1024
SC_CORES = 2
SC_SUBCORES = 16
SC_LANES = 16
SC_GATHER_ROWS = 128


def _dot(a, b):
    return jnp.dot(a, b, preferred_element_type=F32)


def _dot_nt(a, b):
    return lax.dot_general(a, b, (((1,), (1,)), ((), ())), preferred_element_type=F32)


def _dot_tn(a, b):
    return lax.dot_general(a, b, (((0,), (0,)), ((), ())), preferred_element_type=F32)


def _split2(x):
    hi = x.astype(BF16)
    lo = (x - hi.astype(F32)).astype(BF16)
    return hi, lo


def _silu(x):
    return x * jax.nn.sigmoid(x)


def _pack_bf16_pairs(x):
    w = x.shape[1] // 2
    bits = pltpu.bitcast(x.astype(BF16).astype(F32), I32)
    return lax.shift_right_logical(bits[:, :w], 16) | bits[:, w:]


def _unpack_bf16_pairs(words):
    lo = pltpu.bitcast(words << 16, F32)
    hi = pltpu.bitcast(words & jnp.int32(-65536), F32)
    return jnp.concatenate([lo, hi], axis=1).astype(BF16)


def _params(sem):
    return pltpu.CompilerParams(dimension_semantics=sem, vmem_limit_bytes=VMEM_LIMIT)


def _mod_kernel(c_ref, w_ref, b_ref, o_ref):
    s_hi, s_lo = _split2(_silu(c_ref[...]))
    w_hi, w_lo = _split2(w_ref[...])
    o_ref[...] = _dot(s_hi, w_hi) + _dot(s_hi, w_lo) + _dot(s_lo, w_hi) + b_ref[...]


def _modulation(cond, w_mod, b_mod):
    rows, d = cond.shape
    n = w_mod.shape[1]
    bn = 512
    return pl.pallas_call(
        _mod_kernel,
        name="modulation", grid=(n // bn,),
        in_specs=[pl.BlockSpec((rows, d), lambda j: (0, 0)),
                  pl.BlockSpec((d, bn), lambda j: (0, j)),
                  pl.BlockSpec((1, bn), lambda j: (0, j))],
        out_specs=pl.BlockSpec((rows, bn), lambda j: (0, j)),
        out_shape=jax.ShapeDtypeStruct((rows, n), F32),
        compiler_params=_params(("arbitrary",)),
    )(cond, w_mod, b_mod.reshape(1, n))


def _group_rms(z, gm_ref, w):
    ms = _dot((z * z).astype(BF16), gm_ref[...])
    return z * lax.rsqrt(ms + EPS) * w


def _rope(x, cos, sin_signed):
    n = x.shape[-1]
    lane = lax.broadcasted_iota(I32, x.shape, 1)
    partner = jnp.where((lane & 16) == 0, pltpu.roll(x, n - 16, 1), pltpu.roll(x, 16, 1))
    return x * cos + partner * sin_signed


def _premix_kernel(*refs, latent):
    if latent:
        (x_ref, mod_ref, n1_ref, win_ref, bg_ref, lbl_ref, qkw_ref, gm_ref, cos_ref, sin_ref,
         q_o, bf_o, bb_o, kff_o, kfb_o, i_o, g_o, dq_o, dk_o, dv_o, gate_o) = refs
    else:
        (x_ref, mod_ref, n1_ref, win_ref, bg_ref, lbl_ref, qkw_ref, gm_ref,
         q_o, bf_o, bb_o, kff_o, kfb_o, i_o, g_o, dq_o, dk_o, dv_o, gate_o) = refs
    d = x_ref.shape[1]
    mod = mod_ref[0]
    sh1, sc1 = mod[:, 0:d], mod[:, d:2 * d]
    x = x_ref[...]
    xn = x * lax.rsqrt(jnp.mean(x * x, axis=-1, keepdims=True) + EPS) * n1_ref[...]
    hb = (xn * (1.0 + sc1) + sh1).astype(BF16)

    def seg(j):
        return _dot(hb, win_ref[:, j * SEG:(j + 1) * SEG])

    def lower_bound(direction):
        l0 = lbl_ref[2 * direction:2 * direction + 1, :]
        l1 = lbl_ref[2 * direction + 1:2 * direction + 2, :]
        mx = jnp.maximum(l0, l1)
        e0, e1 = jnp.exp(l0 - mx), jnp.exp(l1 - mx)
        return e0 / (e0 + e1)

    tm = x.shape[0]
    row = lax.broadcasted_iota(I32, (tm, tm), 0)
    col = lax.broadcasted_iota(I32, (tm, tm), 1)
    same_chunk = (row >> HG_CHUNK_LOG2) == (col >> HG_CHUNK_LOG2)

    def store(o_ref, val):
        val = val.astype(o_ref.dtype)
        if len(o_ref.shape) == 2:
            o_ref[...] = val
        elif len(o_ref.shape) == 3:
            for h in range(o_ref.shape[0]):
                o_ref[h] = val[:, h * LANES:(h + 1) * LANES]
        else:
            for h in range(o_ref.shape[2]):
                for m in range(2):
                    lo = (2 * h + m) * DA_DQK
                    o_ref[0, :, h, m, :] = val[:, lo:lo + DA_DQK]

    store(q_o, _silu(seg(0)))
    for j, b_o, kf_o, order in ((1, bf_o, kff_o, row >= col), (2, bb_o, kfb_o, row <= col)):
        lbd = lower_bound(j - 1)
        f = lbd + (1.0 - lbd) * jax.nn.sigmoid(seg(j))
        tri = (same_chunk & order).astype(BF16)
        hi, lo = _split2(jnp.log(f))
        store(b_o, _dot(tri, hi) + _dot(tri, lo))
        store(kf_o, 1.0 - f)
    store(i_o, seg(3))
    store(g_o, _silu(seg(4)))
    qn = _group_rms(seg(5), gm_ref, qkw_ref[0:1, :]) * Q_SCALE
    kn = _group_rms(seg(6), gm_ref, qkw_ref[1:2, :])
    if latent:
        qn = _rope(qn, cos_ref[...], sin_ref[...])
        kn = _rope(kn, cos_ref[...], sin_ref[...])
    store(dq_o, qn)
    store(dk_o, kn)
    store(dv_o, seg(7))
    for j in range(4):
        z = seg(8 + j) + bg_ref[:, j * SEG:(j + 1) * SEG]
        gate_o[:, j * SEG:(j + 1) * SEG] = jax.nn.sigmoid(z).astype(gate_o.dtype)


def _premix(x, mod3, mod_row, norm1_w, w_in, b_gate, lb_logits, qkw, gm, rope, tm, kv_dtype,
            kv_head_major):
    n, d = x.shape
    latent = rope is not None
    const = lambda i: (0, 0)
    in_specs = [pl.BlockSpec((tm, d), lambda i: (i, 0)),
                pl.BlockSpec((1, 1, mod3.shape[2]), lambda i: (mod_row(i), 0, 0)),
                pl.BlockSpec((1, d), const),
                pl.BlockSpec(w_in.shape, const),
                pl.BlockSpec(b_gate.shape, const),
                pl.BlockSpec(lb_logits.shape, const),
                pl.BlockSpec(qkw.shape, const),
                pl.BlockSpec(gm.shape, const)]
    args = [x, mod3, norm1_w, w_in, b_gate, lb_logits, qkw, gm]
    if latent:
        cos, sin = rope
        nblk = cos.shape[0] // tm
        in_specs += [pl.BlockSpec((tm, SEG), lambda i: (i % nblk, 0))] * 2
        args += [cos, sin]
    heads = SEG // LANES
    head_spec = pl.BlockSpec((heads, tm, LANES), lambda i: (0, i, 0))
    tok_spec = pl.BlockSpec((tm, SEG), lambda i: (i, 0))
    out_dtypes = [BF16, F32, F32, BF16, BF16, BF16, BF16, BF16]
    out_shape = [jax.ShapeDtypeStruct((heads, n, LANES), t) for t in out_dtypes]
    if kv_head_major:
        out_shape += [jax.ShapeDtypeStruct((heads, n, LANES), kv_dtype)] * 2
        kv_specs = [head_spec] * 2
    else:
        k_shape = (n // tm, tm, DA_HEADS, 2, DA_DQK)
        out_shape += [jax.ShapeDtypeStruct(k_shape, kv_dtype),
                      jax.ShapeDtypeStruct((n, SEG), kv_dtype)]
        kv_specs = [pl.BlockSpec((1,) + k_shape[1:], lambda i: (i, 0, 0, 0, 0)), tok_spec]
    out_shape.append(jax.ShapeDtypeStruct((n, 4 * SEG), BF16))
    out_specs = [head_spec] * 8 + kv_specs + [pl.BlockSpec((tm, 4 * SEG), lambda i: (i, 0))]
    return pl.pallas_call(
        functools.partial(_premix_kernel, latent=latent),
        name=f"premix_n{n}", grid=(n // tm,),
        in_specs=in_specs, out_specs=out_specs, out_shape=out_shape,
        compiler_params=_params(("parallel",)),
    )(*args)


def _hgrn_chunk_local(q, b, total, k, v, keep, safe):
    if safe:
        ref = b[HG_CHUNK // 2:HG_CHUNK // 2 + 1, :]
        qa = q * jnp.exp(b - ref)
        kb = k * jnp.exp(ref - b)
        attn = jnp.where(keep, _dot_nt(qa.astype(BF16), kb.astype(BF16)), 0.0)
        qe = qa * jnp.exp(ref)
        kd = kb * jnp.exp(total - ref)
    else:
        qe = q * jnp.exp(b)
        kd = k * jnp.exp(total - b)
        col = lax.broadcasted_iota(I32, (HG_CHUNK, HG_CHUNK), 1)

        def column(s, acc):
            onehot = (lax.broadcasted_iota(I32, (HG_CHUNK, 1), 0) == s).astype(F32)
            bs = jnp.sum(b * onehot, axis=0, keepdims=True)
            ks = jnp.sum(k * onehot, axis=0, keepdims=True)
            w = jnp.sum(q * ks * jnp.exp(jnp.minimum(b - bs, 0.0)), axis=1, keepdims=True)
            return jnp.where(col == s, w, acc)

        attn = lax.fori_loop(0, HG_CHUNK, column, jnp.zeros((HG_CHUNK, HG_CHUNK), F32))
        attn = jnp.where(keep, attn, 0.0)
    vt = v.astype(F32).T.astype(BF16)
    lhs = jnp.concatenate([qe.astype(BF16), attn.astype(BF16)], axis=1)
    return lhs, vt, _dot(vt, kd.astype(BF16)), jnp.exp(total)


def _hgrn_scan_group(chunks, st):
    outs = []
    for lhs, vt, inc, decay in chunks:
        outs.append(_dot_nt(lhs, jnp.concatenate([st.astype(BF16), vt], axis=1)))
        st = st * decay + inc
    return outs, st


def _hgrn_kernel(*refs, has_state):
    if has_state:
        (q_ref, bf_ref, bb_ref, kff_ref, kfb_ref, i_ref, g_ref, nw_ref, s0_ref,
         o_ref, sn_ref, of_s, ob_s) = refs
    else:
        (q_ref, bf_ref, bb_ref, kff_ref, kfb_ref, i_ref, g_ref, nw_ref,
         o_ref, sn_ref, of_s, ob_s) = refs
    t_len = q_ref.shape[1]
    nc = t_len // HG_CHUNK
    unroll = min(HG_UNROLL, nc)
    head = pl.program_id(1)
    row = lax.broadcasted_iota(I32, (HG_CHUNK, HG_CHUNK), 0)
    col = lax.broadcasted_iota(I32, (HG_CHUNK, HG_CHUNK), 1)
    keep_f, keep_b = row >= col, row <= col

    mid_f = bf_ref[0, pl.ds(HG_CHUNK // 2, nc, stride=HG_CHUNK), :]
    tot_f = bf_ref[0, pl.ds(HG_CHUNK - 1, nc, stride=HG_CHUNK), :]
    mid_b = bb_ref[0, pl.ds(HG_CHUNK // 2, nc, stride=HG_CHUNK), :]
    tot_b = bb_ref[0, pl.ds(0, nc, stride=HG_CHUNK), :]
    worst = jnp.minimum(jnp.min(jnp.minimum(mid_f, tot_f - mid_f)),
                        jnp.min(jnp.minimum(mid_b, tot_b - mid_b)))

    if has_state:
        st_f0, st_b0 = s0_ref[0, 0, 0].T, s0_ref[0, 1, 0].T
    else:
        st_f0 = st_b0 = jnp.zeros((HG_DV, HG_DK), F32)

    def scan(safe):
        def body(it, carry):
            st_f, st_b = carry
            rows_f, rows_b, loc_f, loc_b = [], [], [], []
            for u in range(unroll):
                c = it * unroll + u
                sf = pl.multiple_of(c * HG_CHUNK, HG_CHUNK)
                sb = pl.multiple_of((nc - 1 - c) * HG_CHUNK, HG_CHUNK)
                rf, rb = pl.ds(sf, HG_CHUNK), pl.ds(sb, HG_CHUNK)
                rows_f.append(rf)
                rows_b.append(rb)
                loc_f.append(_hgrn_chunk_local(
                    q_ref[0, rf, :].astype(F32), bf_ref[0, rf, :],
                    bf_ref[0, pl.ds(sf + HG_CHUNK - 1, 1), :],
                    kff_ref[0, rf, :].astype(F32), i_ref[0, rf, :], keep_f, safe))
                loc_b.append(_hgrn_chunk_local(
                    q_ref[0, rb, :].astype(F32), bb_ref[0, rb, :], bb_ref[0, pl.ds(sb, 1), :],
                    kfb_ref[0, rb, :].astype(F32), i_ref[0, rb, :], keep_b, safe))
            outs_f, st_f = _hgrn_scan_group(loc_f, st_f)
            outs_b, st_b = _hgrn_scan_group(loc_b, st_b)
            for rf, rb, o_f, o_b in zip(rows_f, rows_b, outs_f, outs_b):
                of_s[rf, :] = o_f
                ob_s[rb, :] = o_b
            return st_f, st_b
        return lax.fori_loop(0, nc // unroll, body, (st_f0, st_b0))

    st_f, st_b = lax.cond(worst >= -HG_SAFE_DECAY, lambda: scan(True), lambda: scan(False))
    sn_ref[0, 0, 0] = st_f.T
    sn_ref[0, 1, 0] = st_b.T
    o = of_s[...] + ob_s[...]
    nw = nw_ref[pl.ds(head, 1), :]
    on = o * lax.rsqrt(jnp.mean(o * o, axis=-1, keepdims=True) + EPS) * nw
    o_ref[0] = (on * g_ref[0].astype(F32)).astype(o_ref.dtype)


def _hgrn(q, bf, bb, kff, kfb, iv, g, norm_w, s0, batch):
    heads, n, _ = q.shape
    t_len = n // batch
    blk = pl.BlockSpec((1, t_len, HG_DK), lambda b, h: (h, b, 0))
    st_blk = pl.BlockSpec((1, 2, 1, HG_DK, HG_DV), lambda b, h: (b, 0, h, 0, 0))
    args = [q, bf, bb, kff, kfb, iv, g, norm_w]
    in_specs = [blk] * 7 + [pl.BlockSpec(norm_w.shape, lambda b, h: (0, 0))]
    if s0 is not None:
        args.append(s0)
        in_specs.append(st_blk)
    o, s_new = pl.pallas_call(
        functools.partial(_hgrn_kernel, has_state=s0 is not None),
        name=f"hgrn_n{n}", grid=(batch, HG_HEADS),
        in_specs=in_specs,
        out_specs=[blk, st_blk],
        out_shape=[jax.ShapeDtypeStruct((heads, n, HG_DV), BF16),
                   jax.ShapeDtypeStruct((batch, 2, HG_HEADS, HG_DK, HG_DV), F32)],
        scratch_shapes=[pltpu.VMEM((t_len, HG_DV), F32), pltpu.VMEM((t_len, HG_DV), F32)],
        compiler_params=_params(("parallel", "parallel")),
    )(*args)
    return o, s_new


def _attn_kernel(*refs, has_ctx):
    if has_ctx:
        q_ref, k_ref, v_ref, ck_ref, cv_ref, lam_ref, sw_ref, o_ref, k_s, vt_s = refs
    else:
        q_ref, k_ref, v_ref, lam_ref, sw_ref, o_ref, k_s, vt_s = refs
    t_own = k_ref.shape[1]

    def load_keys(ref):
        if len(ref.shape) == 3:
            return ref[0]
        return jnp.concatenate([ref[0, :, 0, 0, :], ref[0, :, 0, 1, :]], axis=1)

    @pl.when(pl.program_id(2) == 0)
    def _():
        k_s[0:t_own, :] = load_keys(k_ref).astype(BF16)
        vt_s[0:DA_DV, 0:t_own] = v_ref[0].astype(F32).T.astype(BF16)
        if has_ctx:
            k_s[t_own:, :] = load_keys(ck_ref).astype(BF16)
            vt_s[0:DA_DV, t_own:] = cv_ref[0].T.astype(BF16)
        vt_s[DA_DV:, :] = jnp.ones((ONES_ROWS, vt_s.shape[1]), BF16)

    lv = lam_ref[...]
    lam = (jnp.exp(jnp.sum(lv[0:1] * lv[1:2], keepdims=True))
           - jnp.exp(jnp.sum(lv[2:3] * lv[3:4], keepdims=True)) + LAM_INIT)
    tq = q_ref.shape[1] // ATT_SPLIT
    dim = lax.broadcasted_iota(I32, (2 * DA_DQK, tq), 0)

    def scores(i):
        qt = q_ref[0, i * tq:(i + 1) * tq, :].astype(F32).T
        q_both = jnp.concatenate([jnp.where(dim < DA_DQK, qt, 0.0),
                                  jnp.where(dim >= DA_DQK, qt, 0.0)], axis=1).astype(BF16)
        return _dot(k_s[...], q_both)

    def finish(i, st):
        pt = jnp.exp2(st - jnp.max(st, axis=0, keepdims=True)).astype(BF16)
        r = _dot(vt_s[...], pt)
        r = r[0:DA_DV] * (1.0 / r[DA_DV:DA_DV + 1])
        o = (r[:, 0:tq] - lam * r[:, tq:2 * tq]).T
        on = o * lax.rsqrt(jnp.mean(o * o, axis=-1, keepdims=True) + EPS) * sw_ref[...]
        o_ref[0, i * tq:(i + 1) * tq, :] = (on * (1.0 - LAM_INIT)).astype(o_ref.dtype)

    sts = [scores(i) for i in range(ATT_SPLIT)]
    for i in range(ATT_SPLIT):
        finish(i, sts[i])


def _attention(q, k, v, ctx_k, ctx_v, lam_p, subln_w, batch, tq):
    heads, n, _ = q.shape
    t_len = n // batch
    nq = t_len // tq
    q_blk = pl.BlockSpec((1, tq, DA_DV), lambda b, h, i: (h, b * nq + i, 0))
    const = lambda b, h, i: (0, 0)
    if k.ndim == 3:
        kv_blk = pl.BlockSpec((1, t_len, DA_DV), lambda b, h, i: (h, b, 0))
        in_specs = [q_blk, kv_blk, kv_blk]
        args = [q, k, v]
    else:
        assert k.shape == (batch, t_len, DA_HEADS, 2, DA_DQK)
        in_specs = [q_blk, pl.BlockSpec((1, t_len, 1, 2, DA_DQK), lambda b, h, i: (b, 0, h, 0, 0)),
                    pl.BlockSpec((1, t_len, DA_DV), lambda b, h, i: (b, 0, h))]
        args = [q, k, v.reshape(batch, t_len, SEG)]
    n_keys = t_len
    if ctx_k is not None:
        past = ctx_k.shape[1]
        n_keys += past
        ck_blk = pl.BlockSpec((1, past, 1, 2, DA_DQK), lambda b, h, i: (b, 0, h, 0, 0))
        cv_blk = pl.BlockSpec((1, past, DA_DV), lambda b, h, i: (b, 0, h))
        args += [ctx_k, ctx_v]
        in_specs += [ck_blk, cv_blk]
    args += [lam_p, subln_w]
    in_specs += [pl.BlockSpec(lam_p.shape, const), pl.BlockSpec(subln_w.shape, const)]
    o = pl.pallas_call(
        functools.partial(_attn_kernel, has_ctx=ctx_k is not None),
        name=f"attn_n{n}", grid=(batch, DA_HEADS, nq),
        in_specs=in_specs, out_specs=q_blk,
        out_shape=jax.ShapeDtypeStruct((heads, n, DA_DV), BF16),
        scratch_shapes=[pltpu.VMEM((n_keys, 2 * DA_DQK), BF16),
                        pltpu.VMEM((DA_DV + ONES_ROWS, n_keys), BF16)],
        compiler_params=_params(("parallel", "parallel", "arbitrary")),
    )(*args)
    return o


def _postmix_kernel(oh_ref, oa_ref, gate_ref, x_ref, mod_ref, wbh_ref, wba_ref, wout_ref,
                    n2_ref, rw_ref, x1_o, h2_o, afft_o, afftok_o):
    d = x_ref.shape[1]
    mod = mod_ref[0]
    g1, sh2, sc2 = mod[:, 2 * d:3 * d], mod[:, 3 * d:4 * d], mod[:, 4 * d:5 * d]
    g_h = gate_ref[:, 0:d].astype(F32)
    g_a = gate_ref[:, d:2 * d].astype(F32)

    def heads_on_lanes(ref):
        return jnp.concatenate([ref[h] for h in range(ref.shape[0])], axis=1)

    merged = (g_h * _dot(heads_on_lanes(oh_ref), wbh_ref[...])
              + g_a * _dot(heads_on_lanes(oa_ref), wba_ref[...]))
    x1 = x_ref[...] + g1 * _dot(merged.astype(BF16), wout_ref[...])
    x1_o[...] = x1
    xn = x1 * lax.rsqrt(jnp.mean(x1 * x1, axis=-1, keepdims=True) + EPS) * n2_ref[...]
    h2 = xn * (1.0 + sc2) + sh2
    h2_o[...] = _pack_bf16_pairs(h2)
    h_hi, h_lo = _split2(h2)
    rw = rw_ref[...]
    t1 = _dot_nt(rw, h_hi)
    t2 = _dot_nt(rw, h_lo)
    e = N_EXPERTS
    logits = t1[0:e] + t1[e:2 * e] + t2[0:e]
    mx = jnp.max(logits, axis=0, keepdims=True)
    p = jnp.exp(logits - mx)
    aff = p / jnp.sum(p, axis=0, keepdims=True)
    afft_o[...] = aff
    pad = jnp.zeros((LANES - e, aff.shape[1]), F32)
    afftok_o[...] = jnp.concatenate([aff, pad], axis=0).T


def _postmix(o_h, o_a, gates, x, mod3, mod_row, w_bh, w_ba, w_out, norm2_w, rw_cat, tm):
    n, d = x.shape
    const = lambda i: (0, 0)
    row = lambda i: (i, 0)
    return pl.pallas_call(
        _postmix_kernel,
        name=f"postmix_n{n}", grid=(n // tm,),
        in_specs=[pl.BlockSpec((o_h.shape[0], tm, LANES), lambda i: (0, i, 0)),
                  pl.BlockSpec((o_a.shape[0], tm, LANES), lambda i: (0, i, 0)),
                  pl.BlockSpec((tm, 4 * SEG), row), pl.BlockSpec((tm, d), row),
                  pl.BlockSpec((1, 1, mod3.shape[2]), lambda i: (mod_row(i), 0, 0)),
                  pl.BlockSpec(w_bh.shape, const), pl.BlockSpec(w_ba.shape, const),
                  pl.BlockSpec(w_out.shape, const), pl.BlockSpec((1, d), const),
                  pl.BlockSpec(rw_cat.shape, const)],
        out_specs=[pl.BlockSpec((tm, d), row), pl.BlockSpec((tm, d // 2), row),
                   pl.BlockSpec((N_EXPERTS, tm), lambda i: (0, i)),
                   pl.BlockSpec((tm, LANES), row)],
        out_shape=[jax.ShapeDtypeStruct((n, d), F32), jax.ShapeDtypeStruct((n, d // 2), I32),
                   jax.ShapeDtypeStruct((N_EXPERTS, n), F32),
                   jax.ShapeDtypeStruct((n, LANES), F32)],
        compiler_params=_params(("parallel",)),
    )(o_h, o_a, gates, x, mod3, w_bh, w_ba, w_out, norm2_w, rw_cat)


def _lane_cumsum_exclusive(x, blk):
    e, t = x.shape
    r = lax.broadcasted_iota(I32, (blk, blk), 0)
    c = lax.broadcasted_iota(I32, (blk, blk), 1)
    upper = (r < c).astype(BF16)
    carry = jnp.zeros((e, 1), F32)
    parts = []
    for j in range(t // blk):
        xb = x[:, j * blk:(j + 1) * blk]
        parts.append(_dot(xb.astype(BF16), upper) + carry)
        carry = carry + jnp.sum(xb, axis=1, keepdims=True)
    return parts[0] if len(parts) == 1 else jnp.concatenate(parts, axis=1)


def _route_kernel(aff_ref, pos_ref, *, cap, req_per_group, t_len):
    n_req = aff_ref.shape[1] // t_len
    bits = [pltpu.bitcast(aff_ref[:, r * t_len:(r + 1) * t_len], I32) for r in range(n_req)]

    def count(mask):
        return jnp.sum(mask.astype(F32), axis=1, keepdims=True)

    def step(i, ths):
        bit = jnp.int32(1) << (30 - i)
        return tuple(jnp.where(count(b >= (th | bit)) >= cap, th | bit, th)
                     for b, th in zip(bits, ths))

    zero = jnp.zeros((aff_ref.shape[0], 1), I32)
    ths = lax.fori_loop(0, 31, step, (zero,) * n_req)
    blk = min(t_len, 256)
    for r, (b, th) in enumerate(zip(bits, ths)):
        gt = b > th
        eq = (b == th).astype(F32)
        need = cap - count(gt)
        tie_rank = _lane_cumsum_exclusive(eq, blk)
        sel = jnp.where(gt, 1.0, jnp.where(tie_rank < need, eq, 0.0))
        slot = _lane_cumsum_exclusive(sel, blk)
        offset = ((pl.program_id(0) * n_req + r) % req_per_group) * cap
        pos_ref[:, r * t_len:(r + 1) * t_len] = jnp.where(sel > 0.0, slot.astype(I32) + offset, -1)


def _route(afft, batch, cap, req_per_group):
    e, n = afft.shape
    t_len = n // batch
    per_step = max(1, min(batch, ROUTE_LANES // t_len))
    assert batch % per_step == 0
    blk = pl.BlockSpec((e, per_step * t_len), lambda b: (0, b))
    return pl.pallas_call(
        functools.partial(_route_kernel, cap=cap, req_per_group=req_per_group, t_len=t_len),
        name=f"route_n{n}", grid=(batch // per_step,), in_specs=[blk], out_specs=blk,
        out_shape=jax.ShapeDtypeStruct((e, n), I32),
        compiler_params=_params(("parallel",)),
    )(afft)


def _one_hot_rows(pos_row, n_slots):
    slot = lax.broadcasted_iota(I32, (n_slots, pos_row.shape[1]), 0)
    return (slot == pos_row).astype(BF16)


def _dispatch(parts, group_tokens, group_slots):
    n_exp = parts[0][0].shape[0]
    width = parts[0][1].shape[1]
    part_groups = [pos.shape[1] // group_tokens for pos, _ in parts]
    workers = SC_CORES * SC_SUBCORES
    chunks = group_slots // SC_GATHER_ROWS
    assert all(g * n_exp % workers == 0 for g in part_groups)
    assert group_slots % SC_GATHER_ROWS == 0 and group_tokens % SC_LANES == 0
    assert n_exp & (n_exp - 1) == 0
    exp_shift = n_exp.bit_length() - 1
    row_shift = SC_GATHER_ROWS.bit_length() - 1

    def body(*refs):
        ins, (out_hbm, pos_v, idx_v, rows_v, sem) = refs[:2 * len(parts)], refs[2 * len(parts):]
        wid = lax.axis_index("s") * SC_CORES + lax.axis_index("c")
        lane = lax.iota(I32, SC_LANES)
        group_base = 0
        for part, groups in enumerate(part_groups):
            pos_hbm, h_hbm = ins[2 * part], ins[2 * part + 1]
            for k in range(groups * n_exp // workers):
                pair = wid + workers * k
                g = pair >> exp_shift
                e = pair & (n_exp - 1)
                pltpu.sync_copy(pos_hbm.at[e, pl.ds(g * group_tokens, group_tokens)], pos_v)

                @pl.loop(0, group_tokens // SC_LANES)
                def _(i):
                    p = pos_v[pl.ds(i * SC_LANES, SC_LANES)]
                    tok = g * group_tokens + i * SC_LANES + lane
                    slot = jnp.maximum(p, 0)
                    plsc.store_scatter(idx_v, [slot >> row_shift, slot & (SC_GATHER_ROWS - 1)],
                                       tok, mask=p >= 0)

                for c in range(chunks):
                    row0 = (group_base + g) * group_slots + c * SC_GATHER_ROWS
                    pltpu.async_copy(h_hbm.at[idx_v.at[c]], rows_v, sem).wait()
                    pltpu.sync_copy(rows_v, out_hbm.at[e, pl.ds(row0, SC_GATHER_ROWS)])
            group_base += groups

    mesh = plsc.VectorSubcoreMesh(core_axis_name="c", subcore_axis_name="s",
                                  num_cores=SC_CORES, num_subcores=SC_SUBCORES)
    return pl.kernel(
        body,
        out_type=jax.ShapeDtypeStruct((n_exp, sum(part_groups) * group_slots, width), I32),
        mesh=mesh,
        scratch_types=[pltpu.VMEM((group_tokens,), I32), pltpu.VMEM((chunks, SC_GATHER_ROWS), I32),
                       pltpu.VMEM((SC_GATHER_ROWS, width), I32), pltpu.SemaphoreType.DMA],
        compiler_params=pltpu.CompilerParams(needs_layout_passes=False),
        name="dispatch",
    )(*[a for part in parts for a in part])


def _expert_kernel(xg_ref, wg_ref, wu_ref, wd_ref, y_ref, wg_s, wu_s, wd_s):
    @pl.when(pl.program_id(1) == 0)
    def _():
        wg_s[...] = wg_ref[0].astype(BF16)
        wu_s[...] = wu_ref[0].astype(BF16)
        wd_s[...] = wd_ref[0].astype(BF16)

    xg = _unpack_bf16_pairs(xg_ref[0])
    a = _silu(_dot(xg, wg_s[...])) * _dot(xg, wu_s[...])
    y_ref[0] = _dot(a.astype(BF16), wd_s[...]).astype(y_ref.dtype)


def _experts(xg, w_gate, w_up, w_down, ts):
    e, s, half = xg.shape
    d, f = w_gate.shape[1:]
    assert d == 2 * half
    x_blk = pl.BlockSpec((1, ts, d), lambda x, i: (x, i, 0))
    return pl.pallas_call(
        _expert_kernel,
        name=f"experts_s{s}", grid=(e, s // ts),
        in_specs=[pl.BlockSpec((1, ts, half), lambda x, i: (x, i, 0)),
                  pl.BlockSpec((1, d, f), lambda x, i: (x, 0, 0)),
                  pl.BlockSpec((1, d, f), lambda x, i: (x, 0, 0)),
                  pl.BlockSpec((1, f, d), lambda x, i: (x, 0, 0))],
        out_specs=x_blk,
        out_shape=jax.ShapeDtypeStruct((e, s, d), BF16),
        scratch_shapes=[pltpu.VMEM((d, f), BF16), pltpu.VMEM((d, f), BF16),
                        pltpu.VMEM((f, d), BF16)],
        compiler_params=_params(("parallel", "arbitrary")),
    )(xg, w_gate, w_up, w_down)


def _combine_kernel(pos_ref, afftok_ref, y_ref, x1_ref, mod_ref, o_ref):
    step = pl.program_id(2)
    d = x1_ref.shape[1]
    per_step = y_ref.shape[0]

    @pl.when(step == 0)
    def _():
        o_ref[...] = jnp.zeros_like(o_ref)

    lane = lax.broadcasted_iota(I32, afftok_ref.shape, 1)
    acc = None
    for k in range(per_step):
        e = step * per_step + k
        p = _one_hot_rows(pos_ref[pl.ds(e, 1), :], y_ref.shape[1])
        gate = jnp.sum(jnp.where(lane == e, afftok_ref[...], 0.0), axis=1, keepdims=True)
        term = gate * _dot_tn(p, y_ref[k])
        acc = term if acc is None else acc + term
    o_ref[...] += acc

    @pl.when(step == pl.num_programs(2) - 1)
    def _():
        g2 = mod_ref[0][:, 5 * d:6 * d]
        o_ref[...] = x1_ref[...] + g2 * o_ref[...]


def _combine(pos, afftok, y, x1, mod3, mod_row, group_tokens, group_slots, group_base, tc):
    e, n = pos.shape
    d = x1.shape[1]
    groups = n // group_tokens
    per = group_tokens // tc
    tok = lambda g, j, x: (g * per + j, 0)
    return pl.pallas_call(
        _combine_kernel,
        name=f"combine_n{n}", grid=(groups, per, e // COMBINE_EXPERTS),
        in_specs=[pl.BlockSpec((e, tc), lambda g, j, x: (0, g * per + j)),
                  pl.BlockSpec((tc, LANES), tok),
                  pl.BlockSpec((COMBINE_EXPERTS, group_slots, d),
                               lambda g, j, x: (x, group_base + g, 0)),
                  pl.BlockSpec((tc, d), tok),
                  pl.BlockSpec((1, 1, mod3.shape[2]), lambda g, j, x: (mod_row(g), 0, 0))],
        out_specs=pl.BlockSpec((tc, d), tok),
        out_shape=jax.ShapeDtypeStruct((n, d), F32),
        compiler_params=_params(("parallel", "parallel", "arbitrary")),
    )(pos, afftok, y, x1, mod3)


def _rope_tables(t_len):
    n_freq = DA_DQK // 4
    inv = ROPE_BASE ** (-jnp.arange(n_freq, dtype=F32) / n_freq)
    t = jnp.arange(t_len)
    pos = jnp.stack([(t // GRID_W).astype(F32), (t % GRID_W).astype(F32)], axis=1)
    ang = pos[:, :, None, None] * inv[None, None, None, :]
    ang = jnp.broadcast_to(ang, (t_len, 2, 2, n_freq))
    sign = jnp.array([-1.0, 1.0], F32)[None, None, :, None]
    cos = jnp.cos(ang).reshape(t_len, DA_DQK)
    sin = (jnp.sin(ang) * sign).reshape(t_len, DA_DQK)
    reps = SEG // DA_DQK
    return jnp.tile(cos, (1, reps)), jnp.tile(sin, (1, reps))


def _trunk(x, batch, mod3, mod_row_tok, weights, ctx_k, ctx_v, s0, rope, group_tokens):
    (norm1_w, norm2_w, w_in, b_gate, lb_logits, hgrn_norm_w, qkw, gm, lam_p, subln_w,
     w_bh, w_ba, w_out, rw_cat) = weights
    n, d = x.shape
    t_len = n // batch
    latent = rope is not None
    (q_h, bf, bb, kff, kfb, i_h, g_h, dq, dk, dv, gates) = _premix(
        x, mod3, functools.partial(mod_row_tok, tm=PREMIX_TOKENS), norm1_w, w_in, b_gate, lb_logits,
        qkw, gm, rope, PREMIX_TOKENS, BF16 if latent else F32, latent)
    o_h, s_new = _hgrn(q_h, bf, bb, kff, kfb, i_h, g_h, hgrn_norm_w, s0, batch)
    o_a = _attention(dq, dk, dv, ctx_k, ctx_v, lam_p, subln_w, batch, min(t_len, ATT_TQ * ATT_SPLIT))
    x1, h2, afft, afftok = _postmix(
        o_h, o_a, gates, x, mod3, functools.partial(mod_row_tok, tm=POSTMIX_TOKENS),
        w_bh, w_ba, w_out, norm2_w, rw_cat, POSTMIX_TOKENS)
    cap = EC_CAPACITY * t_len // N_EXPERTS
    pos = _route(afft, batch, cap, group_tokens // t_len)
    return (pos, h2, afftok, x1), dk, dv, s_new


def _moe(routed, mod3, mod_row_grps, w_eg, w_eu, w_ed, group_tokens, group_slots):
    xg = _dispatch([(pos, h2p) for pos, h2p, _, _ in routed], group_tokens, group_slots)
    y = _experts(xg, w_eg, w_eu, w_ed, group_slots)
    outs, group_base = [], 0
    for (pos, _, afftok, x1), mod_row in zip(routed, mod_row_grps):
        outs.append(_combine(pos, afftok, y, x1, mod3, mod_row, group_tokens, group_slots,
                             group_base, COMBINE_TOKENS))
        group_base += pos.shape[1] // group_tokens
    return outs


def kernel(x_prompt, x_sample, cache_k, cache_v, state_hgrn, c, c_ctx, norm1_w, norm2_w, w_mod,
           b_mod, w_in, b_gate, hgrn_lb_logits, hgrn_norm_w, qk_norm_w, diff_lambda, diff_subln_w,
           w_branch_hgrn, w_branch_attn, w_out, router_w, w_exp_gate, w_exp_up, w_exp_down):
    batch, seq, d = x_prompt.shape
    dec_batch, dec_seq, _ = x_sample.shape
    past = cache_k.shape[2]
    depth = w_in.shape[0]
    assert depth == 1
    group_tokens = dec_seq
    assert group_tokens % seq == 0 and (batch * seq) % group_tokens == 0
    l = 0

    rows = -(-(1 + dec_batch) // 8) * 8
    cond = jnp.zeros((rows, d), F32).at[0].set(c_ctx).at[1:1 + dec_batch].set(c)
    mod = _modulation(cond, w_mod[l], b_mod[l])
    mod3 = mod.reshape(rows, 1, 6 * d)

    gidx = jnp.arange(SEG) // DA_DQK
    gm = (gidx[:, None] == gidx[None, :]).astype(BF16) * (1.0 / DA_DQK)
    qkw = jnp.tile(qk_norm_w[l], (1, SEG // DA_DQK))
    rw_t = router_w[l].T
    rw_hi = rw_t.astype(BF16)
    rw_cat = jnp.concatenate([rw_hi, (rw_t - rw_hi.astype(F32)).astype(BF16)], axis=0)
    weights = (norm1_w[l][None], norm2_w[l][None], w_in[l].astype(BF16), b_gate[l][None],
               hgrn_lb_logits.reshape(4, SEG), hgrn_norm_w[l], qkw, gm, diff_lambda[l],
               diff_subln_w[l][None], w_branch_hgrn[l].astype(BF16),
               w_branch_attn[l].astype(BF16), w_out[l].astype(BF16),
               rw_cat)

    routed_ctx, k_new, v_new, s_new = _trunk(
        x_prompt.reshape(batch * seq, d), batch, mod3,
        lambda i, tm: 0, weights, None, None, None, None, group_tokens)
    per_req = dec_seq
    routed_lat, _, _, _ = _trunk(
        x_sample.reshape(dec_batch * dec_seq, d), dec_batch, mod3,
        lambda i, tm: 1 + (i * tm) // per_req, weights,
        cache_k[:, l], cache_v[:, l].reshape(dec_batch, past, SEG),
        state_hgrn[:, l], _rope_tables(dec_seq), group_tokens)
    group_slots = EC_CAPACITY * group_tokens // N_EXPERTS
    experts = (w_exp_gate[l], w_exp_up[l], w_exp_down[l], group_tokens, group_slots)
    yp, = _moe([routed_ctx], mod3, [lambda g: 0], *experts)
    ys, = _moe([routed_lat], mod3, [lambda g: 1 + g], *experts)

    return (yp.reshape(batch, seq, d), ys.reshape(dec_batch, dec_seq, d),
            k_new.reshape(batch, 1, seq, DA_HEADS, 2, DA_DQK),
            v_new.reshape(batch, 1, seq, DA_HEADS, DA_DV),
            s_new.reshape(batch, 1, 2, HG_HEADS, HG_DK, HG_DV))
```

```python
import functools
import math

import jax
import jax.numpy as jnp
from jax import lax
from jax.experimental import pallas as pl
from jax.experimental.pallas import tpu as pltpu
from jax.experimental.pallas import tpu_sc as plsc

F32 = jnp.float32
BF16 = jnp.bfloat16
I32 = jnp.int32

EPS = 1e-6
GRID_W = 64
HG_HEADS = 4
HG_DK = 128
HG_DV = 128
HG_CHUNK_LOG2 = 7
HG_CHUNK = 1 << HG_CHUNK_LOG2
HG_UNROLL = 4
DA_HEADS = 4
DA_DQK = 64
DA_DV = 128
N_EXPERTS = 16
EC_CAPACITY = 2
ROPE_BASE = 10000.0
SEG = 512
N_SEG = 12
LAM_INIT = 0.8 - 0.6 * math.exp(-0.3 * 0)
LANES = 128
ONES_ROWS = 16
ATT_SPLIT = 2
ATT_TQ = 256
PREMIX_TOKENS = 256
POSTMIX_TOKENS = 512
COMBINE_TOKENS = 512
COMBINE_EXPERTS = 8
ROUTE_LANES = 2048
EXPERT_ROWS = 1024
Q_SCALE = DA_DQK ** -0.5 * math.log2(math.e)
HG_SAFE_DECAY = 80.0
VMEM_LIMIT = 56 * 1024 * 1024
SC_CORES = 2
SC_SUBCORES = 16
SC_LANES = 16
SC_GATHER_ROWS = 128


def _dot(a, b):
    return jnp.dot(a, b, preferred_element_type=F32)


def _dot_nt(a, b):
    return lax.dot_general(a, b, (((1,), (1,)), ((), ())), preferred_element_type=F32)


def _dot_tn(a, b):
    return lax.dot_general(a, b, (((0,), (0,)), ((), ())), preferred_element_type=F32)


def _split2(x):
    hi = x.astype(BF16)
    lo = (x - hi.astype(F32)).astype(BF16)
    return hi, lo


def _silu(x):
    return x * jax.nn.sigmoid(x)


def _pack_bf16_pairs(x):
    w = x.shape[1] // 2
    bits = pltpu.bitcast(x.astype(BF16).astype(F32), I32)
    return lax.shift_right_logical(bits[:, :w], 16) | bits[:, w:]


def _unpack_bf16_pairs(words):
    lo = pltpu.bitcast(words << 16, F32)
    hi = pltpu.bitcast(words & jnp.int32(-65536), F32)
    return jnp.concatenate([lo, hi], axis=1).astype(BF16)


def _params(sem):
    return pltpu.CompilerParams(dimension_semantics=sem, vmem_limit_bytes=VMEM_LIMIT)


def _mod_kernel(c_ref, w_ref, b_ref, o_ref):
    s_hi, s_lo = _split2(_silu(c_ref[...]))
    w_hi, w_lo = _split2(w_ref[...])
    o_ref[...] = _dot(s_hi, w_hi) + _dot(s_hi, w_lo) + _dot(s_lo, w_hi) + b_ref[...]


def _modulation(cond, w_mod, b_mod):
    rows, d = cond.shape
    n = w_mod.shape[1]
    bn = 512
    return pl.pallas_call(
        _mod_kernel,
        name="modulation", grid=(n // bn,),
        in_specs=[pl.BlockSpec((rows, d), lambda j: (0, 0)),
                  pl.BlockSpec((d, bn), lambda j: (0, j)),
                  pl.BlockSpec((1, bn), lambda j: (0, j))],
        out_specs=pl.BlockSpec((rows, bn), lambda j: (0, j)),
        out_shape=jax.ShapeDtypeStruct((rows, n), F32),
        compiler_params=_params(("arbitrary",)),
    )(cond, w_mod, b_mod.reshape(1, n))


def _group_rms(z, gm_ref, w):
    ms = _dot((z * z).astype(BF16), gm_ref[...])
    return z * lax.rsqrt(ms + EPS) * w


def _rope(x, cos, sin_signed):
    n = x.shape[-1]
    lane = lax.broadcasted_iota(I32, x.shape, 1)
    partner = jnp.where((lane & 16) == 0, pltpu.roll(x, n - 16, 1), pltpu.roll(x, 16, 1))
    return x * cos + partner * sin_signed


def _premix_kernel(*refs, latent):
    if latent:
        (x_ref, mod_ref, n1_ref, win_ref, bg_ref, lbl_ref, qkw_ref, gm_ref, cos_ref, sin_ref,
         q_o, bf_o, bb_o, kff_o, kfb_o, i_o, g_o, dq_o, dk_o, dv_o, gate_o) = refs
    else:
        (x_ref, mod_ref, n1_ref, win_ref, bg_ref, lbl_ref, qkw_ref, gm_ref,
         q_o, bf_o, bb_o, kff_o, kfb_o, i_o, g_o, dq_o, dk_o, dv_o, gate_o) = refs
    d = x_ref.shape[1]
    mod = mod_ref[0]
    sh1, sc1 = mod[:, 0:d], mod[:, d:2 * d]
    x = x_ref[...]
    xn = x * lax.rsqrt(jnp.mean(x * x, axis=-1, keepdims=True) + EPS) * n1_ref[...]
    hb = (xn * (1.0 + sc1) + sh1).astype(BF16)

    def seg(j):
        return _dot(hb, win_ref[:, j * SEG:(j + 1) * SEG])

    def lower_bound(direction):
        l0 = lbl_ref[2 * direction:2 * direction + 1, :]
        l1 = lbl_ref[2 * direction + 1:2 * direction + 2, :]
        mx = jnp.maximum(l0, l1)
        e0, e1 = jnp.exp(l0 - mx), jnp.exp(l1 - mx)
        return e0 / (e0 + e1)

    tm = x.shape[0]
    row = lax.broadcasted_iota(I32, (tm, tm), 0)
    col = lax.broadcasted_iota(I32, (tm, tm), 1)
    same_chunk = (row >> HG_CHUNK_LOG2) == (col >> HG_CHUNK_LOG2)

    def store(o_ref, val):
        val = val.astype(o_ref.dtype)
        if len(o_ref.shape) == 2:
            o_ref[...] = val
        elif len(o_ref.shape) == 3:
            for h in range(o_ref.shape[0]):
                o_ref[h] = val[:, h * LANES:(h + 1) * LANES]
        else:
            for h in range(o_ref.shape[2]):
                for m in range(2):
                    lo = (2 * h + m) * DA_DQK
                    o_ref[0, :, h, m, :] = val[:, lo:lo + DA_DQK]

    store(q_o, _silu(seg(0)))
    for j, b_o, kf_o, order in ((1, bf_o, kff_o, row >= col), (2, bb_o, kfb_o, row <= col)):
        lbd = lower_bound(j - 1)
        f = lbd + (1.0 - lbd) * jax.nn.sigmoid(seg(j))
        tri = (same_chunk & order).astype(BF16)
        hi, lo = _split2(jnp.log(f))
        store(b_o, _dot(tri, hi) + _dot(tri, lo))
        store(kf_o, 1.0 - f)
    store(i_o, seg(3))
    store(g_o, _silu(seg(4)))
    qn = _group_rms(seg(5), gm_ref, qkw_ref[0:1, :]) * Q_SCALE
    kn = _group_rms(seg(6), gm_ref, qkw_ref[1:2, :])
    if latent:
        qn = _rope(qn, cos_ref[...], sin_ref[...])
        kn = _rope(kn, cos_ref[...], sin_ref[...])
    store(dq_o, qn)
    store(dk_o, kn)
    store(dv_o, seg(7))
    for j in range(4):
        z = seg(8 + j) + bg_ref[:, j * SEG:(j + 1) * SEG]
        gate_o[:, j * SEG:(j + 1) * SEG] = jax.nn.sigmoid(z).astype(gate_o.dtype)


def _premix(x, mod3, mod_row, norm1_w, w_in, b_gate, lb_logits, qkw, gm, rope, tm, kv_dtype,
            kv_head_major):
    n, d = x.shape
    latent = rope is not None
    const = lambda i: (0, 0)
    in_specs = [pl.BlockSpec((tm, d), lambda i: (i, 0)),
                pl.BlockSpec((1, 1, mod3.shape[2]), lambda i: (mod_row(i), 0, 0)),
                pl.BlockSpec((1, d), const),
                pl.BlockSpec(w_in.shape, const),
                pl.BlockSpec(b_gate.shape, const),
                pl.BlockSpec(lb_logits.shape, const),
                pl.BlockSpec(qkw.shape, const),
                pl.BlockSpec(gm.shape, const)]
    args = [x, mod3, norm1_w, w_in, b_gate, lb_logits, qkw, gm]
    if latent:
        cos, sin = rope
        nblk = cos.shape[0] // tm
        in_specs += [pl.BlockSpec((tm, SEG), lambda i: (i % nblk, 0))] * 2
        args += [cos, sin]
    heads = SEG // LANES
    head_spec = pl.BlockSpec((heads, tm, LANES), lambda i: (0, i, 0))
    tok_spec = pl.BlockSpec((tm, SEG), lambda i: (i, 0))
    out_dtypes = [BF16, F32, F32, BF16, BF16, BF16, BF16, BF16]
    out_shape = [jax.ShapeDtypeStruct((heads, n, LANES), t) for t in out_dtypes]
    if kv_head_major:
        out_shape += [jax.ShapeDtypeStruct((heads, n, LANES), kv_dtype)] * 2
        kv_specs = [head_spec] * 2
    else:
        k_shape = (n // tm, tm, DA_HEADS, 2, DA_DQK)
        out_shape += [jax.ShapeDtypeStruct(k_shape, kv_dtype),
                      jax.ShapeDtypeStruct((n, SEG), kv_dtype)]
        kv_specs = [pl.BlockSpec((1,) + k_shape[1:], lambda i: (i, 0, 0, 0, 0)), tok_spec]
    out_shape.append(jax.ShapeDtypeStruct((n, 4 * SEG), BF16))
    out_specs = [head_spec] * 8 + kv_specs + [pl.BlockSpec((tm, 4 * SEG), lambda i: (i, 0))]
    return pl.pallas_call(
        functools.partial(_premix_kernel, latent=latent),
        name=f"premix_n{n}", grid=(n // tm,),
        in_specs=in_specs, out_specs=out_specs, out_shape=out_shape,
        compiler_params=_params(("parallel",)),
    )(*args)


def _hgrn_chunk_local(q, b, total, k, v, keep, safe):
    if safe:
        ref = b[HG_CHUNK // 2:HG_CHUNK // 2 + 1, :]
        qa = q * jnp.exp(b - ref)
        kb = k * jnp.exp(ref - b)
        attn = jnp.where(keep, _dot_nt(qa.astype(BF16), kb.astype(BF16)), 0.0)
        qe = qa * jnp.exp(ref)
        kd = kb * jnp.exp(total - ref)
    else:
        qe = q * jnp.exp(b)
        kd = k * jnp.exp(total - b)
        col = lax.broadcasted_iota(I32, (HG_CHUNK, HG_CHUNK), 1)

        def column(s, acc):
            onehot = (lax.broadcasted_iota(I32, (HG_CHUNK, 1), 0) == s).astype(F32)
            bs = jnp.sum(b * onehot, axis=0, keepdims=True)
            ks = jnp.sum(k * onehot, axis=0, keepdims=True)
            w = jnp.sum(q * ks * jnp.exp(jnp.minimum(b - bs, 0.0)), axis=1, keepdims=True)
            return jnp.where(col == s, w, acc)

        attn = lax.fori_loop(0, HG_CHUNK, column, jnp.zeros((HG_CHUNK, HG_CHUNK), F32))
        attn = jnp.where(keep, attn, 0.0)
    vt = v.astype(F32).T.astype(BF16)
    lhs = jnp.concatenate([qe.astype(BF16), attn.astype(BF16)], axis=1)
    return lhs, vt, _dot(vt, kd.astype(BF16)), jnp.exp(total)


def _hgrn_scan_group(chunks, st):
    outs = []
    for lhs, vt, inc, decay in chunks:
        outs.append(_dot_nt(lhs, jnp.concatenate([st.astype(BF16), vt], axis=1)))
        st = st * decay + inc
    return outs, st


def _hgrn_kernel(*refs, has_state):
    if has_state:
        (q_ref, bf_ref, bb_ref, kff_ref, kfb_ref, i_ref, g_ref, nw_ref, s0_ref,
         o_ref, sn_ref, of_s, ob_s) = refs
    else:
        (q_ref, bf_ref, bb_ref, kff_ref, kfb_ref, i_ref, g_ref, nw_ref,
         o_ref, sn_ref, of_s, ob_s) = refs
    t_len = q_ref.shape[1]
    nc = t_len // HG_CHUNK
    unroll = min(HG_UNROLL, nc)
    head = pl.program_id(1)
    row = lax.broadcasted_iota(I32, (HG_CHUNK, HG_CHUNK), 0)
    col = lax.broadcasted_iota(I32, (HG_CHUNK, HG_CHUNK), 1)
    keep_f, keep_b = row >= col, row <= col

    mid_f = bf_ref[0, pl.ds(HG_CHUNK // 2, nc, stride=HG_CHUNK), :]
    tot_f = bf_ref[0, pl.ds(HG_CHUNK - 1, nc, stride=HG_CHUNK), :]
    mid_b = bb_ref[0, pl.ds(HG_CHUNK // 2, nc, stride=HG_CHUNK), :]
    tot_b = bb_ref[0, pl.ds(0, nc, stride=HG_CHUNK), :]
    worst = jnp.minimum(jnp.min(jnp.minimum(mid_f, tot_f - mid_f)),
                        jnp.min(jnp.minimum(mid_b, tot_b - mid_b)))

    if has_state:
        st_f0, st_b0 = s0_ref[0, 0, 0].T, s0_ref[0, 1, 0].T
    else:
        st_f0 = st_b0 = jnp.zeros((HG_DV, HG_DK), F32)

    def scan(safe):
        def body(it, carry):
            st_f, st_b = carry
            rows_f, rows_b, loc_f, loc_b = [], [], [], []
            for u in range(unroll):
                c = it * unroll + u
                sf = pl.multiple_of(c * HG_CHUNK, HG_CHUNK)
                sb = pl.multiple_of((nc - 1 - c) * HG_CHUNK, HG_CHUNK)
                rf, rb = pl.ds(sf, HG_CHUNK), pl.ds(sb, HG_CHUNK)
                rows_f.append(rf)
                rows_b.append(rb)
                loc_f.append(_hgrn_chunk_local(
                    q_ref[0, rf, :].astype(F32), bf_ref[0, rf, :],
                    bf_ref[0, pl.ds(sf + HG_CHUNK - 1, 1), :],
                    kff_ref[0, rf, :].astype(F32), i_ref[0, rf, :], keep_f, safe))
                loc_b.append(_hgrn_chunk_local(
                    q_ref[0, rb, :].astype(F32), bb_ref[0, rb, :], bb_ref[0, pl.ds(sb, 1), :],
                    kfb_ref[0, rb, :].astype(F32), i_ref[0, rb, :], keep_b, safe))
            outs_f, st_f = _hgrn_scan_group(loc_f, st_f)
            outs_b, st_b = _hgrn_scan_group(loc_b, st_b)
            for rf, rb, o_f, o_b in zip(rows_f, rows_b, outs_f, outs_b):
                of_s[rf, :] = o_f
                ob_s[rb, :] = o_b
            return st_f, st_b
        return lax.fori_loop(0, nc // unroll, body, (st_f0, st_b0))

    st_f, st_b = lax.cond(worst >= -HG_SAFE_DECAY, lambda: scan(True), lambda: scan(False))
    sn_ref[0, 0, 0] = st_f.T
    sn_ref[0, 1, 0] = st_b.T
    o = of_s[...] + ob_s[...]
    nw = nw_ref[pl.ds(head, 1), :]
    on = o * lax.rsqrt(jnp.mean(o * o, axis=-1, keepdims=True) + EPS) * nw
    o_ref[0] = (on * g_ref[0].astype(F32)).astype(o_ref.dtype)


def _hgrn(q, bf, bb, kff, kfb, iv, g, norm_w, s0, batch):
    heads, n, _ = q.shape
    t_len = n // batch
    blk = pl.BlockSpec((1, t_len, HG_DK), lambda b, h: (h, b, 0))
    st_blk = pl.BlockSpec((1, 2, 1, HG_DK, HG_DV), lambda b, h: (b, 0, h, 0, 0))
    args = [q, bf, bb, kff, kfb, iv, g, norm_w]
    in_specs = [blk] * 7 + [pl.BlockSpec(norm_w.shape, lambda b, h: (0, 0))]
    if s0 is not None:
        args.append(s0)
        in_specs.append(st_blk)
    o, s_new = pl.pallas_call(
        functools.partial(_hgrn_kernel, has_state=s0 is not None),
        name=f"hgrn_n{n}", grid=(batch, HG_HEADS),
        in_specs=in_specs,
        out_specs=[blk, st_blk],
        out_shape=[jax.ShapeDtypeStruct((heads, n, HG_DV), BF16),
                   jax.ShapeDtypeStruct((batch, 2, HG_HEADS, HG_DK, HG_DV), F32)],
        scratch_shapes=[pltpu.VMEM((t_len, HG_DV), F32), pltpu.VMEM((t_len, HG_DV), F32)],
        compiler_params=_params(("parallel", "parallel")),
    )(*args)
    return o, s_new


def _attn_kernel(*refs, has_ctx):
    if has_ctx:
        q_ref, k_ref, v_ref, ck_ref, cv_ref, lam_ref, sw_ref, o_ref, k_s, vt_s = refs
    else:
        q_ref, k_ref, v_ref, lam_ref, sw_ref, o_ref, k_s, vt_s = refs
    t_own = k_ref.shape[1]

    def load_keys(ref):
        if len(ref.shape) == 3:
            return ref[0]
        return jnp.concatenate([ref[0, :, 0, 0, :], ref[0, :, 0, 1, :]], axis=1)

    @pl.when(pl.program_id(2) == 0)
    def _():
        k_s[0:t_own, :] = load_keys(k_ref).astype(BF16)
        vt_s[0:DA_DV, 0:t_own] = v_ref[0].astype(F32).T.astype(BF16)
        if has_ctx:
            k_s[t_own:, :] = load_keys(ck_ref).astype(BF16)
            vt_s[0:DA_DV, t_own:] = cv_ref[0].T.astype(BF16)
        vt_s[DA_DV:, :] = jnp.ones((ONES_ROWS, vt_s.shape[1]), BF16)

    lv = lam_ref[...]
    lam = (jnp.exp(jnp.sum(lv[0:1] * lv[1:2], keepdims=True))
           - jnp.exp(jnp.sum(lv[2:3] * lv[3:4], keepdims=True)) + LAM_INIT)
    tq = q_ref.shape[1] // ATT_SPLIT
    dim = lax.broadcasted_iota(I32, (2 * DA_DQK, tq), 0)

    def scores(i):
        qt = q_ref[0, i * tq:(i + 1) * tq, :].astype(F32).T
        q_both = jnp.concatenate([jnp.where(dim < DA_DQK, qt, 0.0),
                                  jnp.where(dim >= DA_DQK, qt, 0.0)], axis=1).astype(BF16)
        return _dot(k_s[...], q_both)

    def finish(i, st):
        pt = jnp.exp2(st - jnp.max(st, axis=0, keepdims=True)).astype(BF16)
        r = _dot(vt_s[...], pt)
        r = r[0:DA_DV] * (1.0 / r[DA_DV:DA_DV + 1])
        o = (r[:, 0:tq] - lam * r[:, tq:2 * tq]).T
        on = o * lax.rsqrt(jnp.mean(o * o, axis=-1, keepdims=True) + EPS) * sw_ref[...]
        o_ref[0, i * tq:(i + 1) * tq, :] = (on * (1.0 - LAM_INIT)).astype(o_ref.dtype)

    sts = [scores(i) for i in range(ATT_SPLIT)]
    for i in range(ATT_SPLIT):
        finish(i, sts[i])


def _attention(q, k, v, ctx_k, ctx_v, lam_p, subln_w, batch, tq):
    heads, n, _ = q.shape
    t_len = n // batch
    nq = t_len // tq
    q_blk = pl.BlockSpec((1, tq, DA_DV), lambda b, h, i: (h, b * nq + i, 0))
    const = lambda b, h, i: (0, 0)
    if k.ndim == 3:
        kv_blk = pl.BlockSpec((1, t_len, DA_DV), lambda b, h, i: (h, b, 0))
        in_specs = [q_blk, kv_blk, kv_blk]
        args = [q, k, v]
    else:
        assert k.shape == (batch, t_len, DA_HEADS, 2, DA_DQK)
        in_specs = [q_blk, pl.BlockSpec((1, t_len, 1, 2, DA_DQK), lambda b, h, i: (b, 0, h, 0, 0)),
                    pl.BlockSpec((1, t_len, DA_DV), lambda b, h, i: (b, 0, h))]
        args = [q, k, v.reshape(batch, t_len, SEG)]
    n_keys = t_len
    if ctx_k is not None:
        past = ctx_k.shape[1]
        n_keys += past
        ck_blk = pl.BlockSpec((1, past, 1, 2, DA_DQK), lambda b, h, i: (b, 0, h, 0, 0))
        cv_blk = pl.BlockSpec((1, past, DA_DV), lambda b, h, i: (b, 0, h))
        args += [ctx_k, ctx_v]
        in_specs += [ck_blk, cv_blk]
    args += [lam_p, subln_w]
    in_specs += [pl.BlockSpec(lam_p.shape, const), pl.BlockSpec(subln_w.shape, const)]
    o = pl.pallas_call(
        functools.partial(_attn_kernel, has_ctx=ctx_k is not None),
        name=f"attn_n{n}", grid=(batch, DA_HEADS, nq),
        in_specs=in_specs, out_specs=q_blk,
        out_shape=jax.ShapeDtypeStruct((heads, n, DA_DV), BF16),
        scratch_shapes=[pltpu.VMEM((n_keys, 2 * DA_DQK), BF16),
                        pltpu.VMEM((DA_DV + ONES_ROWS, n_keys), BF16)],
        compiler_params=_params(("parallel", "parallel", "arbitrary")),
    )(*args)
    return o


def _postmix_kernel(oh_ref, oa_ref, gate_ref, x_ref, mod_ref, wbh_ref, wba_ref, wout_ref,
                    n2_ref, rw_ref, x1_o, h2_o, afft_o, afftok_o):
    d = x_ref.shape[1]
    mod = mod_ref[0]
    g1, sh2, sc2 = mod[:, 2 * d:3 * d], mod[:, 3 * d:4 * d], mod[:, 4 * d:5 * d]
    g_h = gate_ref[:, 0:d].astype(F32)
    g_a = gate_ref[:, d:2 * d].astype(F32)

    def heads_on_lanes(ref):
        return jnp.concatenate([ref[h] for h in range(ref.shape[0])], axis=1)

    merged = (g_h * _dot(heads_on_lanes(oh_ref), wbh_ref[...])
              + g_a * _dot(heads_on_lanes(oa_ref), wba_ref[...]))
    x1 = x_ref[...] + g1 * _dot(merged.astype(BF16), wout_ref[...])
    x1_o[...] = x1
    xn = x1 * lax.rsqrt(jnp.mean(x1 * x1, axis=-1, keepdims=True) + EPS) * n2_ref[...]
    h2 = xn * (1.0 + sc2) + sh2
    h2_o[...] = _pack_bf16_pairs(h2)
    h_hi, h_lo = _split2(h2)
    rw = rw_ref[...]
    t1 = _dot_nt(rw, h_hi)
    t2 = _dot_nt(rw, h_lo)
    e = N_EXPERTS
    logits = t1[0:e] + t1[e:2 * e] + t2[0:e]
    mx = jnp.max(logits, axis=0, keepdims=True)
    p = jnp.exp(logits - mx)
    aff = p / jnp.sum(p, axis=0, keepdims=True)
    afft_o[...] = aff
    pad = jnp.zeros((LANES - e, aff.shape[1]), F32)
    afftok_o[...] = jnp.concatenate([aff, pad], axis=0).T


def _postmix(o_h, o_a, gates, x, mod3, mod_row, w_bh, w_ba, w_out, norm2_w, rw_cat, tm):
    n, d = x.shape
    const = lambda i: (0, 0)
    row = lambda i: (i, 0)
    return pl.pallas_call(
        _postmix_kernel,
        name=f"postmix_n{n}", grid=(n // tm,),
        in_specs=[pl.BlockSpec((o_h.shape[0], tm, LANES), lambda i: (0, i, 0)),
                  pl.BlockSpec((o_a.shape[0], tm, LANES), lambda i: (0, i, 0)),
                  pl.BlockSpec((tm, 4 * SEG), row), pl.BlockSpec((tm, d), row),
                  pl.BlockSpec((1, 1, mod3.shape[2]), lambda i: (mod_row(i), 0, 0)),
                  pl.BlockSpec(w_bh.shape, const), pl.BlockSpec(w_ba.shape, const),
                  pl.BlockSpec(w_out.shape, const), pl.BlockSpec((1, d), const),
                  pl.BlockSpec(rw_cat.shape, const)],
        out_specs=[pl.BlockSpec((tm, d), row), pl.BlockSpec((tm, d // 2), row),
                   pl.BlockSpec((N_EXPERTS, tm), lambda i: (0, i)),
                   pl.BlockSpec((tm, LANES), row)],
        out_shape=[jax.ShapeDtypeStruct((n, d), F32), jax.ShapeDtypeStruct((n, d // 2), I32),
                   jax.ShapeDtypeStruct((N_EXPERTS, n), F32),
                   jax.ShapeDtypeStruct((n, LANES), F32)],
        compiler_params=_params(("parallel",)),
    )(o_h, o_a, gates, x, mod3, w_bh, w_ba, w_out, norm2_w, rw_cat)


def _lane_cumsum_exclusive(x, blk):
    e, t = x.shape
    r = lax.broadcasted_iota(I32, (blk, blk), 0)
    c = lax.broadcasted_iota(I32, (blk, blk), 1)
    upper = (r < c).astype(BF16)
    carry = jnp.zeros((e, 1), F32)
    parts = []
    for j in range(t // blk):
        xb = x[:, j * blk:(j + 1) * blk]
        parts.append(_dot(xb.astype(BF16), upper) + carry)
        carry = carry + jnp.sum(xb, axis=1, keepdims=True)
    return parts[0] if len(parts) == 1 else jnp.concatenate(parts, axis=1)


def _route_kernel(aff_ref, pos_ref, *, cap, req_per_group, t_len):
    n_req = aff_ref.shape[1] // t_len
    bits = [pltpu.bitcast(aff_ref[:, r * t_len:(r + 1) * t_len], I32) for r in range(n_req)]

    def count(mask):
        return jnp.sum(mask.astype(F32), axis=1, keepdims=True)

    def step(i, ths):
        bit = jnp.int32(1) << (30 - i)
        return tuple(jnp.where(count(b >= (th | bit)) >= cap, th | bit, th)
                     for b, th in zip(bits, ths))

    zero = jnp.zeros((aff_ref.shape[0], 1), I32)
    ths = lax.fori_loop(0, 31, step, (zero,) * n_req)
    blk = min(t_len, 256)
    for r, (b, th) in enumerate(zip(bits, ths)):
        gt = b > th
        eq = (b == th).astype(F32)
        need = cap - count(gt)
        tie_rank = _lane_cumsum_exclusive(eq, blk)
        sel = jnp.where(gt, 1.0, jnp.where(tie_rank < need, eq, 0.0))
        slot = _lane_cumsum_exclusive(sel, blk)
        offset = ((pl.program_id(0) * n_req + r) % req_per_group) * cap
        pos_ref[:, r * t_len:(r + 1) * t_len] = jnp.where(sel > 0.0, slot.astype(I32) + offset, -1)


def _route(afft, batch, cap, req_per_group):
    e, n = afft.shape
    t_len = n // batch
    per_step = max(1, min(batch, ROUTE_LANES // t_len))
    assert batch % per_step == 0
    blk = pl.BlockSpec((e, per_step * t_len), lambda b: (0, b))
    return pl.pallas_call(
        functools.partial(_route_kernel, cap=cap, req_per_group=req_per_group, t_len=t_len),
        name=f"route_n{n}", grid=(batch // per_step,), in_specs=[blk], out_specs=blk,
        out_shape=jax.ShapeDtypeStruct((e, n), I32),
        compiler_params=_params(("parallel",)),
    )(afft)


def _one_hot_rows(pos_row, n_slots):
    slot = lax.broadcasted_iota(I32, (n_slots, pos_row.shape[1]), 0)
    return (slot == pos_row).astype(BF16)


def _dispatch(parts, group_tokens, group_slots):
    n_exp = parts[0][0].shape[0]
    width = parts[0][1].shape[1]
    part_groups = [pos.shape[1] // group_tokens for pos, _ in parts]
    workers = SC_CORES * SC_SUBCORES
    chunks = group_slots // SC_GATHER_ROWS
    assert all(g * n_exp % workers == 0 for g in part_groups)
    assert group_slots % SC_GATHER_ROWS == 0 and group_tokens % SC_LANES == 0
    assert n_exp & (n_exp - 1) == 0
    exp_shift = n_exp.bit_length() - 1
    row_shift = SC_GATHER_ROWS.bit_length() - 1

    def body(*refs):
        ins, (out_hbm, pos_v, idx_v, rows_v, sem) = refs[:2 * len(parts)], refs[2 * len(parts):]
        wid = lax.axis_index("s") * SC_CORES + lax.axis_index("c")
        lane = lax.iota(I32, SC_LANES)
        group_base = 0
        for part, groups in enumerate(part_groups):
            pos_hbm, h_hbm = ins[2 * part], ins[2 * part + 1]
            for k in range(groups * n_exp // workers):
                pair = wid + workers * k
                g = pair >> exp_shift
                e = pair & (n_exp - 1)
                pltpu.sync_copy(pos_hbm.at[e, pl.ds(g * group_tokens, group_tokens)], pos_v)

                @pl.loop(0, group_tokens // SC_LANES)
                def _(i):
                    p = pos_v[pl.ds(i * SC_LANES, SC_LANES)]
                    tok = g * group_tokens + i * SC_LANES + lane
                    slot = jnp.maximum(p, 0)
                    plsc.store_scatter(idx_v, [slot >> row_shift, slot & (SC_GATHER_ROWS - 1)],
                                       tok, mask=p >= 0)

                for c in range(chunks):
                    row0 = (group_base + g) * group_slots + c * SC_GATHER_ROWS
                    pltpu.async_copy(h_hbm.at[idx_v.at[c]], rows_v, sem).wait()
                    pltpu.sync_copy(rows_v, out_hbm.at[e, pl.ds(row0, SC_GATHER_ROWS)])
            group_base += groups

    mesh = plsc.VectorSubcoreMesh(core_axis_name="c", subcore_axis_name="s",
                                  num_cores=SC_CORES, num_subcores=SC_SUBCORES)
    return pl.kernel(
        body,
        out_type=jax.ShapeDtypeStruct((n_exp, sum(part_groups) * group_slots, width), I32),
        mesh=mesh,
        scratch_types=[pltpu.VMEM((group_tokens,), I32), pltpu.VMEM((chunks, SC_GATHER_ROWS), I32),
                       pltpu.VMEM((SC_GATHER_ROWS, width), I32), pltpu.SemaphoreType.DMA],
        compiler_params=pltpu.CompilerParams(needs_layout_passes=False),
        name="dispatch",
    )(*[a for part in parts for a in part])


def _expert_kernel(xg_ref, wg_ref, wu_ref, wd_ref, y_ref, wg_s, wu_s, wd_s):
    @pl.when(pl.program_id(1) == 0)
    def _():
        wg_s[...] = wg_ref[0].astype(BF16)
        wu_s[...] = wu_ref[0].astype(BF16)
        wd_s[...] = wd_ref[0].astype(BF16)

    xg = _unpack_bf16_pairs(xg_ref[0])
    a = _silu(_dot(xg, wg_s[...])) * _dot(xg, wu_s[...])
    y_ref[0] = _dot(a.astype(BF16), wd_s[...]).astype(y_ref.dtype)


def _experts(xg, w_gate, w_up, w_down, ts):
    e, s, half = xg.shape
    d, f = w_gate.shape[1:]
    assert d == 2 * half
    x_blk = pl.BlockSpec((1, ts, d), lambda x, i: (x, i, 0))
    return pl.pallas_call(
        _expert_kernel,
        name=f"experts_s{s}", grid=(e, s // ts),
        in_specs=[pl.BlockSpec((1, ts, half), lambda x, i: (x, i, 0)),
                  pl.BlockSpec((1, d, f), lambda x, i: (x, 0, 0)),
                  pl.BlockSpec((1, d, f), lambda x, i: (x, 0, 0)),
                  pl.BlockSpec((1, f, d), lambda x, i: (x, 0, 0))],
        out_specs=x_blk,
        out_shape=jax.ShapeDtypeStruct((e, s, d), BF16),
        scratch_shapes=[pltpu.VMEM((d, f), BF16), pltpu.VMEM((d, f), BF16),
                        pltpu.VMEM((f, d), BF16)],
        compiler_params=_params(("parallel", "arbitrary")),
    )(xg, w_gate, w_up, w_down)


def _combine_kernel(pos_ref, afftok_ref, y_ref, x1_ref, mod_ref, o_ref):
    step = pl.program_id(2)
    d = x1_ref.shape[1]
    per_step = y_ref.shape[0]

    @pl.when(step == 0)
    def _():
        o_ref[...] = jnp.zeros_like(o_ref)

    lane = lax.broadcasted_iota(I32, afftok_ref.shape, 1)
    acc = None
    for k in range(per_step):
        e = step * per_step + k
        p = _one_hot_rows(pos_ref[pl.ds(e, 1), :], y_ref.shape[1])
        gate = jnp.sum(jnp.where(lane == e, afftok_ref[...], 0.0), axis=1, keepdims=True)
        term = gate * _dot_tn(p, y_ref[k])
        acc = term if acc is None else acc + term
    o_ref[...] += acc

    @pl.when(step == pl.num_programs(2) - 1)
    def _():
        g2 = mod_ref[0][:, 5 * d:6 * d]
        o_ref[...] = x1_ref[...] + g2 * o_ref[...]


def _combine(pos, afftok, y, x1, mod3, mod_row, group_tokens, group_slots, group_base, tc):
    e, n = pos.shape
    d = x1.shape[1]
    groups = n // group_tokens
    per = group_tokens // tc
    tok = lambda g, j, x: (g * per + j, 0)
    return pl.pallas_call(
        _combine_kernel,
        name=f"combine_n{n}", grid=(groups, per, e // COMBINE_EXPERTS),
        in_specs=[pl.BlockSpec((e, tc), lambda g, j, x: (0, g * per + j)),
                  pl.BlockSpec((tc, LANES), tok),
                  pl.BlockSpec((COMBINE_EXPERTS, group_slots, d),
                               lambda g, j, x: (x, group_base + g, 0)),
                  pl.BlockSpec((tc, d), tok),
                  pl.BlockSpec((1, 1, mod3.shape[2]), lambda g, j, x: (mod_row(g), 0, 0))],
        out_specs=pl.BlockSpec((tc, d), tok),
        out_shape=jax.ShapeDtypeStruct((n, d), F32),
        compiler_params=_params(("parallel", "parallel", "arbitrary")),
    )(pos, afftok, y, x1, mod3)


def _rope_tables(t_len):
    n_freq = DA_DQK // 4
    inv = ROPE_BASE ** (-jnp.arange(n_freq, dtype=F32) / n_freq)
    t = jnp.arange(t_len)
    pos = jnp.stack([(t // GRID_W).astype(F32), (t % GRID_W).astype(F32)], axis=1)
    ang = pos[:, :, None, None] * inv[None, None, None, :]
    ang = jnp.broadcast_to(ang, (t_len, 2, 2, n_freq))
    sign = jnp.array([-1.0, 1.0], F32)[None, None, :, None]
    cos = jnp.cos(ang).reshape(t_len, DA_DQK)
    sin = (jnp.sin(ang) * sign).reshape(t_len, DA_DQK)
    reps = SEG // DA_DQK
    return jnp.tile(cos, (1, reps)), jnp.tile(sin, (1, reps))


def _trunk(x, batch, mod3, mod_row_tok, weights, ctx_k, ctx_v, s0, rope, group_tokens):
    (norm1_w, norm2_w, w_in, b_gate, lb_logits, hgrn_norm_w, qkw, gm, lam_p, subln_w,
     w_bh, w_ba, w_out, rw_cat) = weights
    n, d = x.shape
    t_len = n // batch
    latent = rope is not None
    (q_h, bf, bb, kff, kfb, i_h, g_h, dq, dk, dv, gates) = _premix(
        x, mod3, functools.partial(mod_row_tok, tm=PREMIX_TOKENS), norm1_w, w_in, b_gate, lb_logits,
        qkw, gm, rope, PREMIX_TOKENS, BF16 if latent else F32, latent)
    o_h, s_new = _hgrn(q_h, bf, bb, kff, kfb, i_h, g_h, hgrn_norm_w, s0, batch)
    o_a = _attention(dq, dk, dv, ctx_k, ctx_v, lam_p, subln_w, batch, min(t_len, ATT_TQ * ATT_SPLIT))
    x1, h2, afft, afftok = _postmix(
        o_h, o_a, gates, x, mod3, functools.partial(mod_row_tok, tm=POSTMIX_TOKENS),
        w_bh, w_ba, w_out, norm2_w, rw_cat, POSTMIX_TOKENS)
    cap = EC_CAPACITY * t_len // N_EXPERTS
    pos = _route(afft, batch, cap, group_tokens // t_len)
    return (pos, h2, afftok, x1), dk, dv, s_new


def _moe(routed, mod3, mod_row_grps, w_eg, w_eu, w_ed, group_tokens, group_slots):
    xg = _dispatch([(pos, h2p) for pos, h2p, _, _ in routed], group_tokens, group_slots)
    y = _experts(xg, w_eg, w_eu, w_ed, min(xg.shape[1], EXPERT_ROWS))
    outs, group_base = [], 0
    for (pos, _, afftok, x1), mod_row in zip(routed, mod_row_grps):
        outs.append(_combine(pos, afftok, y, x1, mod3, mod_row, group_tokens, group_slots,
                             group_base, COMBINE_TOKENS))
        group_base += pos.shape[1] // group_tokens
    return outs


def kernel(x_prompt, x_sample, cache_k, cache_v, state_hgrn, c, c_ctx, norm1_w, norm2_w, w_mod,
           b_mod, w_in, b_gate, hgrn_lb_logits, hgrn_norm_w, qk_norm_w, diff_lambda, diff_subln_w,
           w_branch_hgrn, w_branch_attn, w_out, router_w, w_exp_gate, w_exp_up, w_exp_down):
    batch, seq, d = x_prompt.shape
    dec_batch, dec_seq, _ = x_sample.shape
    past = cache_k.shape[2]
    depth = w_in.shape[0]
    assert depth == 1
    group_tokens = dec_seq
    assert group_tokens % seq == 0 and (batch * seq) % group_tokens == 0
    l = 0

    rows = -(-(1 + dec_batch) // 8) * 8
    cond = jnp.zeros((rows, d), F32).at[0].set(c_ctx).at[1:1 + dec_batch].set(c)
    mod = _modulation(cond, w_mod[l], b_mod[l])
    mod3 = mod.reshape(rows, 1, 6 * d)

    gidx = jnp.arange(SEG) // DA_DQK
    gm = (gidx[:, None] == gidx[None, :]).astype(BF16) * (1.0 / DA_DQK)
    qkw = jnp.tile(qk_norm_w[l], (1, SEG // DA_DQK))
    rw_t = router_w[l].T
    rw_hi = rw_t.astype(BF16)
    rw_cat = jnp.concatenate([rw_hi, (rw_t - rw_hi.astype(F32)).astype(BF16)], axis=0)
    weights = (norm1_w[l][None], norm2_w[l][None], w_in[l].astype(BF16), b_gate[l][None],
               hgrn_lb_logits.reshape(4, SEG), hgrn_norm_w[l], qkw, gm, diff_lambda[l],
               diff_subln_w[l][None], w_branch_hgrn[l].astype(BF16),
               w_branch_attn[l].astype(BF16), w_out[l].astype(BF16),
               rw_cat)

    routed_ctx, k_new, v_new, s_new = _trunk(
        x_prompt.reshape(batch * seq, d), batch, mod3,
        lambda i, tm: 0, weights, None, None, None, None, group_tokens)
    per_req = dec_seq
    routed_lat, _, _, _ = _trunk(
        x_sample.reshape(dec_batch * dec_seq, d), dec_batch, mod3,
        lambda i, tm: 1 + (i * tm) // per_req, weights,
        cache_k[:, l], cache_v[:, l].reshape(dec_batch, past, SEG),
        state_hgrn[:, l], _rope_tables(dec_seq), group_tokens)
    group_slots = EC_CAPACITY * group_tokens // N_EXPERTS
    experts = (w_exp_gate[l], w_exp_up[l], w_exp_down[l], group_tokens, group_slots)
    yp, = _moe([routed_ctx], mod3, [lambda g: 0], *experts)
    ys, = _moe([routed_lat], mod3, [lambda g: 1 + g], *experts)

    return (yp.reshape(batch, seq, d), ys.reshape(dec_batch, dec_seq, d),
            k_new.reshape(batch, 1, seq, DA_HEADS, 2, DA_DQK),
            v_new.reshape(batch, 1, seq, DA_HEADS, DA_DV),
            s_new.reshape(batch, 1, 2, HG_HEADS, HG_DK, HG_DV))
```

```python
import functools
import math

import jax
import jax.numpy as jnp
from jax import lax
from jax.experimental import pallas as pl
from jax.experimental.pallas import tpu as pltpu
from jax.experimental.pallas import tpu_sc as plsc

F32 = jnp.float32
BF16 = jnp.bfloat16
I32 = jnp.int32

EPS = 1e-6
GRID_W = 64
HG_HEADS = 4
HG_DK = 128
HG_DV = 128
HG_CHUNK_LOG2 = 7
HG_CHUNK = 1 << HG_CHUNK_LOG2
HG_UNROLL = 4
HG_ALL_HEADS_TOKENS = 1024
DA_HEADS = 4
DA_DQK = 64
DA_DV = 128
N_EXPERTS = 16
EC_CAPACITY = 2
ROPE_BASE = 10000.0
SEG = 512
N_SEG = 12
LAM_INIT = 0.8 - 0.6 * math.exp(-0.3 * 0)
LANES = 128
ONES_ROWS = 16
ATT_SPLIT = 2
ATT_TQ = 256
PREMIX_TOKENS = 512
POSTMIX_TOKENS = 1024
COMBINE_TOKENS = 512
COMBINE_EXPERTS = 8
ROUTE_LANES = 2048
EXPERT_ROWS = 1024
Q_SCALE = DA_DQK ** -0.5 * math.log2(math.e)
HG_SAFE_DECAY = 80.0
VMEM_LIMIT = 56 * 1024 * 1024
SC_CORES = 2
SC_SUBCORES = 16
SC_LANES = 16
SC_GATHER_ROWS = 128


def _dot(a, b):
    return jnp.dot(a, b, preferred_element_type=F32)


def _dot_nt(a, b):
    return lax.dot_general(a, b, (((1,), (1,)), ((), ())), preferred_element_type=F32)


def _dot_tn(a, b):
    return lax.dot_general(a, b, (((0,), (0,)), ((), ())), preferred_element_type=F32)


def _split2(x):
    hi = x.astype(BF16)
    lo = (x - hi.astype(F32)).astype(BF16)
    return hi, lo


def _silu(x):
    return x * jax.nn.sigmoid(x)


def _pack_bf16_pairs(x):
    w = x.shape[1] // 2
    bits = pltpu.bitcast(x.astype(BF16).astype(F32), I32)
    return lax.shift_right_logical(bits[:, :w], 16) | bits[:, w:]


def _unpack_bf16_pairs(words):
    lo = pltpu.bitcast(words << 16, F32)
    hi = pltpu.bitcast(words & jnp.int32(-65536), F32)
    return jnp.concatenate([lo, hi], axis=1).astype(BF16)


def _params(sem):
    return pltpu.CompilerParams(dimension_semantics=sem, vmem_limit_bytes=VMEM_LIMIT)


def _mod_kernel(c_ref, w_ref, b_ref, o_ref):
    s_hi, s_lo = _split2(_silu(c_ref[...]))
    w_hi, w_lo = _split2(w_ref[...])
    o_ref[...] = _dot(s_hi, w_hi) + _dot(s_hi, w_lo) + _dot(s_lo, w_hi) + b_ref[...]


def _modulation(cond, w_mod, b_mod):
    rows, d = cond.shape
    n = w_mod.shape[1]
    bn = 512
    return pl.pallas_call(
        _mod_kernel,
        name="modulation", grid=(n // bn,),
        in_specs=[pl.BlockSpec((rows, d), lambda j: (0, 0)),
                  pl.BlockSpec((d, bn), lambda j: (0, j)),
                  pl.BlockSpec((1, bn), lambda j: (0, j))],
        out_specs=pl.BlockSpec((rows, bn), lambda j: (0, j)),
        out_shape=jax.ShapeDtypeStruct((rows, n), F32),
        compiler_params=_params(("arbitrary",)),
    )(cond, w_mod, b_mod.reshape(1, n))


def _group_rms(z, gm_ref, w):
    ms = _dot((z * z).astype(BF16), gm_ref[...])
    return z * lax.rsqrt(ms + EPS) * w


def _rope(x, cos, sin_signed):
    n = x.shape[-1]
    lane = lax.broadcasted_iota(I32, x.shape, 1)
    partner = jnp.where((lane & 16) == 0, pltpu.roll(x, n - 16, 1), pltpu.roll(x, 16, 1))
    return x * cos + partner * sin_signed


def _premix_kernel(*refs, latent):
    if latent:
        (x_ref, mod_ref, n1_ref, win_ref, bg_ref, lbl_ref, qkw_ref, gm_ref, cos_ref, sin_ref,
         q_o, bf_o, bb_o, kff_o, kfb_o, i_o, g_o, dq_o, dk_o, dv_o, gate_o) = refs
    else:
        (x_ref, mod_ref, n1_ref, win_ref, bg_ref, lbl_ref, qkw_ref, gm_ref,
         q_o, bf_o, bb_o, kff_o, kfb_o, i_o, g_o, dq_o, dk_o, dv_o, gate_o) = refs
    d = x_ref.shape[1]
    mod = mod_ref[0]
    sh1, sc1 = mod[:, 0:d], mod[:, d:2 * d]
    x = x_ref[...]
    xn = x * lax.rsqrt(jnp.mean(x * x, axis=-1, keepdims=True) + EPS) * n1_ref[...]
    hb = (xn * (1.0 + sc1) + sh1).astype(BF16)

    def seg(j):
        return _dot(hb, win_ref[:, j * SEG:(j + 1) * SEG])

    def lower_bound(direction):
        l0 = lbl_ref[2 * direction:2 * direction + 1, :]
        l1 = lbl_ref[2 * direction + 1:2 * direction + 2, :]
        mx = jnp.maximum(l0, l1)
        e0, e1 = jnp.exp(l0 - mx), jnp.exp(l1 - mx)
        return e0 / (e0 + e1)

    tm = x.shape[0]
    row = lax.broadcasted_iota(I32, (tm, tm), 0)
    col = lax.broadcasted_iota(I32, (tm, tm), 1)
    same_chunk = (row >> HG_CHUNK_LOG2) == (col >> HG_CHUNK_LOG2)

    def store(o_ref, val):
        val = val.astype(o_ref.dtype)
        if len(o_ref.shape) == 2:
            o_ref[...] = val
        elif len(o_ref.shape) == 3:
            for h in range(o_ref.shape[0]):
                o_ref[h] = val[:, h * LANES:(h + 1) * LANES]
        else:
            for h in range(o_ref.shape[2]):
                for m in range(2):
                    lo = (2 * h + m) * DA_DQK
                    o_ref[0, :, h, m, :] = val[:, lo:lo + DA_DQK]

    store(q_o, _silu(seg(0)))
    for j, b_o, kf_o, order in ((1, bf_o, kff_o, row >= col), (2, bb_o, kfb_o, row <= col)):
        lbd = lower_bound(j - 1)
        f = lbd + (1.0 - lbd) * jax.nn.sigmoid(seg(j))
        tri = (same_chunk & order).astype(BF16)
        hi, lo = _split2(jnp.log(f))
        store(b_o, _dot(tri, hi) + _dot(tri, lo))
        store(kf_o, 1.0 - f)
    store(i_o, seg(3))
    store(g_o, _silu(seg(4)))
    qn = _group_rms(seg(5), gm_ref, qkw_ref[0:1, :]) * Q_SCALE
    kn = _group_rms(seg(6), gm_ref, qkw_ref[1:2, :])
    if latent:
        qn = _rope(qn, cos_ref[...], sin_ref[...])
        kn = _rope(kn, cos_ref[...], sin_ref[...])
    store(dq_o, qn)
    store(dk_o, kn)
    store(dv_o, seg(7))
    for j in range(4):
        z = seg(8 + j) + bg_ref[:, j * SEG:(j + 1) * SEG]
        gate_o[:, j * SEG:(j + 1) * SEG] = jax.nn.sigmoid(z).astype(gate_o.dtype)


def _premix(x, mod3, mod_row, norm1_w, w_in, b_gate, lb_logits, qkw, gm, rope, tm, kv_dtype,
            kv_head_major):
    n, d = x.shape
    latent = rope is not None
    const = lambda i: (0, 0)
    in_specs = [pl.BlockSpec((tm, d), lambda i: (i, 0)),
                pl.BlockSpec((1, 1, mod3.shape[2]), lambda i: (mod_row(i), 0, 0)),
                pl.BlockSpec((1, d), const),
                pl.BlockSpec(w_in.shape, const),
                pl.BlockSpec(b_gate.shape, const),
                pl.BlockSpec(lb_logits.shape, const),
                pl.BlockSpec(qkw.shape, const),
                pl.BlockSpec(gm.shape, const)]
    args = [x, mod3, norm1_w, w_in, b_gate, lb_logits, qkw, gm]
    if latent:
        cos, sin = rope
        nblk = cos.shape[0] // tm
        in_specs += [pl.BlockSpec((tm, SEG), lambda i: (i % nblk, 0))] * 2
        args += [cos, sin]
    heads = SEG // LANES
    head_spec = pl.BlockSpec((heads, tm, LANES), lambda i: (0, i, 0))
    tok_spec = pl.BlockSpec((tm, SEG), lambda i: (i, 0))
    out_dtypes = [BF16, F32, F32, BF16, BF16, BF16, BF16, BF16]
    out_shape = [jax.ShapeDtypeStruct((heads, n, LANES), t) for t in out_dtypes]
    if kv_head_major:
        out_shape += [jax.ShapeDtypeStruct((heads, n, LANES), kv_dtype)] * 2
        kv_specs = [head_spec] * 2
    else:
        k_shape = (n // tm, tm, DA_HEADS, 2, DA_DQK)
        out_shape += [jax.ShapeDtypeStruct(k_shape, kv_dtype),
                      jax.ShapeDtypeStruct((n, SEG), kv_dtype)]
        kv_specs = [pl.BlockSpec((1,) + k_shape[1:], lambda i: (i, 0, 0, 0, 0)), tok_spec]
    out_shape.append(jax.ShapeDtypeStruct((n, 4 * SEG), BF16))
    out_specs = [head_spec] * 8 + kv_specs + [pl.BlockSpec((tm, 4 * SEG), lambda i: (i, 0))]
    return pl.pallas_call(
        functools.partial(_premix_kernel, latent=latent),
        name=f"premix_n{n}", grid=(n // tm,),
        in_specs=in_specs, out_specs=out_specs, out_shape=out_shape,
        compiler_params=_params(("parallel",)),
    )(*args)


def _hgrn_chunk_local(q, b, total, k, v, keep, safe):
    if safe:
        ref = b[HG_CHUNK // 2:HG_CHUNK // 2 + 1, :]
        qa = q * jnp.exp(b - ref)
        kb = k * jnp.exp(ref - b)
        attn = jnp.where(keep, _dot_nt(qa.astype(BF16), kb.astype(BF16)), 0.0)
        qe = qa * jnp.exp(ref)
        kd = kb * jnp.exp(total - ref)
    else:
        qe = q * jnp.exp(b)
        kd = k * jnp.exp(total - b)
        col = lax.broadcasted_iota(I32, (HG_CHUNK, HG_CHUNK), 1)

        def column(s, acc):
            onehot = (lax.broadcasted_iota(I32, (HG_CHUNK, 1), 0) == s).astype(F32)
            bs = jnp.sum(b * onehot, axis=0, keepdims=True)
            ks = jnp.sum(k * onehot, axis=0, keepdims=True)
            w = jnp.sum(q * ks * jnp.exp(jnp.minimum(b - bs, 0.0)), axis=1, keepdims=True)
            return jnp.where(col == s, w, acc)

        attn = lax.fori_loop(0, HG_CHUNK, column, jnp.zeros((HG_CHUNK, HG_CHUNK), F32))
        attn = jnp.where(keep, attn, 0.0)
    vt = v.astype(F32).T.astype(BF16)
    lhs = jnp.concatenate([qe.astype(BF16), attn.astype(BF16)], axis=1)
    return lhs, vt, _dot(vt, kd.astype(BF16)), jnp.exp(total)


def _hgrn_scan_group(chunks, st):
    outs = []
    for lhs, vt, inc, decay in chunks:
        outs.append(_dot_nt(lhs, jnp.concatenate([st.astype(BF16), vt], axis=1)))
        st = st * decay + inc
    return outs, st


def _hgrn_head(hh, head, refs, has_state):
    if has_state:
        (q_ref, bf_ref, bb_ref, kff_ref, kfb_ref, i_ref, g_ref, nw_ref, s0_ref,
         o_ref, sn_ref, of_s, ob_s) = refs
    else:
        (q_ref, bf_ref, bb_ref, kff_ref, kfb_ref, i_ref, g_ref, nw_ref,
         o_ref, sn_ref, of_s, ob_s) = refs
    t_len = q_ref.shape[1]
    nc = t_len // HG_CHUNK
    unroll = min(HG_UNROLL, nc)
    row = lax.broadcasted_iota(I32, (HG_CHUNK, HG_CHUNK), 0)
    col = lax.broadcasted_iota(I32, (HG_CHUNK, HG_CHUNK), 1)
    keep_f, keep_b = row >= col, row <= col

    mid_f = bf_ref[hh, pl.ds(HG_CHUNK // 2, nc, stride=HG_CHUNK), :]
    tot_f = bf_ref[hh, pl.ds(HG_CHUNK - 1, nc, stride=HG_CHUNK), :]
    mid_b = bb_ref[hh, pl.ds(HG_CHUNK // 2, nc, stride=HG_CHUNK), :]
    tot_b = bb_ref[hh, pl.ds(0, nc, stride=HG_CHUNK), :]
    worst = jnp.minimum(jnp.min(jnp.minimum(mid_f, tot_f - mid_f)),
                        jnp.min(jnp.minimum(mid_b, tot_b - mid_b)))

    if has_state:
        st_f0, st_b0 = s0_ref[0, 0, hh].T, s0_ref[0, 1, hh].T
    else:
        st_f0 = st_b0 = jnp.zeros((HG_DV, HG_DK), F32)

    def scan(safe):
        def body(it, carry):
            st_f, st_b = carry
            rows_f, rows_b, loc_f, loc_b = [], [], [], []
            for u in range(unroll):
                c = it * unroll + u
                sf = pl.multiple_of(c * HG_CHUNK, HG_CHUNK)
                sb = pl.multiple_of((nc - 1 - c) * HG_CHUNK, HG_CHUNK)
                rf, rb = pl.ds(sf, HG_CHUNK), pl.ds(sb, HG_CHUNK)
                rows_f.append(rf)
                rows_b.append(rb)
                loc_f.append(_hgrn_chunk_local(
                    q_ref[hh, rf, :].astype(F32), bf_ref[hh, rf, :],
                    bf_ref[hh, pl.ds(sf + HG_CHUNK - 1, 1), :],
                    kff_ref[hh, rf, :].astype(F32), i_ref[hh, rf, :], keep_f, safe))
                loc_b.append(_hgrn_chunk_local(
                    q_ref[hh, rb, :].astype(F32), bb_ref[hh, rb, :], bb_ref[hh, pl.ds(sb, 1), :],
                    kfb_ref[hh, rb, :].astype(F32), i_ref[hh, rb, :], keep_b, safe))
            outs_f, st_f = _hgrn_scan_group(loc_f, st_f)
            outs_b, st_b = _hgrn_scan_group(loc_b, st_b)
            for rf, rb, o_f, o_b in zip(rows_f, rows_b, outs_f, outs_b):
                of_s[rf, :] = o_f
                ob_s[rb, :] = o_b
            return st_f, st_b
        return lax.fori_loop(0, nc // unroll, body, (st_f0, st_b0))

    st_f, st_b = lax.cond(worst >= -HG_SAFE_DECAY, lambda: scan(True), lambda: scan(False))
    sn_ref[0, 0, hh] = st_f.T
    sn_ref[0, 1, hh] = st_b.T
    o = of_s[...] + ob_s[...]
    nw = nw_ref[pl.ds(head, 1), :]
    on = o * lax.rsqrt(jnp.mean(o * o, axis=-1, keepdims=True) + EPS) * nw
    o_ref[hh] = (on * g_ref[hh].astype(F32)).astype(o_ref.dtype)


def _hgrn_kernel(*refs, has_state):
    per_step = refs[0].shape[0]
    for hh in range(per_step):
        _hgrn_head(hh, pl.program_id(1) * per_step + hh, refs, has_state)


def _hgrn(q, bf, bb, kff, kfb, iv, g, norm_w, s0, batch):
    heads, n, _ = q.shape
    t_len = n // batch
    per_step = heads if t_len <= HG_ALL_HEADS_TOKENS else 1
    blk = pl.BlockSpec((per_step, t_len, HG_DK), lambda b, h: (h, b, 0))
    st_blk = pl.BlockSpec((1, 2, per_step, HG_DK, HG_DV), lambda b, h: (b, 0, h, 0, 0))
    args = [q, bf, bb, kff, kfb, iv, g, norm_w]
    in_specs = [blk] * 7 + [pl.BlockSpec(norm_w.shape, lambda b, h: (0, 0))]
    if s0 is not None:
        args.append(s0)
        in_specs.append(st_blk)
    o, s_new = pl.pallas_call(
        functools.partial(_hgrn_kernel, has_state=s0 is not None),
        name=f"hgrn_n{n}", grid=(batch, heads // per_step),
        in_specs=in_specs,
        out_specs=[blk, st_blk],
        out_shape=[jax.ShapeDtypeStruct((heads, n, HG_DV), BF16),
                   jax.ShapeDtypeStruct((batch, 2, HG_HEADS, HG_DK, HG_DV), F32)],
        scratch_shapes=[pltpu.VMEM((t_len, HG_DV), F32), pltpu.VMEM((t_len, HG_DV), F32)],
        compiler_params=_params(("parallel", "parallel")),
    )(*args)
    return o, s_new


def _attn_kernel(*refs, has_ctx):
    if has_ctx:
        q_ref, k_ref, v_ref, ck_ref, cv_ref, lam_ref, sw_ref, o_ref, k_s, vt_s = refs
    else:
        q_ref, k_ref, v_ref, lam_ref, sw_ref, o_ref, k_s, vt_s = refs
    t_own = k_ref.shape[1]

    def load_keys(ref):
        if len(ref.shape) == 3:
            return ref[0]
        return jnp.concatenate([ref[0, :, 0, 0, :], ref[0, :, 0, 1, :]], axis=1)

    @pl.when(pl.program_id(2) == 0)
    def _():
        k_s[0:t_own, :] = load_keys(k_ref).astype(BF16)
        vt_s[0:DA_DV, 0:t_own] = v_ref[0].astype(F32).T.astype(BF16)
        if has_ctx:
            k_s[t_own:, :] = load_keys(ck_ref).astype(BF16)
            vt_s[0:DA_DV, t_own:] = cv_ref[0].T.astype(BF16)
        vt_s[DA_DV:, :] = jnp.ones((ONES_ROWS, vt_s.shape[1]), BF16)

    lv = lam_ref[...]
    lam = (jnp.exp(jnp.sum(lv[0:1] * lv[1:2], keepdims=True))
           - jnp.exp(jnp.sum(lv[2:3] * lv[3:4], keepdims=True)) + LAM_INIT)
    tq = q_ref.shape[1] // ATT_SPLIT
    dim = lax.broadcasted_iota(I32, (2 * DA_DQK, tq), 0)

    def scores(i):
        qt = q_ref[0, i * tq:(i + 1) * tq, :].astype(F32).T
        q_both = jnp.concatenate([jnp.where(dim < DA_DQK, qt, 0.0),
                                  jnp.where(dim >= DA_DQK, qt, 0.0)], axis=1).astype(BF16)
        return _dot(k_s[...], q_both)

    def finish(i, st):
        pt = jnp.exp2(st - jnp.max(st, axis=0, keepdims=True)).astype(BF16)
        r = _dot(vt_s[...], pt)
        r = r[0:DA_DV] * (1.0 / r[DA_DV:DA_DV + 1])
        o = (r[:, 0:tq] - lam * r[:, tq:2 * tq]).T
        on = o * lax.rsqrt(jnp.mean(o * o, axis=-1, keepdims=True) + EPS) * sw_ref[...]
        o_ref[0, i * tq:(i + 1) * tq, :] = (on * (1.0 - LAM_INIT)).astype(o_ref.dtype)

    sts = [scores(i) for i in range(ATT_SPLIT)]
    for i in range(ATT_SPLIT):
        finish(i, sts[i])


def _attention(q, k, v, ctx_k, ctx_v, lam_p, subln_w, batch, tq):
    heads, n, _ = q.shape
    t_len = n // batch
    nq = t_len // tq
    q_blk = pl.BlockSpec((1, tq, DA_DV), lambda b, h, i: (h, b * nq + i, 0))
    const = lambda b, h, i: (0, 0)
    if k.ndim == 3:
        kv_blk = pl.BlockSpec((1, t_len, DA_DV), lambda b, h, i: (h, b, 0))
        in_specs = [q_blk, kv_blk, kv_blk]
        args = [q, k, v]
    else:
        k = k.reshape(batch, t_len, DA_HEADS, 2, DA_DQK)
        in_specs = [q_blk, pl.BlockSpec((1, t_len, 1, 2, DA_DQK), lambda b, h, i: (b, 0, h, 0, 0)),
                    pl.BlockSpec((1, t_len, DA_DV), lambda b, h, i: (b, 0, h))]
        args = [q, k, v.reshape(batch, t_len, SEG)]
    n_keys = t_len
    if ctx_k is not None:
        past = ctx_k.shape[1]
        n_keys += past
        ck_blk = pl.BlockSpec((1, past, 1, 2, DA_DQK), lambda b, h, i: (b, 0, h, 0, 0))
        cv_blk = pl.BlockSpec((1, past, DA_DV), lambda b, h, i: (b, 0, h))
        args += [ctx_k, ctx_v]
        in_specs += [ck_blk, cv_blk]
    args += [lam_p, subln_w]
    in_specs += [pl.BlockSpec(lam_p.shape, const), pl.BlockSpec(subln_w.shape, const)]
    o = pl.pallas_call(
        functools.partial(_attn_kernel, has_ctx=ctx_k is not None),
        name=f"attn_n{n}", grid=(batch, DA_HEADS, nq),
        in_specs=in_specs, out_specs=q_blk,
        out_shape=jax.ShapeDtypeStruct((heads, n, DA_DV), BF16),
        scratch_shapes=[pltpu.VMEM((n_keys, 2 * DA_DQK), BF16),
                        pltpu.VMEM((DA_DV + ONES_ROWS, n_keys), BF16)],
        compiler_params=_params(("parallel", "parallel", "arbitrary")),
    )(*args)
    return o


def _postmix_kernel(oh_ref, oa_ref, gate_ref, x_ref, mod_ref, wbh_ref, wba_ref, wout_ref,
                    n2_ref, rw_ref, x1_o, h2_o, afft_o, afftok_o):
    d = x_ref.shape[1]
    mod = mod_ref[0]
    g1, sh2, sc2 = mod[:, 2 * d:3 * d], mod[:, 3 * d:4 * d], mod[:, 4 * d:5 * d]
    g_h = gate_ref[:, 0:d].astype(F32)
    g_a = gate_ref[:, d:2 * d].astype(F32)

    def heads_on_lanes(ref):
        return jnp.concatenate([ref[h] for h in range(ref.shape[0])], axis=1)

    merged = (g_h * _dot(heads_on_lanes(oh_ref), wbh_ref[...])
              + g_a * _dot(heads_on_lanes(oa_ref), wba_ref[...]))
    x1 = x_ref[...] + g1 * _dot(merged.astype(BF16), wout_ref[...])
    x1_o[...] = x1
    xn = x1 * lax.rsqrt(jnp.mean(x1 * x1, axis=-1, keepdims=True) + EPS) * n2_ref[...]
    h2 = xn * (1.0 + sc2) + sh2
    h2_o[...] = _pack_bf16_pairs(h2)
    h_hi, h_lo = _split2(h2)
    rw = rw_ref[...]
    t1 = _dot_nt(rw, h_hi)
    t2 = _dot_nt(rw, h_lo)
    e = N_EXPERTS
    logits = t1[0:e] + t1[e:2 * e] + t2[0:e]
    mx = jnp.max(logits, axis=0, keepdims=True)
    p = jnp.exp(logits - mx)
    aff = p / jnp.sum(p, axis=0, keepdims=True)
    afft_o[...] = aff
    pad = jnp.zeros((LANES - e, aff.shape[1]), F32)
    afftok_o[...] = jnp.concatenate([aff, pad], axis=0).T


def _postmix(o_h, o_a, gates, x, mod3, mod_row, w_bh, w_ba, w_out, norm2_w, rw_cat, tm):
    n, d = x.shape
    const = lambda i: (0, 0)
    row = lambda i: (i, 0)
    return pl.pallas_call(
        _postmix_kernel,
        name=f"postmix_n{n}", grid=(n // tm,),
        in_specs=[pl.BlockSpec((o_h.shape[0], tm, LANES), lambda i: (0, i, 0)),
                  pl.BlockSpec((o_a.shape[0], tm, LANES), lambda i: (0, i, 0)),
                  pl.BlockSpec((tm, 4 * SEG), row), pl.BlockSpec((tm, d), row),
                  pl.BlockSpec((1, 1, mod3.shape[2]), lambda i: (mod_row(i), 0, 0)),
                  pl.BlockSpec(w_bh.shape, const), pl.BlockSpec(w_ba.shape, const),
                  pl.BlockSpec(w_out.shape, const), pl.BlockSpec((1, d), const),
                  pl.BlockSpec(rw_cat.shape, const)],
        out_specs=[pl.BlockSpec((tm, d), row), pl.BlockSpec((tm, d // 2), row),
                   pl.BlockSpec((N_EXPERTS, tm), lambda i: (0, i)),
                   pl.BlockSpec((tm, LANES), row)],
        out_shape=[jax.ShapeDtypeStruct((n, d), F32), jax.ShapeDtypeStruct((n, d // 2), I32),
                   jax.ShapeDtypeStruct((N_EXPERTS, n), F32),
                   jax.ShapeDtypeStruct((n, LANES), F32)],
        compiler_params=_params(("parallel",)),
    )(o_h, o_a, gates, x, mod3, w_bh, w_ba, w_out, norm2_w, rw_cat)


def _lane_cumsum_exclusive(x, blk):
    e, t = x.shape
    r = lax.broadcasted_iota(I32, (blk, blk), 0)
    c = lax.broadcasted_iota(I32, (blk, blk), 1)
    upper = (r < c).astype(BF16)
    carry = jnp.zeros((e, 1), F32)
    parts = []
    for j in range(t // blk):
        xb = x[:, j * blk:(j + 1) * blk]
        parts.append(_dot(xb.astype(BF16), upper) + carry)
        carry = carry + jnp.sum(xb, axis=1, keepdims=True)
    return parts[0] if len(parts) == 1 else jnp.concatenate(parts, axis=1)


def _route_kernel(aff_ref, pos_ref, *, cap, req_per_group, t_len):
    n_req = aff_ref.shape[1] // t_len
    bits = [pltpu.bitcast(aff_ref[:, r * t_len:(r + 1) * t_len], I32) for r in range(n_req)]

    def count(mask):
        return jnp.sum(mask.astype(F32), axis=1, keepdims=True)

    def step(i, ths):
        bit = jnp.int32(1) << (30 - i)
        return tuple(jnp.where(count(b >= (th | bit)) >= cap, th | bit, th)
                     for b, th in zip(bits, ths))

    zero = jnp.zeros((aff_ref.shape[0], 1), I32)
    ths = lax.fori_loop(0, 31, step, (zero,) * n_req)
    blk = min(t_len, 256)
    for r, (b, th) in enumerate(zip(bits, ths)):
        gt = b > th
        eq = (b == th).astype(F32)
        need = cap - count(gt)
        tie_rank = _lane_cumsum_exclusive(eq, blk)
        sel = jnp.where(gt, 1.0, jnp.where(tie_rank < need, eq, 0.0))
        slot = _lane_cumsum_exclusive(sel, blk)
        offset = ((pl.program_id(0) * n_req + r) % req_per_group) * cap
        pos_ref[:, r * t_len:(r + 1) * t_len] = jnp.where(sel > 0.0, slot.astype(I32) + offset, -1)


def _route(afft, batch, cap, req_per_group):
    e, n = afft.shape
    t_len = n // batch
    per_step = max(1, min(batch, ROUTE_LANES // t_len))
    assert batch % per_step == 0
    blk = pl.BlockSpec((e, per_step * t_len), lambda b: (0, b))
    return pl.pallas_call(
        functools.partial(_route_kernel, cap=cap, req_per_group=req_per_group, t_len=t_len),
        name=f"route_n{n}", grid=(batch // per_step,), in_specs=[blk], out_specs=blk,
        out_shape=jax.ShapeDtypeStruct((e, n), I32),
        compiler_params=_params(("parallel",)),
    )(afft)


def _one_hot_rows(pos_row, n_slots):
    slot = lax.broadcasted_iota(I32, (n_slots, pos_row.shape[1]), 0)
    return (slot == pos_row).astype(BF16)


def _dispatch(parts, group_tokens, group_slots):
    n_exp = parts[0][0].shape[0]
    width = parts[0][1].shape[1]
    part_groups = [pos.shape[1] // group_tokens for pos, _ in parts]
    workers = SC_CORES * SC_SUBCORES
    chunks = group_slots // SC_GATHER_ROWS
    assert all(g * n_exp % workers == 0 for g in part_groups)
    assert group_slots % SC_GATHER_ROWS == 0 and group_tokens % SC_LANES == 0
    assert n_exp & (n_exp - 1) == 0
    exp_shift = n_exp.bit_length() - 1
    row_shift = SC_GATHER_ROWS.bit_length() - 1

    def body(*refs):
        ins, (out_hbm, pos_v, idx_v, rows_v, sem) = refs[:2 * len(parts)], refs[2 * len(parts):]
        wid = lax.axis_index("s") * SC_CORES + lax.axis_index("c")
        lane = lax.iota(I32, SC_LANES)
        group_base = 0
        for part, groups in enumerate(part_groups):
            pos_hbm, h_hbm = ins[2 * part], ins[2 * part + 1]
            for k in range(groups * n_exp // workers):
                pair = wid + workers * k
                g = pair >> exp_shift
                e = pair & (n_exp - 1)
                pltpu.sync_copy(pos_hbm.at[e, pl.ds(g * group_tokens, group_tokens)], pos_v)

                @pl.loop(0, group_tokens // SC_LANES)
                def _(i):
                    p = pos_v[pl.ds(i * SC_LANES, SC_LANES)]
                    tok = g * group_tokens + i * SC_LANES + lane
                    slot = jnp.maximum(p, 0)
                    plsc.store_scatter(idx_v, [slot >> row_shift, slot & (SC_GATHER_ROWS - 1)],
                                       tok, mask=p >= 0)

                for c in range(chunks):
                    row0 = (group_base + g) * group_slots + c * SC_GATHER_ROWS
                    pltpu.async_copy(h_hbm.at[idx_v.at[c]], rows_v, sem).wait()
                    pltpu.sync_copy(rows_v, out_hbm.at[e, pl.ds(row0, SC_GATHER_ROWS)])
            group_base += groups

    mesh = plsc.VectorSubcoreMesh(core_axis_name="c", subcore_axis_name="s",
                                  num_cores=SC_CORES, num_subcores=SC_SUBCORES)
    return pl.kernel(
        body,
        out_type=jax.ShapeDtypeStruct((n_exp, sum(part_groups) * group_slots, width), I32),
        mesh=mesh,
        scratch_types=[pltpu.VMEM((group_tokens,), I32), pltpu.VMEM((chunks, SC_GATHER_ROWS), I32),
                       pltpu.VMEM((SC_GATHER_ROWS, width), I32), pltpu.SemaphoreType.DMA],
        compiler_params=pltpu.CompilerParams(needs_layout_passes=False),
        name="dispatch",
    )(*[a for part in parts for a in part])


def _expert_kernel(xg_ref, wg_ref, wu_ref, wd_ref, y_ref, wg_s, wu_s, wd_s):
    @pl.when(pl.program_id(1) == 0)
    def _():
        wg_s[...] = wg_ref[0].astype(BF16)
        wu_s[...] = wu_ref[0].astype(BF16)
        wd_s[...] = wd_ref[0].astype(BF16)

    xg = _unpack_bf16_pairs(xg_ref[0])
    a = _silu(_dot(xg, wg_s[...])) * _dot(xg, wu_s[...])
    y_ref[0] = _dot(a.astype(BF16), wd_s[...]).astype(y_ref.dtype)


def _experts(xg, w_gate, w_up, w_down, ts):
    e, s, half = xg.shape
    d, f = w_gate.shape[1:]
    assert d == 2 * half
    x_blk = pl.BlockSpec((1, ts, d), lambda x, i: (x, i, 0))
    return pl.pallas_call(
        _expert_kernel,
        name=f"experts_s{s}", grid=(e, s // ts),
        in_specs=[pl.BlockSpec((1, ts, half), lambda x, i: (x, i, 0)),
                  pl.BlockSpec((1, d, f), lambda x, i: (x, 0, 0)),
                  pl.BlockSpec((1, d, f), lambda x, i: (x, 0, 0)),
                  pl.BlockSpec((1, f, d), lambda x, i: (x, 0, 0))],
        out_specs=x_blk,
        out_shape=jax.ShapeDtypeStruct((e, s, d), BF16),
        scratch_shapes=[pltpu.VMEM((d, f), BF16), pltpu.VMEM((d, f), BF16),
                        pltpu.VMEM((f, d), BF16)],
        compiler_params=_params(("parallel", "arbitrary")),
    )(xg, w_gate, w_up, w_down)


def _combine_kernel(pos_ref, afftok_ref, y_ref, x1_ref, mod_ref, o_ref):
    step = pl.program_id(2)
    d = x1_ref.shape[1]
    per_step = y_ref.shape[0]

    @pl.when(step == 0)
    def _():
        o_ref[...] = jnp.zeros_like(o_ref)

    lane = lax.broadcasted_iota(I32, afftok_ref.shape, 1)
    acc = None
    for k in range(per_step):
        e = step * per_step + k
        p = _one_hot_rows(pos_ref[pl.ds(e, 1), :], y_ref.shape[1])
        gate = jnp.sum(jnp.where(lane == e, afftok_ref[...], 0.0), axis=1, keepdims=True)
        term = gate * _dot_tn(p, y_ref[k])
        acc = term if acc is None else acc + term
    o_ref[...] += acc

    @pl.when(step == pl.num_programs(2) - 1)
    def _():
        g2 = mod_ref[0][:, 5 * d:6 * d]
        o_ref[...] = x1_ref[...] + g2 * o_ref[...]


def _combine(pos, afftok, y, x1, mod3, mod_row, group_tokens, group_slots, group_base, tc):
    e, n = pos.shape
    d = x1.shape[1]
    groups = n // group_tokens
    per = group_tokens // tc
    tok = lambda g, j, x: (g * per + j, 0)
    return pl.pallas_call(
        _combine_kernel,
        name=f"combine_n{n}", grid=(groups, per, e // COMBINE_EXPERTS),
        in_specs=[pl.BlockSpec((e, tc), lambda g, j, x: (0, g * per + j)),
                  pl.BlockSpec((tc, LANES), tok),
                  pl.BlockSpec((COMBINE_EXPERTS, group_slots, d),
                               lambda g, j, x: (x, group_base + g, 0)),
                  pl.BlockSpec((tc, d), tok),
                  pl.BlockSpec((1, 1, mod3.shape[2]), lambda g, j, x: (mod_row(g), 0, 0))],
        out_specs=pl.BlockSpec((tc, d), tok),
        out_shape=jax.ShapeDtypeStruct((n, d), F32),
        compiler_params=_params(("parallel", "parallel", "arbitrary")),
    )(pos, afftok, y, x1, mod3)


def _rope_tables(t_len):
    n_freq = DA_DQK // 4
    inv = ROPE_BASE ** (-jnp.arange(n_freq, dtype=F32) / n_freq)
    t = jnp.arange(t_len)
    pos = jnp.stack([(t // GRID_W).astype(F32), (t % GRID_W).astype(F32)], axis=1)
    ang = pos[:, :, None, None] * inv[None, None, None, :]
    ang = jnp.broadcast_to(ang, (t_len, 2, 2, n_freq))
    sign = jnp.array([-1.0, 1.0], F32)[None, None, :, None]
    cos = jnp.cos(ang).reshape(t_len, DA_DQK)
    sin = (jnp.sin(ang) * sign).reshape(t_len, DA_DQK)
    reps = SEG // DA_DQK
    return jnp.tile(cos, (1, reps)), jnp.tile(sin, (1, reps))


def _trunk(x, batch, mod3, mod_row_tok, weights, ctx_k, ctx_v, s0, rope, group_tokens):
    (norm1_w, norm2_w, w_in, b_gate, lb_logits, hgrn_norm_w, qkw, gm, lam_p, subln_w,
     w_bh, w_ba, w_out, rw_cat) = weights
    n, d = x.shape
    t_len = n // batch
    latent = rope is not None
    (q_h, bf, bb, kff, kfb, i_h, g_h, dq, dk, dv, gates) = _premix(
        x, mod3, functools.partial(mod_row_tok, tm=PREMIX_TOKENS), norm1_w, w_in, b_gate, lb_logits,
        qkw, gm, rope, PREMIX_TOKENS, BF16 if latent else F32, latent)
    o_h, s_new = _hgrn(q_h, bf, bb, kff, kfb, i_h, g_h, hgrn_norm_w, s0, batch)
    o_a = _attention(dq, dk, dv, ctx_k, ctx_v, lam_p, subln_w, batch, min(t_len, ATT_TQ * ATT_SPLIT))
    x1, h2, afft, afftok = _postmix(
        o_h, o_a, gates, x, mod3, functools.partial(mod_row_tok, tm=POSTMIX_TOKENS),
        w_bh, w_ba, w_out, norm2_w, rw_cat, POSTMIX_TOKENS)
    cap = EC_CAPACITY * t_len // N_EXPERTS
    pos = _route(afft, batch, cap, group_tokens // t_len)
    return (pos, h2, afftok, x1), dk, dv, s_new


def _moe(routed, mod3, mod_row_grps, w_eg, w_eu, w_ed, group_tokens, group_slots):
    xg = _dispatch([(pos, h2p) for pos, h2p, _, _ in routed], group_tokens, group_slots)
    y = _experts(xg, w_eg, w_eu, w_ed, min(xg.shape[1], EXPERT_ROWS))
    outs, group_base = [], 0
    for (pos, _, afftok, x1), mod_row in zip(routed, mod_row_grps):
        outs.append(_combine(pos, afftok, y, x1, mod3, mod_row, group_tokens, group_slots,
                             group_base, COMBINE_TOKENS))
        group_base += pos.shape[1] // group_tokens
    return outs


def kernel(x_prompt, x_sample, cache_k, cache_v, state_hgrn, c, c_ctx, norm1_w, norm2_w, w_mod,
           b_mod, w_in, b_gate, hgrn_lb_logits, hgrn_norm_w, qk_norm_w, diff_lambda, diff_subln_w,
           w_branch_hgrn, w_branch_attn, w_out, router_w, w_exp_gate, w_exp_up, w_exp_down):
    batch, seq, d = x_prompt.shape
    dec_batch, dec_seq, _ = x_sample.shape
    past = cache_k.shape[2]
    depth = w_in.shape[0]
    assert depth == 1
    group_tokens = dec_seq
    assert group_tokens % seq == 0 and (batch * seq) % group_tokens == 0
    l = 0

    rows = -(-(1 + dec_batch) // 8) * 8
    cond = jnp.zeros((rows, d), F32).at[0].set(c_ctx).at[1:1 + dec_batch].set(c)
    mod = _modulation(cond, w_mod[l], b_mod[l])
    mod3 = mod.reshape(rows, 1, 6 * d)

    gidx = jnp.arange(SEG) // DA_DQK
    gm = (gidx[:, None] == gidx[None, :]).astype(BF16) * (1.0 / DA_DQK)
    qkw = jnp.tile(qk_norm_w[l], (1, SEG // DA_DQK))
    rw_t = router_w[l].T
    rw_hi = rw_t.astype(BF16)
    rw_cat = jnp.concatenate([rw_hi, (rw_t - rw_hi.astype(F32)).astype(BF16)], axis=0)
    weights = (norm1_w[l][None], norm2_w[l][None], w_in[l].astype(BF16), b_gate[l][None],
               hgrn_lb_logits.reshape(4, SEG), hgrn_norm_w[l], qkw, gm, diff_lambda[l],
               diff_subln_w[l][None], w_branch_hgrn[l].astype(BF16),
               w_branch_attn[l].astype(BF16), w_out[l].astype(BF16),
               rw_cat)

    routed_ctx, k_new, v_new, s_new = _trunk(
        x_prompt.reshape(batch * seq, d), batch, mod3,
        lambda i, tm: 0, weights, None, None, None, None, group_tokens)
    per_req = dec_seq
    routed_lat, _, _, _ = _trunk(
        x_sample.reshape(dec_batch * dec_seq, d), dec_batch, mod3,
        lambda i, tm: 1 + (i * tm) // per_req, weights,
        cache_k[:, l], cache_v[:, l].reshape(dec_batch, past, SEG),
        state_hgrn[:, l], _rope_tables(dec_seq), group_tokens)
    group_slots = EC_CAPACITY * group_tokens // N_EXPERTS
    experts = (w_exp_gate[l], w_exp_up[l], w_exp_down[l], group_tokens, group_slots)
    yp, = _moe([routed_ctx], mod3, [lambda g: 0], *experts)
    ys, = _moe([routed_lat], mod3, [lambda g: 1 + g], *experts)

    return (yp.reshape(batch, seq, d), ys.reshape(dec_batch, dec_seq, d),
            k_new.reshape(batch, 1, seq, DA_HEADS, 2, DA_DQK),
            v_new.reshape(batch, 1, seq, DA_HEADS, DA_DV),
            s_new.reshape(batch, 1, 2, HG_HEADS, HG_DK, HG_DV))
```

```python
import functools
import math

import jax
import jax.numpy as jnp
from jax import lax
from jax.experimental import pallas as pl
from jax.experimental.pallas import tpu as pltpu
from jax.experimental.pallas import tpu_sc as plsc

F32 = jnp.float32
BF16 = jnp.bfloat16
I32 = jnp.int32

EPS = 1e-6
GRID_W = 64
HG_HEADS = 4
HG_DK = 128
HG_DV = 128
HG_CHUNK_LOG2 = 7
HG_CHUNK = 1 << HG_CHUNK_LOG2
HG_UNROLL = 4
HG_ALL_HEADS_TOKENS = 1024
DA_HEADS = 4
DA_DQK = 64
DA_DV = 128
N_EXPERTS = 16
EC_CAPACITY = 2
ROPE_BASE = 10000.0
SEG = 512
N_SEG = 12
LAM_INIT = 0.8 - 0.6 * math.exp(-0.3 * 0)
LANES = 128
ONES_ROWS = 16
ATT_SPLIT = 2
ATT_TQ = 256
PREMIX_TOKENS = 512
POSTMIX_TOKENS = 1024
COMBINE_TOKENS = 512
COMBINE_EXPERTS = 8
ROUTE_LANES = 2048
EXPERT_ROWS = 1024
Q_SCALE = DA_DQK ** -0.5 * math.log2(math.e)
HG_SAFE_DECAY = 80.0
VMEM_LIMIT = 56 * 1024 * 1024
SC_CORES = 2
SC_SUBCORES = 16
SC_LANES = 16
SC_GATHER_ROWS = 128


def _dot(a, b):
    return jnp.dot(a, b, preferred_element_type=F32)


def _dot_nt(a, b):
    return lax.dot_general(a, b, (((1,), (1,)), ((), ())), preferred_element_type=F32)


def _dot_tn(a, b):
    return lax.dot_general(a, b, (((0,), (0,)), ((), ())), preferred_element_type=F32)


def _split2(x):
    hi = x.astype(BF16)
    lo = (x - hi.astype(F32)).astype(BF16)
    return hi, lo


def _silu(x):
    return x * jax.nn.sigmoid(x)


def _pack_bf16_pairs(x):
    w = x.shape[1] // 2
    bits = pltpu.bitcast(x.astype(BF16).astype(F32), I32)
    return lax.shift_right_logical(bits[:, :w], 16) | bits[:, w:]


def _unpack_bf16_pairs(words):
    lo = pltpu.bitcast(words << 16, F32)
    hi = pltpu.bitcast(words & jnp.int32(-65536), F32)
    return jnp.concatenate([lo, hi], axis=1).astype(BF16)


def _params(sem):
    return pltpu.CompilerParams(dimension_semantics=sem, vmem_limit_bytes=VMEM_LIMIT)


def _mod_kernel(c_ref, w_ref, b_ref, o_ref):
    s_hi, s_lo = _split2(_silu(c_ref[...]))
    w_hi, w_lo = _split2(w_ref[...])
    o_ref[...] = _dot(s_hi, w_hi) + _dot(s_hi, w_lo) + _dot(s_lo, w_hi) + b_ref[...]


def _modulation(cond, w_mod, b_mod):
    rows, d = cond.shape
    n = w_mod.shape[1]
    bn = 512
    return pl.pallas_call(
        _mod_kernel,
        name="modulation", grid=(n // bn,),
        in_specs=[pl.BlockSpec((rows, d), lambda j: (0, 0)),
                  pl.BlockSpec((d, bn), lambda j: (0, j)),
                  pl.BlockSpec((1, bn), lambda j: (0, j))],
        out_specs=pl.BlockSpec((rows, bn), lambda j: (0, j)),
        out_shape=jax.ShapeDtypeStruct((rows, n), F32),
        compiler_params=_params(("arbitrary",)),
    )(cond, w_mod, b_mod.reshape(1, n))


def _group_rms(z, gm_ref, w):
    ms = _dot((z * z).astype(BF16), gm_ref[...])
    return z * lax.rsqrt(ms + EPS) * w


def _rope(x, cos, sin_signed):
    n = x.shape[-1]
    lane = lax.broadcasted_iota(I32, x.shape, 1)
    partner = jnp.where((lane & 16) == 0, pltpu.roll(x, n - 16, 1), pltpu.roll(x, 16, 1))
    return x * cos + partner * sin_signed


def _premix_kernel(*refs, latent):
    if latent:
        (x_ref, mod_ref, n1_ref, win_ref, bg_ref, lbl_ref, qkw_ref, gm_ref, cos_ref, sin_ref,
         q_o, bf_o, bb_o, kff_o, kfb_o, i_o, g_o, dq_o, dk_o, dv_o, gate_o) = refs
    else:
        (x_ref, mod_ref, n1_ref, win_ref, bg_ref, lbl_ref, qkw_ref, gm_ref,
         q_o, bf_o, bb_o, kff_o, kfb_o, i_o, g_o, dq_o, dk_o, dv_o, gate_o) = refs
    d = x_ref.shape[1]
    mod = mod_ref[0]
    sh1, sc1 = mod[:, 0:d], mod[:, d:2 * d]
    x = x_ref[...]
    xn = x * lax.rsqrt(jnp.mean(x * x, axis=-1, keepdims=True) + EPS) * n1_ref[...]
    hb = (xn * (1.0 + sc1) + sh1).astype(BF16)

    def seg(j):
        return _dot(hb, win_ref[:, j * SEG:(j + 1) * SEG])

    def lower_bound(direction):
        l0 = lbl_ref[2 * direction:2 * direction + 1, :]
        l1 = lbl_ref[2 * direction + 1:2 * direction + 2, :]
        mx = jnp.maximum(l0, l1)
        e0, e1 = jnp.exp(l0 - mx), jnp.exp(l1 - mx)
        return e0 / (e0 + e1)

    tm = x.shape[0]
    row = lax.broadcasted_iota(I32, (tm, tm), 0)
    col = lax.broadcasted_iota(I32, (tm, tm), 1)
    same_chunk = (row >> HG_CHUNK_LOG2) == (col >> HG_CHUNK_LOG2)

    def store(o_ref, val):
        val = val.astype(o_ref.dtype)
        if len(o_ref.shape) == 2:
            o_ref[...] = val
        elif len(o_ref.shape) == 3:
            for h in range(o_ref.shape[0]):
                o_ref[h] = val[:, h * LANES:(h + 1) * LANES]
        else:
            for h in range(o_ref.shape[2]):
                for m in range(2):
                    lo = (2 * h + m) * DA_DQK
                    o_ref[0, :, h, m, :] = val[:, lo:lo + DA_DQK]

    store(q_o, _silu(seg(0)))
    for j, b_o, kf_o, order in ((1, bf_o, kff_o, row >= col), (2, bb_o, kfb_o, row <= col)):
        lbd = lower_bound(j - 1)
        f = lbd + (1.0 - lbd) * jax.nn.sigmoid(seg(j))
        tri = (same_chunk & order).astype(BF16)
        hi, lo = _split2(jnp.log(f))
        store(b_o, _dot(tri, hi) + _dot(tri, lo))
        store(kf_o, 1.0 - f)
    store(i_o, seg(3))
    store(g_o, _silu(seg(4)))
    qn = _group_rms(seg(5), gm_ref, qkw_ref[0:1, :]) * Q_SCALE
    kn = _group_rms(seg(6), gm_ref, qkw_ref[1:2, :])
    if latent:
        qn = _rope(qn, cos_ref[...], sin_ref[...])
        kn = _rope(kn, cos_ref[...], sin_ref[...])
    store(dq_o, qn)
    store(dk_o, kn)
    store(dv_o, seg(7))
    for j in range(4):
        z = seg(8 + j) + bg_ref[:, j * SEG:(j + 1) * SEG]
        gate_o[:, j * SEG:(j + 1) * SEG] = jax.nn.sigmoid(z).astype(gate_o.dtype)


def _premix(x, mod3, mod_row, norm1_w, w_in, b_gate, lb_logits, qkw, gm, rope, tm, kv_dtype,
            kv_head_major):
    n, d = x.shape
    latent = rope is not None
    const = lambda i: (0, 0)
    in_specs = [pl.BlockSpec((tm, d), lambda i: (i, 0)),
                pl.BlockSpec((1, 1, mod3.shape[2]), lambda i: (mod_row(i), 0, 0)),
                pl.BlockSpec((1, d), const),
                pl.BlockSpec(w_in.shape, const),
                pl.BlockSpec(b_gate.shape, const),
                pl.BlockSpec(lb_logits.shape, const),
                pl.BlockSpec(qkw.shape, const),
                pl.BlockSpec(gm.shape, const)]
    args = [x, mod3, norm1_w, w_in, b_gate, lb_logits, qkw, gm]
    if latent:
        cos, sin = rope
        nblk = cos.shape[0] // tm
        in_specs += [pl.BlockSpec((tm, SEG), lambda i: (i % nblk, 0))] * 2
        args += [cos, sin]
    heads = SEG // LANES
    head_spec = pl.BlockSpec((heads, tm, LANES), lambda i: (0, i, 0))
    tok_spec = pl.BlockSpec((tm, SEG), lambda i: (i, 0))
    out_dtypes = [BF16, F32, F32, BF16, BF16, BF16, BF16, BF16]
    out_shape = [jax.ShapeDtypeStruct((heads, n, LANES), t) for t in out_dtypes]
    if kv_head_major:
        out_shape += [jax.ShapeDtypeStruct((heads, n, LANES), kv_dtype)] * 2
        kv_specs = [head_spec] * 2
    else:
        k_shape = (n // tm, tm, DA_HEADS, 2, DA_DQK)
        out_shape += [jax.ShapeDtypeStruct(k_shape, kv_dtype),
                      jax.ShapeDtypeStruct((n, SEG), kv_dtype)]
        kv_specs = [pl.BlockSpec((1,) + k_shape[1:], lambda i: (i, 0, 0, 0, 0)), tok_spec]
    out_shape.append(jax.ShapeDtypeStruct((n, 4 * SEG), BF16))
    out_specs = [head_spec] * 8 + kv_specs + [pl.BlockSpec((tm, 4 * SEG), lambda i: (i, 0))]
    return pl.pallas_call(
        functools.partial(_premix_kernel, latent=latent),
        name=f"premix_n{n}", grid=(n // tm,),
        in_specs=in_specs, out_specs=out_specs, out_shape=out_shape,
        compiler_params=_params(("parallel",)),
    )(*args)


def _hgrn_chunk_local(q, b, total, k, v, keep, safe):
    if safe:
        ref = b[HG_CHUNK // 2:HG_CHUNK // 2 + 1, :]
        qa = q * jnp.exp(b - ref)
        kb = k * jnp.exp(ref - b)
        attn = jnp.where(keep, _dot_nt(qa.astype(BF16), kb.astype(BF16)), 0.0)
        qe = qa * jnp.exp(ref)
        kd = kb * jnp.exp(total - ref)
    else:
        qe = q * jnp.exp(b)
        kd = k * jnp.exp(total - b)
        col = lax.broadcasted_iota(I32, (HG_CHUNK, HG_CHUNK), 1)

        def column(s, acc):
            onehot = (lax.broadcasted_iota(I32, (HG_CHUNK, 1), 0) == s).astype(F32)
            bs = jnp.sum(b * onehot, axis=0, keepdims=True)
            ks = jnp.sum(k * onehot, axis=0, keepdims=True)
            w = jnp.sum(q * ks * jnp.exp(jnp.minimum(b - bs, 0.0)), axis=1, keepdims=True)
            return jnp.where(col == s, w, acc)

        attn = lax.fori_loop(0, HG_CHUNK, column, jnp.zeros((HG_CHUNK, HG_CHUNK), F32))
        attn = jnp.where(keep, attn, 0.0)
    vt = v.astype(F32).T.astype(BF16)
    lhs = jnp.concatenate([qe.astype(BF16), attn.astype(BF16)], axis=1)
    return lhs, vt, _dot(vt, kd.astype(BF16)), jnp.exp(total)


def _hgrn_scan_group(chunks, st):
    outs = []
    for lhs, vt, inc, decay in chunks:
        outs.append(_dot_nt(lhs, jnp.concatenate([st.astype(BF16), vt], axis=1)))
        st = st * decay + inc
    return outs, st


def _hgrn_head(hh, head, refs, has_state):
    if has_state:
        (q_ref, bf_ref, bb_ref, kff_ref, kfb_ref, i_ref, g_ref, nw_ref, s0_ref,
         o_ref, sn_ref, of_s, ob_s) = refs
    else:
        (q_ref, bf_ref, bb_ref, kff_ref, kfb_ref, i_ref, g_ref, nw_ref,
         o_ref, sn_ref, of_s, ob_s) = refs
    t_len = q_ref.shape[1]
    nc = t_len // HG_CHUNK
    unroll = min(HG_UNROLL, nc)
    row = lax.broadcasted_iota(I32, (HG_CHUNK, HG_CHUNK), 0)
    col = lax.broadcasted_iota(I32, (HG_CHUNK, HG_CHUNK), 1)
    keep_f, keep_b = row >= col, row <= col

    mid_f = bf_ref[hh, pl.ds(HG_CHUNK // 2, nc, stride=HG_CHUNK), :]
    tot_f = bf_ref[hh, pl.ds(HG_CHUNK - 1, nc, stride=HG_CHUNK), :]
    mid_b = bb_ref[hh, pl.ds(HG_CHUNK // 2, nc, stride=HG_CHUNK), :]
    tot_b = bb_ref[hh, pl.ds(0, nc, stride=HG_CHUNK), :]
    worst = jnp.minimum(jnp.min(jnp.minimum(mid_f, tot_f - mid_f)),
                        jnp.min(jnp.minimum(mid_b, tot_b - mid_b)))

    if has_state:
        st_f0, st_b0 = s0_ref[0, 0, hh].T, s0_ref[0, 1, hh].T
    else:
        st_f0 = st_b0 = jnp.zeros((HG_DV, HG_DK), F32)

    def scan(safe):
        def body(it, carry):
            st_f, st_b = carry
            rows_f, rows_b, loc_f, loc_b = [], [], [], []
            for u in range(unroll):
                c = it * unroll + u
                sf = pl.multiple_of(c * HG_CHUNK, HG_CHUNK)
                sb = pl.multiple_of((nc - 1 - c) * HG_CHUNK, HG_CHUNK)
                rf, rb = pl.ds(sf, HG_CHUNK), pl.ds(sb, HG_CHUNK)
                rows_f.append(rf)
                rows_b.append(rb)
                loc_f.append(_hgrn_chunk_local(
                    q_ref[hh, rf, :].astype(F32), bf_ref[hh, rf, :],
                    bf_ref[hh, pl.ds(sf + HG_CHUNK - 1, 1), :],
                    kff_ref[hh, rf, :].astype(F32), i_ref[hh, rf, :], keep_f, safe))
                loc_b.append(_hgrn_chunk_local(
                    q_ref[hh, rb, :].astype(F32), bb_ref[hh, rb, :], bb_ref[hh, pl.ds(sb, 1), :],
                    kfb_ref[hh, rb, :].astype(F32), i_ref[hh, rb, :], keep_b, safe))
            outs_f, st_f = _hgrn_scan_group(loc_f, st_f)
            outs_b, st_b = _hgrn_scan_group(loc_b, st_b)
            for rf, rb, o_f, o_b in zip(rows_f, rows_b, outs_f, outs_b):
                of_s[rf, :] = o_f
                ob_s[rb, :] = o_b
            return st_f, st_b
        return lax.fori_loop(0, nc // unroll, body, (st_f0, st_b0))

    st_f, st_b = lax.cond(worst >= -HG_SAFE_DECAY, lambda: scan(True), lambda: scan(False))
    sn_ref[0, 0, hh] = st_f.T
    sn_ref[0, 1, hh] = st_b.T
    o = of_s[...] + ob_s[...]
    nw = nw_ref[pl.ds(head, 1), :]
    on = o * lax.rsqrt(jnp.mean(o * o, axis=-1, keepdims=True) + EPS) * nw
    o_ref[hh] = (on * g_ref[hh].astype(F32)).astype(o_ref.dtype)


def _hgrn_kernel(*refs, has_state):
    per_step = refs[0].shape[0]
    for hh in range(per_step):
        _hgrn_head(hh, pl.program_id(1) * per_step + hh, refs, has_state)


def _hgrn(q, bf, bb, kff, kfb, iv, g, norm_w, s0, batch):
    heads, n, _ = q.shape
    t_len = n // batch
    per_step = heads if t_len <= HG_ALL_HEADS_TOKENS else 1
    blk = pl.BlockSpec((per_step, t_len, HG_DK), lambda b, h: (h, b, 0))
    st_blk = pl.BlockSpec((1, 2, per_step, HG_DK, HG_DV), lambda b, h: (b, 0, h, 0, 0))
    args = [q, bf, bb, kff, kfb, iv, g, norm_w]
    in_specs = [blk] * 7 + [pl.BlockSpec(norm_w.shape, lambda b, h: (0, 0))]
    if s0 is not None:
        args.append(s0)
        in_specs.append(st_blk)
    o, s_new = pl.pallas_call(
        functools.partial(_hgrn_kernel, has_state=s0 is not None),
        name=f"hgrn_n{n}", grid=(batch, heads // per_step),
        in_specs=in_specs,
        out_specs=[blk, st_blk],
        out_shape=[jax.ShapeDtypeStruct((heads, n, HG_DV), BF16),
                   jax.ShapeDtypeStruct((batch, 2, HG_HEADS, HG_DK, HG_DV), F32)],
        scratch_shapes=[pltpu.VMEM((t_len, HG_DV), F32), pltpu.VMEM((t_len, HG_DV), F32)],
        compiler_params=_params(("parallel", "parallel")),
    )(*args)
    return o, s_new


def _attn_kernel(*refs, has_ctx):
    if has_ctx:
        q_ref, k_ref, v_ref, ck_ref, cv_ref, lam_ref, sw_ref, o_ref, k_s, vt_s = refs
    else:
        q_ref, k_ref, v_ref, lam_ref, sw_ref, o_ref, k_s, vt_s = refs
    t_own = k_ref.shape[1]

    def load_keys(ref):
        if len(ref.shape) == 3:
            return ref[0]
        return jnp.concatenate([ref[0, :, 0, 0, :], ref[0, :, 0, 1, :]], axis=1)

    @pl.when(pl.program_id(2) == 0)
    def _():
        k_s[0:t_own, :] = load_keys(k_ref).astype(BF16)
        vt_s[0:DA_DV, 0:t_own] = v_ref[0].astype(F32).T.astype(BF16)
        if has_ctx:
            k_s[t_own:, :] = load_keys(ck_ref).astype(BF16)
            vt_s[0:DA_DV, t_own:] = cv_ref[0].T.astype(BF16)
        vt_s[DA_DV:, :] = jnp.ones((ONES_ROWS, vt_s.shape[1]), BF16)

    lv = lam_ref[...]
    lam = (jnp.exp(jnp.sum(lv[0:1] * lv[1:2], keepdims=True))
           - jnp.exp(jnp.sum(lv[2:3] * lv[3:4], keepdims=True)) + LAM_INIT)
    tq = q_ref.shape[1] // ATT_SPLIT
    dim = lax.broadcasted_iota(I32, (2 * DA_DQK, tq), 0)

    def scores(i):
        qt = q_ref[0, i * tq:(i + 1) * tq, :].astype(F32).T
        q_both = jnp.concatenate([jnp.where(dim < DA_DQK, qt, 0.0),
                                  jnp.where(dim >= DA_DQK, qt, 0.0)], axis=1).astype(BF16)
        return _dot(k_s[...], q_both)

    def finish(i, st):
        pt = jnp.exp2(st - jnp.max(st, axis=0, keepdims=True)).astype(BF16)
        r = _dot(vt_s[...], pt)
        r = r[0:DA_DV] * (1.0 / r[DA_DV:DA_DV + 1])
        o = (r[:, 0:tq] - lam * r[:, tq:2 * tq]).T
        on = o * lax.rsqrt(jnp.mean(o * o, axis=-1, keepdims=True) + EPS) * sw_ref[...]
        o_ref[0, i * tq:(i + 1) * tq, :] = (on * (1.0 - LAM_INIT)).astype(o_ref.dtype)

    sts = [scores(i) for i in range(ATT_SPLIT)]
    for i in range(ATT_SPLIT):
        finish(i, sts[i])


def _attention(q, k, v, ctx_k, ctx_v, lam_p, subln_w, batch, tq):
    heads, n, _ = q.shape
    t_len = n // batch
    nq = t_len // tq
    q_blk = pl.BlockSpec((1, tq, DA_DV), lambda b, h, i: (h, b * nq + i, 0))
    const = lambda b, h, i: (0, 0)
    if k.ndim == 3:
        kv_blk = pl.BlockSpec((1, t_len, DA_DV), lambda b, h, i: (h, b, 0))
        in_specs = [q_blk, kv_blk, kv_blk]
        args = [q, k, v]
    else:
        k = k.reshape(batch, t_len, DA_HEADS, 2, DA_DQK)
        in_specs = [q_blk, pl.BlockSpec((1, t_len, 1, 2, DA_DQK), lambda b, h, i: (b, 0, h, 0, 0)),
                    pl.BlockSpec((1, t_len, DA_DV), lambda b, h, i: (b, 0, h))]
        args = [q, k, v.reshape(batch, t_len, SEG)]
    n_keys = t_len
    if ctx_k is not None:
        past = ctx_k.shape[1]
        n_keys += past
        ck_blk = pl.BlockSpec((1, past, 1, 2, DA_DQK), lambda b, h, i: (b, 0, h, 0, 0))
        cv_blk = pl.BlockSpec((1, past, DA_DV), lambda b, h, i: (b, 0, h))
        args += [ctx_k, ctx_v]
        in_specs += [ck_blk, cv_blk]
    args += [lam_p, subln_w]
    in_specs += [pl.BlockSpec(lam_p.shape, const), pl.BlockSpec(subln_w.shape, const)]
    o = pl.pallas_call(
        functools.partial(_attn_kernel, has_ctx=ctx_k is not None),
        name=f"attn_n{n}", grid=(batch, DA_HEADS, nq),
        in_specs=in_specs, out_specs=q_blk,
        out_shape=jax.ShapeDtypeStruct((heads, n, DA_DV), BF16),
        scratch_shapes=[pltpu.VMEM((n_keys, 2 * DA_DQK), BF16),
                        pltpu.VMEM((DA_DV + ONES_ROWS, n_keys), BF16)],
        compiler_params=_params(("parallel", "parallel", "arbitrary")),
    )(*args)
    return o


def _postmix_kernel(oh_ref, oa_ref, gate_ref, x_ref, mod_ref, wbh_ref, wba_ref, wout_ref,
                    n2_ref, rw_ref, x1_o, h2_o, afft_o, afftok_o):
    d = x_ref.shape[1]
    mod = mod_ref[0]
    g1, sh2, sc2 = mod[:, 2 * d:3 * d], mod[:, 3 * d:4 * d], mod[:, 4 * d:5 * d]
    g_h = gate_ref[:, 0:d].astype(F32)
    g_a = gate_ref[:, d:2 * d].astype(F32)

    def heads_on_lanes(ref):
        return jnp.concatenate([ref[h] for h in range(ref.shape[0])], axis=1)

    merged = (g_h * _dot(heads_on_lanes(oh_ref), wbh_ref[...])
              + g_a * _dot(heads_on_lanes(oa_ref), wba_ref[...]))
    x1 = x_ref[...] + g1 * _dot(merged.astype(BF16), wout_ref[...])
    x1_o[...] = x1
    xn = x1 * lax.rsqrt(jnp.mean(x1 * x1, axis=-1, keepdims=True) + EPS) * n2_ref[...]
    h2 = xn * (1.0 + sc2) + sh2
    h2_o[...] = _pack_bf16_pairs(h2)
    h_hi, h_lo = _split2(h2)
    rw = rw_ref[...]
    t1 = _dot_nt(rw, h_hi)
    t2 = _dot_nt(rw, h_lo)
    e = N_EXPERTS
    logits = t1[0:e] + t1[e:2 * e] + t2[0:e]
    mx = jnp.max(logits, axis=0, keepdims=True)
    p = jnp.exp(logits - mx)
    aff = p / jnp.sum(p, axis=0, keepdims=True)
    afft_o[...] = aff
    pad = jnp.zeros((LANES - e, aff.shape[1]), F32)
    afftok_o[...] = jnp.concatenate([aff, pad], axis=0).T


def _postmix(o_h, o_a, gates, x, mod3, mod_row, w_bh, w_ba, w_out, norm2_w, rw_cat, tm):
    n, d = x.shape
    const = lambda i: (0, 0)
    row = lambda i: (i, 0)
    return pl.pallas_call(
        _postmix_kernel,
        name=f"postmix_n{n}", grid=(n // tm,),
        in_specs=[pl.BlockSpec((o_h.shape[0], tm, LANES), lambda i: (0, i, 0)),
                  pl.BlockSpec((o_a.shape[0], tm, LANES), lambda i: (0, i, 0)),
                  pl.BlockSpec((tm, 4 * SEG), row), pl.BlockSpec((tm, d), row),
                  pl.BlockSpec((1, 1, mod3.shape[2]), lambda i: (mod_row(i), 0, 0)),
                  pl.BlockSpec(w_bh.shape, const), pl.BlockSpec(w_ba.shape, const),
                  pl.BlockSpec(w_out.shape, const), pl.BlockSpec((1, d), const),
                  pl.BlockSpec(rw_cat.shape, const)],
        out_specs=[pl.BlockSpec((tm, d), row), pl.BlockSpec((tm, d // 2), row),
                   pl.BlockSpec((N_EXPERTS, tm), lambda i: (0, i)),
                   pl.BlockSpec((tm, LANES), row)],
        out_shape=[jax.ShapeDtypeStruct((n, d), F32), jax.ShapeDtypeStruct((n, d // 2), I32),
                   jax.ShapeDtypeStruct((N_EXPERTS, n), F32),
                   jax.ShapeDtypeStruct((n, LANES), F32)],
        compiler_params=_params(("parallel",)),
    )(o_h, o_a, gates, x, mod3, w_bh, w_ba, w_out, norm2_w, rw_cat)


def _lane_cumsum_exclusive(x, blk):
    e, t = x.shape
    r = lax.broadcasted_iota(I32, (blk, blk), 0)
    c = lax.broadcasted_iota(I32, (blk, blk), 1)
    upper = (r < c).astype(BF16)
    carry = jnp.zeros((e, 1), F32)
    parts = []
    for j in range(t // blk):
        xb = x[:, j * blk:(j + 1) * blk]
        parts.append(_dot(xb.astype(BF16), upper) + carry)
        carry = carry + jnp.sum(xb, axis=1, keepdims=True)
    return parts[0] if len(parts) == 1 else jnp.concatenate(parts, axis=1)


def _route_kernel(aff_ref, pos_ref, *, cap, req_per_group, t_len):
    n_req = aff_ref.shape[1] // t_len
    bits = [pltpu.bitcast(aff_ref[:, r * t_len:(r + 1) * t_len], I32) for r in range(n_req)]

    def count(mask):
        return jnp.sum(mask.astype(F32), axis=1, keepdims=True)

    def step(i, ths):
        bit = jnp.int32(1) << (30 - i)
        return tuple(jnp.where(count(b >= (th | bit)) >= cap, th | bit, th)
                     for b, th in zip(bits, ths))

    zero = jnp.zeros((aff_ref.shape[0], 1), I32)
    ths = lax.fori_loop(0, 31, step, (zero,) * n_req)
    blk = min(t_len, 256)
    for r, (b, th) in enumerate(zip(bits, ths)):
        gt = b > th
        eq = (b == th).astype(F32)
        need = cap - count(gt)
        tie_rank = _lane_cumsum_exclusive(eq, blk)
        sel = jnp.where(gt, 1.0, jnp.where(tie_rank < need, eq, 0.0))
        slot = _lane_cumsum_exclusive(sel, blk)
        offset = ((pl.program_id(0) * n_req + r) % req_per_group) * cap
        pos_ref[:, r * t_len:(r + 1) * t_len] = jnp.where(sel > 0.0, slot.astype(I32) + offset, -1)


def _route(afft, batch, cap, req_per_group):
    e, n = afft.shape
    t_len = n // batch
    per_step = max(1, min(batch, ROUTE_LANES // t_len))
    assert batch % per_step == 0
    blk = pl.BlockSpec((e, per_step * t_len), lambda b: (0, b))
    return pl.pallas_call(
        functools.partial(_route_kernel, cap=cap, req_per_group=req_per_group, t_len=t_len),
        name=f"route_n{n}", grid=(batch // per_step,), in_specs=[blk], out_specs=blk,
        out_shape=jax.ShapeDtypeStruct((e, n), I32),
        compiler_params=_params(("parallel",)),
    )(afft)


def _one_hot_rows(pos_row, n_slots):
    slot = lax.broadcasted_iota(I32, (n_slots, pos_row.shape[1]), 0)
    return (slot == pos_row).astype(BF16)


def _dispatch(parts, group_tokens, group_slots):
    n_exp = parts[0][0].shape[0]
    width = parts[0][1].shape[1]
    part_groups = [pos.shape[1] // group_tokens for pos, _ in parts]
    workers = SC_CORES * SC_SUBCORES
    chunks = group_slots // SC_GATHER_ROWS
    assert all(g * n_exp % workers == 0 for g in part_groups)
    assert group_slots % SC_GATHER_ROWS == 0 and group_tokens % SC_LANES == 0
    assert n_exp & (n_exp - 1) == 0
    exp_shift = n_exp.bit_length() - 1
    row_shift = SC_GATHER_ROWS.bit_length() - 1

    def body(*refs):
        ins, (out_hbm, pos_v, idx_v, rows_v, sem) = refs[:2 * len(parts)], refs[2 * len(parts):]
        wid = lax.axis_index("s") * SC_CORES + lax.axis_index("c")
        lane = lax.iota(I32, SC_LANES)
        group_base = 0
        for part, groups in enumerate(part_groups):
            pos_hbm, h_hbm = ins[2 * part], ins[2 * part + 1]
            for k in range(groups * n_exp // workers):
                pair = wid + workers * k
                g = pair >> exp_shift
                e = pair & (n_exp - 1)
                pltpu.sync_copy(pos_hbm.at[e, pl.ds(g * group_tokens, group_tokens)], pos_v)

                @pl.loop(0, group_tokens // SC_LANES)
                def _(i):
                    p = pos_v[pl.ds(i * SC_LANES, SC_LANES)]
                    tok = g * group_tokens + i * SC_LANES + lane
                    slot = jnp.maximum(p, 0)
                    plsc.store_scatter(idx_v, [slot >> row_shift, slot & (SC_GATHER_ROWS - 1)],
                                       tok, mask=p >= 0)

                for c in range(chunks):
                    row0 = (group_base + g) * group_slots + c * SC_GATHER_ROWS
                    pltpu.async_copy(h_hbm.at[idx_v.at[c]], rows_v, sem).wait()
                    pltpu.sync_copy(rows_v, out_hbm.at[e, pl.ds(row0, SC_GATHER_ROWS)])
            group_base += groups

    mesh = plsc.VectorSubcoreMesh(core_axis_name="c", subcore_axis_name="s",
                                  num_cores=SC_CORES, num_subcores=SC_SUBCORES)
    return pl.kernel(
        body,
        out_type=jax.ShapeDtypeStruct((n_exp, sum(part_groups) * group_slots, width), I32),
        mesh=mesh,
        scratch_types=[pltpu.VMEM((group_tokens,), I32), pltpu.VMEM((chunks, SC_GATHER_ROWS), I32),
                       pltpu.VMEM((SC_GATHER_ROWS, width), I32), pltpu.SemaphoreType.DMA],
        compiler_params=pltpu.CompilerParams(needs_layout_passes=False),
        name="dispatch",
    )(*[a for part in parts for a in part])


def _expert_kernel(xg_ref, wg_ref, wu_ref, wd_ref, y_ref, wg_s, wu_s, wd_s):
    @pl.when(pl.program_id(1) == 0)
    def _():
        wg_s[...] = wg_ref[0].astype(BF16)
        wu_s[...] = wu_ref[0].astype(BF16)
        wd_s[...] = wd_ref[0].astype(BF16)

    xg = _unpack_bf16_pairs(xg_ref[0])
    a = _silu(_dot(xg, wg_s[...])) * _dot(xg, wu_s[...])
    y_ref[0] = _dot(a.astype(BF16), wd_s[...]).astype(y_ref.dtype)


def _experts(xg, w_gate, w_up, w_down, ts):
    e, s, half = xg.shape
    d, f = w_gate.shape[1:]
    assert d == 2 * half
    x_blk = pl.BlockSpec((1, ts, d), lambda x, i: (x, i, 0))
    return pl.pallas_call(
        _expert_kernel,
        name=f"experts_s{s}", grid=(e, s // ts),
        in_specs=[pl.BlockSpec((1, ts, half), lambda x, i: (x, i, 0)),
                  pl.BlockSpec((1, d, f), lambda x, i: (x, 0, 0)),
                  pl.BlockSpec((1, d, f), lambda x, i: (x, 0, 0)),
                  pl.BlockSpec((1, f, d), lambda x, i: (x, 0, 0))],
        out_specs=x_blk,
        out_shape=jax.ShapeDtypeStruct((e, s, d), BF16),
        scratch_shapes=[pltpu.VMEM((d, f), BF16), pltpu.VMEM((d, f), BF16),
                        pltpu.VMEM((f, d), BF16)],
        compiler_params=_params(("parallel", "arbitrary")),
    )(xg, w_gate, w_up, w_down)


def _combine_kernel(pos_ref, afftok_ref, y_ref, x1_ref, mod_ref, o_ref, *, windowed):
    step = pl.program_id(2)
    d = x1_ref.shape[1]
    per_step = y_ref.shape[0]
    slot_base = pl.program_id(1) * y_ref.shape[1] if windowed else 0

    @pl.when(step == 0)
    def _():
        o_ref[...] = jnp.zeros_like(o_ref)

    lane = lax.broadcasted_iota(I32, afftok_ref.shape, 1)
    acc = None
    for k in range(per_step):
        e = step * per_step + k
        p = _one_hot_rows(pos_ref[pl.ds(e, 1), :] - slot_base, y_ref.shape[1])
        gate = jnp.sum(jnp.where(lane == e, afftok_ref[...], 0.0), axis=1, keepdims=True)
        term = gate * _dot_tn(p, y_ref[k])
        acc = term if acc is None else acc + term
    o_ref[...] += acc

    @pl.when(step == pl.num_programs(2) - 1)
    def _():
        g2 = mod_ref[0][:, 5 * d:6 * d]
        o_ref[...] = x1_ref[...] + g2 * o_ref[...]


def _combine(pos, afftok, y, x1, mod3, mod_row, group_tokens, group_slots, group_base, tc, t_len):
    e, n = pos.shape
    d = x1.shape[1]
    groups = n // group_tokens
    per = group_tokens // tc
    tok = lambda g, j, x: (g * per + j, 0)
    windowed = tc % t_len == 0 and per > 1
    win = group_slots // per if windowed else group_slots
    wins = group_slots // win
    return pl.pallas_call(
        functools.partial(_combine_kernel, windowed=windowed),
        name=f"combine_n{n}", grid=(groups, per, e // COMBINE_EXPERTS),
        in_specs=[pl.BlockSpec((e, tc), lambda g, j, x: (0, g * per + j)),
                  pl.BlockSpec((tc, LANES), tok),
                  pl.BlockSpec((COMBINE_EXPERTS, win, d),
                               lambda g, j, x: (x, (group_base + g) * wins + j * (wins // per), 0)),
                  pl.BlockSpec((tc, d), tok),
                  pl.BlockSpec((1, 1, mod3.shape[2]), lambda g, j, x: (mod_row(g), 0, 0))],
        out_specs=pl.BlockSpec((tc, d), tok),
        out_shape=jax.ShapeDtypeStruct((n, d), F32),
        compiler_params=_params(("parallel", "parallel", "arbitrary")),
    )(pos, afftok, y, x1, mod3)


def _rope_tables(t_len):
    n_freq = DA_DQK // 4
    inv = ROPE_BASE ** (-jnp.arange(n_freq, dtype=F32) / n_freq)
    t = jnp.arange(t_len)
    pos = jnp.stack([(t // GRID_W).astype(F32), (t % GRID_W).astype(F32)], axis=1)
    ang = pos[:, :, None, None] * inv[None, None, None, :]
    ang = jnp.broadcast_to(ang, (t_len, 2, 2, n_freq))
    sign = jnp.array([-1.0, 1.0], F32)[None, None, :, None]
    cos = jnp.cos(ang).reshape(t_len, DA_DQK)
    sin = (jnp.sin(ang) * sign).reshape(t_len, DA_DQK)
    reps = SEG // DA_DQK
    return jnp.tile(cos, (1, reps)), jnp.tile(sin, (1, reps))


def _trunk(x, batch, mod3, mod_row_tok, weights, ctx_k, ctx_v, s0, rope, group_tokens):
    (norm1_w, norm2_w, w_in, b_gate, lb_logits, hgrn_norm_w, qkw, gm, lam_p, subln_w,
     w_bh, w_ba, w_out, rw_cat) = weights
    n, d = x.shape
    t_len = n // batch
    latent = rope is not None
    (q_h, bf, bb, kff, kfb, i_h, g_h, dq, dk, dv, gates) = _premix(
        x, mod3, functools.partial(mod_row_tok, tm=PREMIX_TOKENS), norm1_w, w_in, b_gate, lb_logits,
        qkw, gm, rope, PREMIX_TOKENS, BF16 if latent else F32, latent)
    o_h, s_new = _hgrn(q_h, bf, bb, kff, kfb, i_h, g_h, hgrn_norm_w, s0, batch)
    o_a = _attention(dq, dk, dv, ctx_k, ctx_v, lam_p, subln_w, batch, min(t_len, ATT_TQ * ATT_SPLIT))
    x1, h2, afft, afftok = _postmix(
        o_h, o_a, gates, x, mod3, functools.partial(mod_row_tok, tm=POSTMIX_TOKENS),
        w_bh, w_ba, w_out, norm2_w, rw_cat, POSTMIX_TOKENS)
    cap = EC_CAPACITY * t_len // N_EXPERTS
    pos = _route(afft, batch, cap, group_tokens // t_len)
    return (pos, h2, afftok, x1, t_len), dk, dv, s_new


def _moe(routed, mod3, mod_row_grps, w_eg, w_eu, w_ed, group_tokens, group_slots):
    xg = _dispatch([(pos, h2p) for pos, h2p, _, _, _ in routed], group_tokens, group_slots)
    y = _experts(xg, w_eg, w_eu, w_ed, min(xg.shape[1], EXPERT_ROWS))
    outs, group_base = [], 0
    for (pos, _, afftok, x1, t_len), mod_row in zip(routed, mod_row_grps):
        outs.append(_combine(pos, afftok, y, x1, mod3, mod_row, group_tokens, group_slots,
                             group_base, COMBINE_TOKENS, t_len))
        group_base += pos.shape[1] // group_tokens
    return outs


def kernel(x_prompt, x_sample, cache_k, cache_v, state_hgrn, c, c_ctx, norm1_w, norm2_w, w_mod,
           b_mod, w_in, b_gate, hgrn_lb_logits, hgrn_norm_w, qk_norm_w, diff_lambda, diff_subln_w,
           w_branch_hgrn, w_branch_attn, w_out, router_w, w_exp_gate, w_exp_up, w_exp_down):
    batch, seq, d = x_prompt.shape
    dec_batch, dec_seq, _ = x_sample.shape
    past = cache_k.shape[2]
    depth = w_in.shape[0]
    assert depth == 1
    group_tokens = dec_seq
    assert group_tokens % seq == 0 and (batch * seq) % group_tokens == 0
    l = 0

    rows = -(-(1 + dec_batch) // 8) * 8
    cond = jnp.zeros((rows, d), F32).at[0].set(c_ctx).at[1:1 + dec_batch].set(c)
    mod = _modulation(cond, w_mod[l], b_mod[l])
    mod3 = mod.reshape(rows, 1, 6 * d)

    gidx = jnp.arange(SEG) // DA_DQK
    gm = (gidx[:, None] == gidx[None, :]).astype(BF16) * (1.0 / DA_DQK)
    qkw = jnp.tile(qk_norm_w[l], (1, SEG // DA_DQK))
    rw_t = router_w[l].T
    rw_hi = rw_t.astype(BF16)
    rw_cat = jnp.concatenate([rw_hi, (rw_t - rw_hi.astype(F32)).astype(BF16)], axis=0)
    weights = (norm1_w[l][None], norm2_w[l][None], w_in[l].astype(BF16), b_gate[l][None],
               hgrn_lb_logits.reshape(4, SEG), hgrn_norm_w[l], qkw, gm, diff_lambda[l],
               diff_subln_w[l][None], w_branch_hgrn[l].astype(BF16),
               w_branch_attn[l].astype(BF16), w_out[l].astype(BF16),
               rw_cat)

    routed_ctx, k_new, v_new, s_new = _trunk(
        x_prompt.reshape(batch * seq, d), batch, mod3,
        lambda i, tm: 0, weights, None, None, None, None, group_tokens)
    per_req = dec_seq
    routed_lat, _, _, _ = _trunk(
        x_sample.reshape(dec_batch * dec_seq, d), dec_batch, mod3,
        lambda i, tm: 1 + (i * tm) // per_req, weights,
        cache_k[:, l], cache_v[:, l].reshape(dec_batch, past, SEG),
        state_hgrn[:, l], _rope_tables(dec_seq), group_tokens)
    group_slots = EC_CAPACITY * group_tokens // N_EXPERTS
    experts = (w_exp_gate[l], w_exp_up[l], w_exp_down[l], group_tokens, group_slots)
    yp, = _moe([routed_ctx], mod3, [lambda g: 0], *experts)
    ys, = _moe([routed_lat], mod3, [lambda g: 1 + g], *experts)

    return (yp.reshape(batch, seq, d), ys.reshape(dec_batch, dec_seq, d),
            k_new.reshape(batch, 1, seq, DA_HEADS, 2, DA_DQK),
            v_new.reshape(batch, 1, seq, DA_HEADS, DA_DV),
            s_new.reshape(batch, 1, 2, HG_HEADS, HG_DK, HG_DV))
```

```python
import functools
import math

import jax
import jax.numpy as jnp
from jax import lax
from jax.experimental import pallas as pl
from jax.experimental.pallas import tpu as pltpu
from jax.experimental.pallas import tpu_sc as plsc

F32 = jnp.float32
BF16 = jnp.bfloat16
I32 = jnp.int32

EPS = 1e-6
GRID_W = 64
HG_HEADS = 4
HG_DK = 128
HG_DV = 128
HG_CHUNK_LOG2 = 7
HG_CHUNK = 1 << HG_CHUNK_LOG2
HG_UNROLL = 4
HG_ALL_HEADS_TOKENS = 1024
DA_HEADS = 4
DA_DQK = 64
DA_DV = 128
N_EXPERTS = 16
EC_CAPACITY = 2
ROPE_BASE = 10000.0
SEG = 512
N_SEG = 12
LAM_INIT = 0.8 - 0.6 * math.exp(-0.3 * 0)
LANES = 128
ONES_ROWS = 16
ATT_SPLIT = 2
ATT_TQ = 256
PREMIX_TOKENS = 512
POSTMIX_TOKENS = 1024
COMBINE_TOKENS = 512
COMBINE_EXPERTS = 8
ROUTE_LANES = 2048
EXPERT_ROWS = 1024
Q_SCALE = DA_DQK ** -0.5 * math.log2(math.e)
HG_SAFE_DECAY = 80.0
VMEM_LIMIT = 56 * 1024 * 1024
SC_CORES = 2
SC_SUBCORES = 16
SC_LANES = 16
SC_GATHER_ROWS = 128


def _dot(a, b):
    return jnp.dot(a, b, preferred_element_type=F32)


def _dot_nt(a, b):
    return lax.dot_general(a, b, (((1,), (1,)), ((), ())), preferred_element_type=F32)


def _dot_tn(a, b):
    return lax.dot_general(a, b, (((0,), (0,)), ((), ())), preferred_element_type=F32)


def _split2(x):
    hi = x.astype(BF16)
    lo = (x - hi.astype(F32)).astype(BF16)
    return hi, lo


def _silu(x):
    return x * jax.nn.sigmoid(x)


def _pack_bf16_pairs(x):
    w = x.shape[1] // 2
    bits = pltpu.bitcast(x.astype(BF16).astype(F32), I32)
    return lax.shift_right_logical(bits[:, :w], 16) | bits[:, w:]


def _unpack_bf16_pairs(words):
    lo = pltpu.bitcast(words << 16, F32)
    hi = pltpu.bitcast(words & jnp.int32(-65536), F32)
    return jnp.concatenate([lo, hi], axis=1).astype(BF16)


def _params(sem):
    return pltpu.CompilerParams(dimension_semantics=sem, vmem_limit_bytes=VMEM_LIMIT)


def _mod_kernel(c_ref, w_ref, b_ref, o_ref):
    s_hi, s_lo = _split2(_silu(c_ref[...]))
    w_hi, w_lo = _split2(w_ref[...])
    o_ref[...] = _dot(s_hi, w_hi) + _dot(s_hi, w_lo) + _dot(s_lo, w_hi) + b_ref[...]


def _modulation(cond, w_mod, b_mod):
    rows, d = cond.shape
    n = w_mod.shape[1]
    bn = 512
    return pl.pallas_call(
        _mod_kernel,
        name="modulation", grid=(n // bn,),
        in_specs=[pl.BlockSpec((rows, d), lambda j: (0, 0)),
                  pl.BlockSpec((d, bn), lambda j: (0, j)),
                  pl.BlockSpec((1, bn), lambda j: (0, j))],
        out_specs=pl.BlockSpec((rows, bn), lambda j: (0, j)),
        out_shape=jax.ShapeDtypeStruct((rows, n), F32),
        compiler_params=_params(("arbitrary",)),
    )(cond, w_mod, b_mod.reshape(1, n))


def _group_rms(z, gm_ref, w):
    ms = _dot((z * z).astype(BF16), gm_ref[...])
    return z * lax.rsqrt(ms + EPS) * w


def _rope(x, cos, sin_signed):
    n = x.shape[-1]
    lane = lax.broadcasted_iota(I32, x.shape, 1)
    partner = jnp.where((lane & 16) == 0, pltpu.roll(x, n - 16, 1), pltpu.roll(x, 16, 1))
    return x * cos + partner * sin_signed


def _premix_kernel(*refs, latent):
    if latent:
        (x_ref, mod_ref, n1_ref, win_ref, bg_ref, lbl_ref, qkw_ref, gm_ref, cos_ref, sin_ref,
         q_o, bf_o, bb_o, kff_o, kfb_o, i_o, g_o, dq_o, dk_o, dv_o, gate_o) = refs
    else:
        (x_ref, mod_ref, n1_ref, win_ref, bg_ref, lbl_ref, qkw_ref, gm_ref,
         q_o, bf_o, bb_o, kff_o, kfb_o, i_o, g_o, dq_o, dk_o, dv_o, gate_o) = refs
    d = x_ref.shape[1]
    mod = mod_ref[0]
    sh1, sc1 = mod[:, 0:d], mod[:, d:2 * d]
    x = x_ref[...]
    xn = x * lax.rsqrt(jnp.mean(x * x, axis=-1, keepdims=True) + EPS) * n1_ref[...]
    hb = (xn * (1.0 + sc1) + sh1).astype(BF16)

    def seg(j):
        return _dot(hb, win_ref[:, j * SEG:(j + 1) * SEG])

    def lower_bound(direction):
        l0 = lbl_ref[2 * direction:2 * direction + 1, :]
        l1 = lbl_ref[2 * direction + 1:2 * direction + 2, :]
        mx = jnp.maximum(l0, l1)
        e0, e1 = jnp.exp(l0 - mx), jnp.exp(l1 - mx)
        return e0 / (e0 + e1)

    tm = x.shape[0]
    row = lax.broadcasted_iota(I32, (HG_CHUNK, HG_CHUNK), 0)
    col = lax.broadcasted_iota(I32, (HG_CHUNK, HG_CHUNK), 1)

    def chunk_cumsum(lf, order):
        tri = order.astype(BF16)
        hi, lo = _split2(lf)
        parts = [_dot(tri, hi[c:c + HG_CHUNK]) + _dot(tri, lo[c:c + HG_CHUNK])
                 for c in range(0, tm, HG_CHUNK)]
        return jnp.concatenate(parts, axis=0)

    def store(o_ref, val):
        val = val.astype(o_ref.dtype)
        if len(o_ref.shape) == 2:
            o_ref[...] = val
        elif len(o_ref.shape) == 3:
            for h in range(o_ref.shape[0]):
                o_ref[h] = val[:, h * LANES:(h + 1) * LANES]
        elif len(o_ref.shape) == 4:
            for h in range(o_ref.shape[2]):
                o_ref[0, :, h, :] = val[:, h * LANES:(h + 1) * LANES]
        else:
            for h in range(o_ref.shape[2]):
                for m in range(2):
                    lo = (2 * h + m) * DA_DQK
                    o_ref[0, :, h, m, :] = val[:, lo:lo + DA_DQK]

    store(q_o, _silu(seg(0)))
    for j, b_o, kf_o, order in ((1, bf_o, kff_o, row >= col), (2, bb_o, kfb_o, row <= col)):
        lbd = lower_bound(j - 1)
        f = lbd + (1.0 - lbd) * jax.nn.sigmoid(seg(j))
        store(b_o, chunk_cumsum(jnp.log(f), order))
        store(kf_o, 1.0 - f)
    store(i_o, seg(3))
    store(g_o, _silu(seg(4)))
    qn = _group_rms(seg(5), gm_ref, qkw_ref[0:1, :]) * Q_SCALE
    kn = _group_rms(seg(6), gm_ref, qkw_ref[1:2, :])
    if latent:
        qn = _rope(qn, cos_ref[...], sin_ref[...])
        kn = _rope(kn, cos_ref[...], sin_ref[...])
    store(dq_o, qn)
    store(dk_o, kn)
    store(dv_o, seg(7))
    for j in range(4):
        z = seg(8 + j) + bg_ref[:, j * SEG:(j + 1) * SEG]
        gate_o[:, j * SEG:(j + 1) * SEG] = jax.nn.sigmoid(z).astype(gate_o.dtype)


def _premix(x, mod3, mod_row, norm1_w, w_in, b_gate, lb_logits, qkw, gm, rope, tm, kv_dtype,
            kv_head_major):
    n, d = x.shape
    latent = rope is not None
    const = lambda i: (0, 0)
    in_specs = [pl.BlockSpec((tm, d), lambda i: (i, 0)),
                pl.BlockSpec((1, 1, mod3.shape[2]), lambda i: (mod_row(i), 0, 0)),
                pl.BlockSpec((1, d), const),
                pl.BlockSpec(w_in.shape, const),
                pl.BlockSpec(b_gate.shape, const),
                pl.BlockSpec(lb_logits.shape, const),
                pl.BlockSpec(qkw.shape, const),
                pl.BlockSpec(gm.shape, const)]
    args = [x, mod3, norm1_w, w_in, b_gate, lb_logits, qkw, gm]
    if latent:
        cos, sin = rope
        nblk = cos.shape[0] // tm
        in_specs += [pl.BlockSpec((tm, SEG), lambda i: (i % nblk, 0))] * 2
        args += [cos, sin]
    heads = SEG // LANES
    head_spec = pl.BlockSpec((heads, tm, LANES), lambda i: (0, i, 0))
    out_dtypes = [BF16, F32, F32, BF16, BF16, BF16, BF16, BF16]
    out_shape = [jax.ShapeDtypeStruct((heads, n, LANES), t) for t in out_dtypes]
    if kv_head_major:
        out_shape += [jax.ShapeDtypeStruct((heads, n, LANES), kv_dtype)] * 2
        kv_specs = [head_spec] * 2
    else:
        k_shape = (n // tm, tm, DA_HEADS, 2, DA_DQK)
        v_shape = (n // tm, tm, DA_HEADS, DA_DV)
        out_shape += [jax.ShapeDtypeStruct(k_shape, kv_dtype),
                      jax.ShapeDtypeStruct(v_shape, kv_dtype)]
        kv_specs = [pl.BlockSpec((1,) + k_shape[1:], lambda i: (i, 0, 0, 0, 0)),
                    pl.BlockSpec((1,) + v_shape[1:], lambda i: (i, 0, 0, 0))]
    out_shape.append(jax.ShapeDtypeStruct((n, 4 * SEG), BF16))
    out_specs = [head_spec] * 8 + kv_specs + [pl.BlockSpec((tm, 4 * SEG), lambda i: (i, 0))]
    return pl.pallas_call(
        functools.partial(_premix_kernel, latent=latent),
        name=f"premix_n{n}", grid=(n // tm,),
        in_specs=in_specs, out_specs=out_specs, out_shape=out_shape,
        compiler_params=_params(("parallel",)),
    )(*args)


def _hgrn_chunk_local(q, b, total, k, v, keep, safe):
    if safe:
        ref = b[HG_CHUNK // 2:HG_CHUNK // 2 + 1, :]
        qa = q * jnp.exp(b - ref)
        kb = k * jnp.exp(ref - b)
        attn = jnp.where(keep, _dot_nt(qa.astype(BF16), kb.astype(BF16)), 0.0)
        qe = qa * jnp.exp(ref)
        kd = kb * jnp.exp(total - ref)
    else:
        qe = q * jnp.exp(b)
        kd = k * jnp.exp(total - b)
        col = lax.broadcasted_iota(I32, (HG_CHUNK, HG_CHUNK), 1)

        def column(s, acc):
            onehot = (lax.broadcasted_iota(I32, (HG_CHUNK, 1), 0) == s).astype(F32)
            bs = jnp.sum(b * onehot, axis=0, keepdims=True)
            ks = jnp.sum(k * onehot, axis=0, keepdims=True)
            w = jnp.sum(q * ks * jnp.exp(jnp.minimum(b - bs, 0.0)), axis=1, keepdims=True)
            return jnp.where(col == s, w, acc)

        attn = lax.fori_loop(0, HG_CHUNK, column, jnp.zeros((HG_CHUNK, HG_CHUNK), F32))
        attn = jnp.where(keep, attn, 0.0)
    vt = v.astype(F32).T.astype(BF16)
    lhs = jnp.concatenate([qe.astype(BF16), attn.astype(BF16)], axis=1)
    return lhs, vt, _dot(vt, kd.astype(BF16)), jnp.exp(total)


def _hgrn_scan_group(chunks, st):
    outs = []
    for lhs, vt, inc, decay in chunks:
        outs.append(_dot_nt(lhs, jnp.concatenate([st.astype(BF16), vt], axis=1)))
        st = st * decay + inc
    return outs, st


def _hgrn_head(hh, head, refs, has_state):
    if has_state:
        (q_ref, bf_ref, bb_ref, kff_ref, kfb_ref, i_ref, g_ref, nw_ref, s0_ref,
         o_ref, sn_ref, of_s, ob_s) = refs
    else:
        (q_ref, bf_ref, bb_ref, kff_ref, kfb_ref, i_ref, g_ref, nw_ref,
         o_ref, sn_ref, of_s, ob_s) = refs
    t_len = q_ref.shape[1]
    nc = t_len // HG_CHUNK
    unroll = min(HG_UNROLL, nc)
    row = lax.broadcasted_iota(I32, (HG_CHUNK, HG_CHUNK), 0)
    col = lax.broadcasted_iota(I32, (HG_CHUNK, HG_CHUNK), 1)
    keep_f, keep_b = row >= col, row <= col

    mid_f = bf_ref[hh, pl.ds(HG_CHUNK // 2, nc, stride=HG_CHUNK), :]
    tot_f = bf_ref[hh, pl.ds(HG_CHUNK - 1, nc, stride=HG_CHUNK), :]
    mid_b = bb_ref[hh, pl.ds(HG_CHUNK // 2, nc, stride=HG_CHUNK), :]
    tot_b = bb_ref[hh, pl.ds(0, nc, stride=HG_CHUNK), :]
    worst = jnp.minimum(jnp.min(jnp.minimum(mid_f, tot_f - mid_f)),
                        jnp.min(jnp.minimum(mid_b, tot_b - mid_b)))

    if has_state:
        st_f0, st_b0 = s0_ref[0, 0, hh].T, s0_ref[0, 1, hh].T
    else:
        st_f0 = st_b0 = jnp.zeros((HG_DV, HG_DK), F32)

    def scan(safe):
        def body(it, carry):
            st_f, st_b = carry
            rows_f, rows_b, loc_f, loc_b = [], [], [], []
            for u in range(unroll):
                c = it * unroll + u
                sf = pl.multiple_of(c * HG_CHUNK, HG_CHUNK)
                sb = pl.multiple_of((nc - 1 - c) * HG_CHUNK, HG_CHUNK)
                rf, rb = pl.ds(sf, HG_CHUNK), pl.ds(sb, HG_CHUNK)
                rows_f.append(rf)
                rows_b.append(rb)
                loc_f.append(_hgrn_chunk_local(
                    q_ref[hh, rf, :].astype(F32), bf_ref[hh, rf, :],
                    bf_ref[hh, pl.ds(sf + HG_CHUNK - 1, 1), :],
                    kff_ref[hh, rf, :].astype(F32), i_ref[hh, rf, :], keep_f, safe))
                loc_b.append(_hgrn_chunk_local(
                    q_ref[hh, rb, :].astype(F32), bb_ref[hh, rb, :], bb_ref[hh, pl.ds(sb, 1), :],
                    kfb_ref[hh, rb, :].astype(F32), i_ref[hh, rb, :], keep_b, safe))
            outs_f, st_f = _hgrn_scan_group(loc_f, st_f)
            outs_b, st_b = _hgrn_scan_group(loc_b, st_b)
            for rf, rb, o_f, o_b in zip(rows_f, rows_b, outs_f, outs_b):
                of_s[rf, :] = o_f
                ob_s[rb, :] = o_b
            return st_f, st_b
        return lax.fori_loop(0, nc // unroll, body, (st_f0, st_b0))

    st_f, st_b = lax.cond(worst >= -HG_SAFE_DECAY, lambda: scan(True), lambda: scan(False))
    sn_ref[0, 0, hh] = st_f.T
    sn_ref[0, 1, hh] = st_b.T
    o = of_s[...] + ob_s[...]
    nw = nw_ref[pl.ds(head, 1), :]
    on = o * lax.rsqrt(jnp.mean(o * o, axis=-1, keepdims=True) + EPS) * nw
    o_ref[hh] = (on * g_ref[hh].astype(F32)).astype(o_ref.dtype)


def _hgrn_kernel(*refs, has_state):
    per_step = refs[0].shape[0]
    for hh in range(per_step):
        _hgrn_head(hh, pl.program_id(1) * per_step + hh, refs, has_state)


def _hgrn(q, bf, bb, kff, kfb, iv, g, norm_w, s0, batch):
    heads, n, _ = q.shape
    t_len = n // batch
    per_step = heads if t_len <= HG_ALL_HEADS_TOKENS else 1
    blk = pl.BlockSpec((per_step, t_len, HG_DK), lambda b, h: (h, b, 0))
    st_blk = pl.BlockSpec((1, 2, per_step, HG_DK, HG_DV), lambda b, h: (b, 0, h, 0, 0))
    args = [q, bf, bb, kff, kfb, iv, g, norm_w]
    in_specs = [blk] * 7 + [pl.BlockSpec(norm_w.shape, lambda b, h: (0, 0))]
    if s0 is not None:
        args.append(s0)
        in_specs.append(st_blk)
    o, s_new = pl.pallas_call(
        functools.partial(_hgrn_kernel, has_state=s0 is not None),
        name=f"hgrn_n{n}", grid=(batch, heads // per_step),
        in_specs=in_specs,
        out_specs=[blk, st_blk],
        out_shape=[jax.ShapeDtypeStruct((heads, n, HG_DV), BF16),
                   jax.ShapeDtypeStruct((batch, 2, HG_HEADS, HG_DK, HG_DV), F32)],
        scratch_shapes=[pltpu.VMEM((t_len, HG_DV), F32), pltpu.VMEM((t_len, HG_DV), F32)],
        compiler_params=_params(("parallel", "parallel")),
    )(*args)
    return o, s_new


def _attn_kernel(*refs, has_ctx):
    if has_ctx:
        q_ref, k_ref, v_ref, ck_ref, cv_ref, lam_ref, sw_ref, o_ref, k_s, vt_s = refs
    else:
        q_ref, k_ref, v_ref, lam_ref, sw_ref, o_ref, k_s, vt_s = refs
    t_own = k_ref.shape[1]

    def load_keys(ref, hh):
        if len(ref.shape) == 3:
            return ref[hh]
        return jnp.concatenate([ref[0, :, hh, 0, :], ref[0, :, hh, 1, :]], axis=1)

    def load_vals(ref, hh):
        return ref[hh] if len(ref.shape) == 3 else ref[0, :, hh, :]

    lv = lam_ref[...]
    lam = (jnp.exp(jnp.sum(lv[0:1] * lv[1:2], keepdims=True))
           - jnp.exp(jnp.sum(lv[2:3] * lv[3:4], keepdims=True)) + LAM_INIT)
    tq = q_ref.shape[1] // ATT_SPLIT
    dim = lax.broadcasted_iota(I32, (2 * DA_DQK, tq), 0)

    for hh in range(q_ref.shape[0]):
        @pl.when(pl.program_id(2) == 0)
        def _():
            k_s[0:t_own, :] = load_keys(k_ref, hh).astype(BF16)
            vt_s[0:DA_DV, 0:t_own] = load_vals(v_ref, hh).astype(F32).T.astype(BF16)
            if has_ctx:
                k_s[t_own:, :] = load_keys(ck_ref, hh).astype(BF16)
                vt_s[0:DA_DV, t_own:] = load_vals(cv_ref, hh).T.astype(BF16)
            vt_s[DA_DV:, :] = jnp.ones((ONES_ROWS, vt_s.shape[1]), BF16)

        def scores(i):
            qt = q_ref[hh, i * tq:(i + 1) * tq, :].astype(F32).T
            q_both = jnp.concatenate([jnp.where(dim < DA_DQK, qt, 0.0),
                                      jnp.where(dim >= DA_DQK, qt, 0.0)], axis=1).astype(BF16)
            return _dot(k_s[...], q_both)

        def finish(i, st):
            pt = jnp.exp2(st - jnp.max(st, axis=0, keepdims=True)).astype(BF16)
            r = _dot(vt_s[...], pt)
            r = r[0:DA_DV] * (1.0 / r[DA_DV:DA_DV + 1])
            o = (r[:, 0:tq] - lam * r[:, tq:2 * tq]).T
            on = o * lax.rsqrt(jnp.mean(o * o, axis=-1, keepdims=True) + EPS) * sw_ref[...]
            o_ref[hh, i * tq:(i + 1) * tq, :] = (on * (1.0 - LAM_INIT)).astype(o_ref.dtype)

        sts = [scores(i) for i in range(ATT_SPLIT)]
        for i in range(ATT_SPLIT):
            finish(i, sts[i])


def _attention(q, k, v, ctx_k, ctx_v, lam_p, subln_w, batch, tq):
    heads, n, _ = q.shape
    t_len = n // batch
    nq = t_len // tq
    const = lambda b, h, i: (0, 0)
    if k.ndim == 3:
        per_step = 1
        q_blk = pl.BlockSpec((1, tq, DA_DV), lambda b, h, i: (h, b * nq + i, 0))
        kv_blk = pl.BlockSpec((1, t_len, DA_DV), lambda b, h, i: (h, b, 0))
        in_specs = [q_blk, kv_blk, kv_blk]
        args = [q, k, v]
    else:
        per_step = heads
        assert nq == 1
        q_blk = pl.BlockSpec((heads, tq, DA_DV), lambda b, h, i: (0, b * nq + i, 0))
        k = k.reshape(batch, t_len, heads, 2, DA_DQK)
        v = v.reshape(batch, t_len, heads, DA_DV)
        in_specs = [q_blk,
                    pl.BlockSpec((1, t_len, heads, 2, DA_DQK), lambda b, h, i: (b, 0, 0, 0, 0)),
                    pl.BlockSpec((1, t_len, heads, DA_DV), lambda b, h, i: (b, 0, 0, 0))]
        args = [q, k, v]
    n_keys = t_len
    if ctx_k is not None:
        past = ctx_k.shape[1]
        n_keys += past
        ck_blk = pl.BlockSpec((1, past, 1, 2, DA_DQK), lambda b, h, i: (b, 0, h, 0, 0))
        cv_blk = pl.BlockSpec((1, past, DA_DV), lambda b, h, i: (b, 0, h))
        args += [ctx_k, ctx_v]
        in_specs += [ck_blk, cv_blk]
    args += [lam_p, subln_w]
    in_specs += [pl.BlockSpec(lam_p.shape, const), pl.BlockSpec(subln_w.shape, const)]
    o = pl.pallas_call(
        functools.partial(_attn_kernel, has_ctx=ctx_k is not None),
        name=f"attn_n{n}", grid=(batch, heads // per_step, nq),
        in_specs=in_specs, out_specs=q_blk,
        out_shape=jax.ShapeDtypeStruct((heads, n, DA_DV), BF16),
        scratch_shapes=[pltpu.VMEM((n_keys, 2 * DA_DQK), BF16),
                        pltpu.VMEM((DA_DV + ONES_ROWS, n_keys), BF16)],
        compiler_params=_params(("parallel", "parallel", "arbitrary")),
    )(*args)
    return o


def _postmix_kernel(oh_ref, oa_ref, gate_ref, x_ref, mod_ref, wbh_ref, wba_ref, wout_ref,
                    n2_ref, rw_ref, x1_o, h2_o, afft_o, afftok_o):
    d = x_ref.shape[1]
    mod = mod_ref[0]
    g1, sh2, sc2 = mod[:, 2 * d:3 * d], mod[:, 3 * d:4 * d], mod[:, 4 * d:5 * d]
    g_h = gate_ref[:, 0:d].astype(F32)
    g_a = gate_ref[:, d:2 * d].astype(F32)

    def heads_on_lanes(ref):
        return jnp.concatenate([ref[h] for h in range(ref.shape[0])], axis=1)

    merged = (g_h * _dot(heads_on_lanes(oh_ref), wbh_ref[...])
              + g_a * _dot(heads_on_lanes(oa_ref), wba_ref[...]))
    x1 = x_ref[...] + g1 * _dot(merged.astype(BF16), wout_ref[...])
    x1_o[...] = x1
    xn = x1 * lax.rsqrt(jnp.mean(x1 * x1, axis=-1, keepdims=True) + EPS) * n2_ref[...]
    h2 = xn * (1.0 + sc2) + sh2
    h2_o[...] = _pack_bf16_pairs(h2)
    h_hi, h_lo = _split2(h2)
    rw = rw_ref[...]
    t1 = _dot_nt(rw, h_hi)
    t2 = _dot_nt(rw, h_lo)
    e = N_EXPERTS
    logits = t1[0:e] + t1[e:2 * e] + t2[0:e]
    mx = jnp.max(logits, axis=0, keepdims=True)
    p = jnp.exp(logits - mx)
    aff = p / jnp.sum(p, axis=0, keepdims=True)
    afft_o[...] = aff
    pad = jnp.zeros((LANES - e, aff.shape[1]), F32)
    afftok_o[...] = jnp.concatenate([aff, pad], axis=0).T


def _postmix(o_h, o_a, gates, x, mod3, mod_row, w_bh, w_ba, w_out, norm2_w, rw_cat, tm):
    n, d = x.shape
    const = lambda i: (0, 0)
    row = lambda i: (i, 0)
    return pl.pallas_call(
        _postmix_kernel,
        name=f"postmix_n{n}", grid=(n // tm,),
        in_specs=[pl.BlockSpec((o_h.shape[0], tm, LANES), lambda i: (0, i, 0)),
                  pl.BlockSpec((o_a.shape[0], tm, LANES), lambda i: (0, i, 0)),
                  pl.BlockSpec((tm, 4 * SEG), row), pl.BlockSpec((tm, d), row),
                  pl.BlockSpec((1, 1, mod3.shape[2]), lambda i: (mod_row(i), 0, 0)),
                  pl.BlockSpec(w_bh.shape, const), pl.BlockSpec(w_ba.shape, const),
                  pl.BlockSpec(w_out.shape, const), pl.BlockSpec((1, d), const),
                  pl.BlockSpec(rw_cat.shape, const)],
        out_specs=[pl.BlockSpec((tm, d), row), pl.BlockSpec((tm, d // 2), row),
                   pl.BlockSpec((N_EXPERTS, tm), lambda i: (0, i)),
                   pl.BlockSpec((tm, LANES), row)],
        out_shape=[jax.ShapeDtypeStruct((n, d), F32), jax.ShapeDtypeStruct((n, d // 2), I32),
                   jax.ShapeDtypeStruct((N_EXPERTS, n), F32),
                   jax.ShapeDtypeStruct((n, LANES), F32)],
        compiler_params=_params(("parallel",)),
    )(o_h, o_a, gates, x, mod3, w_bh, w_ba, w_out, norm2_w, rw_cat)


def _lane_cumsum_exclusive(x, blk):
    e, t = x.shape
    r = lax.broadcasted_iota(I32, (blk, blk), 0)
    c = lax.broadcasted_iota(I32, (blk, blk), 1)
    upper = (r < c).astype(BF16)
    carry = jnp.zeros((e, 1), F32)
    parts = []
    for j in range(t // blk):
        xb = x[:, j * blk:(j + 1) * blk]
        parts.append(_dot(xb.astype(BF16), upper) + carry)
        carry = carry + jnp.sum(xb, axis=1, keepdims=True)
    return parts[0] if len(parts) == 1 else jnp.concatenate(parts, axis=1)


def _route_kernel(aff_ref, pos_ref, *, cap, req_per_group, t_len):
    n_req = aff_ref.shape[1] // t_len
    bits = [pltpu.bitcast(aff_ref[:, r * t_len:(r + 1) * t_len], I32) for r in range(n_req)]

    def count(mask):
        return jnp.sum(mask.astype(F32), axis=1, keepdims=True)

    def step(i, ths):
        bit = jnp.int32(1) << (30 - i)
        return tuple(jnp.where(count(b >= (th | bit)) >= cap, th | bit, th)
                     for b, th in zip(bits, ths))

    zero = jnp.zeros((aff_ref.shape[0], 1), I32)
    ths = lax.fori_loop(0, 31, step, (zero,) * n_req)
    blk = min(t_len, 256)
    for r, (b, th) in enumerate(zip(bits, ths)):
        gt = b > th
        eq = (b == th).astype(F32)
        need = cap - count(gt)
        tie_rank = _lane_cumsum_exclusive(eq, blk)
        sel = jnp.where(gt, 1.0, jnp.where(tie_rank < need, eq, 0.0))
        slot = _lane_cumsum_exclusive(sel, blk)
        offset = ((pl.program_id(0) * n_req + r) % req_per_group) * cap
        pos_ref[:, r * t_len:(r + 1) * t_len] = jnp.where(sel > 0.0, slot.astype(I32) + offset, -1)


def _route(afft, batch, cap, req_per_group):
    e, n = afft.shape
    t_len = n // batch
    per_step = max(1, min(batch, ROUTE_LANES // t_len))
    assert batch % per_step == 0
    blk = pl.BlockSpec((e, per_step * t_len), lambda b: (0, b))
    return pl.pallas_call(
        functools.partial(_route_kernel, cap=cap, req_per_group=req_per_group, t_len=t_len),
        name=f"route_n{n}", grid=(batch // per_step,), in_specs=[blk], out_specs=blk,
        out_shape=jax.ShapeDtypeStruct((e, n), I32),
        compiler_params=_params(("parallel",)),
    )(afft)


def _one_hot_rows(pos_row, n_slots):
    slot = lax.broadcasted_iota(I32, (n_slots, pos_row.shape[1]), 0)
    return (slot == pos_row).astype(BF16)


def _dispatch(parts, group_tokens, group_slots):
    n_exp = parts[0][0].shape[0]
    width = parts[0][1].shape[1]
    part_groups = [pos.shape[1] // group_tokens for pos, _ in parts]
    workers = SC_CORES * SC_SUBCORES
    chunks = group_slots // SC_GATHER_ROWS
    assert all(g * n_exp % workers == 0 for g in part_groups)
    assert group_slots % SC_GATHER_ROWS == 0 and group_tokens % SC_LANES == 0
    assert n_exp & (n_exp - 1) == 0
    exp_shift = n_exp.bit_length() - 1
    row_shift = SC_GATHER_ROWS.bit_length() - 1

    def body(*refs):
        ins, (out_hbm, pos_v, idx_v, rows_v, sem) = refs[:2 * len(parts)], refs[2 * len(parts):]
        wid = lax.axis_index("s") * SC_CORES + lax.axis_index("c")
        lane = lax.iota(I32, SC_LANES)
        group_base = 0
        for part, groups in enumerate(part_groups):
            pos_hbm, h_hbm = ins[2 * part], ins[2 * part + 1]
            for k in range(groups * n_exp // workers):
                pair = wid + workers * k
                g = pair >> exp_shift
                e = pair & (n_exp - 1)
                pltpu.sync_copy(pos_hbm.at[e, pl.ds(g * group_tokens, group_tokens)], pos_v)

                @pl.loop(0, group_tokens // SC_LANES)
                def _(i):
                    p = pos_v[pl.ds(i * SC_LANES, SC_LANES)]
                    tok = g * group_tokens + i * SC_LANES + lane
                    slot = jnp.maximum(p, 0)
                    plsc.store_scatter(idx_v, [slot >> row_shift, slot & (SC_GATHER_ROWS - 1)],
                                       tok, mask=p >= 0)

                for c in range(chunks):
                    row0 = (group_base + g) * group_slots + c * SC_GATHER_ROWS
                    pltpu.async_copy(h_hbm.at[idx_v.at[c]], rows_v, sem).wait()
                    pltpu.sync_copy(rows_v, out_hbm.at[e, pl.ds(row0, SC_GATHER_ROWS)])
            group_base += groups

    mesh = plsc.VectorSubcoreMesh(core_axis_name="c", subcore_axis_name="s",
                                  num_cores=SC_CORES, num_subcores=SC_SUBCORES)
    return pl.kernel(
        body,
        out_type=jax.ShapeDtypeStruct((n_exp, sum(part_groups) * group_slots, width), I32),
        mesh=mesh,
        scratch_types=[pltpu.VMEM((group_tokens,), I32), pltpu.VMEM((chunks, SC_GATHER_ROWS), I32),
                       pltpu.VMEM((SC_GATHER_ROWS, width), I32), pltpu.SemaphoreType.DMA],
        compiler_params=pltpu.CompilerParams(needs_layout_passes=False),
        name="dispatch",
    )(*[a for part in parts for a in part])


def _expert_kernel(xg_ref, wg_ref, wu_ref, wd_ref, y_ref, wg_s, wu_s, wd_s):
    @pl.when(pl.program_id(1) == 0)
    def _():
        wg_s[...] = wg_ref[0].astype(BF16)
        wu_s[...] = wu_ref[0].astype(BF16)
        wd_s[...] = wd_ref[0].astype(BF16)

    xg = _unpack_bf16_pairs(xg_ref[0])
    a = _silu(_dot(xg, wg_s[...])) * _dot(xg, wu_s[...])
    y_ref[0] = _dot(a.astype(BF16), wd_s[...]).astype(y_ref.dtype)


def _experts(xg, w_gate, w_up, w_down, ts):
    e, s, half = xg.shape
    d, f = w_gate.shape[1:]
    assert d == 2 * half
    x_blk = pl.BlockSpec((1, ts, d), lambda x, i: (x, i, 0))
    return pl.pallas_call(
        _expert_kernel,
        name=f"experts_s{s}", grid=(e, s // ts),
        in_specs=[pl.BlockSpec((1, ts, half), lambda x, i: (x, i, 0)),
                  pl.BlockSpec((1, d, f), lambda x, i: (x, 0, 0)),
                  pl.BlockSpec((1, d, f), lambda x, i: (x, 0, 0)),
                  pl.BlockSpec((1, f, d), lambda x, i: (x, 0, 0))],
        out_specs=x_blk,
        out_shape=jax.ShapeDtypeStruct((e, s, d), BF16),
        scratch_shapes=[pltpu.VMEM((d, f), BF16), pltpu.VMEM((d, f), BF16),
                        pltpu.VMEM((f, d), BF16)],
        compiler_params=_params(("parallel", "arbitrary")),
    )(xg, w_gate, w_up, w_down)


def _combine_kernel(pos_ref, afftok_ref, y_ref, x1_ref, mod_ref, o_ref, *, windowed):
    step = pl.program_id(2)
    d = x1_ref.shape[1]
    per_step = y_ref.shape[0]
    slot_base = pl.program_id(1) * y_ref.shape[1] if windowed else 0

    @pl.when(step == 0)
    def _():
        o_ref[...] = jnp.zeros_like(o_ref)

    lane = lax.broadcasted_iota(I32, afftok_ref.shape, 1)
    acc = None
    for k in range(per_step):
        e = step * per_step + k
        p = _one_hot_rows(pos_ref[pl.ds(e, 1), :] - slot_base, y_ref.shape[1])
        gate = jnp.sum(jnp.where(lane == e, afftok_ref[...], 0.0), axis=1, keepdims=True)
        term = gate * _dot_tn(p, y_ref[k])
        acc = term if acc is None else acc + term
    o_ref[...] += acc

    @pl.when(step == pl.num_programs(2) - 1)
    def _():
        g2 = mod_ref[0][:, 5 * d:6 * d]
        o_ref[...] = x1_ref[...] + g2 * o_ref[...]


def _combine(pos, afftok, y, x1, mod3, mod_row, group_tokens, group_slots, group_base, tc, t_len):
    e, n = pos.shape
    d = x1.shape[1]
    groups = n // group_tokens
    per = group_tokens // tc
    tok = lambda g, j, x: (g * per + j, 0)
    windowed = tc % t_len == 0 and per > 1
    win = group_slots // per if windowed else group_slots
    wins = group_slots // win
    return pl.pallas_call(
        functools.partial(_combine_kernel, windowed=windowed),
        name=f"combine_n{n}", grid=(groups, per, e // COMBINE_EXPERTS),
        in_specs=[pl.BlockSpec((e, tc), lambda g, j, x: (0, g * per + j)),
                  pl.BlockSpec((tc, LANES), tok),
                  pl.BlockSpec((COMBINE_EXPERTS, win, d),
                               lambda g, j, x: (x, (group_base + g) * wins + j * (wins // per), 0)),
                  pl.BlockSpec((tc, d), tok),
                  pl.BlockSpec((1, 1, mod3.shape[2]), lambda g, j, x: (mod_row(g), 0, 0))],
        out_specs=pl.BlockSpec((tc, d), tok),
        out_shape=jax.ShapeDtypeStruct((n, d), F32),
        compiler_params=_params(("parallel", "parallel", "arbitrary")),
    )(pos, afftok, y, x1, mod3)


def _rope_tables(t_len):
    n_freq = DA_DQK // 4
    inv = ROPE_BASE ** (-jnp.arange(n_freq, dtype=F32) / n_freq)
    t = jnp.arange(t_len)
    pos = jnp.stack([(t // GRID_W).astype(F32), (t % GRID_W).astype(F32)], axis=1)
    ang = pos[:, :, None, None] * inv[None, None, None, :]
    ang = jnp.broadcast_to(ang, (t_len, 2, 2, n_freq))
    sign = jnp.array([-1.0, 1.0], F32)[None, None, :, None]
    cos = jnp.cos(ang).reshape(t_len, DA_DQK)
    sin = (jnp.sin(ang) * sign).reshape(t_len, DA_DQK)
    reps = SEG // DA_DQK
    return jnp.tile(cos, (1, reps)), jnp.tile(sin, (1, reps))


def _trunk(x, batch, mod3, mod_row_tok, weights, ctx_k, ctx_v, s0, rope, group_tokens):
    (norm1_w, norm2_w, w_in, b_gate, lb_logits, hgrn_norm_w, qkw, gm, lam_p, subln_w,
     w_bh, w_ba, w_out, rw_cat) = weights
    n, d = x.shape
    t_len = n // batch
    latent = rope is not None
    (q_h, bf, bb, kff, kfb, i_h, g_h, dq, dk, dv, gates) = _premix(
        x, mod3, functools.partial(mod_row_tok, tm=PREMIX_TOKENS), norm1_w, w_in, b_gate, lb_logits,
        qkw, gm, rope, PREMIX_TOKENS, BF16 if latent else F32, latent)
    o_h, s_new = _hgrn(q_h, bf, bb, kff, kfb, i_h, g_h, hgrn_norm_w, s0, batch)
    o_a = _attention(dq, dk, dv, ctx_k, ctx_v, lam_p, subln_w, batch, min(t_len, ATT_TQ * ATT_SPLIT))
    x1, h2, afft, afftok = _postmix(
        o_h, o_a, gates, x, mod3, functools.partial(mod_row_tok, tm=POSTMIX_TOKENS),
        w_bh, w_ba, w_out, norm2_w, rw_cat, POSTMIX_TOKENS)
    cap = EC_CAPACITY * t_len // N_EXPERTS
    pos = _route(afft, batch, cap, group_tokens // t_len)
    return (pos, h2, afftok, x1, t_len), dk, dv, s_new


def _moe(routed, mod3, mod_row_grps, w_eg, w_eu, w_ed, group_tokens, group_slots):
    xg = _dispatch([(pos, h2p) for pos, h2p, _, _, _ in routed], group_tokens, group_slots)
    y = _experts(xg, w_eg, w_eu, w_ed, min(xg.shape[1], EXPERT_ROWS))
    outs, group_base = [], 0
    for (pos, _, afftok, x1, t_len), mod_row in zip(routed, mod_row_grps):
        outs.append(_combine(pos, afftok, y, x1, mod3, mod_row, group_tokens, group_slots,
                             group_base, COMBINE_TOKENS, t_len))
        group_base += pos.shape[1] // group_tokens
    return outs


def kernel(x_prompt, x_sample, cache_k, cache_v, state_hgrn, c, c_ctx, norm1_w, norm2_w, w_mod,
           b_mod, w_in, b_gate, hgrn_lb_logits, hgrn_norm_w, qk_norm_w, diff_lambda, diff_subln_w,
           w_branch_hgrn, w_branch_attn, w_out, router_w, w_exp_gate, w_exp_up, w_exp_down):
    batch, seq, d = x_prompt.shape
    dec_batch, dec_seq, _ = x_sample.shape
    past = cache_k.shape[2]
    depth = w_in.shape[0]
    assert depth == 1
    group_tokens = dec_seq
    assert group_tokens % seq == 0 and (batch * seq) % group_tokens == 0
    l = 0

    rows = -(-(1 + dec_batch) // 8) * 8
    cond = jnp.zeros((rows, d), F32).at[0].set(c_ctx).at[1:1 + dec_batch].set(c)
    mod = _modulation(cond, w_mod[l], b_mod[l])
    mod3 = mod.reshape(rows, 1, 6 * d)

    gidx = jnp.arange(SEG) // DA_DQK
    gm = (gidx[:, None] == gidx[None, :]).astype(BF16) * (1.0 / DA_DQK)
    qkw = jnp.tile(qk_norm_w[l], (1, SEG // DA_DQK))
    rw_t = router_w[l].T
    rw_hi = rw_t.astype(BF16)
    rw_cat = jnp.concatenate([rw_hi, (rw_t - rw_hi.astype(F32)).astype(BF16)], axis=0)
    weights = (norm1_w[l][None], norm2_w[l][None], w_in[l].astype(BF16), b_gate[l][None],
               hgrn_lb_logits.reshape(4, SEG), hgrn_norm_w[l], qkw, gm, diff_lambda[l],
               diff_subln_w[l][None], w_branch_hgrn[l].astype(BF16),
               w_branch_attn[l].astype(BF16), w_out[l].astype(BF16),
               rw_cat)

    routed_ctx, k_new, v_new, s_new = _trunk(
        x_prompt.reshape(batch * seq, d), batch, mod3,
        lambda i, tm: 0, weights, None, None, None, None, group_tokens)
    per_req = dec_seq
    routed_lat, _, _, _ = _trunk(
        x_sample.reshape(dec_batch * dec_seq, d), dec_batch, mod3,
        lambda i, tm: 1 + (i * tm) // per_req, weights,
        cache_k[:, l], cache_v[:, l].reshape(dec_batch, past, SEG),
        state_hgrn[:, l], _rope_tables(dec_seq), group_tokens)
    group_slots = EC_CAPACITY * group_tokens // N_EXPERTS
    experts = (w_exp_gate[l], w_exp_up[l], w_exp_down[l], group_tokens, group_slots)
    yp, = _moe([routed_ctx], mod3, [lambda g: 0], *experts)
    ys, = _moe([routed_lat], mod3, [lambda g: 1 + g], *experts)

    return (yp.reshape(batch, seq, d), ys.reshape(dec_batch, dec_seq, d),
            k_new.reshape(batch, 1, seq, DA_HEADS, 2, DA_DQK),
            v_new.reshape(batch, 1, seq, DA_HEADS, DA_DV),
            s_new.reshape(batch, 1, 2, HG_HEADS, HG_DK, HG_DV))
```

```python
import functools
import math

import jax
import jax.numpy as jnp
from jax import lax
from jax.experimental import pallas as pl
from jax.experimental.pallas import tpu as pltpu
from jax.experimental.pallas import tpu_sc as plsc

F32 = jnp.float32
BF16 = jnp.bfloat16
I32 = jnp.int32

EPS = 1e-6
GRID_W = 64
HG_HEADS = 4
HG_DK = 128
HG_DV = 128
HG_CHUNK_LOG2 = 7
HG_CHUNK = 1 << HG_CHUNK_LOG2
HG_UNROLL = 4
HG_ALL_HEADS_TOKENS = 1024
DA_HEADS = 4
DA_DQK = 64
DA_DV = 128
N_EXPERTS = 16
EC_CAPACITY = 2
ROPE_BASE = 10000.0
SEG = 512
N_SEG = 12
LAM_INIT = 0.8 - 0.6 * math.exp(-0.3 * 0)
LANES = 128
ONES_ROWS = 16
ATT_SPLIT = 2
ATT_TQ = 256
PREMIX_TOKENS = 512
POSTMIX_TOKENS = 1024
COMBINE_TOKENS = 1024
COMBINE_EXPERTS = 4
ROUTE_LANES = 2048
GROUP_SPAN = 256
EXPERT_ROWS = 1024
Q_SCALE = DA_DQK ** -0.5 * math.log2(math.e)
HG_SAFE_DECAY = 80.0
VMEM_LIMIT = 56 * 1024 * 1024
SC_CORES = 2
SC_SUBCORES = 16
SC_LANES = 16
SC_GATHER_ROWS = 128


def _dot(a, b):
    return jnp.dot(a, b, preferred_element_type=F32)


def _dot_nt(a, b):
    return lax.dot_general(a, b, (((1,), (1,)), ((), ())), preferred_element_type=F32)


def _dot_tn(a, b):
    return lax.dot_general(a, b, (((0,), (0,)), ((), ())), preferred_element_type=F32)


def _split2(x):
    hi = x.astype(BF16)
    lo = (x - hi.astype(F32)).astype(BF16)
    return hi, lo


def _silu(x):
    return x * jax.nn.sigmoid(x)


def _pack_bf16_pairs(x):
    w = x.shape[1] // 2
    bits = pltpu.bitcast(x.astype(BF16).astype(F32), I32)
    return lax.shift_right_logical(bits[:, :w], 16) | bits[:, w:]


def _unpack_bf16_pairs(words):
    lo = pltpu.bitcast(words << 16, F32)
    hi = pltpu.bitcast(words & jnp.int32(-65536), F32)
    return jnp.concatenate([lo, hi], axis=1).astype(BF16)


def _params(sem):
    return pltpu.CompilerParams(dimension_semantics=sem, vmem_limit_bytes=VMEM_LIMIT)


def _mod_kernel(c_ref, w_ref, b_ref, o_ref):
    s_hi, s_lo = _split2(_silu(c_ref[...]))
    w_hi, w_lo = _split2(w_ref[...])
    o_ref[...] = _dot(s_hi, w_hi) + _dot(s_hi, w_lo) + _dot(s_lo, w_hi) + b_ref[...]


def _modulation(cond, w_mod, b_mod):
    rows, d = cond.shape
    n = w_mod.shape[1]
    bn = 512
    return pl.pallas_call(
        _mod_kernel,
        name="modulation", grid=(n // bn,),
        in_specs=[pl.BlockSpec((rows, d), lambda j: (0, 0)),
                  pl.BlockSpec((d, bn), lambda j: (0, j)),
                  pl.BlockSpec((1, bn), lambda j: (0, j))],
        out_specs=pl.BlockSpec((rows, bn), lambda j: (0, j)),
        out_shape=jax.ShapeDtypeStruct((rows, n), F32),
        compiler_params=_params(("arbitrary",)),
    )(cond, w_mod, b_mod.reshape(1, n))


def _group_rms(z, gm_ref, w):
    sq = (z * z).astype(BF16)
    span = gm_ref.shape[0]
    ms = jnp.concatenate([_dot(sq[:, c:c + span], gm_ref[...]) for c in range(0, z.shape[1], span)],
                         axis=1)
    return z * lax.rsqrt(ms + EPS) * w


def _rope(x, cos, sin_signed):
    n = x.shape[-1]
    lane = lax.broadcasted_iota(I32, x.shape, 1)
    partner = jnp.where((lane & 16) == 0, pltpu.roll(x, n - 16, 1), pltpu.roll(x, 16, 1))
    return x * cos + partner * sin_signed


def _premix_kernel(*refs, latent):
    if latent:
        (x_ref, mod_ref, n1_ref, win_ref, bg_ref, lbl_ref, qkw_ref, gm_ref, cos_ref, sin_ref,
         q_o, bf_o, bb_o, kff_o, kfb_o, i_o, g_o, dq_o, dk_o, dv_o, gate_o) = refs
    else:
        (x_ref, mod_ref, n1_ref, win_ref, bg_ref, lbl_ref, qkw_ref, gm_ref,
         q_o, bf_o, bb_o, kff_o, kfb_o, i_o, g_o, dq_o, dk_o, dv_o, gate_o) = refs
    d = x_ref.shape[1]
    mod = mod_ref[0]
    sh1, sc1 = mod[:, 0:d], mod[:, d:2 * d]
    x = x_ref[...]
    xn = x * lax.rsqrt(jnp.mean(x * x, axis=-1, keepdims=True) + EPS) * n1_ref[...]
    hb = (xn * (1.0 + sc1) + sh1).astype(BF16)

    def seg(j):
        return _dot(hb, win_ref[:, j * SEG:(j + 1) * SEG])

    def lower_bound(direction):
        l0 = lbl_ref[2 * direction:2 * direction + 1, :]
        l1 = lbl_ref[2 * direction + 1:2 * direction + 2, :]
        mx = jnp.maximum(l0, l1)
        e0, e1 = jnp.exp(l0 - mx), jnp.exp(l1 - mx)
        return e0 / (e0 + e1)

    tm = x.shape[0]
    row = lax.broadcasted_iota(I32, (HG_CHUNK, HG_CHUNK), 0)
    col = lax.broadcasted_iota(I32, (HG_CHUNK, HG_CHUNK), 1)

    def chunk_cumsum(lf, order):
        tri = order.astype(BF16)
        hi, lo = _split2(lf)
        parts = [_dot(tri, hi[c:c + HG_CHUNK]) + _dot(tri, lo[c:c + HG_CHUNK])
                 for c in range(0, tm, HG_CHUNK)]
        return jnp.concatenate(parts, axis=0)

    def store(o_ref, val):
        val = val.astype(o_ref.dtype)
        if len(o_ref.shape) == 2:
            o_ref[...] = val
        elif len(o_ref.shape) == 3:
            for h in range(o_ref.shape[0]):
                o_ref[h] = val[:, h * LANES:(h + 1) * LANES]
        elif len(o_ref.shape) == 4:
            for h in range(o_ref.shape[2]):
                o_ref[0, :, h, :] = val[:, h * LANES:(h + 1) * LANES]
        else:
            for h in range(o_ref.shape[2]):
                for m in range(2):
                    lo = (2 * h + m) * DA_DQK
                    o_ref[0, :, h, m, :] = val[:, lo:lo + DA_DQK]

    store(q_o, _silu(seg(0)))
    for j, b_o, kf_o, order in ((1, bf_o, kff_o, row >= col), (2, bb_o, kfb_o, row <= col)):
        lbd = lower_bound(j - 1)
        f = lbd + (1.0 - lbd) * jax.nn.sigmoid(seg(j))
        store(b_o, chunk_cumsum(jnp.log(f), order))
        store(kf_o, 1.0 - f)
    store(i_o, seg(3))
    store(g_o, _silu(seg(4)))
    qn = _group_rms(seg(5), gm_ref, qkw_ref[0:1, :]) * Q_SCALE
    kn = _group_rms(seg(6), gm_ref, qkw_ref[1:2, :])
    if latent:
        qn = _rope(qn, cos_ref[...], sin_ref[...])
        kn = _rope(kn, cos_ref[...], sin_ref[...])
    store(dq_o, qn)
    store(dk_o, kn)
    store(dv_o, seg(7))
    for j in range(4):
        z = seg(8 + j) + bg_ref[:, j * SEG:(j + 1) * SEG]
        gate_o[:, j * SEG:(j + 1) * SEG] = jax.nn.sigmoid(z).astype(gate_o.dtype)


def _premix(x, mod3, mod_row, norm1_w, w_in, b_gate, lb_logits, qkw, gm, rope, tm, kv_dtype,
            kv_head_major):
    n, d = x.shape
    latent = rope is not None
    const = lambda i: (0, 0)
    in_specs = [pl.BlockSpec((tm, d), lambda i: (i, 0)),
                pl.BlockSpec((1, 1, mod3.shape[2]), lambda i: (mod_row(i), 0, 0)),
                pl.BlockSpec((1, d), const),
                pl.BlockSpec(w_in.shape, const),
                pl.BlockSpec(b_gate.shape, const),
                pl.BlockSpec(lb_logits.shape, const),
                pl.BlockSpec(qkw.shape, const),
                pl.BlockSpec(gm.shape, const)]
    args = [x, mod3, norm1_w, w_in, b_gate, lb_logits, qkw, gm]
    if latent:
        cos, sin = rope
        nblk = cos.shape[0] // tm
        in_specs += [pl.BlockSpec((tm, SEG), lambda i: (i % nblk, 0))] * 2
        args += [cos, sin]
    heads = SEG // LANES
    head_spec = pl.BlockSpec((heads, tm, LANES), lambda i: (0, i, 0))
    out_dtypes = [BF16, F32, F32, BF16, BF16, BF16, BF16, BF16]
    out_shape = [jax.ShapeDtypeStruct((heads, n, LANES), t) for t in out_dtypes]
    if kv_head_major:
        out_shape += [jax.ShapeDtypeStruct((heads, n, LANES), kv_dtype)] * 2
        kv_specs = [head_spec] * 2
    else:
        k_shape = (n // tm, tm, DA_HEADS, 2, DA_DQK)
        v_shape = (n // tm, tm, DA_HEADS, DA_DV)
        out_shape += [jax.ShapeDtypeStruct(k_shape, kv_dtype),
                      jax.ShapeDtypeStruct(v_shape, kv_dtype)]
        kv_specs = [pl.BlockSpec((1,) + k_shape[1:], lambda i: (i, 0, 0, 0, 0)),
                    pl.BlockSpec((1,) + v_shape[1:], lambda i: (i, 0, 0, 0))]
    out_shape.append(jax.ShapeDtypeStruct((n, 4 * SEG), BF16))
    out_specs = [head_spec] * 8 + kv_specs + [pl.BlockSpec((tm, 4 * SEG), lambda i: (i, 0))]
    return pl.pallas_call(
        functools.partial(_premix_kernel, latent=latent),
        name=f"premix_n{n}", grid=(n // tm,),
        in_specs=in_specs, out_specs=out_specs, out_shape=out_shape,
        compiler_params=_params(("parallel",)),
    )(*args)


def _hgrn_chunk_local(q, b, total, k, v, keep, safe):
    if safe:
        ref = b[HG_CHUNK // 2:HG_CHUNK // 2 + 1, :]
        qa = q * jnp.exp(b - ref)
        kb = k * jnp.exp(ref - b)
        attn = jnp.where(keep, _dot_nt(qa.astype(BF16), kb.astype(BF16)), 0.0)
        qe = qa * jnp.exp(ref)
        kd = kb * jnp.exp(total - ref)
    else:
        qe = q * jnp.exp(b)
        kd = k * jnp.exp(total - b)
        col = lax.broadcasted_iota(I32, (HG_CHUNK, HG_CHUNK), 1)

        def column(s, acc):
            onehot = (lax.broadcasted_iota(I32, (HG_CHUNK, 1), 0) == s).astype(F32)
            bs = jnp.sum(b * onehot, axis=0, keepdims=True)
            ks = jnp.sum(k * onehot, axis=0, keepdims=True)
            w = jnp.sum(q * ks * jnp.exp(jnp.minimum(b - bs, 0.0)), axis=1, keepdims=True)
            return jnp.where(col == s, w, acc)

        attn = lax.fori_loop(0, HG_CHUNK, column, jnp.zeros((HG_CHUNK, HG_CHUNK), F32))
        attn = jnp.where(keep, attn, 0.0)
    vt = v.astype(F32).T.astype(BF16)
    lhs = jnp.concatenate([qe.astype(BF16), attn.astype(BF16)], axis=1)
    return lhs, vt, _dot(vt, kd.astype(BF16)), jnp.exp(total)


def _hgrn_scan_group(chunks, st):
    outs = []
    for lhs, vt, inc, decay in chunks:
        outs.append(_dot_nt(lhs, jnp.concatenate([st.astype(BF16), vt], axis=1)))
        st = st * decay + inc
    return outs, st


def _hgrn_head(hh, head, refs, has_state):
    if has_state:
        (q_ref, bf_ref, bb_ref, kff_ref, kfb_ref, i_ref, g_ref, nw_ref, s0_ref,
         o_ref, sn_ref, of_s, ob_s) = refs
    else:
        (q_ref, bf_ref, bb_ref, kff_ref, kfb_ref, i_ref, g_ref, nw_ref,
         o_ref, sn_ref, of_s, ob_s) = refs
    t_len = q_ref.shape[1]
    nc = t_len // HG_CHUNK
    unroll = min(HG_UNROLL, nc)
    row = lax.broadcasted_iota(I32, (HG_CHUNK, HG_CHUNK), 0)
    col = lax.broadcasted_iota(I32, (HG_CHUNK, HG_CHUNK), 1)
    keep_f, keep_b = row >= col, row <= col

    mid_f = bf_ref[hh, pl.ds(HG_CHUNK // 2, nc, stride=HG_CHUNK), :]
    tot_f = bf_ref[hh, pl.ds(HG_CHUNK - 1, nc, stride=HG_CHUNK), :]
    mid_b = bb_ref[hh, pl.ds(HG_CHUNK // 2, nc, stride=HG_CHUNK), :]
    tot_b = bb_ref[hh, pl.ds(0, nc, stride=HG_CHUNK), :]
    worst = jnp.minimum(jnp.min(jnp.minimum(mid_f, tot_f - mid_f)),
                        jnp.min(jnp.minimum(mid_b, tot_b - mid_b)))

    if has_state:
        st_f0, st_b0 = s0_ref[0, 0, hh].T, s0_ref[0, 1, hh].T
    else:
        st_f0 = st_b0 = jnp.zeros((HG_DV, HG_DK), F32)

    def scan(safe):
        def body(it, carry):
            st_f, st_b = carry
            rows_f, rows_b, loc_f, loc_b = [], [], [], []
            for u in range(unroll):
                c = it * unroll + u
                sf = pl.multiple_of(c * HG_CHUNK, HG_CHUNK)
                sb = pl.multiple_of((nc - 1 - c) * HG_CHUNK, HG_CHUNK)
                rf, rb = pl.ds(sf, HG_CHUNK), pl.ds(sb, HG_CHUNK)
                rows_f.append(rf)
                rows_b.append(rb)
                loc_f.append(_hgrn_chunk_local(
                    q_ref[hh, rf, :].astype(F32), bf_ref[hh, rf, :],
                    bf_ref[hh, pl.ds(sf + HG_CHUNK - 1, 1), :],
                    kff_ref[hh, rf, :].astype(F32), i_ref[hh, rf, :], keep_f, safe))
                loc_b.append(_hgrn_chunk_local(
                    q_ref[hh, rb, :].astype(F32), bb_ref[hh, rb, :], bb_ref[hh, pl.ds(sb, 1), :],
                    kfb_ref[hh, rb, :].astype(F32), i_ref[hh, rb, :], keep_b, safe))
            outs_f, st_f = _hgrn_scan_group(loc_f, st_f)
            outs_b, st_b = _hgrn_scan_group(loc_b, st_b)
            for rf, rb, o_f, o_b in zip(rows_f, rows_b, outs_f, outs_b):
                of_s[rf, :] = o_f
                ob_s[rb, :] = o_b
            return st_f, st_b
        return lax.fori_loop(0, nc // unroll, body, (st_f0, st_b0))

    st_f, st_b = lax.cond(worst >= -HG_SAFE_DECAY, lambda: scan(True), lambda: scan(False))
    sn_ref[0, 0, hh] = st_f.T
    sn_ref[0, 1, hh] = st_b.T
    o = of_s[...] + ob_s[...]
    nw = nw_ref[pl.ds(head, 1), :]
    on = o * lax.rsqrt(jnp.mean(o * o, axis=-1, keepdims=True) + EPS) * nw
    o_ref[hh] = (on * g_ref[hh].astype(F32)).astype(o_ref.dtype)


def _hgrn_kernel(*refs, has_state):
    per_step = refs[0].shape[0]
    for hh in range(per_step):
        _hgrn_head(hh, pl.program_id(1) * per_step + hh, refs, has_state)


def _hgrn(q, bf, bb, kff, kfb, iv, g, norm_w, s0, batch):
    heads, n, _ = q.shape
    t_len = n // batch
    per_step = heads if t_len <= HG_ALL_HEADS_TOKENS else 1
    blk = pl.BlockSpec((per_step, t_len, HG_DK), lambda b, h: (h, b, 0))
    st_blk = pl.BlockSpec((1, 2, per_step, HG_DK, HG_DV), lambda b, h: (b, 0, h, 0, 0))
    args = [q, bf, bb, kff, kfb, iv, g, norm_w]
    in_specs = [blk] * 7 + [pl.BlockSpec(norm_w.shape, lambda b, h: (0, 0))]
    if s0 is not None:
        args.append(s0)
        in_specs.append(st_blk)
    o, s_new = pl.pallas_call(
        functools.partial(_hgrn_kernel, has_state=s0 is not None),
        name=f"hgrn_n{n}", grid=(batch, heads // per_step),
        in_specs=in_specs,
        out_specs=[blk, st_blk],
        out_shape=[jax.ShapeDtypeStruct((heads, n, HG_DV), BF16),
                   jax.ShapeDtypeStruct((batch, 2, HG_HEADS, HG_DK, HG_DV), F32)],
        scratch_shapes=[pltpu.VMEM((t_len, HG_DV), F32), pltpu.VMEM((t_len, HG_DV), F32)],
        compiler_params=_params(("parallel", "parallel")),
    )(*args)
    return o, s_new


def _attn_kernel(*refs, has_ctx):
    if has_ctx:
        q_ref, k_ref, v_ref, ck_ref, cv_ref, lam_ref, sw_ref, o_ref, k_s, vt_s = refs
    else:
        q_ref, k_ref, v_ref, lam_ref, sw_ref, o_ref, k_s, vt_s = refs
    t_own = k_ref.shape[1]

    def load_keys(ref, hh):
        if len(ref.shape) == 3:
            return ref[hh]
        return jnp.concatenate([ref[0, :, hh, 0, :], ref[0, :, hh, 1, :]], axis=1)

    def load_vals(ref, hh):
        return ref[hh] if len(ref.shape) == 3 else ref[0, :, hh, :]

    lv = lam_ref[...]
    lam = (jnp.exp(jnp.sum(lv[0:1] * lv[1:2], keepdims=True))
           - jnp.exp(jnp.sum(lv[2:3] * lv[3:4], keepdims=True)) + LAM_INIT)
    tq = q_ref.shape[1] // ATT_SPLIT
    dim = lax.broadcasted_iota(I32, (2 * DA_DQK, tq), 0)

    for hh in range(q_ref.shape[0]):
        @pl.when(pl.program_id(2) == 0)
        def _():
            k_s[0:t_own, :] = load_keys(k_ref, hh).astype(BF16)
            vt_s[0:DA_DV, 0:t_own] = load_vals(v_ref, hh).astype(F32).T.astype(BF16)
            if has_ctx:
                k_s[t_own:, :] = load_keys(ck_ref, hh).astype(BF16)
                vt_s[0:DA_DV, t_own:] = load_vals(cv_ref, hh).T.astype(BF16)
            vt_s[DA_DV:, :] = jnp.ones((ONES_ROWS, vt_s.shape[1]), BF16)

        def scores(i):
            qt = q_ref[hh, i * tq:(i + 1) * tq, :].astype(F32).T
            q_both = jnp.concatenate([jnp.where(dim < DA_DQK, qt, 0.0),
                                      jnp.where(dim >= DA_DQK, qt, 0.0)], axis=1).astype(BF16)
            return _dot(k_s[...], q_both)

        def finish(i, st):
            pt = jnp.exp2(st - jnp.max(st, axis=0, keepdims=True)).astype(BF16)
            r = _dot(vt_s[...], pt)
            r = r[0:DA_DV] * (1.0 / r[DA_DV:DA_DV + 1])
            o = (r[:, 0:tq] - lam * r[:, tq:2 * tq]).T
            on = o * lax.rsqrt(jnp.mean(o * o, axis=-1, keepdims=True) + EPS) * sw_ref[...]
            o_ref[hh, i * tq:(i + 1) * tq, :] = (on * (1.0 - LAM_INIT)).astype(o_ref.dtype)

        sts = [scores(i) for i in range(ATT_SPLIT)]
        for i in range(ATT_SPLIT):
            finish(i, sts[i])


def _attention(q, k, v, ctx_k, ctx_v, lam_p, subln_w, batch, tq):
    heads, n, _ = q.shape
    t_len = n // batch
    nq = t_len // tq
    const = lambda b, h, i: (0, 0)
    if k.ndim == 3:
        per_step = 1
        q_blk = pl.BlockSpec((1, tq, DA_DV), lambda b, h, i: (h, b * nq + i, 0))
        kv_blk = pl.BlockSpec((1, t_len, DA_DV), lambda b, h, i: (h, b, 0))
        in_specs = [q_blk, kv_blk, kv_blk]
        args = [q, k, v]
    else:
        per_step = heads
        assert nq == 1
        q_blk = pl.BlockSpec((heads, tq, DA_DV), lambda b, h, i: (0, b * nq + i, 0))
        k = k.reshape(batch, t_len, heads, 2, DA_DQK)
        v = v.reshape(batch, t_len, heads, DA_DV)
        in_specs = [q_blk,
                    pl.BlockSpec((1, t_len, heads, 2, DA_DQK), lambda b, h, i: (b, 0, 0, 0, 0)),
                    pl.BlockSpec((1, t_len, heads, DA_DV), lambda b, h, i: (b, 0, 0, 0))]
        args = [q, k, v]
    n_keys = t_len
    if ctx_k is not None:
        past = ctx_k.shape[1]
        n_keys += past
        ck_blk = pl.BlockSpec((1, past, 1, 2, DA_DQK), lambda b, h, i: (b, 0, h, 0, 0))
        cv_blk = pl.BlockSpec((1, past, DA_DV), lambda b, h, i: (b, 0, h))
        args += [ctx_k, ctx_v]
        in_specs += [ck_blk, cv_blk]
    args += [lam_p, subln_w]
    in_specs += [pl.BlockSpec(lam_p.shape, const), pl.BlockSpec(subln_w.shape, const)]
    o = pl.pallas_call(
        functools.partial(_attn_kernel, has_ctx=ctx_k is not None),
        name=f"attn_n{n}", grid=(batch, heads // per_step, nq),
        in_specs=in_specs, out_specs=q_blk,
        out_shape=jax.ShapeDtypeStruct((heads, n, DA_DV), BF16),
        scratch_shapes=[pltpu.VMEM((n_keys, 2 * DA_DQK), BF16),
                        pltpu.VMEM((DA_DV + ONES_ROWS, n_keys), BF16)],
        compiler_params=_params(("parallel", "parallel", "arbitrary")),
    )(*args)
    return o


def _postmix_kernel(oh_ref, oa_ref, gate_ref, x_ref, mod_ref, wbh_ref, wba_ref, wout_ref,
                    n2_ref, rw_ref, x1_o, h2_o, afft_o, afftok_o):
    d = x_ref.shape[1]
    mod = mod_ref[0]
    g1, sh2, sc2 = mod[:, 2 * d:3 * d], mod[:, 3 * d:4 * d], mod[:, 4 * d:5 * d]
    g_h = gate_ref[:, 0:d].astype(F32)
    g_a = gate_ref[:, d:2 * d].astype(F32)

    def heads_on_lanes(ref):
        return jnp.concatenate([ref[h] for h in range(ref.shape[0])], axis=1)

    merged = (g_h * _dot(heads_on_lanes(oh_ref), wbh_ref[...])
              + g_a * _dot(heads_on_lanes(oa_ref), wba_ref[...]))
    x1 = x_ref[...] + g1 * _dot(merged.astype(BF16), wout_ref[...])
    x1_o[...] = x1
    xn = x1 * lax.rsqrt(jnp.mean(x1 * x1, axis=-1, keepdims=True) + EPS) * n2_ref[...]
    h2 = xn * (1.0 + sc2) + sh2
    h2_o[...] = _pack_bf16_pairs(h2)
    h_hi, h_lo = _split2(h2)
    rw = rw_ref[...]
    t1 = _dot_nt(rw, h_hi)
    t2 = _dot_nt(rw, h_lo)
    e = N_EXPERTS
    logits = t1[0:e] + t1[e:2 * e] + t2[0:e]
    mx = jnp.max(logits, axis=0, keepdims=True)
    p = jnp.exp(logits - mx)
    aff = p / jnp.sum(p, axis=0, keepdims=True)
    afft_o[...] = aff
    pad = jnp.zeros((LANES - e, aff.shape[1]), F32)
    afftok_o[...] = jnp.concatenate([aff, pad], axis=0).T


def _postmix(o_h, o_a, gates, x, mod3, mod_row, w_bh, w_ba, w_out, norm2_w, rw_cat, tm):
    n, d = x.shape
    const = lambda i: (0, 0)
    row = lambda i: (i, 0)
    return pl.pallas_call(
        _postmix_kernel,
        name=f"postmix_n{n}", grid=(n // tm,),
        in_specs=[pl.BlockSpec((o_h.shape[0], tm, LANES), lambda i: (0, i, 0)),
                  pl.BlockSpec((o_a.shape[0], tm, LANES), lambda i: (0, i, 0)),
                  pl.BlockSpec((tm, 4 * SEG), row), pl.BlockSpec((tm, d), row),
                  pl.BlockSpec((1, 1, mod3.shape[2]), lambda i: (mod_row(i), 0, 0)),
                  pl.BlockSpec(w_bh.shape, const), pl.BlockSpec(w_ba.shape, const),
                  pl.BlockSpec(w_out.shape, const), pl.BlockSpec((1, d), const),
                  pl.BlockSpec(rw_cat.shape, const)],
        out_specs=[pl.BlockSpec((tm, d), row), pl.BlockSpec((tm, d // 2), row),
                   pl.BlockSpec((N_EXPERTS, tm), lambda i: (0, i)),
                   pl.BlockSpec((tm, LANES), row)],
        out_shape=[jax.ShapeDtypeStruct((n, d), F32), jax.ShapeDtypeStruct((n, d // 2), I32),
                   jax.ShapeDtypeStruct((N_EXPERTS, n), F32),
                   jax.ShapeDtypeStruct((n, LANES), F32)],
        compiler_params=_params(("parallel",)),
    )(o_h, o_a, gates, x, mod3, w_bh, w_ba, w_out, norm2_w, rw_cat)


def _lane_cumsum_exclusive(x, blk):
    e, t = x.shape
    r = lax.broadcasted_iota(I32, (blk, blk), 0)
    c = lax.broadcasted_iota(I32, (blk, blk), 1)
    upper = (r < c).astype(BF16)
    carry = jnp.zeros((e, 1), F32)
    parts = []
    for j in range(t // blk):
        xb = x[:, j * blk:(j + 1) * blk]
        parts.append(_dot(xb.astype(BF16), upper) + carry)
        carry = carry + jnp.sum(xb, axis=1, keepdims=True)
    return parts[0] if len(parts) == 1 else jnp.concatenate(parts, axis=1)


def _route_kernel(aff_ref, pos_ref, *, cap, req_per_group, t_len):
    n_req = aff_ref.shape[1] // t_len
    bits = [pltpu.bitcast(aff_ref[:, r * t_len:(r + 1) * t_len], I32) for r in range(n_req)]

    def count(mask):
        return jnp.sum(mask.astype(F32), axis=1, keepdims=True)

    def step(i, ths):
        bit = jnp.int32(1) << (30 - i)
        return tuple(jnp.where(count(b >= (th | bit)) >= cap, th | bit, th)
                     for b, th in zip(bits, ths))

    zero = jnp.zeros((aff_ref.shape[0], 1), I32)
    ths = lax.fori_loop(0, 31, step, (zero,) * n_req)
    blk = min(t_len, 256)
    for r, (b, th) in enumerate(zip(bits, ths)):
        gt = b > th
        eq = (b == th).astype(F32)
        need = cap - count(gt)
        tie_rank = _lane_cumsum_exclusive(eq, blk)
        sel = jnp.where(gt, 1.0, jnp.where(tie_rank < need, eq, 0.0))
        slot = _lane_cumsum_exclusive(sel, blk)
        offset = ((pl.program_id(0) * n_req + r) % req_per_group) * cap
        pos_ref[:, r * t_len:(r + 1) * t_len] = jnp.where(sel > 0.0, slot.astype(I32) + offset, -1)


def _route(afft, batch, cap, req_per_group):
    e, n = afft.shape
    t_len = n // batch
    per_step = max(1, min(batch, ROUTE_LANES // t_len))
    assert batch % per_step == 0
    blk = pl.BlockSpec((e, per_step * t_len), lambda b: (0, b))
    return pl.pallas_call(
        functools.partial(_route_kernel, cap=cap, req_per_group=req_per_group, t_len=t_len),
        name=f"route_n{n}", grid=(batch // per_step,), in_specs=[blk], out_specs=blk,
        out_shape=jax.ShapeDtypeStruct((e, n), I32),
        compiler_params=_params(("parallel",)),
    )(afft)


def _one_hot_rows(pos_row, n_slots):
    slot = lax.broadcasted_iota(I32, (n_slots, pos_row.shape[1]), 0)
    return (slot == pos_row).astype(BF16)


def _dispatch(parts, group_tokens, group_slots):
    n_exp = parts[0][0].shape[0]
    width = parts[0][1].shape[1]
    part_groups = [pos.shape[1] // group_tokens for pos, _ in parts]
    workers = SC_CORES * SC_SUBCORES
    chunks = group_slots // SC_GATHER_ROWS
    assert all(g * n_exp % workers == 0 for g in part_groups)
    assert group_slots % SC_GATHER_ROWS == 0 and group_tokens % SC_LANES == 0
    assert n_exp & (n_exp - 1) == 0
    exp_shift = n_exp.bit_length() - 1
    row_shift = SC_GATHER_ROWS.bit_length() - 1

    def body(*refs):
        ins, (out_hbm, pos_v, idx_v, rows_v, sem) = refs[:2 * len(parts)], refs[2 * len(parts):]
        wid = lax.axis_index("s") * SC_CORES + lax.axis_index("c")
        lane = lax.iota(I32, SC_LANES)
        group_base = 0
        for part, groups in enumerate(part_groups):
            pos_hbm, h_hbm = ins[2 * part], ins[2 * part + 1]
            for k in range(groups * n_exp // workers):
                pair = wid + workers * k
                g = pair >> exp_shift
                e = pair & (n_exp - 1)
                pltpu.sync_copy(pos_hbm.at[e, pl.ds(g * group_tokens, group_tokens)], pos_v)

                @pl.loop(0, group_tokens // SC_LANES)
                def _(i):
                    p = pos_v[pl.ds(i * SC_LANES, SC_LANES)]
                    tok = g * group_tokens + i * SC_LANES + lane
                    slot = jnp.maximum(p, 0)
                    plsc.store_scatter(idx_v, [slot >> row_shift, slot & (SC_GATHER_ROWS - 1)],
                                       tok, mask=p >= 0)

                for c in range(chunks):
                    row0 = (group_base + g) * group_slots + c * SC_GATHER_ROWS
                    pltpu.async_copy(h_hbm.at[idx_v.at[c]], rows_v, sem).wait()
                    pltpu.sync_copy(rows_v, out_hbm.at[e, pl.ds(row0, SC_GATHER_ROWS)])
            group_base += groups

    mesh = plsc.VectorSubcoreMesh(core_axis_name="c", subcore_axis_name="s",
                                  num_cores=SC_CORES, num_subcores=SC_SUBCORES)
    return pl.kernel(
        body,
        out_type=jax.ShapeDtypeStruct((n_exp, sum(part_groups) * group_slots, width), I32),
        mesh=mesh,
        scratch_types=[pltpu.VMEM((group_tokens,), I32), pltpu.VMEM((chunks, SC_GATHER_ROWS), I32),
                       pltpu.VMEM((SC_GATHER_ROWS, width), I32), pltpu.SemaphoreType.DMA],
        compiler_params=pltpu.CompilerParams(needs_layout_passes=False),
        name="dispatch",
    )(*[a for part in parts for a in part])


def _expert_kernel(xg_ref, wg_ref, wu_ref, wd_ref, y_ref, wg_s, wu_s, wd_s):
    @pl.when(pl.program_id(1) == 0)
    def _():
        wg_s[...] = wg_ref[0].astype(BF16)
        wu_s[...] = wu_ref[0].astype(BF16)
        wd_s[...] = wd_ref[0].astype(BF16)

    xg = _unpack_bf16_pairs(xg_ref[0])
    a = _silu(_dot(xg, wg_s[...])) * _dot(xg, wu_s[...])
    y_ref[0] = _dot(a.astype(BF16), wd_s[...]).astype(y_ref.dtype)


def _experts(xg, w_gate, w_up, w_down, ts):
    e, s, half = xg.shape
    d, f = w_gate.shape[1:]
    assert d == 2 * half
    x_blk = pl.BlockSpec((1, ts, d), lambda x, i: (x, i, 0))
    return pl.pallas_call(
        _expert_kernel,
        name=f"experts_s{s}", grid=(e, s // ts),
        in_specs=[pl.BlockSpec((1, ts, half), lambda x, i: (x, i, 0)),
                  pl.BlockSpec((1, d, f), lambda x, i: (x, 0, 0)),
                  pl.BlockSpec((1, d, f), lambda x, i: (x, 0, 0)),
                  pl.BlockSpec((1, f, d), lambda x, i: (x, 0, 0))],
        out_specs=x_blk,
        out_shape=jax.ShapeDtypeStruct((e, s, d), BF16),
        scratch_shapes=[pltpu.VMEM((d, f), BF16), pltpu.VMEM((d, f), BF16),
                        pltpu.VMEM((f, d), BF16)],
        compiler_params=_params(("parallel", "arbitrary")),
    )(xg, w_gate, w_up, w_down)


def _combine_kernel(pos_ref, afftok_ref, y_ref, x1_ref, mod_ref, o_ref, *, windowed):
    step = pl.program_id(2)
    d = x1_ref.shape[1]
    per_step = y_ref.shape[0]
    slot_base = pl.program_id(1) * y_ref.shape[1] if windowed else 0

    @pl.when(step == 0)
    def _():
        o_ref[...] = jnp.zeros_like(o_ref)

    lane = lax.broadcasted_iota(I32, afftok_ref.shape, 1)
    acc = None
    for k in range(per_step):
        e = step * per_step + k
        p = _one_hot_rows(pos_ref[pl.ds(e, 1), :] - slot_base, y_ref.shape[1])
        gate = jnp.sum(jnp.where(lane == e, afftok_ref[...], 0.0), axis=1, keepdims=True)
        term = gate * _dot_tn(p, y_ref[k])
        acc = term if acc is None else acc + term
    o_ref[...] += acc

    @pl.when(step == pl.num_programs(2) - 1)
    def _():
        g2 = mod_ref[0][:, 5 * d:6 * d]
        o_ref[...] = x1_ref[...] + g2 * o_ref[...]


def _combine(pos, afftok, y, x1, mod3, mod_row, group_tokens, group_slots, group_base, tc, t_len):
    e, n = pos.shape
    d = x1.shape[1]
    groups = n // group_tokens
    per = group_tokens // tc
    tok = lambda g, j, x: (g * per + j, 0)
    windowed = tc % t_len == 0 and per > 1
    win = group_slots // per if windowed else group_slots
    wins = group_slots // win
    return pl.pallas_call(
        functools.partial(_combine_kernel, windowed=windowed),
        name=f"combine_n{n}", grid=(groups, per, e // COMBINE_EXPERTS),
        in_specs=[pl.BlockSpec((e, tc), lambda g, j, x: (0, g * per + j)),
                  pl.BlockSpec((tc, LANES), tok),
                  pl.BlockSpec((COMBINE_EXPERTS, win, d),
                               lambda g, j, x: (x, (group_base + g) * wins + j * (wins // per), 0)),
                  pl.BlockSpec((tc, d), tok),
                  pl.BlockSpec((1, 1, mod3.shape[2]), lambda g, j, x: (mod_row(g), 0, 0))],
        out_specs=pl.BlockSpec((tc, d), tok),
        out_shape=jax.ShapeDtypeStruct((n, d), F32),
        compiler_params=_params(("parallel", "parallel", "arbitrary")),
    )(pos, afftok, y, x1, mod3)


def _rope_tables(t_len):
    n_freq = DA_DQK // 4
    inv = ROPE_BASE ** (-jnp.arange(n_freq, dtype=F32) / n_freq)
    t = jnp.arange(t_len)
    pos = jnp.stack([(t // GRID_W).astype(F32), (t % GRID_W).astype(F32)], axis=1)
    ang = pos[:, :, None, None] * inv[None, None, None, :]
    ang = jnp.broadcast_to(ang, (t_len, 2, 2, n_freq))
    sign = jnp.array([-1.0, 1.0], F32)[None, None, :, None]
    cos = jnp.cos(ang).reshape(t_len, DA_DQK)
    sin = (jnp.sin(ang) * sign).reshape(t_len, DA_DQK)
    reps = SEG // DA_DQK
    return jnp.tile(cos, (1, reps)), jnp.tile(sin, (1, reps))


def _trunk(x, batch, mod3, mod_row_tok, weights, ctx_k, ctx_v, s0, rope, group_tokens):
    (norm1_w, norm2_w, w_in, b_gate, lb_logits, hgrn_norm_w, qkw, gm, lam_p, subln_w,
     w_bh, w_ba, w_out, rw_cat) = weights
    n, d = x.shape
    t_len = n // batch
    latent = rope is not None
    (q_h, bf, bb, kff, kfb, i_h, g_h, dq, dk, dv, gates) = _premix(
        x, mod3, functools.partial(mod_row_tok, tm=PREMIX_TOKENS), norm1_w, w_in, b_gate, lb_logits,
        qkw, gm, rope, PREMIX_TOKENS, BF16 if latent else F32, latent)
    o_h, s_new = _hgrn(q_h, bf, bb, kff, kfb, i_h, g_h, hgrn_norm_w, s0, batch)
    o_a = _attention(dq, dk, dv, ctx_k, ctx_v, lam_p, subln_w, batch, min(t_len, ATT_TQ * ATT_SPLIT))
    x1, h2, afft, afftok = _postmix(
        o_h, o_a, gates, x, mod3, functools.partial(mod_row_tok, tm=POSTMIX_TOKENS),
        w_bh, w_ba, w_out, norm2_w, rw_cat, POSTMIX_TOKENS)
    cap = EC_CAPACITY * t_len // N_EXPERTS
    pos = _route(afft, batch, cap, group_tokens // t_len)
    return (pos, h2, afftok, x1, t_len), dk, dv, s_new


def _moe(routed, mod3, mod_row_grps, w_eg, w_eu, w_ed, group_tokens, group_slots):
    xg = _dispatch([(pos, h2p) for pos, h2p, _, _, _ in routed], group_tokens, group_slots)
    y = _experts(xg, w_eg, w_eu, w_ed, min(xg.shape[1], EXPERT_ROWS))
    outs, group_base = [], 0
    for (pos, _, afftok, x1, t_len), mod_row in zip(routed, mod_row_grps):
        outs.append(_combine(pos, afftok, y, x1, mod3, mod_row, group_tokens, group_slots,
                             group_base, COMBINE_TOKENS, t_len))
        group_base += pos.shape[1] // group_tokens
    return outs


def kernel(x_prompt, x_sample, cache_k, cache_v, state_hgrn, c, c_ctx, norm1_w, norm2_w, w_mod,
           b_mod, w_in, b_gate, hgrn_lb_logits, hgrn_norm_w, qk_norm_w, diff_lambda, diff_subln_w,
           w_branch_hgrn, w_branch_attn, w_out, router_w, w_exp_gate, w_exp_up, w_exp_down):
    batch, seq, d = x_prompt.shape
    dec_batch, dec_seq, _ = x_sample.shape
    past = cache_k.shape[2]
    depth = w_in.shape[0]
    assert depth == 1
    group_tokens = dec_seq
    assert group_tokens % seq == 0 and (batch * seq) % group_tokens == 0
    l = 0

    rows = -(-(1 + dec_batch) // 8) * 8
    cond = jnp.zeros((rows, d), F32).at[0].set(c_ctx).at[1:1 + dec_batch].set(c)
    mod = _modulation(cond, w_mod[l], b_mod[l])
    mod3 = mod.reshape(rows, 1, 6 * d)

    gidx = jnp.arange(GROUP_SPAN) // DA_DQK
    gm = (gidx[:, None] == gidx[None, :]).astype(BF16) * (1.0 / DA_DQK)
    qkw = jnp.tile(qk_norm_w[l], (1, SEG // DA_DQK))
    rw_t = router_w[l].T
    rw_hi = rw_t.astype(BF16)
    rw_cat = jnp.concatenate([rw_hi, (rw_t - rw_hi.astype(F32)).astype(BF16)], axis=0)
    weights = (norm1_w[l][None], norm2_w[l][None], w_in[l].astype(BF16), b_gate[l][None],
               hgrn_lb_logits.reshape(4, SEG), hgrn_norm_w[l], qkw, gm, diff_lambda[l],
               diff_subln_w[l][None], w_branch_hgrn[l].astype(BF16),
               w_branch_attn[l].astype(BF16), w_out[l].astype(BF16),
               rw_cat)

    routed_ctx, k_new, v_new, s_new = _trunk(
        x_prompt.reshape(batch * seq, d), batch, mod3,
        lambda i, tm: 0, weights, None, None, None, None, group_tokens)
    per_req = dec_seq
    routed_lat, _, _, _ = _trunk(
        x_sample.reshape(dec_batch * dec_seq, d), dec_batch, mod3,
        lambda i, tm: 1 + (i * tm) // per_req, weights,
        cache_k[:, l], cache_v[:, l].reshape(dec_batch, past, SEG),
        state_hgrn[:, l], _rope_tables(dec_seq), group_tokens)
    group_slots = EC_CAPACITY * group_tokens // N_EXPERTS
    experts = (w_exp_gate[l], w_exp_up[l], w_exp_down[l], group_tokens, group_slots)
    yp, = _moe([routed_ctx], mod3, [lambda g: 0], *experts)
    ys, = _moe([routed_lat], mod3, [lambda g: 1 + g], *experts)

    return (yp.reshape(batch, seq, d), ys.reshape(dec_batch, dec_seq, d),
            k_new.reshape(batch, 1, seq, DA_HEADS, 2, DA_DQK),
            v_new.reshape(batch, 1, seq, DA_HEADS, DA_DV),
            s_new.reshape(batch, 1, 2, HG_HEADS, HG_DK, HG_DV))
```

```python
import functools
import math

import jax
import jax.numpy as jnp
from jax import lax
from jax.experimental import pallas as pl
from jax.experimental.pallas import tpu as pltpu
from jax.experimental.pallas import tpu_sc as plsc

F32 = jnp.float32
BF16 = jnp.bfloat16
I32 = jnp.int32

EPS = 1e-6
GRID_W = 64
HG_HEADS = 4
HG_DK = 128
HG_DV = 128
HG_CHUNK_LOG2 = 7
HG_CHUNK = 1 << HG_CHUNK_LOG2
HG_UNROLL = 4
HG_ALL_HEADS_TOKENS = 1024
DA_HEADS = 4
DA_DQK = 64
DA_DV = 128
N_EXPERTS = 16
EC_CAPACITY = 2
ROPE_BASE = 10000.0
SEG = 512
N_SEG = 12
LAM_INIT = 0.8 - 0.6 * math.exp(-0.3 * 0)
LANES = 128
ONES_ROWS = 16
ATT_SPLIT = 2
ATT_TQ = 256
PREMIX_TOKENS = 512
POSTMIX_TOKENS = 1024
COMBINE_TOKENS = 1024
COMBINE_EXPERTS = 4
ROUTE_LANES = 2048
GROUP_SPAN = 256
EXPERT_ROWS = 1024
Q_SCALE = DA_DQK ** -0.5 * math.log2(math.e)
HG_SAFE_DECAY = 80.0
VMEM_LIMIT = 56 * 1024 * 1024
SC_CORES = 2
SC_SUBCORES = 16
SC_LANES = 16
SC_GATHER_ROWS = 128


def _dot(a, b):
    return jnp.dot(a, b, preferred_element_type=F32)


def _dot_nt(a, b):
    return lax.dot_general(a, b, (((1,), (1,)), ((), ())), preferred_element_type=F32)


def _dot_tn(a, b):
    return lax.dot_general(a, b, (((0,), (0,)), ((), ())), preferred_element_type=F32)


def _split2(x):
    hi = x.astype(BF16)
    lo = (x - hi.astype(F32)).astype(BF16)
    return hi, lo


def _silu(x):
    return x * jax.nn.sigmoid(x)


def _pack_bf16_pairs(x):
    w = x.shape[1] // 2
    bits = pltpu.bitcast(x.astype(BF16).astype(F32), I32)
    return lax.shift_right_logical(bits[:, :w], 16) | bits[:, w:]


def _unpack_bf16_pairs(words):
    lo = pltpu.bitcast(words << 16, F32)
    hi = pltpu.bitcast(words & jnp.int32(-65536), F32)
    return jnp.concatenate([lo, hi], axis=1).astype(BF16)


def _params(sem):
    return pltpu.CompilerParams(dimension_semantics=sem, vmem_limit_bytes=VMEM_LIMIT)


def _mod_kernel(c_ref, w_ref, b_ref, o_ref):
    s_hi, s_lo = _split2(_silu(c_ref[...]))
    w_hi, w_lo = _split2(w_ref[...])
    o_ref[...] = _dot(s_hi, w_hi) + _dot(s_hi, w_lo) + _dot(s_lo, w_hi) + b_ref[...]


def _modulation(cond, w_mod, b_mod):
    rows, d = cond.shape
    n = w_mod.shape[1]
    bn = 512
    return pl.pallas_call(
        _mod_kernel,
        name="modulation", grid=(n // bn,),
        in_specs=[pl.BlockSpec((rows, d), lambda j: (0, 0)),
                  pl.BlockSpec((d, bn), lambda j: (0, j)),
                  pl.BlockSpec((1, bn), lambda j: (0, j))],
        out_specs=pl.BlockSpec((rows, bn), lambda j: (0, j)),
        out_shape=jax.ShapeDtypeStruct((rows, n), F32),
        compiler_params=_params(("arbitrary",)),
    )(cond, w_mod, b_mod.reshape(1, n))


def _group_rms(z, gm_ref, w):
    sq = (z * z).astype(BF16)
    span = gm_ref.shape[0]
    ms = jnp.concatenate([_dot(sq[:, c:c + span], gm_ref[...]) for c in range(0, z.shape[1], span)],
                         axis=1)
    return z * lax.rsqrt(ms + EPS) * w


def _rope(x, cos, sin_signed):
    n = x.shape[-1]
    lane = lax.broadcasted_iota(I32, x.shape, 1)
    partner = jnp.where((lane & 16) == 0, pltpu.roll(x, n - 16, 1), pltpu.roll(x, 16, 1))
    return x * cos + partner * sin_signed


def _premix_kernel(*refs, latent):
    if latent:
        (x_ref, mod_ref, n1_ref, win_ref, bg_ref, lbl_ref, qkw_ref, gm_ref, cos_ref, sin_ref,
         q_o, bf_o, bb_o, kff_o, kfb_o, i_o, g_o, dq_o, dk_o, dv_o, gate_o) = refs
    else:
        (x_ref, mod_ref, n1_ref, win_ref, bg_ref, lbl_ref, qkw_ref, gm_ref,
         q_o, bf_o, bb_o, kff_o, kfb_o, i_o, g_o, dq_o, dk_o, dv_o, gate_o) = refs
    d = x_ref.shape[1]
    mod = mod_ref[0]
    sh1, sc1 = mod[:, 0:d], mod[:, d:2 * d]
    x = x_ref[...]
    xn = x * lax.rsqrt(jnp.mean(x * x, axis=-1, keepdims=True) + EPS) * n1_ref[...]
    hb = (xn * (1.0 + sc1) + sh1).astype(BF16)

    def seg(j):
        return _dot(hb, win_ref[:, j * SEG:(j + 1) * SEG])

    def lower_bound(direction):
        l0 = lbl_ref[2 * direction:2 * direction + 1, :]
        l1 = lbl_ref[2 * direction + 1:2 * direction + 2, :]
        mx = jnp.maximum(l0, l1)
        e0, e1 = jnp.exp(l0 - mx), jnp.exp(l1 - mx)
        return e0 / (e0 + e1)

    tm = x.shape[0]
    row = lax.broadcasted_iota(I32, (HG_CHUNK, HG_CHUNK), 0)
    col = lax.broadcasted_iota(I32, (HG_CHUNK, HG_CHUNK), 1)

    def chunk_cumsum(lf, order):
        tri = order.astype(BF16)
        hi, lo = _split2(lf)
        parts = [_dot(tri, hi[c:c + HG_CHUNK]) + _dot(tri, lo[c:c + HG_CHUNK])
                 for c in range(0, tm, HG_CHUNK)]
        return jnp.concatenate(parts, axis=0)

    def store(o_ref, val):
        val = val.astype(o_ref.dtype)
        if len(o_ref.shape) == 2:
            o_ref[...] = val
        elif len(o_ref.shape) == 3:
            for h in range(o_ref.shape[0]):
                o_ref[h] = val[:, h * LANES:(h + 1) * LANES]
        elif len(o_ref.shape) == 4:
            for h in range(o_ref.shape[2]):
                o_ref[0, :, h, :] = val[:, h * LANES:(h + 1) * LANES]
        else:
            for h in range(o_ref.shape[2]):
                for m in range(2):
                    lo = (2 * h + m) * DA_DQK
                    o_ref[0, :, h, m, :] = val[:, lo:lo + DA_DQK]

    store(q_o, _silu(seg(0)))
    for j, b_o, kf_o, order in ((1, bf_o, kff_o, row >= col), (2, bb_o, kfb_o, row <= col)):
        lbd = lower_bound(j - 1)
        f = lbd + (1.0 - lbd) * jax.nn.sigmoid(seg(j))
        store(b_o, chunk_cumsum(jnp.log(f), order))
        store(kf_o, 1.0 - f)
    store(i_o, seg(3))
    store(g_o, _silu(seg(4)))
    qn = _group_rms(seg(5), gm_ref, qkw_ref[0:1, :]) * Q_SCALE
    kn = _group_rms(seg(6), gm_ref, qkw_ref[1:2, :])
    if latent:
        cos = jnp.tile(cos_ref[...], (1, SEG // LANES))
        sin = jnp.tile(sin_ref[...], (1, SEG // LANES))
        qn = _rope(qn, cos, sin)
        kn = _rope(kn, cos, sin)
    store(dq_o, qn)
    store(dk_o, kn)
    store(dv_o, seg(7))
    for j in range(4):
        z = seg(8 + j) + bg_ref[:, j * SEG:(j + 1) * SEG]
        gate_o[:, j * SEG:(j + 1) * SEG] = jax.nn.sigmoid(z).astype(gate_o.dtype)


def _premix(x, mod3, mod_row, norm1_w, w_in, b_gate, lb_logits, qkw, gm, rope, tm, kv_dtype,
            kv_head_major):
    n, d = x.shape
    latent = rope is not None
    const = lambda i: (0, 0)
    in_specs = [pl.BlockSpec((tm, d), lambda i: (i, 0)),
                pl.BlockSpec((1, 1, mod3.shape[2]), lambda i: (mod_row(i), 0, 0)),
                pl.BlockSpec((1, d), const),
                pl.BlockSpec(w_in.shape, const),
                pl.BlockSpec(b_gate.shape, const),
                pl.BlockSpec(lb_logits.shape, const),
                pl.BlockSpec(qkw.shape, const),
                pl.BlockSpec(gm.shape, const)]
    args = [x, mod3, norm1_w, w_in, b_gate, lb_logits, qkw, gm]
    if latent:
        cos, sin = rope
        nblk = cos.shape[0] // tm
        in_specs += [pl.BlockSpec((tm, LANES), lambda i: (i % nblk, 0))] * 2
        args += [cos, sin]
    heads = SEG // LANES
    head_spec = pl.BlockSpec((heads, tm, LANES), lambda i: (0, i, 0))
    out_dtypes = [BF16, F32, F32, BF16, BF16, BF16, BF16, BF16]
    out_shape = [jax.ShapeDtypeStruct((heads, n, LANES), t) for t in out_dtypes]
    if kv_head_major:
        out_shape += [jax.ShapeDtypeStruct((heads, n, LANES), kv_dtype)] * 2
        kv_specs = [head_spec] * 2
    else:
        k_shape = (n // tm, tm, DA_HEADS, 2, DA_DQK)
        v_shape = (n // tm, tm, DA_HEADS, DA_DV)
        out_shape += [jax.ShapeDtypeStruct(k_shape, kv_dtype),
                      jax.ShapeDtypeStruct(v_shape, kv_dtype)]
        kv_specs = [pl.BlockSpec((1,) + k_shape[1:], lambda i: (i, 0, 0, 0, 0)),
                    pl.BlockSpec((1,) + v_shape[1:], lambda i: (i, 0, 0, 0))]
    out_shape.append(jax.ShapeDtypeStruct((n, 4 * SEG), BF16))
    out_specs = [head_spec] * 8 + kv_specs + [pl.BlockSpec((tm, 4 * SEG), lambda i: (i, 0))]
    return pl.pallas_call(
        functools.partial(_premix_kernel, latent=latent),
        name=f"premix_n{n}", grid=(n // tm,),
        in_specs=in_specs, out_specs=out_specs, out_shape=out_shape,
        compiler_params=_params(("parallel",)),
    )(*args)


def _hgrn_chunk_local(q, b, total, k, v, keep, safe):
    if safe:
        ref = b[HG_CHUNK // 2:HG_CHUNK // 2 + 1, :]
        qa = q * jnp.exp(b - ref)
        kb = k * jnp.exp(ref - b)
        attn = jnp.where(keep, _dot_nt(qa.astype(BF16), kb.astype(BF16)), 0.0)
        qe = qa * jnp.exp(ref)
        kd = kb * jnp.exp(total - ref)
    else:
        qe = q * jnp.exp(b)
        kd = k * jnp.exp(total - b)
        col = lax.broadcasted_iota(I32, (HG_CHUNK, HG_CHUNK), 1)

        def column(s, acc):
            onehot = (lax.broadcasted_iota(I32, (HG_CHUNK, 1), 0) == s).astype(F32)
            bs = jnp.sum(b * onehot, axis=0, keepdims=True)
            ks = jnp.sum(k * onehot, axis=0, keepdims=True)
            w = jnp.sum(q * ks * jnp.exp(jnp.minimum(b - bs, 0.0)), axis=1, keepdims=True)
            return jnp.where(col == s, w, acc)

        attn = lax.fori_loop(0, HG_CHUNK, column, jnp.zeros((HG_CHUNK, HG_CHUNK), F32))
        attn = jnp.where(keep, attn, 0.0)
    vt = v.astype(F32).T.astype(BF16)
    lhs = jnp.concatenate([qe.astype(BF16), attn.astype(BF16)], axis=1)
    return lhs, vt, _dot(vt, kd.astype(BF16)), jnp.exp(total)


def _hgrn_scan_group(chunks, st):
    outs = []
    for lhs, vt, inc, decay in chunks:
        outs.append(_dot_nt(lhs, jnp.concatenate([st.astype(BF16), vt], axis=1)))
        st = st * decay + inc
    return outs, st


def _hgrn_head(hh, head, refs, has_state):
    if has_state:
        (q_ref, bf_ref, bb_ref, kff_ref, kfb_ref, i_ref, g_ref, nw_ref, s0_ref,
         o_ref, sn_ref, of_s, ob_s) = refs
    else:
        (q_ref, bf_ref, bb_ref, kff_ref, kfb_ref, i_ref, g_ref, nw_ref,
         o_ref, sn_ref, of_s, ob_s) = refs
    t_len = q_ref.shape[1]
    nc = t_len // HG_CHUNK
    unroll = min(HG_UNROLL, nc)
    row = lax.broadcasted_iota(I32, (HG_CHUNK, HG_CHUNK), 0)
    col = lax.broadcasted_iota(I32, (HG_CHUNK, HG_CHUNK), 1)
    keep_f, keep_b = row >= col, row <= col

    mid_f = bf_ref[hh, pl.ds(HG_CHUNK // 2, nc, stride=HG_CHUNK), :]
    tot_f = bf_ref[hh, pl.ds(HG_CHUNK - 1, nc, stride=HG_CHUNK), :]
    mid_b = bb_ref[hh, pl.ds(HG_CHUNK // 2, nc, stride=HG_CHUNK), :]
    tot_b = bb_ref[hh, pl.ds(0, nc, stride=HG_CHUNK), :]
    worst = jnp.minimum(jnp.min(jnp.minimum(mid_f, tot_f - mid_f)),
                        jnp.min(jnp.minimum(mid_b, tot_b - mid_b)))

    if has_state:
        st_f0, st_b0 = s0_ref[0, 0, hh].T, s0_ref[0, 1, hh].T
    else:
        st_f0 = st_b0 = jnp.zeros((HG_DV, HG_DK), F32)

    def scan(safe):
        def body(it, carry):
            st_f, st_b = carry
            rows_f, rows_b, loc_f, loc_b = [], [], [], []
            for u in range(unroll):
                c = it * unroll + u
                sf = pl.multiple_of(c * HG_CHUNK, HG_CHUNK)
                sb = pl.multiple_of((nc - 1 - c) * HG_CHUNK, HG_CHUNK)
                rf, rb = pl.ds(sf, HG_CHUNK), pl.ds(sb, HG_CHUNK)
                rows_f.append(rf)
                rows_b.append(rb)
                loc_f.append(_hgrn_chunk_local(
                    q_ref[hh, rf, :].astype(F32), bf_ref[hh, rf, :],
                    bf_ref[hh, pl.ds(sf + HG_CHUNK - 1, 1), :],
                    kff_ref[hh, rf, :].astype(F32), i_ref[hh, rf, :], keep_f, safe))
                loc_b.append(_hgrn_chunk_local(
                    q_ref[hh, rb, :].astype(F32), bb_ref[hh, rb, :], bb_ref[hh, pl.ds(sb, 1), :],
                    kfb_ref[hh, rb, :].astype(F32), i_ref[hh, rb, :], keep_b, safe))
            outs_f, st_f = _hgrn_scan_group(loc_f, st_f)
            outs_b, st_b = _hgrn_scan_group(loc_b, st_b)
            for rf, rb, o_f, o_b in zip(rows_f, rows_b, outs_f, outs_b):
                of_s[rf, :] = o_f
                ob_s[rb, :] = o_b
            return st_f, st_b
        return lax.fori_loop(0, nc // unroll, body, (st_f0, st_b0))

    st_f, st_b = lax.cond(worst >= -HG_SAFE_DECAY, lambda: scan(True), lambda: scan(False))
    sn_ref[0, 0, hh] = st_f.T
    sn_ref[0, 1, hh] = st_b.T
    o = of_s[...] + ob_s[...]
    nw = nw_ref[pl.ds(head, 1), :]
    on = o * lax.rsqrt(jnp.mean(o * o, axis=-1, keepdims=True) + EPS) * nw
    o_ref[hh] = (on * g_ref[hh].astype(F32)).astype(o_ref.dtype)


def _hgrn_kernel(*refs, has_state):
    per_step = refs[0].shape[0]
    for hh in range(per_step):
        _hgrn_head(hh, pl.program_id(1) * per_step + hh, refs, has_state)


def _hgrn(q, bf, bb, kff, kfb, iv, g, norm_w, s0, batch):
    heads, n, _ = q.shape
    t_len = n // batch
    per_step = heads if t_len <= HG_ALL_HEADS_TOKENS else 1
    blk = pl.BlockSpec((per_step, t_len, HG_DK), lambda b, h: (h, b, 0))
    st_blk = pl.BlockSpec((1, 2, per_step, HG_DK, HG_DV), lambda b, h: (b, 0, h, 0, 0))
    args = [q, bf, bb, kff, kfb, iv, g, norm_w]
    in_specs = [blk] * 7 + [pl.BlockSpec(norm_w.shape, lambda b, h: (0, 0))]
    if s0 is not None:
        args.append(s0)
        in_specs.append(st_blk)
    o, s_new = pl.pallas_call(
        functools.partial(_hgrn_kernel, has_state=s0 is not None),
        name=f"hgrn_n{n}", grid=(batch, heads // per_step),
        in_specs=in_specs,
        out_specs=[blk, st_blk],
        out_shape=[jax.ShapeDtypeStruct((heads, n, HG_DV), BF16),
                   jax.ShapeDtypeStruct((batch, 2, HG_HEADS, HG_DK, HG_DV), F32)],
        scratch_shapes=[pltpu.VMEM((t_len, HG_DV), F32), pltpu.VMEM((t_len, HG_DV), F32)],
        compiler_params=_params(("parallel", "parallel")),
    )(*args)
    return o, s_new


def _attn_kernel(*refs, has_ctx):
    if has_ctx:
        q_ref, k_ref, v_ref, ck_ref, cv_ref, lam_ref, sw_ref, o_ref, k_s, vt_s = refs
    else:
        q_ref, k_ref, v_ref, lam_ref, sw_ref, o_ref, k_s, vt_s = refs
    t_own = k_ref.shape[1]

    def load_keys(ref, hh):
        if len(ref.shape) == 3:
            return ref[hh]
        return jnp.concatenate([ref[0, :, hh, 0, :], ref[0, :, hh, 1, :]], axis=1)

    def load_vals(ref, hh):
        return ref[hh] if len(ref.shape) == 3 else ref[0, :, hh, :]

    lv = lam_ref[...]
    lam = (jnp.exp(jnp.sum(lv[0:1] * lv[1:2], keepdims=True))
           - jnp.exp(jnp.sum(lv[2:3] * lv[3:4], keepdims=True)) + LAM_INIT)
    tq = q_ref.shape[1] // ATT_SPLIT
    dim = lax.broadcasted_iota(I32, (2 * DA_DQK, tq), 0)

    for hh in range(q_ref.shape[0]):
        @pl.when(pl.program_id(2) == 0)
        def _():
            k_s[0:t_own, :] = load_keys(k_ref, hh).astype(BF16)
            vt_s[0:DA_DV, 0:t_own] = load_vals(v_ref, hh).astype(F32).T.astype(BF16)
            if has_ctx:
                k_s[t_own:, :] = load_keys(ck_ref, hh).astype(BF16)
                vt_s[0:DA_DV, t_own:] = load_vals(cv_ref, hh).T.astype(BF16)
            vt_s[DA_DV:, :] = jnp.ones((ONES_ROWS, vt_s.shape[1]), BF16)

        def scores(i):
            qt = q_ref[hh, i * tq:(i + 1) * tq, :].astype(F32).T
            q_both = jnp.concatenate([jnp.where(dim < DA_DQK, qt, 0.0),
                                      jnp.where(dim >= DA_DQK, qt, 0.0)], axis=1).astype(BF16)
            return _dot(k_s[...], q_both)

        def finish(i, st):
            pt = jnp.exp2(st - jnp.max(st, axis=0, keepdims=True)).astype(BF16)
            r = _dot(vt_s[...], pt)
            r = r[0:DA_DV] * (1.0 / r[DA_DV:DA_DV + 1])
            o = (r[:, 0:tq] - lam * r[:, tq:2 * tq]).T
            on = o * lax.rsqrt(jnp.mean(o * o, axis=-1, keepdims=True) + EPS) * sw_ref[...]
            o_ref[hh, i * tq:(i + 1) * tq, :] = (on * (1.0 - LAM_INIT)).astype(o_ref.dtype)

        sts = [scores(i) for i in range(ATT_SPLIT)]
        for i in range(ATT_SPLIT):
            finish(i, sts[i])


def _attention(q, k, v, ctx_k, ctx_v, lam_p, subln_w, batch, tq):
    heads, n, _ = q.shape
    t_len = n // batch
    nq = t_len // tq
    const = lambda b, h, i: (0, 0)
    if k.ndim == 3:
        per_step = 1
        q_blk = pl.BlockSpec((1, tq, DA_DV), lambda b, h, i: (h, b * nq + i, 0))
        kv_blk = pl.BlockSpec((1, t_len, DA_DV), lambda b, h, i: (h, b, 0))
        in_specs = [q_blk, kv_blk, kv_blk]
        args = [q, k, v]
    else:
        per_step = heads
        assert nq == 1
        q_blk = pl.BlockSpec((heads, tq, DA_DV), lambda b, h, i: (0, b * nq + i, 0))
        k = k.reshape(batch, t_len, heads, 2, DA_DQK)
        v = v.reshape(batch, t_len, heads, DA_DV)
        in_specs = [q_blk,
                    pl.BlockSpec((1, t_len, heads, 2, DA_DQK), lambda b, h, i: (b, 0, 0, 0, 0)),
                    pl.BlockSpec((1, t_len, heads, DA_DV), lambda b, h, i: (b, 0, 0, 0))]
        args = [q, k, v]
    n_keys = t_len
    if ctx_k is not None:
        past = ctx_k.shape[1]
        n_keys += past
        ck_blk = pl.BlockSpec((1, past, 1, 2, DA_DQK), lambda b, h, i: (b, 0, h, 0, 0))
        cv_blk = pl.BlockSpec((1, past, DA_DV), lambda b, h, i: (b, 0, h))
        args += [ctx_k, ctx_v]
        in_specs += [ck_blk, cv_blk]
    args += [lam_p, subln_w]
    in_specs += [pl.BlockSpec(lam_p.shape, const), pl.BlockSpec(subln_w.shape, const)]
    o = pl.pallas_call(
        functools.partial(_attn_kernel, has_ctx=ctx_k is not None),
        name=f"attn_n{n}", grid=(batch, heads // per_step, nq),
        in_specs=in_specs, out_specs=q_blk,
        out_shape=jax.ShapeDtypeStruct((heads, n, DA_DV), BF16),
        scratch_shapes=[pltpu.VMEM((n_keys, 2 * DA_DQK), BF16),
                        pltpu.VMEM((DA_DV + ONES_ROWS, n_keys), BF16)],
        compiler_params=_params(("parallel", "parallel", "arbitrary")),
    )(*args)
    return o


def _postmix_kernel(oh_ref, oa_ref, gate_ref, x_ref, mod_ref, wbh_ref, wba_ref, wout_ref,
                    n2_ref, rw_ref, x1_o, h2_o, afft_o, afftok_o):
    d = x_ref.shape[1]
    mod = mod_ref[0]
    g1, sh2, sc2 = mod[:, 2 * d:3 * d], mod[:, 3 * d:4 * d], mod[:, 4 * d:5 * d]
    g_h = gate_ref[:, 0:d].astype(F32)
    g_a = gate_ref[:, d:2 * d].astype(F32)

    def heads_on_lanes(ref):
        return jnp.concatenate([ref[h] for h in range(ref.shape[0])], axis=1)

    merged = (g_h * _dot(heads_on_lanes(oh_ref), wbh_ref[...])
              + g_a * _dot(heads_on_lanes(oa_ref), wba_ref[...]))
    x1 = x_ref[...] + g1 * _dot(merged.astype(BF16), wout_ref[...])
    x1_o[...] = x1
    xn = x1 * lax.rsqrt(jnp.mean(x1 * x1, axis=-1, keepdims=True) + EPS) * n2_ref[...]
    h2 = xn * (1.0 + sc2) + sh2
    h2_o[...] = _pack_bf16_pairs(h2)
    h_hi, h_lo = _split2(h2)
    rw = rw_ref[...]
    t1 = _dot_nt(rw, h_hi)
    t2 = _dot_nt(rw, h_lo)
    e = N_EXPERTS
    logits = t1[0:e] + t1[e:2 * e] + t2[0:e]
    mx = jnp.max(logits, axis=0, keepdims=True)
    p = jnp.exp(logits - mx)
    aff = p / jnp.sum(p, axis=0, keepdims=True)
    afft_o[...] = aff
    pad = jnp.zeros((LANES - e, aff.shape[1]), F32)
    afftok_o[...] = jnp.concatenate([aff, pad], axis=0).T


def _postmix(o_h, o_a, gates, x, mod3, mod_row, w_bh, w_ba, w_out, norm2_w, rw_cat, tm):
    n, d = x.shape
    const = lambda i: (0, 0)
    row = lambda i: (i, 0)
    return pl.pallas_call(
        _postmix_kernel,
        name=f"postmix_n{n}", grid=(n // tm,),
        in_specs=[pl.BlockSpec((o_h.shape[0], tm, LANES), lambda i: (0, i, 0)),
                  pl.BlockSpec((o_a.shape[0], tm, LANES), lambda i: (0, i, 0)),
                  pl.BlockSpec((tm, 4 * SEG), row), pl.BlockSpec((tm, d), row),
                  pl.BlockSpec((1, 1, mod3.shape[2]), lambda i: (mod_row(i), 0, 0)),
                  pl.BlockSpec(w_bh.shape, const), pl.BlockSpec(w_ba.shape, const),
                  pl.BlockSpec(w_out.shape, const), pl.BlockSpec((1, d), const),
                  pl.BlockSpec(rw_cat.shape, const)],
        out_specs=[pl.BlockSpec((tm, d), row), pl.BlockSpec((tm, d // 2), row),
                   pl.BlockSpec((N_EXPERTS, tm), lambda i: (0, i)),
                   pl.BlockSpec((tm, LANES), row)],
        out_shape=[jax.ShapeDtypeStruct((n, d), F32), jax.ShapeDtypeStruct((n, d // 2), I32),
                   jax.ShapeDtypeStruct((N_EXPERTS, n), F32),
                   jax.ShapeDtypeStruct((n, LANES), F32)],
        compiler_params=_params(("parallel",)),
    )(o_h, o_a, gates, x, mod3, w_bh, w_ba, w_out, norm2_w, rw_cat)


def _lane_cumsum_exclusive(x, blk):
    e, t = x.shape
    r = lax.broadcasted_iota(I32, (blk, blk), 0)
    c = lax.broadcasted_iota(I32, (blk, blk), 1)
    upper = (r < c).astype(BF16)
    carry = jnp.zeros((e, 1), F32)
    parts = []
    for j in range(t // blk):
        xb = x[:, j * blk:(j + 1) * blk]
        parts.append(_dot(xb.astype(BF16), upper) + carry)
        carry = carry + jnp.sum(xb, axis=1, keepdims=True)
    return parts[0] if len(parts) == 1 else jnp.concatenate(parts, axis=1)


def _route_kernel(aff_ref, pos_ref, *, cap, req_per_group, t_len):
    n_req = aff_ref.shape[1] // t_len
    bits = [pltpu.bitcast(aff_ref[:, r * t_len:(r + 1) * t_len], I32) for r in range(n_req)]

    def count(mask):
        return jnp.sum(mask.astype(F32), axis=1, keepdims=True)

    def step(i, ths):
        bit = jnp.int32(1) << (30 - i)
        return tuple(jnp.where(count(b >= (th | bit)) >= cap, th | bit, th)
                     for b, th in zip(bits, ths))

    zero = jnp.zeros((aff_ref.shape[0], 1), I32)
    ths = lax.fori_loop(0, 31, step, (zero,) * n_req)
    blk = min(t_len, 256)
    for r, (b, th) in enumerate(zip(bits, ths)):
        gt = b > th
        eq = (b == th).astype(F32)
        need = cap - count(gt)
        tie_rank = _lane_cumsum_exclusive(eq, blk)
        sel = jnp.where(gt, 1.0, jnp.where(tie_rank < need, eq, 0.0))
        slot = _lane_cumsum_exclusive(sel, blk)
        offset = ((pl.program_id(0) * n_req + r) % req_per_group) * cap
        pos_ref[:, r * t_len:(r + 1) * t_len] = jnp.where(sel > 0.0, slot.astype(I32) + offset, -1)


def _route(afft, batch, cap, req_per_group):
    e, n = afft.shape
    t_len = n // batch
    per_step = max(1, min(batch, ROUTE_LANES // t_len))
    assert batch % per_step == 0
    blk = pl.BlockSpec((e, per_step * t_len), lambda b: (0, b))
    return pl.pallas_call(
        functools.partial(_route_kernel, cap=cap, req_per_group=req_per_group, t_len=t_len),
        name=f"route_n{n}", grid=(batch // per_step,), in_specs=[blk], out_specs=blk,
        out_shape=jax.ShapeDtypeStruct((e, n), I32),
        compiler_params=_params(("parallel",)),
    )(afft)


def _one_hot_rows(pos_row, n_slots):
    slot = lax.broadcasted_iota(I32, (n_slots, pos_row.shape[1]), 0)
    return (slot == pos_row).astype(BF16)


def _dispatch(parts, group_tokens, group_slots):
    n_exp = parts[0][0].shape[0]
    width = parts[0][1].shape[1]
    part_groups = [pos.shape[1] // group_tokens for pos, _ in parts]
    workers = SC_CORES * SC_SUBCORES
    chunks = group_slots // SC_GATHER_ROWS
    assert all(g * n_exp % workers == 0 for g in part_groups)
    assert group_slots % SC_GATHER_ROWS == 0 and group_tokens % SC_LANES == 0
    assert n_exp & (n_exp - 1) == 0
    exp_shift = n_exp.bit_length() - 1
    row_shift = SC_GATHER_ROWS.bit_length() - 1

    def body(*refs):
        ins, (out_hbm, pos_v, idx_v, rows_v, sem) = refs[:2 * len(parts)], refs[2 * len(parts):]
        wid = lax.axis_index("s") * SC_CORES + lax.axis_index("c")
        lane = lax.iota(I32, SC_LANES)
        group_base = 0
        for part, groups in enumerate(part_groups):
            pos_hbm, h_hbm = ins[2 * part], ins[2 * part + 1]
            for k in range(groups * n_exp // workers):
                pair = wid + workers * k
                g = pair >> exp_shift
                e = pair & (n_exp - 1)
                pltpu.sync_copy(pos_hbm.at[e, pl.ds(g * group_tokens, group_tokens)], pos_v)

                @pl.loop(0, group_tokens // SC_LANES)
                def _(i):
                    p = pos_v[pl.ds(i * SC_LANES, SC_LANES)]
                    tok = g * group_tokens + i * SC_LANES + lane
                    slot = jnp.maximum(p, 0)
                    plsc.store_scatter(idx_v, [slot >> row_shift, slot & (SC_GATHER_ROWS - 1)],
                                       tok, mask=p >= 0)

                for c in range(chunks):
                    row0 = (group_base + g) * group_slots + c * SC_GATHER_ROWS
                    pltpu.async_copy(h_hbm.at[idx_v.at[c]], rows_v, sem).wait()
                    pltpu.sync_copy(rows_v, out_hbm.at[e, pl.ds(row0, SC_GATHER_ROWS)])
            group_base += groups

    mesh = plsc.VectorSubcoreMesh(core_axis_name="c", subcore_axis_name="s",
                                  num_cores=SC_CORES, num_subcores=SC_SUBCORES)
    return pl.kernel(
        body,
        out_type=jax.ShapeDtypeStruct((n_exp, sum(part_groups) * group_slots, width), I32),
        mesh=mesh,
        scratch_types=[pltpu.VMEM((group_tokens,), I32), pltpu.VMEM((chunks, SC_GATHER_ROWS), I32),
                       pltpu.VMEM((SC_GATHER_ROWS, width), I32), pltpu.SemaphoreType.DMA],
        compiler_params=pltpu.CompilerParams(needs_layout_passes=False),
        name="dispatch",
    )(*[a for part in parts for a in part])


def _expert_kernel(xg_ref, wg_ref, wu_ref, wd_ref, y_ref, wg_s, wu_s, wd_s):
    @pl.when(pl.program_id(1) == 0)
    def _():
        wg_s[...] = wg_ref[0].astype(BF16)
        wu_s[...] = wu_ref[0].astype(BF16)
        wd_s[...] = wd_ref[0].astype(BF16)

    xg = _unpack_bf16_pairs(xg_ref[0])
    a = _silu(_dot(xg, wg_s[...])) * _dot(xg, wu_s[...])
    y_ref[0] = _dot(a.astype(BF16), wd_s[...]).astype(y_ref.dtype)


def _experts(xg, w_gate, w_up, w_down, ts):
    e, s, half = xg.shape
    d, f = w_gate.shape[1:]
    assert d == 2 * half
    x_blk = pl.BlockSpec((1, ts, d), lambda x, i: (x, i, 0))
    return pl.pallas_call(
        _expert_kernel,
        name=f"experts_s{s}", grid=(e, s // ts),
        in_specs=[pl.BlockSpec((1, ts, half), lambda x, i: (x, i, 0)),
                  pl.BlockSpec((1, d, f), lambda x, i: (x, 0, 0)),
                  pl.BlockSpec((1, d, f), lambda x, i: (x, 0, 0)),
                  pl.BlockSpec((1, f, d), lambda x, i: (x, 0, 0))],
        out_specs=x_blk,
        out_shape=jax.ShapeDtypeStruct((e, s, d), BF16),
        scratch_shapes=[pltpu.VMEM((d, f), BF16), pltpu.VMEM((d, f), BF16),
                        pltpu.VMEM((f, d), BF16)],
        compiler_params=_params(("parallel", "arbitrary")),
    )(xg, w_gate, w_up, w_down)


def _combine_kernel(pos_ref, afftok_ref, y_ref, x1_ref, mod_ref, o_ref, *, windowed):
    step = pl.program_id(2)
    d = x1_ref.shape[1]
    per_step = y_ref.shape[0]
    slot_base = pl.program_id(1) * y_ref.shape[1] if windowed else 0

    @pl.when(step == 0)
    def _():
        o_ref[...] = jnp.zeros_like(o_ref)

    lane = lax.broadcasted_iota(I32, afftok_ref.shape, 1)
    acc = None
    for k in range(per_step):
        e = step * per_step + k
        p = _one_hot_rows(pos_ref[pl.ds(e, 1), :] - slot_base, y_ref.shape[1])
        gate = jnp.sum(jnp.where(lane == e, afftok_ref[...], 0.0), axis=1, keepdims=True)
        term = gate * _dot_tn(p, y_ref[k])
        acc = term if acc is None else acc + term
    o_ref[...] += acc

    @pl.when(step == pl.num_programs(2) - 1)
    def _():
        g2 = mod_ref[0][:, 5 * d:6 * d]
        o_ref[...] = x1_ref[...] + g2 * o_ref[...]


def _combine(pos, afftok, y, x1, mod3, mod_row, group_tokens, group_slots, group_base, tc, t_len):
    e, n = pos.shape
    d = x1.shape[1]
    groups = n // group_tokens
    per = group_tokens // tc
    tok = lambda g, j, x: (g * per + j, 0)
    windowed = tc % t_len == 0 and per > 1
    win = group_slots // per if windowed else group_slots
    wins = group_slots // win
    return pl.pallas_call(
        functools.partial(_combine_kernel, windowed=windowed),
        name=f"combine_n{n}", grid=(groups, per, e // COMBINE_EXPERTS),
        in_specs=[pl.BlockSpec((e, tc), lambda g, j, x: (0, g * per + j)),
                  pl.BlockSpec((tc, LANES), tok),
                  pl.BlockSpec((COMBINE_EXPERTS, win, d),
                               lambda g, j, x: (x, (group_base + g) * wins + j * (wins // per), 0)),
                  pl.BlockSpec((tc, d), tok),
                  pl.BlockSpec((1, 1, mod3.shape[2]), lambda g, j, x: (mod_row(g), 0, 0))],
        out_specs=pl.BlockSpec((tc, d), tok),
        out_shape=jax.ShapeDtypeStruct((n, d), F32),
        compiler_params=_params(("parallel", "parallel", "arbitrary")),
    )(pos, afftok, y, x1, mod3)


def _rope_tables(t_len):
    n_freq = DA_DQK // 4
    inv = ROPE_BASE ** (-jnp.arange(n_freq, dtype=F32) / n_freq)
    t = jnp.arange(t_len)
    pos = jnp.stack([(t // GRID_W).astype(F32), (t % GRID_W).astype(F32)], axis=1)
    ang = pos[:, :, None, None] * inv[None, None, None, :]
    ang = jnp.broadcast_to(ang, (t_len, 2, 2, n_freq))
    sign = jnp.array([-1.0, 1.0], F32)[None, None, :, None]
    cos = jnp.cos(ang).reshape(t_len, DA_DQK)
    sin = (jnp.sin(ang) * sign).reshape(t_len, DA_DQK)
    reps = LANES // DA_DQK
    return jnp.tile(cos, (1, reps)), jnp.tile(sin, (1, reps))


def _trunk(x, batch, mod3, mod_row_tok, weights, ctx_k, ctx_v, s0, rope, group_tokens):
    (norm1_w, norm2_w, w_in, b_gate, lb_logits, hgrn_norm_w, qkw, gm, lam_p, subln_w,
     w_bh, w_ba, w_out, rw_cat) = weights
    n, d = x.shape
    t_len = n // batch
    latent = rope is not None
    (q_h, bf, bb, kff, kfb, i_h, g_h, dq, dk, dv, gates) = _premix(
        x, mod3, functools.partial(mod_row_tok, tm=PREMIX_TOKENS), norm1_w, w_in, b_gate, lb_logits,
        qkw, gm, rope, PREMIX_TOKENS, BF16 if latent else F32, latent)
    o_h, s_new = _hgrn(q_h, bf, bb, kff, kfb, i_h, g_h, hgrn_norm_w, s0, batch)
    o_a = _attention(dq, dk, dv, ctx_k, ctx_v, lam_p, subln_w, batch, min(t_len, ATT_TQ * ATT_SPLIT))
    x1, h2, afft, afftok = _postmix(
        o_h, o_a, gates, x, mod3, functools.partial(mod_row_tok, tm=POSTMIX_TOKENS),
        w_bh, w_ba, w_out, norm2_w, rw_cat, POSTMIX_TOKENS)
    cap = EC_CAPACITY * t_len // N_EXPERTS
    pos = _route(afft, batch, cap, group_tokens // t_len)
    return (pos, h2, afftok, x1, t_len), dk, dv, s_new


def _moe(routed, mod3, mod_row_grps, w_eg, w_eu, w_ed, group_tokens, group_slots):
    xg = _dispatch([(pos, h2p) for pos, h2p, _, _, _ in routed], group_tokens, group_slots)
    y = _experts(xg, w_eg, w_eu, w_ed, min(xg.shape[1], EXPERT_ROWS))
    outs, group_base = [], 0
    for (pos, _, afftok, x1, t_len), mod_row in zip(routed, mod_row_grps):
        outs.append(_combine(pos, afftok, y, x1, mod3, mod_row, group_tokens, group_slots,
                             group_base, COMBINE_TOKENS, t_len))
        group_base += pos.shape[1] // group_tokens
    return outs


def kernel(x_prompt, x_sample, cache_k, cache_v, state_hgrn, c, c_ctx, norm1_w, norm2_w, w_mod,
           b_mod, w_in, b_gate, hgrn_lb_logits, hgrn_norm_w, qk_norm_w, diff_lambda, diff_subln_w,
           w_branch_hgrn, w_branch_attn, w_out, router_w, w_exp_gate, w_exp_up, w_exp_down):
    batch, seq, d = x_prompt.shape
    dec_batch, dec_seq, _ = x_sample.shape
    past = cache_k.shape[2]
    depth = w_in.shape[0]
    assert depth == 1
    group_tokens = dec_seq
    assert group_tokens % seq == 0 and (batch * seq) % group_tokens == 0
    l = 0

    rows = -(-(1 + dec_batch) // 8) * 8
    cond = jnp.zeros((rows, d), F32).at[0].set(c_ctx).at[1:1 + dec_batch].set(c)
    mod = _modulation(cond, w_mod[l], b_mod[l])
    mod3 = mod.reshape(rows, 1, 6 * d)

    gidx = jnp.arange(GROUP_SPAN) // DA_DQK
    gm = (gidx[:, None] == gidx[None, :]).astype(BF16) * (1.0 / DA_DQK)
    qkw = jnp.tile(qk_norm_w[l], (1, SEG // DA_DQK))
    rw_t = router_w[l].T
    rw_hi = rw_t.astype(BF16)
    rw_cat = jnp.concatenate([rw_hi, (rw_t - rw_hi.astype(F32)).astype(BF16)], axis=0)
    weights = (norm1_w[l][None], norm2_w[l][None], w_in[l].astype(BF16), b_gate[l][None],
               hgrn_lb_logits.reshape(4, SEG), hgrn_norm_w[l], qkw, gm, diff_lambda[l],
               diff_subln_w[l][None], w_branch_hgrn[l].astype(BF16),
               w_branch_attn[l].astype(BF16), w_out[l].astype(BF16),
               rw_cat)

    routed_ctx, k_new, v_new, s_new = _trunk(
        x_prompt.reshape(batch * seq, d), batch, mod3,
        lambda i, tm: 0, weights, None, None, None, None, group_tokens)
    per_req = dec_seq
    routed_lat, _, _, _ = _trunk(
        x_sample.reshape(dec_batch * dec_seq, d), dec_batch, mod3,
        lambda i, tm: 1 + (i * tm) // per_req, weights,
        cache_k[:, l], cache_v[:, l].reshape(dec_batch, past, SEG),
        state_hgrn[:, l], _rope_tables(dec_seq), group_tokens)
    group_slots = EC_CAPACITY * group_tokens // N_EXPERTS
    experts = (w_exp_gate[l], w_exp_up[l], w_exp_down[l], group_tokens, group_slots)
    yp, = _moe([routed_ctx], mod3, [lambda g: 0], *experts)
    ys, = _moe([routed_lat], mod3, [lambda g: 1 + g], *experts)

    return (yp.reshape(batch, seq, d), ys.reshape(dec_batch, dec_seq, d),
            k_new.reshape(batch, 1, seq, DA_HEADS, 2, DA_DQK),
            v_new.reshape(batch, 1, seq, DA_HEADS, DA_DV),
            s_new.reshape(batch, 1, 2, HG_HEADS, HG_DK, HG_DV))
```

```python
import functools
import math

import jax
import jax.numpy as jnp
from jax import lax
from jax.experimental import pallas as pl
from jax.experimental.pallas import tpu as pltpu
from jax.experimental.pallas import tpu_sc as plsc

F32 = jnp.float32
BF16 = jnp.bfloat16
I32 = jnp.int32

EPS = 1e-6
GRID_W = 64
HG_HEADS = 4
HG_DK = 128
HG_DV = 128
HG_CHUNK_LOG2 = 7
HG_CHUNK = 1 << HG_CHUNK_LOG2
HG_UNROLL = 4
HG_ALL_HEADS_TOKENS = 1024
DA_HEADS = 4
DA_DQK = 64
DA_DV = 128
N_EXPERTS = 16
EC_CAPACITY = 2
ROPE_BASE = 10000.0
SEG = 512
N_SEG = 12
LAM_INIT = 0.8 - 0.6 * math.exp(-0.3 * 0)
LANES = 128
ONES_ROWS = 16
ATT_SPLIT = 8
ATT_TQ = 256
PREMIX_TOKENS = 512
POSTMIX_TOKENS = 1024
COMBINE_TOKENS = 1024
COMBINE_EXPERTS = 4
ROUTE_LANES = 2048
GROUP_SPAN = 256
EXPERT_ROWS = 1024
Q_SCALE = DA_DQK ** -0.5 * math.log2(math.e)
HG_SAFE_DECAY = 80.0
VMEM_LIMIT = 56 * 1024 * 1024
SC_CORES = 2
SC_SUBCORES = 16
SC_LANES = 16
SC_GATHER_ROWS = 128


def _dot(a, b):
    return jnp.dot(a, b, preferred_element_type=F32)


def _dot_nt(a, b):
    return lax.dot_general(a, b, (((1,), (1,)), ((), ())), preferred_element_type=F32)


def _dot_tn(a, b):
    return lax.dot_general(a, b, (((0,), (0,)), ((), ())), preferred_element_type=F32)


def _split2(x):
    hi = x.astype(BF16)
    lo = (x - hi.astype(F32)).astype(BF16)
    return hi, lo


def _silu(x):
    return x * jax.nn.sigmoid(x)


def _pack_bf16_pairs(x):
    w = x.shape[1] // 2
    bits = pltpu.bitcast(x.astype(BF16).astype(F32), I32)
    return lax.shift_right_logical(bits[:, :w], 16) | bits[:, w:]


def _unpack_bf16_pairs(words):
    lo = pltpu.bitcast(words << 16, F32)
    hi = pltpu.bitcast(words & jnp.int32(-65536), F32)
    return jnp.concatenate([lo, hi], axis=1).astype(BF16)


def _params(sem):
    return pltpu.CompilerParams(dimension_semantics=sem, vmem_limit_bytes=VMEM_LIMIT)


def _mod_kernel(c_ref, w_ref, b_ref, o_ref):
    s_hi, s_lo = _split2(_silu(c_ref[...]))
    w_hi, w_lo = _split2(w_ref[...])
    o_ref[...] = _dot(s_hi, w_hi) + _dot(s_hi, w_lo) + _dot(s_lo, w_hi) + b_ref[...]


def _modulation(cond, w_mod, b_mod):
    rows, d = cond.shape
    n = w_mod.shape[1]
    bn = 512
    return pl.pallas_call(
        _mod_kernel,
        name="modulation", grid=(n // bn,),
        in_specs=[pl.BlockSpec((rows, d), lambda j: (0, 0)),
                  pl.BlockSpec((d, bn), lambda j: (0, j)),
                  pl.BlockSpec((1, bn), lambda j: (0, j))],
        out_specs=pl.BlockSpec((rows, bn), lambda j: (0, j)),
        out_shape=jax.ShapeDtypeStruct((rows, n), F32),
        compiler_params=_params(("arbitrary",)),
    )(cond, w_mod, b_mod.reshape(1, n))


def _group_rms(z, gm_ref, w):
    sq = (z * z).astype(BF16)
    span = gm_ref.shape[0]
    ms = jnp.concatenate([_dot(sq[:, c:c + span], gm_ref[...]) for c in range(0, z.shape[1], span)],
                         axis=1)
    return z * lax.rsqrt(ms + EPS) * w


def _rope(x, cos, sin_signed):
    n = x.shape[-1]
    lane = lax.broadcasted_iota(I32, x.shape, 1)
    partner = jnp.where((lane & 16) == 0, pltpu.roll(x, n - 16, 1), pltpu.roll(x, 16, 1))
    return x * cos + partner * sin_signed


def _premix_kernel(*refs, latent):
    if latent:
        (x_ref, mod_ref, n1_ref, win_ref, bg_ref, lbl_ref, qkw_ref, gm_ref, cos_ref, sin_ref,
         q_o, bf_o, bb_o, kff_o, kfb_o, i_o, g_o, dq_o, dk_o, dv_o, gate_o) = refs
    else:
        (x_ref, mod_ref, n1_ref, win_ref, bg_ref, lbl_ref, qkw_ref, gm_ref,
         q_o, bf_o, bb_o, kff_o, kfb_o, i_o, g_o, dq_o, dk_o, dv_o, gate_o) = refs
    d = x_ref.shape[1]
    mod = mod_ref[0]
    sh1, sc1 = mod[:, 0:d], mod[:, d:2 * d]
    x = x_ref[...]
    xn = x * lax.rsqrt(jnp.mean(x * x, axis=-1, keepdims=True) + EPS) * n1_ref[...]
    hb = (xn * (1.0 + sc1) + sh1).astype(BF16)

    def seg(j):
        return _dot(hb, win_ref[:, j * SEG:(j + 1) * SEG])

    def lower_bound(direction):
        l0 = lbl_ref[2 * direction:2 * direction + 1, :]
        l1 = lbl_ref[2 * direction + 1:2 * direction + 2, :]
        mx = jnp.maximum(l0, l1)
        e0, e1 = jnp.exp(l0 - mx), jnp.exp(l1 - mx)
        return e0 / (e0 + e1)

    tm = x.shape[0]
    row = lax.broadcasted_iota(I32, (HG_CHUNK, HG_CHUNK), 0)
    col = lax.broadcasted_iota(I32, (HG_CHUNK, HG_CHUNK), 1)

    def chunk_cumsum(lf, order):
        tri = order.astype(BF16)
        hi, lo = _split2(lf)
        parts = [_dot(tri, hi[c:c + HG_CHUNK]) + _dot(tri, lo[c:c + HG_CHUNK])
                 for c in range(0, tm, HG_CHUNK)]
        return jnp.concatenate(parts, axis=0)

    def store(o_ref, val):
        val = val.astype(o_ref.dtype)
        if len(o_ref.shape) == 2:
            o_ref[...] = val
        elif len(o_ref.shape) == 3:
            for h in range(o_ref.shape[0]):
                o_ref[h] = val[:, h * LANES:(h + 1) * LANES]
        elif len(o_ref.shape) == 4:
            for h in range(o_ref.shape[2]):
                o_ref[0, :, h, :] = val[:, h * LANES:(h + 1) * LANES]
        else:
            for h in range(o_ref.shape[2]):
                for m in range(2):
                    lo = (2 * h + m) * DA_DQK
                    o_ref[0, :, h, m, :] = val[:, lo:lo + DA_DQK]

    store(q_o, _silu(seg(0)))
    for j, b_o, kf_o, order in ((1, bf_o, kff_o, row >= col), (2, bb_o, kfb_o, row <= col)):
        lbd = lower_bound(j - 1)
        f = lbd + (1.0 - lbd) * jax.nn.sigmoid(seg(j))
        store(b_o, chunk_cumsum(jnp.log(f), order))
        store(kf_o, 1.0 - f)
    store(i_o, seg(3))
    store(g_o, _silu(seg(4)))
    qn = _group_rms(seg(5), gm_ref, qkw_ref[0:1, :]) * Q_SCALE
    kn = _group_rms(seg(6), gm_ref, qkw_ref[1:2, :])
    if latent:
        cos = jnp.tile(cos_ref[...], (1, SEG // LANES))
        sin = jnp.tile(sin_ref[...], (1, SEG // LANES))
        qn = _rope(qn, cos, sin)
        kn = _rope(kn, cos, sin)
    store(dq_o, qn)
    store(dk_o, kn)
    store(dv_o, seg(7))
    for j in range(4):
        z = seg(8 + j) + bg_ref[:, j * SEG:(j + 1) * SEG]
        gate_o[:, j * SEG:(j + 1) * SEG] = jax.nn.sigmoid(z).astype(gate_o.dtype)


def _premix(x, mod3, mod_row, norm1_w, w_in, b_gate, lb_logits, qkw, gm, rope, tm, kv_dtype,
            kv_head_major):
    n, d = x.shape
    latent = rope is not None
    const = lambda i: (0, 0)
    in_specs = [pl.BlockSpec((tm, d), lambda i: (i, 0)),
                pl.BlockSpec((1, 1, mod3.shape[2]), lambda i: (mod_row(i), 0, 0)),
                pl.BlockSpec((1, d), const),
                pl.BlockSpec(w_in.shape, const),
                pl.BlockSpec(b_gate.shape, const),
                pl.BlockSpec(lb_logits.shape, const),
                pl.BlockSpec(qkw.shape, const),
                pl.BlockSpec(gm.shape, const)]
    args = [x, mod3, norm1_w, w_in, b_gate, lb_logits, qkw, gm]
    if latent:
        cos, sin = rope
        nblk = cos.shape[0] // tm
        in_specs += [pl.BlockSpec((tm, LANES), lambda i: (i % nblk, 0))] * 2
        args += [cos, sin]
    heads = SEG // LANES
    head_spec = pl.BlockSpec((heads, tm, LANES), lambda i: (0, i, 0))
    out_dtypes = [BF16, F32, F32, BF16, BF16, BF16, BF16, BF16]
    out_shape = [jax.ShapeDtypeStruct((heads, n, LANES), t) for t in out_dtypes]
    if kv_head_major:
        out_shape += [jax.ShapeDtypeStruct((heads, n, LANES), kv_dtype)] * 2
        kv_specs = [head_spec] * 2
    else:
        k_shape = (n // tm, tm, DA_HEADS, 2, DA_DQK)
        v_shape = (n // tm, tm, DA_HEADS, DA_DV)
        out_shape += [jax.ShapeDtypeStruct(k_shape, kv_dtype),
                      jax.ShapeDtypeStruct(v_shape, kv_dtype)]
        kv_specs = [pl.BlockSpec((1,) + k_shape[1:], lambda i: (i, 0, 0, 0, 0)),
                    pl.BlockSpec((1,) + v_shape[1:], lambda i: (i, 0, 0, 0))]
    out_shape.append(jax.ShapeDtypeStruct((n, 4 * SEG), BF16))
    out_specs = [head_spec] * 8 + kv_specs + [pl.BlockSpec((tm, 4 * SEG), lambda i: (i, 0))]
    return pl.pallas_call(
        functools.partial(_premix_kernel, latent=latent),
        name=f"premix_n{n}", grid=(n // tm,),
        in_specs=in_specs, out_specs=out_specs, out_shape=out_shape,
        compiler_params=_params(("parallel",)),
    )(*args)


def _hgrn_chunk_local(q, b, total, k, v, keep, safe):
    if safe:
        ref = b[HG_CHUNK // 2:HG_CHUNK // 2 + 1, :]
        qa = q * jnp.exp(b - ref)
        kb = k * jnp.exp(ref - b)
        attn = jnp.where(keep, _dot_nt(qa.astype(BF16), kb.astype(BF16)), 0.0)
        qe = qa * jnp.exp(ref)
        kd = kb * jnp.exp(total - ref)
    else:
        qe = q * jnp.exp(b)
        kd = k * jnp.exp(total - b)
        col = lax.broadcasted_iota(I32, (HG_CHUNK, HG_CHUNK), 1)

        def column(s, acc):
            onehot = (lax.broadcasted_iota(I32, (HG_CHUNK, 1), 0) == s).astype(F32)
            bs = jnp.sum(b * onehot, axis=0, keepdims=True)
            ks = jnp.sum(k * onehot, axis=0, keepdims=True)
            w = jnp.sum(q * ks * jnp.exp(jnp.minimum(b - bs, 0.0)), axis=1, keepdims=True)
            return jnp.where(col == s, w, acc)

        attn = lax.fori_loop(0, HG_CHUNK, column, jnp.zeros((HG_CHUNK, HG_CHUNK), F32))
        attn = jnp.where(keep, attn, 0.0)
    vt = v.astype(F32).T.astype(BF16)
    lhs = jnp.concatenate([qe.astype(BF16), attn.astype(BF16)], axis=1)
    return lhs, vt, _dot(vt, kd.astype(BF16)), jnp.exp(total)


def _hgrn_scan_group(chunks, st):
    outs = []
    for lhs, vt, inc, decay in chunks:
        outs.append(_dot_nt(lhs, jnp.concatenate([st.astype(BF16), vt], axis=1)))
        st = st * decay + inc
    return outs, st


def _hgrn_head(hh, head, refs, has_state):
    if has_state:
        (q_ref, bf_ref, bb_ref, kff_ref, kfb_ref, i_ref, g_ref, nw_ref, s0_ref,
         o_ref, sn_ref, of_s, ob_s) = refs
    else:
        (q_ref, bf_ref, bb_ref, kff_ref, kfb_ref, i_ref, g_ref, nw_ref,
         o_ref, sn_ref, of_s, ob_s) = refs
    t_len = q_ref.shape[1]
    nc = t_len // HG_CHUNK
    unroll = min(HG_UNROLL, nc)
    row = lax.broadcasted_iota(I32, (HG_CHUNK, HG_CHUNK), 0)
    col = lax.broadcasted_iota(I32, (HG_CHUNK, HG_CHUNK), 1)
    keep_f, keep_b = row >= col, row <= col

    mid_f = bf_ref[hh, pl.ds(HG_CHUNK // 2, nc, stride=HG_CHUNK), :]
    tot_f = bf_ref[hh, pl.ds(HG_CHUNK - 1, nc, stride=HG_CHUNK), :]
    mid_b = bb_ref[hh, pl.ds(HG_CHUNK // 2, nc, stride=HG_CHUNK), :]
    tot_b = bb_ref[hh, pl.ds(0, nc, stride=HG_CHUNK), :]
    worst = jnp.minimum(jnp.min(jnp.minimum(mid_f, tot_f - mid_f)),
                        jnp.min(jnp.minimum(mid_b, tot_b - mid_b)))

    if has_state:
        st_f0, st_b0 = s0_ref[0, 0, hh].T, s0_ref[0, 1, hh].T
    else:
        st_f0 = st_b0 = jnp.zeros((HG_DV, HG_DK), F32)

    def scan(safe):
        def body(it, carry):
            st_f, st_b = carry
            rows_f, rows_b, loc_f, loc_b = [], [], [], []
            for u in range(unroll):
                c = it * unroll + u
                sf = pl.multiple_of(c * HG_CHUNK, HG_CHUNK)
                sb = pl.multiple_of((nc - 1 - c) * HG_CHUNK, HG_CHUNK)
                rf, rb = pl.ds(sf, HG_CHUNK), pl.ds(sb, HG_CHUNK)
                rows_f.append(rf)
                rows_b.append(rb)
                loc_f.append(_hgrn_chunk_local(
                    q_ref[hh, rf, :].astype(F32), bf_ref[hh, rf, :],
                    bf_ref[hh, pl.ds(sf + HG_CHUNK - 1, 1), :],
                    kff_ref[hh, rf, :].astype(F32), i_ref[hh, rf, :], keep_f, safe))
                loc_b.append(_hgrn_chunk_local(
                    q_ref[hh, rb, :].astype(F32), bb_ref[hh, rb, :], bb_ref[hh, pl.ds(sb, 1), :],
                    kfb_ref[hh, rb, :].astype(F32), i_ref[hh, rb, :], keep_b, safe))
            outs_f, st_f = _hgrn_scan_group(loc_f, st_f)
            outs_b, st_b = _hgrn_scan_group(loc_b, st_b)
            for rf, rb, o_f, o_b in zip(rows_f, rows_b, outs_f, outs_b):
                of_s[rf, :] = o_f
                ob_s[rb, :] = o_b
            return st_f, st_b
        return lax.fori_loop(0, nc // unroll, body, (st_f0, st_b0))

    st_f, st_b = lax.cond(worst >= -HG_SAFE_DECAY, lambda: scan(True), lambda: scan(False))
    sn_ref[0, 0, hh] = st_f.T
    sn_ref[0, 1, hh] = st_b.T
    o = of_s[...] + ob_s[...]
    nw = nw_ref[pl.ds(head, 1), :]
    on = o * lax.rsqrt(jnp.mean(o * o, axis=-1, keepdims=True) + EPS) * nw
    o_ref[hh] = (on * g_ref[hh].astype(F32)).astype(o_ref.dtype)


def _hgrn_kernel(*refs, has_state):
    per_step = refs[0].shape[0]
    for hh in range(per_step):
        _hgrn_head(hh, pl.program_id(1) * per_step + hh, refs, has_state)


def _hgrn(q, bf, bb, kff, kfb, iv, g, norm_w, s0, batch):
    heads, n, _ = q.shape
    t_len = n // batch
    per_step = heads if t_len <= HG_ALL_HEADS_TOKENS else 1
    blk = pl.BlockSpec((per_step, t_len, HG_DK), lambda b, h: (h, b, 0))
    st_blk = pl.BlockSpec((1, 2, per_step, HG_DK, HG_DV), lambda b, h: (b, 0, h, 0, 0))
    args = [q, bf, bb, kff, kfb, iv, g, norm_w]
    in_specs = [blk] * 7 + [pl.BlockSpec(norm_w.shape, lambda b, h: (0, 0))]
    if s0 is not None:
        args.append(s0)
        in_specs.append(st_blk)
    o, s_new = pl.pallas_call(
        functools.partial(_hgrn_kernel, has_state=s0 is not None),
        name=f"hgrn_n{n}", grid=(batch, heads // per_step),
        in_specs=in_specs,
        out_specs=[blk, st_blk],
        out_shape=[jax.ShapeDtypeStruct((heads, n, HG_DV), BF16),
                   jax.ShapeDtypeStruct((batch, 2, HG_HEADS, HG_DK, HG_DV), F32)],
        scratch_shapes=[pltpu.VMEM((t_len, HG_DV), F32), pltpu.VMEM((t_len, HG_DV), F32)],
        compiler_params=_params(("parallel", "parallel")),
    )(*args)
    return o, s_new


def _attn_kernel(*refs, has_ctx):
    if has_ctx:
        q_ref, k_ref, v_ref, ck_ref, cv_ref, lam_ref, sw_ref, o_ref, k_s, vt_s = refs
    else:
        q_ref, k_ref, v_ref, lam_ref, sw_ref, o_ref, k_s, vt_s = refs
    t_own = k_ref.shape[1]

    def load_keys(ref, hh):
        if len(ref.shape) == 3:
            return ref[hh]
        return jnp.concatenate([ref[0, :, hh, 0, :], ref[0, :, hh, 1, :]], axis=1)

    def load_vals(ref, hh):
        return ref[hh] if len(ref.shape) == 3 else ref[0, :, hh, :]

    lv = lam_ref[...]
    lam = (jnp.exp(jnp.sum(lv[0:1] * lv[1:2], keepdims=True))
           - jnp.exp(jnp.sum(lv[2:3] * lv[3:4], keepdims=True)) + LAM_INIT)
    n_sub = max(1, q_ref.shape[1] // ATT_TQ)
    tq = q_ref.shape[1] // n_sub
    dim = lax.broadcasted_iota(I32, (2 * DA_DQK, tq), 0)

    for hh in range(q_ref.shape[0]):
        @pl.when(pl.program_id(2) == 0)
        def _():
            k_s[0:t_own, :] = load_keys(k_ref, hh).astype(BF16)
            vt_s[0:DA_DV, 0:t_own] = load_vals(v_ref, hh).astype(F32).T.astype(BF16)
            if has_ctx:
                k_s[t_own:, :] = load_keys(ck_ref, hh).astype(BF16)
                vt_s[0:DA_DV, t_own:] = load_vals(cv_ref, hh).T.astype(BF16)
            vt_s[DA_DV:, :] = jnp.ones((ONES_ROWS, vt_s.shape[1]), BF16)

        def scores(i):
            qt = q_ref[hh, i * tq:(i + 1) * tq, :].astype(F32).T
            q_both = jnp.concatenate([jnp.where(dim < DA_DQK, qt, 0.0),
                                      jnp.where(dim >= DA_DQK, qt, 0.0)], axis=1).astype(BF16)
            return _dot(k_s[...], q_both)

        def finish(i, st):
            pt = jnp.exp2(st - jnp.max(st, axis=0, keepdims=True)).astype(BF16)
            r = _dot(vt_s[...], pt)
            r = r[0:DA_DV] * (1.0 / r[DA_DV:DA_DV + 1])
            o = (r[:, 0:tq] - lam * r[:, tq:2 * tq]).T
            on = o * lax.rsqrt(jnp.mean(o * o, axis=-1, keepdims=True) + EPS) * sw_ref[...]
            o_ref[hh, i * tq:(i + 1) * tq, :] = (on * (1.0 - LAM_INIT)).astype(o_ref.dtype)

        st = scores(0)
        for i in range(n_sub):
            nxt = scores(i + 1) if i + 1 < n_sub else None
            finish(i, st)
            st = nxt


def _attention(q, k, v, ctx_k, ctx_v, lam_p, subln_w, batch, tq):
    heads, n, _ = q.shape
    t_len = n // batch
    nq = t_len // tq
    const = lambda b, h, i: (0, 0)
    if k.ndim == 3:
        per_step = 1
        q_blk = pl.BlockSpec((1, tq, DA_DV), lambda b, h, i: (h, b * nq + i, 0))
        kv_blk = pl.BlockSpec((1, t_len, DA_DV), lambda b, h, i: (h, b, 0))
        in_specs = [q_blk, kv_blk, kv_blk]
        args = [q, k, v]
    else:
        per_step = heads
        assert nq == 1
        q_blk = pl.BlockSpec((heads, tq, DA_DV), lambda b, h, i: (0, b * nq + i, 0))
        k = k.reshape(batch, t_len, heads, 2, DA_DQK)
        v = v.reshape(batch, t_len, heads, DA_DV)
        in_specs = [q_blk,
                    pl.BlockSpec((1, t_len, heads, 2, DA_DQK), lambda b, h, i: (b, 0, 0, 0, 0)),
                    pl.BlockSpec((1, t_len, heads, DA_DV), lambda b, h, i: (b, 0, 0, 0))]
        args = [q, k, v]
    n_keys = t_len
    if ctx_k is not None:
        past = ctx_k.shape[1]
        n_keys += past
        ck_blk = pl.BlockSpec((1, past, 1, 2, DA_DQK), lambda b, h, i: (b, 0, h, 0, 0))
        cv_blk = pl.BlockSpec((1, past, DA_DV), lambda b, h, i: (b, 0, h))
        args += [ctx_k, ctx_v]
        in_specs += [ck_blk, cv_blk]
    args += [lam_p, subln_w]
    in_specs += [pl.BlockSpec(lam_p.shape, const), pl.BlockSpec(subln_w.shape, const)]
    o = pl.pallas_call(
        functools.partial(_attn_kernel, has_ctx=ctx_k is not None),
        name=f"attn_n{n}", grid=(batch, heads // per_step, nq),
        in_specs=in_specs, out_specs=q_blk,
        out_shape=jax.ShapeDtypeStruct((heads, n, DA_DV), BF16),
        scratch_shapes=[pltpu.VMEM((n_keys, 2 * DA_DQK), BF16),
                        pltpu.VMEM((DA_DV + ONES_ROWS, n_keys), BF16)],
        compiler_params=_params(("parallel", "parallel", "arbitrary")),
    )(*args)
    return o


def _postmix_kernel(oh_ref, oa_ref, gate_ref, x_ref, mod_ref, wbh_ref, wba_ref, wout_ref,
                    n2_ref, rw_ref, x1_o, h2_o, afft_o, afftok_o):
    d = x_ref.shape[1]
    mod = mod_ref[0]
    g1, sh2, sc2 = mod[:, 2 * d:3 * d], mod[:, 3 * d:4 * d], mod[:, 4 * d:5 * d]
    g_h = gate_ref[:, 0:d].astype(F32)
    g_a = gate_ref[:, d:2 * d].astype(F32)

    def heads_on_lanes(ref):
        return jnp.concatenate([ref[h] for h in range(ref.shape[0])], axis=1)

    merged = (g_h * _dot(heads_on_lanes(oh_ref), wbh_ref[...])
              + g_a * _dot(heads_on_lanes(oa_ref), wba_ref[...]))
    x1 = x_ref[...] + g1 * _dot(merged.astype(BF16), wout_ref[...])
    x1_o[...] = x1
    xn = x1 * lax.rsqrt(jnp.mean(x1 * x1, axis=-1, keepdims=True) + EPS) * n2_ref[...]
    h2 = xn * (1.0 + sc2) + sh2
    h2_o[...] = _pack_bf16_pairs(h2)
    h_hi, h_lo = _split2(h2)
    rw = rw_ref[...]
    t1 = _dot_nt(rw, h_hi)
    t2 = _dot_nt(rw, h_lo)
    e = N_EXPERTS
    logits = t1[0:e] + t1[e:2 * e] + t2[0:e]
    mx = jnp.max(logits, axis=0, keepdims=True)
    p = jnp.exp(logits - mx)
    aff = p / jnp.sum(p, axis=0, keepdims=True)
    afft_o[...] = aff
    pad = jnp.zeros((LANES - e, aff.shape[1]), F32)
    afftok_o[...] = jnp.concatenate([aff, pad], axis=0).T


def _postmix(o_h, o_a, gates, x, mod3, mod_row, w_bh, w_ba, w_out, norm2_w, rw_cat, tm):
    n, d = x.shape
    const = lambda i: (0, 0)
    row = lambda i: (i, 0)
    return pl.pallas_call(
        _postmix_kernel,
        name=f"postmix_n{n}", grid=(n // tm,),
        in_specs=[pl.BlockSpec((o_h.shape[0], tm, LANES), lambda i: (0, i, 0)),
                  pl.BlockSpec((o_a.shape[0], tm, LANES), lambda i: (0, i, 0)),
                  pl.BlockSpec((tm, 4 * SEG), row), pl.BlockSpec((tm, d), row),
                  pl.BlockSpec((1, 1, mod3.shape[2]), lambda i: (mod_row(i), 0, 0)),
                  pl.BlockSpec(w_bh.shape, const), pl.BlockSpec(w_ba.shape, const),
                  pl.BlockSpec(w_out.shape, const), pl.BlockSpec((1, d), const),
                  pl.BlockSpec(rw_cat.shape, const)],
        out_specs=[pl.BlockSpec((tm, d), row), pl.BlockSpec((tm, d // 2), row),
                   pl.BlockSpec((N_EXPERTS, tm), lambda i: (0, i)),
                   pl.BlockSpec((tm, LANES), row)],
        out_shape=[jax.ShapeDtypeStruct((n, d), F32), jax.ShapeDtypeStruct((n, d // 2), I32),
                   jax.ShapeDtypeStruct((N_EXPERTS, n), F32),
                   jax.ShapeDtypeStruct((n, LANES), F32)],
        compiler_params=_params(("parallel",)),
    )(o_h, o_a, gates, x, mod3, w_bh, w_ba, w_out, norm2_w, rw_cat)


def _lane_cumsum_exclusive(x, blk):
    e, t = x.shape
    r = lax.broadcasted_iota(I32, (blk, blk), 0)
    c = lax.broadcasted_iota(I32, (blk, blk), 1)
    upper = (r < c).astype(BF16)
    carry = jnp.zeros((e, 1), F32)
    parts = []
    for j in range(t // blk):
        xb = x[:, j * blk:(j + 1) * blk]
        parts.append(_dot(xb.astype(BF16), upper) + carry)
        carry = carry + jnp.sum(xb, axis=1, keepdims=True)
    return parts[0] if len(parts) == 1 else jnp.concatenate(parts, axis=1)


def _route_kernel(aff_ref, pos_ref, *, cap, req_per_group, t_len):
    n_req = aff_ref.shape[1] // t_len
    bits = [pltpu.bitcast(aff_ref[:, r * t_len:(r + 1) * t_len], I32) for r in range(n_req)]

    def count(mask):
        return jnp.sum(mask.astype(F32), axis=1, keepdims=True)

    def step(i, ths):
        bit = jnp.int32(1) << (30 - i)
        return tuple(jnp.where(count(b >= (th | bit)) >= cap, th | bit, th)
                     for b, th in zip(bits, ths))

    zero = jnp.zeros((aff_ref.shape[0], 1), I32)
    ths = lax.fori_loop(0, 31, step, (zero,) * n_req)
    blk = min(t_len, 256)
    for r, (b, th) in enumerate(zip(bits, ths)):
        gt = b > th
        eq = (b == th).astype(F32)
        need = cap - count(gt)
        tie_rank = _lane_cumsum_exclusive(eq, blk)
        sel = jnp.where(gt, 1.0, jnp.where(tie_rank < need, eq, 0.0))
        slot = _lane_cumsum_exclusive(sel, blk)
        offset = ((pl.program_id(0) * n_req + r) % req_per_group) * cap
        pos_ref[:, r * t_len:(r + 1) * t_len] = jnp.where(sel > 0.0, slot.astype(I32) + offset, -1)


def _route(afft, batch, cap, req_per_group):
    e, n = afft.shape
    t_len = n // batch
    per_step = max(1, min(batch, ROUTE_LANES // t_len))
    assert batch % per_step == 0
    blk = pl.BlockSpec((e, per_step * t_len), lambda b: (0, b))
    return pl.pallas_call(
        functools.partial(_route_kernel, cap=cap, req_per_group=req_per_group, t_len=t_len),
        name=f"route_n{n}", grid=(batch // per_step,), in_specs=[blk], out_specs=blk,
        out_shape=jax.ShapeDtypeStruct((e, n), I32),
        compiler_params=_params(("parallel",)),
    )(afft)


def _one_hot_rows(pos_row, n_slots):
    slot = lax.broadcasted_iota(I32, (n_slots, pos_row.shape[1]), 0)
    return (slot == pos_row).astype(BF16)


def _dispatch(parts, group_tokens, group_slots):
    n_exp = parts[0][0].shape[0]
    width = parts[0][1].shape[1]
    part_groups = [pos.shape[1] // group_tokens for pos, _ in parts]
    workers = SC_CORES * SC_SUBCORES
    chunks = group_slots // SC_GATHER_ROWS
    assert all(g * n_exp % workers == 0 for g in part_groups)
    assert group_slots % SC_GATHER_ROWS == 0 and group_tokens % SC_LANES == 0
    assert n_exp & (n_exp - 1) == 0
    exp_shift = n_exp.bit_length() - 1
    row_shift = SC_GATHER_ROWS.bit_length() - 1

    def body(*refs):
        ins, (out_hbm, pos_v, idx_v, rows_v, sem) = refs[:2 * len(parts)], refs[2 * len(parts):]
        wid = lax.axis_index("s") * SC_CORES + lax.axis_index("c")
        lane = lax.iota(I32, SC_LANES)
        group_base = 0
        for part, groups in enumerate(part_groups):
            pos_hbm, h_hbm = ins[2 * part], ins[2 * part + 1]
            for k in range(groups * n_exp // workers):
                pair = wid + workers * k
                g = pair >> exp_shift
                e = pair & (n_exp - 1)
                pltpu.sync_copy(pos_hbm.at[e, pl.ds(g * group_tokens, group_tokens)], pos_v)

                @pl.loop(0, group_tokens // SC_LANES)
                def _(i):
                    p = pos_v[pl.ds(i * SC_LANES, SC_LANES)]
                    tok = g * group_tokens + i * SC_LANES + lane
                    slot = jnp.maximum(p, 0)
                    plsc.store_scatter(idx_v, [slot >> row_shift, slot & (SC_GATHER_ROWS - 1)],
                                       tok, mask=p >= 0)

                for c in range(chunks):
                    row0 = (group_base + g) * group_slots + c * SC_GATHER_ROWS
                    pltpu.async_copy(h_hbm.at[idx_v.at[c]], rows_v, sem).wait()
                    pltpu.sync_copy(rows_v, out_hbm.at[e, pl.ds(row0, SC_GATHER_ROWS)])
            group_base += groups

    mesh = plsc.VectorSubcoreMesh(core_axis_name="c", subcore_axis_name="s",
                                  num_cores=SC_CORES, num_subcores=SC_SUBCORES)
    return pl.kernel(
        body,
        out_type=jax.ShapeDtypeStruct((n_exp, sum(part_groups) * group_slots, width), I32),
        mesh=mesh,
        scratch_types=[pltpu.VMEM((group_tokens,), I32), pltpu.VMEM((chunks, SC_GATHER_ROWS), I32),
                       pltpu.VMEM((SC_GATHER_ROWS, width), I32), pltpu.SemaphoreType.DMA],
        compiler_params=pltpu.CompilerParams(needs_layout_passes=False),
        name="dispatch",
    )(*[a for part in parts for a in part])


def _expert_kernel(xg_ref, wg_ref, wu_ref, wd_ref, y_ref, wg_s, wu_s, wd_s):
    @pl.when(pl.program_id(1) == 0)
    def _():
        wg_s[...] = wg_ref[0].astype(BF16)
        wu_s[...] = wu_ref[0].astype(BF16)
        wd_s[...] = wd_ref[0].astype(BF16)

    xg = _unpack_bf16_pairs(xg_ref[0])
    a = _silu(_dot(xg, wg_s[...])) * _dot(xg, wu_s[...])
    y_ref[0] = _dot(a.astype(BF16), wd_s[...]).astype(y_ref.dtype)


def _experts(xg, w_gate, w_up, w_down, ts):
    e, s, half = xg.shape
    d, f = w_gate.shape[1:]
    assert d == 2 * half
    x_blk = pl.BlockSpec((1, ts, d), lambda x, i: (x, i, 0))
    return pl.pallas_call(
        _expert_kernel,
        name=f"experts_s{s}", grid=(e, s // ts),
        in_specs=[pl.BlockSpec((1, ts, half), lambda x, i: (x, i, 0)),
                  pl.BlockSpec((1, d, f), lambda x, i: (x, 0, 0)),
                  pl.BlockSpec((1, d, f), lambda x, i: (x, 0, 0)),
                  pl.BlockSpec((1, f, d), lambda x, i: (x, 0, 0))],
        out_specs=x_blk,
        out_shape=jax.ShapeDtypeStruct((e, s, d), BF16),
        scratch_shapes=[pltpu.VMEM((d, f), BF16), pltpu.VMEM((d, f), BF16),
                        pltpu.VMEM((f, d), BF16)],
        compiler_params=_params(("parallel", "arbitrary")),
    )(xg, w_gate, w_up, w_down)


def _combine_kernel(pos_ref, afftok_ref, y_ref, x1_ref, mod_ref, o_ref, *, windowed):
    step = pl.program_id(2)
    d = x1_ref.shape[1]
    per_step = y_ref.shape[0]
    slot_base = pl.program_id(1) * y_ref.shape[1] if windowed else 0

    @pl.when(step == 0)
    def _():
        o_ref[...] = jnp.zeros_like(o_ref)

    lane = lax.broadcasted_iota(I32, afftok_ref.shape, 1)
    acc = None
    for k in range(per_step):
        e = step * per_step + k
        p = _one_hot_rows(pos_ref[pl.ds(e, 1), :] - slot_base, y_ref.shape[1])
        gate = jnp.sum(jnp.where(lane == e, afftok_ref[...], 0.0), axis=1, keepdims=True)
        term = gate * _dot_tn(p, y_ref[k])
        acc = term if acc is None else acc + term
    o_ref[...] += acc

    @pl.when(step == pl.num_programs(2) - 1)
    def _():
        g2 = mod_ref[0][:, 5 * d:6 * d]
        o_ref[...] = x1_ref[...] + g2 * o_ref[...]


def _combine(pos, afftok, y, x1, mod3, mod_row, group_tokens, group_slots, group_base, tc, t_len):
    e, n = pos.shape
    d = x1.shape[1]
    groups = n // group_tokens
    per = group_tokens // tc
    tok = lambda g, j, x: (g * per + j, 0)
    windowed = tc % t_len == 0 and per > 1
    win = group_slots // per if windowed else group_slots
    wins = group_slots // win
    return pl.pallas_call(
        functools.partial(_combine_kernel, windowed=windowed),
        name=f"combine_n{n}", grid=(groups, per, e // COMBINE_EXPERTS),
        in_specs=[pl.BlockSpec((e, tc), lambda g, j, x: (0, g * per + j)),
                  pl.BlockSpec((tc, LANES), tok),
                  pl.BlockSpec((COMBINE_EXPERTS, win, d),
                               lambda g, j, x: (x, (group_base + g) * wins + j * (wins // per), 0)),
                  pl.BlockSpec((tc, d), tok),
                  pl.BlockSpec((1, 1, mod3.shape[2]), lambda g, j, x: (mod_row(g), 0, 0))],
        out_specs=pl.BlockSpec((tc, d), tok),
        out_shape=jax.ShapeDtypeStruct((n, d), F32),
        compiler_params=_params(("parallel", "parallel", "arbitrary")),
    )(pos, afftok, y, x1, mod3)


def _rope_tables(t_len):
    n_freq = DA_DQK // 4
    inv = ROPE_BASE ** (-jnp.arange(n_freq, dtype=F32) / n_freq)
    t = jnp.arange(t_len)
    pos = jnp.stack([(t // GRID_W).astype(F32), (t % GRID_W).astype(F32)], axis=1)
    ang = pos[:, :, None, None] * inv[None, None, None, :]
    ang = jnp.broadcast_to(ang, (t_len, 2, 2, n_freq))
    sign = jnp.array([-1.0, 1.0], F32)[None, None, :, None]
    cos = jnp.cos(ang).reshape(t_len, DA_DQK)
    sin = (jnp.sin(ang) * sign).reshape(t_len, DA_DQK)
    reps = LANES // DA_DQK
    return jnp.tile(cos, (1, reps)), jnp.tile(sin, (1, reps))


def _trunk(x, batch, mod3, mod_row_tok, weights, ctx_k, ctx_v, s0, rope, group_tokens):
    (norm1_w, norm2_w, w_in, b_gate, lb_logits, hgrn_norm_w, qkw, gm, lam_p, subln_w,
     w_bh, w_ba, w_out, rw_cat) = weights
    n, d = x.shape
    t_len = n // batch
    latent = rope is not None
    (q_h, bf, bb, kff, kfb, i_h, g_h, dq, dk, dv, gates) = _premix(
        x, mod3, functools.partial(mod_row_tok, tm=PREMIX_TOKENS), norm1_w, w_in, b_gate, lb_logits,
        qkw, gm, rope, PREMIX_TOKENS, BF16 if latent else F32, latent)
    o_h, s_new = _hgrn(q_h, bf, bb, kff, kfb, i_h, g_h, hgrn_norm_w, s0, batch)
    o_a = _attention(dq, dk, dv, ctx_k, ctx_v, lam_p, subln_w, batch, min(t_len, ATT_TQ * ATT_SPLIT))
    x1, h2, afft, afftok = _postmix(
        o_h, o_a, gates, x, mod3, functools.partial(mod_row_tok, tm=POSTMIX_TOKENS),
        w_bh, w_ba, w_out, norm2_w, rw_cat, POSTMIX_TOKENS)
    cap = EC_CAPACITY * t_len // N_EXPERTS
    pos = _route(afft, batch, cap, group_tokens // t_len)
    return (pos, h2, afftok, x1, t_len), dk, dv, s_new


def _moe(routed, mod3, mod_row_grps, w_eg, w_eu, w_ed, group_tokens, group_slots):
    xg = _dispatch([(pos, h2p) for pos, h2p, _, _, _ in routed], group_tokens, group_slots)
    y = _experts(xg, w_eg, w_eu, w_ed, min(xg.shape[1], EXPERT_ROWS))
    outs, group_base = [], 0
    for (pos, _, afftok, x1, t_len), mod_row in zip(routed, mod_row_grps):
        outs.append(_combine(pos, afftok, y, x1, mod3, mod_row, group_tokens, group_slots,
                             group_base, COMBINE_TOKENS, t_len))
        group_base += pos.shape[1] // group_tokens
    return outs


def kernel(x_prompt, x_sample, cache_k, cache_v, state_hgrn, c, c_ctx, norm1_w, norm2_w, w_mod,
           b_mod, w_in, b_gate, hgrn_lb_logits, hgrn_norm_w, qk_norm_w, diff_lambda, diff_subln_w,
           w_branch_hgrn, w_branch_attn, w_out, router_w, w_exp_gate, w_exp_up, w_exp_down):
    batch, seq, d = x_prompt.shape
    dec_batch, dec_seq, _ = x_sample.shape
    past = cache_k.shape[2]
    depth = w_in.shape[0]
    assert depth == 1
    group_tokens = dec_seq
    assert group_tokens % seq == 0 and (batch * seq) % group_tokens == 0
    l = 0

    rows = -(-(1 + dec_batch) // 8) * 8
    cond = jnp.zeros((rows, d), F32).at[0].set(c_ctx).at[1:1 + dec_batch].set(c)
    mod = _modulation(cond, w_mod[l], b_mod[l])
    mod3 = mod.reshape(rows, 1, 6 * d)

    gidx = jnp.arange(GROUP_SPAN) // DA_DQK
    gm = (gidx[:, None] == gidx[None, :]).astype(BF16) * (1.0 / DA_DQK)
    qkw = jnp.tile(qk_norm_w[l], (1, SEG // DA_DQK))
    rw_t = router_w[l].T
    rw_hi = rw_t.astype(BF16)
    rw_cat = jnp.concatenate([rw_hi, (rw_t - rw_hi.astype(F32)).astype(BF16)], axis=0)
    weights = (norm1_w[l][None], norm2_w[l][None], w_in[l].astype(BF16), b_gate[l][None],
               hgrn_lb_logits.reshape(4, SEG), hgrn_norm_w[l], qkw, gm, diff_lambda[l],
               diff_subln_w[l][None], w_branch_hgrn[l].astype(BF16),
               w_branch_attn[l].astype(BF16), w_out[l].astype(BF16),
               rw_cat)

    routed_ctx, k_new, v_new, s_new = _trunk(
        x_prompt.reshape(batch * seq, d), batch, mod3,
        lambda i, tm: 0, weights, None, None, None, None, group_tokens)
    per_req = dec_seq
    routed_lat, _, _, _ = _trunk(
        x_sample.reshape(dec_batch * dec_seq, d), dec_batch, mod3,
        lambda i, tm: 1 + (i * tm) // per_req, weights,
        cache_k[:, l], cache_v[:, l].reshape(dec_batch, past, SEG),
        state_hgrn[:, l], _rope_tables(dec_seq), group_tokens)
    group_slots = EC_CAPACITY * group_tokens // N_EXPERTS
    experts = (w_exp_gate[l], w_exp_up[l], w_exp_down[l], group_tokens, group_slots)
    yp, = _moe([routed_ctx], mod3, [lambda g: 0], *experts)
    ys, = _moe([routed_lat], mod3, [lambda g: 1 + g], *experts)

    return (yp.reshape(batch, seq, d), ys.reshape(dec_batch, dec_seq, d),
            k_new.reshape(batch, 1, seq, DA_HEADS, 2, DA_DQK),
            v_new.reshape(batch, 1, seq, DA_HEADS, DA_DV),
            s_new.reshape(batch, 1, 2, HG_HEADS, HG_DK, HG_DV))
```
